```python
import jax, jax.numpy as jnp
from jax import lax
import numpy as np

D_MODEL = 1024
BATCH = 32
SEQ = 256
DEPTH = 2
DEC_BATCH = 2
DEC_SEQ = 1024
PAST_LEN = 256

GRID_W = 64
A_HEADS = 8
A_KV_HEADS = 2
A_HEAD_DIM = 64
A_WINDOW = 128
A_BLOCK = 128
ROPE_BASE = 10000.0
B_HEADS = 4
B_HEAD_DIM = 64
B_DECAY_RANK = 64
B_AAA_RANK = 64
B_GATE_RANK = 128
B_GN_EPS = 64e-5
C_HEADS = 4
C_KEY_DIM = 32
C_VAL_DIM = 64
C_GATE_RANK = 16
C_GATE_NORM = 16.0
C_CHUNK = 64
D_FF = 2816
N_EXPERTS = 8
TOP_K = 2
D_FF_EXPERT = 3584
LN_EPS = 1e-5
DN_ALPHA = (2.0 * DEPTH) ** 0.25
DN_BETA = (8.0 * DEPTH) ** -0.25

A_Q_W = A_HEADS * A_HEAD_DIM
A_KV_W = A_KV_HEADS * A_HEAD_DIM
B_W = B_HEADS * B_HEAD_DIM
C_K_W = C_HEADS * C_KEY_DIM
C_V_W = C_HEADS * C_VAL_DIM
IN_SPLITS = (A_Q_W, A_KV_W, A_KV_W, B_W, B_W, B_W, C_K_W, C_K_W, C_V_W, C_V_W, D_MODEL, D_MODEL, D_MODEL)
IN_W = sum(IN_SPLITS)
NEG_INF = -1e30
F32 = jnp.float32

kernel_name = "hybrid_prefix_swa_rwkv7_gla_diffusion_step"


def _layer_norm(x, g, b):
    xf = x.astype(F32)
    mu = jnp.mean(xf, -1, keepdims=True)
    var = jnp.mean(jnp.square(xf - mu), -1, keepdims=True)
    return ((xf - mu) * lax.rsqrt(var + LN_EPS) * g.astype(F32) + b.astype(F32)).astype(x.dtype)


def _sink_softmax(logits, sink):
    m = jnp.maximum(jnp.max(logits, -1, keepdims=True), sink)
    e = jnp.exp(logits - m)
    return e / (jnp.sum(e, -1, keepdims=True) + jnp.exp(sink - m))


def _flip(t):
    return t[:, ::-1]


def _axial_rope(x):
    T = x.shape[1]
    n_rows = T // GRID_W
    rows = jnp.repeat(jnp.arange(n_rows, dtype=F32), GRID_W)
    cols = jnp.tile(jnp.arange(GRID_W, dtype=F32), n_rows)
    half = A_HEAD_DIM // 2
    inv_freq = ROPE_BASE ** (-jnp.arange(0, half, 2, dtype=F32) / half)

    def rot(xa, pos):
        ang = pos[:, None] * inv_freq[None, :]
        cos = jnp.cos(ang)[None, :, None, :]
        sin = jnp.sin(ang)[None, :, None, :]
        x1, x2 = xa[..., :half // 2], xa[..., half // 2:]
        return jnp.concatenate([x1 * cos - x2 * sin, x1 * sin + x2 * cos], -1)

    xf = x.astype(F32)
    return jnp.concatenate([rot(xf[..., :half], rows), rot(xf[..., half:], cols)], -1).astype(x.dtype)


def _attn_context(q, k, v, sink):
    B, S = q.shape[:2]
    G = A_HEADS // A_KV_HEADS
    qg = q.reshape(B, S, A_KV_HEADS, G, A_HEAD_DIM)
    s = jnp.einsum('bqkgd,bskd->bkgqs', qg, k).astype(F32) * (A_HEAD_DIM ** -0.5)
    p = _sink_softmax(s, sink.astype(F32).reshape(1, A_KV_HEADS, G, 1, 1))
    o = jnp.einsum('bkgqs,bskd->bqkgd', p.astype(v.dtype), v)
    return o.reshape(B, S, A_Q_W)


def _attn_latent(q, k, v, k_ctx, v_ctx, sink):
    B, T = q.shape[:2]
    nb = T // A_BLOCK
    G = A_HEADS // A_KV_HEADS
    q = _axial_rope(q)
    k = _axial_rope(k)
    qb = q.reshape(B, nb, A_BLOCK, A_KV_HEADS, G, A_HEAD_DIM)
    pad = ((0, 0), (A_BLOCK, A_BLOCK), (0, 0), (0, 0))
    kp = jnp.pad(k, pad).reshape(B, nb + 2, A_BLOCK, A_KV_HEADS, A_HEAD_DIM)
    vp = jnp.pad(v, pad).reshape(B, nb + 2, A_BLOCK, A_KV_HEADS, A_HEAD_DIM)
    kband = jnp.concatenate([kp[:, :-2], kp[:, 1:-1], kp[:, 2:]], axis=2)
    vband = jnp.concatenate([vp[:, :-2], vp[:, 1:-1], vp[:, 2:]], axis=2)
    scale = A_HEAD_DIM ** -0.5
    s_loc = jnp.einsum('bnqkgd,bnskd->bnkgqs', qb, kband).astype(F32) * scale
    qi = jnp.arange(A_BLOCK)[:, None]
    kj = jnp.arange(3 * A_BLOCK)[None, :]
    rel = kj - A_BLOCK - qi
    kpos = (jnp.arange(nb)[:, None, None] - 1) * A_BLOCK + kj[None]
    valid = (jnp.abs(rel)[None] <= A_WINDOW) & (kpos >= 0) & (kpos < T)
    s_loc = jnp.where(valid[None, :, None, None], s_loc, NEG_INF)
    s_ctx = jnp.einsum('bnqkgd,bpkd->bnkgqp', qb, k_ctx).astype(F32) * scale
    s = jnp.concatenate([s_loc, s_ctx], -1)
    p = _sink_softmax(s, sink.astype(F32).reshape(1, 1, A_KV_HEADS, G, 1, 1))
    p_loc = p[..., :3 * A_BLOCK].astype(v.dtype)
    p_ctx = p[..., 3 * A_BLOCK:].astype(v.dtype)
    o = (jnp.einsum('bnkgqs,bnskd->bnqkgd', p_loc, vband)
         + jnp.einsum('bnkgqp,bpkd->bnqkgd', p_ctx, v_ctx))
    return o.reshape(B, T, A_Q_W)


def _rwkv_scan(r, w, kk, a, v, kt, s0):
    def step(S, xs):
        r_t, w_t, kk_t, a_t, v_t, k_t = xs
        sk = jnp.einsum('bhvk,bhk->bhv', S, kk_t)
        S = S * w_t[:, :, None, :] - sk[..., None] * (a_t * kk_t)[:, :, None, :] + v_t[..., :, None] * k_t[..., None, :]
        return S, jnp.einsum('bhvk,bhk->bhv', S, r_t)

    xs = tuple(jnp.moveaxis(t, 1, 0) for t in (r, w, kk, a, v, kt))
    S, y = lax.scan(step, s0, xs)
    return jnp.moveaxis(y, 0, 1), S


def _rwkv_branch(h, r, k, v, P, l, s0):
    B, T, _ = h.shape
    hs = (B, T, B_HEADS, B_HEAD_DIM)
    rf = r.astype(F32).reshape(hs)
    kf = k.astype(F32).reshape(hs)
    vf = v.astype(F32).reshape(hs)
    kk = kf * P['rwkv_k_k'][l].astype(F32).reshape(B_HEADS, B_HEAD_DIM)
    kk = kk * lax.rsqrt(jnp.sum(kk * kk, -1, keepdims=True) + 1e-12)
    k_a = P['rwkv_k_a'][l].astype(F32).reshape(B_HEADS, B_HEAD_DIM)
    r_k = P['rwkv_r_k'][l].astype(F32)
    y = jnp.zeros(hs, F32)
    bonus = jnp.zeros(hs, F32)
    finals = []
    for d in range(2):
        z = (P['rwkv_w0'][l, d] + jnp.tanh(h @ P['rwkv_w_a'][l, d]) @ P['rwkv_w_b'][l, d]).astype(F32)
        w = jnp.exp(-jnp.exp(-jax.nn.softplus(-z) - 0.5)).reshape(hs)
        a = jax.nn.sigmoid((P['rwkv_a0'][l, d] + (h @ P['rwkv_a_a'][l, d]) @ P['rwkv_a_b'][l, d]).astype(F32)).reshape(hs)
        kt = kf * (1.0 + (a - 1.0) * k_a)
        seq = (rf, w, kk, a, vf, kt)
        if d == 1:
            seq = tuple(_flip(t) for t in seq)
        yd, sd = _rwkv_scan(*seq, s0[:, d].astype(F32))
        if d == 1:
            yd = _flip(yd)
        y = y + yd
        bonus = bonus + jnp.sum(rf * kt * r_k, -1, keepdims=True) * vf
        finals.append(sd)
    mu = jnp.mean(y, -1, keepdims=True)
    var = jnp.mean(jnp.square(y - mu), -1, keepdims=True)
    y = ((y - mu) * lax.rsqrt(var + B_GN_EPS)).reshape(B, T, B_W)
    y = y * P['rwkv_ln_g'][l].astype(F32) + P['rwkv_ln_b'][l].astype(F32) + bonus.reshape(B, T, B_W)
    g = jax.nn.sigmoid(h @ P['rwkv_g_a'][l]) @ P['rwkv_g_b'][l]
    return y.astype(h.dtype) * g, jnp.stack(finals, 1)


def _gla_chunked(q, k, v, log_a, s0):
    B, T, H, K = q.shape
    V = v.shape[-1]
    n = T // C_CHUNK
    q = q.reshape(B, n, C_CHUNK, H, K)
    k = k.reshape(B, n, C_CHUNK, H, K)
    v = v.reshape(B, n, C_CHUNK, H, V)
    b = jnp.cumsum(log_a.reshape(B, n, C_CHUNK, H, K), axis=2)
    causal = jnp.tril(jnp.ones((C_CHUNK, C_CHUNK), bool))[None, None, :, :, None, None]
    dec = jnp.exp(jnp.where(causal, b[:, :, :, None] - b[:, :, None, :], -jnp.inf))
    att = jnp.einsum('bnthk,bnshk,bntshk->bnhts', q, k, dec)
    o = jnp.einsum('bnhts,bnshv->bnthv', att, v)
    b_last = b[:, :, -1]
    u = jnp.einsum('bnshk,bnshv->bnhkv', k * jnp.exp(b_last[:, :, None] - b), v)

    def step(S, xs):
        d_last, u_n = xs
        return d_last[..., None] * S + u_n, S

    S_fin, S_start = lax.scan(step, s0, (jnp.moveaxis(jnp.exp(b_last), 1, 0), jnp.moveaxis(u, 1, 0)))
    o = o + jnp.einsum('bnthk,bnhkv->bnthv', q * jnp.exp(b), jnp.moveaxis(S_start, 0, 1))
    return o.reshape(B, T, H, V), S_fin


def _gla_branch(h, q, k, v, og, P, l, s0):
    B, T, _ = h.shape
    qf = q.astype(F32).reshape(B, T, C_HEADS, C_KEY_DIM) * (C_KEY_DIM ** -0.5)
    kf = k.astype(F32).reshape(B, T, C_HEADS, C_KEY_DIM)
    vf = v.astype(F32).reshape(B, T, C_HEADS, C_VAL_DIM)
    o = jnp.zeros((B, T, C_HEADS, C_VAL_DIM), F32)
    finals = []
    for d in range(2):
        gl = (h @ P['gla_gate_a'][l, d]) @ P['gla_gate_b'][l, d] + P['gla_gate_bias'][l, d]
        log_a = (jax.nn.log_sigmoid(gl.astype(F32)) / C_GATE_NORM).reshape(B, T, C_HEADS, C_KEY_DIM)
        seq = (qf, kf, vf, log_a)
        if d == 1:
            seq = tuple(_flip(t) for t in seq)
        od, sd = _gla_chunked(*seq, s0[:, d].astype(F32))
        if d == 1:
            od = _flip(od)
        o = o + od
        finals.append(sd)
    o = o * lax.rsqrt(jnp.mean(o * o, -1, keepdims=True) + LN_EPS) * P['gla_norm_g'][l].astype(F32)
    return o.reshape(B, T, C_V_W).astype(h.dtype) * jax.nn.silu(og), jnp.stack(finals, 1)


def _token_mixer(h, P, l, ctx):
    B, T, _ = h.shape
    idx = np.cumsum(IN_SPLITS)[:-1]
    aq, ak, av, br, bk, bv, cq, ck, cv, cog, ga, gb, gc = jnp.split(h @ P['w_in'][l], idx, axis=-1)
    aq = aq.reshape(B, T, A_HEADS, A_HEAD_DIM)
    ak = ak.reshape(B, T, A_KV_HEADS, A_HEAD_DIM)
    av = av.reshape(B, T, A_KV_HEADS, A_HEAD_DIM)
    if ctx is None:
        o_a = _attn_context(aq, ak, av, P['attn_sink'][l])
        s_r0 = jnp.zeros((B, 2, B_HEADS, B_HEAD_DIM, B_HEAD_DIM), F32)
        s_g0 = jnp.zeros((B, 2, C_HEADS, C_KEY_DIM, C_VAL_DIM), F32)
    else:
        k_ctx, v_ctx, s_r0, s_g0 = ctx
        o_a = _attn_latent(aq, ak, av, k_ctx.astype(h.dtype), v_ctx.astype(h.dtype), P['attn_sink'][l])
    o_b, s_r = _rwkv_branch(h, br, bk, bv, P, l, s_r0)
    o_c, s_g = _gla_branch(h, cq, ck, cv, cog, P, l, s_g0)
    merged = (jax.nn.sigmoid(ga) * (o_a @ P['w_up_a'][l])
              + jax.nn.sigmoid(gb) * (o_b @ P['w_up_b'][l])
              + jax.nn.sigmoid(gc) * (o_c @ P['w_up_c'][l]))
    return merged @ P['w_out'][l], (ak, av, s_r, s_g)


def _dense_ffn(h, P, i):
    return (jax.nn.silu(h @ P['ffn_w_gate'][i]) * (h @ P['ffn_w_up'][i])) @ P['ffn_w_down'][i]


def _moe_ffn(h, P, i):
    B, T, D = h.shape
    xt = h.reshape(B * T, D)
    logits = (xt @ P['moe_router'][i]).astype(F32)
    top_v, top_i = lax.top_k(logits, TOP_K)
    w = jax.nn.softmax(top_v, axis=-1)
    gates = jnp.sum(w[..., None] * jax.nn.one_hot(top_i, N_EXPERTS, dtype=F32), axis=1).astype(h.dtype)
    out = jnp.zeros_like(xt)
    for e in range(N_EXPERTS):
        y = (jax.nn.silu(xt @ P['moe_w_gate'][i, e]) * (xt @ P['moe_w_up'][i, e])) @ P['moe_w_down'][i, e]
        out = out + gates[:, e:e + 1] * y
    return out.reshape(B, T, D)


def _trunk(x, cvec, P, caches):
    kept = []
    for l in range(DEPTH):
        mod = (jax.nn.silu(cvec) @ P['w_ada'][l] + P['b_ada'][l])[:, None, :]
        sh1, sc1, g1, sh2, sc2, g2 = jnp.split(mod, 6, axis=-1)
        ctx = None if caches is None else tuple(t[:, l] for t in caches)
        mix, ctx_tensors = _token_mixer(x * (1.0 + sc1) + sh1, P, l, ctx)
        x = _layer_norm(DN_ALPHA * x + g1 * mix, P['ln1_g'][l], P['ln1_b'][l])
        h = x * (1.0 + sc2) + sh2
        f = _dense_ffn(h, P, l // 2) if l % 2 == 0 else _moe_ffn(h, P, l // 2)
        x = _layer_norm(DN_ALPHA * x + g2 * f, P['ln2_g'][l], P['ln2_b'][l])
        kept.append(ctx_tensors)
    return x, kept


def setup_inputs(seed: int = 0) -> dict:
    key = jax.random.key(seed)
    ks = iter(jax.random.split(key, 64))

    def nrm(shape, scale=1.0):
        return jax.random.normal(next(ks), shape, F32) * scale

    L = DEPTH
    nd = (DEPTH + 1) // 2
    nm = DEPTH // 2
    D = D_MODEL
    return {
        'x_prompt': nrm((BATCH, SEQ, D)),
        'x_sample': nrm((DEC_BATCH, DEC_SEQ, D)),
        'cache_attn_k': nrm((DEC_BATCH, L, PAST_LEN, A_KV_HEADS, A_HEAD_DIM)),
        'cache_attn_v': nrm((DEC_BATCH, L, PAST_LEN, A_KV_HEADS, A_HEAD_DIM)),
        'state_rwkv': nrm((DEC_BATCH, L, 2, B_HEADS, B_HEAD_DIM, B_HEAD_DIM), 0.3),
        'state_gla': nrm((DEC_BATCH, L, 2, C_HEADS, C_KEY_DIM, C_VAL_DIM), 0.3),
        'c': nrm((DEC_BATCH, D)),
        'c_ctx': nrm((D,)),
        'w_ada': nrm((L, D, 6 * D), D ** -0.5),
        'b_ada': nrm((L, 6 * D), 0.02),
        'w_in': nrm((L, D, IN_W), D ** -0.5),
        'attn_sink': nrm((L, A_HEADS), 0.5),
        'rwkv_w0': nrm((L, 2, B_W), 0.5),
        'rwkv_w_a': nrm((L, 2, D, B_DECAY_RANK), D ** -0.5),
        'rwkv_w_b': nrm((L, 2, B_DECAY_RANK, B_W), 0.5 * B_DECAY_RANK ** -0.5),
        'rwkv_a0': nrm((L, 2, B_W), 0.5),
        'rwkv_a_a': nrm((L, 2, D, B_AAA_RANK), D ** -0.5),
        'rwkv_a_b': nrm((L, 2, B_AAA_RANK, B_W), 0.5 * B_AAA_RANK ** -0.5),
        'rwkv_g_a': nrm((L, D, B_GATE_RANK), D ** -0.5),
        'rwkv_g_b': nrm((L, B_GATE_RANK, B_W), B_GATE_RANK ** -0.5),
        'rwkv_k_k': 0.85 + nrm((L, B_W), 0.05),
        'rwkv_k_a': 1.0 + nrm((L, B_W), 0.05),
        'rwkv_r_k': nrm((L, B_HEADS, B_HEAD_DIM), 0.1),
        'rwkv_ln_g': 1.0 + nrm((L, B_W), 0.02),
        'rwkv_ln_b': nrm((L, B_W), 0.02),
        'gla_gate_a': nrm((L, 2, D, C_GATE_RANK), D ** -0.5),
        'gla_gate_b': nrm((L, 2, C_GATE_RANK, C_K_W), C_GATE_RANK ** -0.5),
        'gla_gate_bias': 1.0 + nrm((L, 2, C_K_W), 0.5),
        'gla_norm_g': 1.0 + nrm((L, C_VAL_DIM), 0.02),
        'w_up_a': nrm((L, A_Q_W, D), A_Q_W ** -0.5),
        'w_up_b': nrm((L, B_W, D), B_W ** -0.5),
        'w_up_c': nrm((L, C_V_W, D), C_V_W ** -0.5),
        'w_out': nrm((L, D, D), DN_BETA * D ** -0.5),
        'ln1_g': 1.0 + nrm((L, D), 0.02),
        'ln1_b': nrm((L, D), 0.02),
        'ln2_g': 1.0 + nrm((L, D), 0.02),
        'ln2_b': nrm((L, D), 0.02),
        'ffn_w_gate': nrm((nd, D, D_FF), D ** -0.5),
        'ffn_w_up': nrm((nd, D, D_FF), D ** -0.5),
        'ffn_w_down': nrm((nd, D_FF, D), DN_BETA * D_FF ** -0.5),
        'moe_router': nrm((nm, D, N_EXPERTS), D ** -0.5),
        'moe_w_gate': nrm((nm, N_EXPERTS, D, D_FF_EXPERT), D ** -0.5),
        'moe_w_up': nrm((nm, N_EXPERTS, D, D_FF_EXPERT), D ** -0.5),
        'moe_w_down': nrm((nm, N_EXPERTS, D_FF_EXPERT, D), DN_BETA * D_FF_EXPERT ** -0.5),
    }


def reference(x_prompt, x_sample, cache_attn_k, cache_attn_v, state_rwkv, state_gla, c, c_ctx,
              w_ada, b_ada, w_in, attn_sink,
              rwkv_w0, rwkv_w_a, rwkv_w_b, rwkv_a0, rwkv_a_a, rwkv_a_b, rwkv_g_a, rwkv_g_b,
              rwkv_k_k, rwkv_k_a, rwkv_r_k, rwkv_ln_g, rwkv_ln_b,
              gla_gate_a, gla_gate_b, gla_gate_bias, gla_norm_g,
              w_up_a, w_up_b, w_up_c, w_out, ln1_g, ln1_b, ln2_g, ln2_b,
              ffn_w_gate, ffn_w_up, ffn_w_down, moe_router, moe_w_gate, moe_w_up, moe_w_down):
    P = dict(w_ada=w_ada, b_ada=b_ada, w_in=w_in, attn_sink=attn_sink,
             rwkv_w0=rwkv_w0, rwkv_w_a=rwkv_w_a, rwkv_w_b=rwkv_w_b, rwkv_a0=rwkv_a0,
             rwkv_a_a=rwkv_a_a, rwkv_a_b=rwkv_a_b, rwkv_g_a=rwkv_g_a, rwkv_g_b=rwkv_g_b,
             rwkv_k_k=rwkv_k_k, rwkv_k_a=rwkv_k_a, rwkv_r_k=rwkv_r_k,
             rwkv_ln_g=rwkv_ln_g, rwkv_ln_b=rwkv_ln_b,
             gla_gate_a=gla_gate_a, gla_gate_b=gla_gate_b, gla_gate_bias=gla_gate_bias, gla_norm_g=gla_norm_g,
             w_up_a=w_up_a, w_up_b=w_up_b, w_up_c=w_up_c, w_out=w_out,
             ln1_g=ln1_g, ln1_b=ln1_b, ln2_g=ln2_g, ln2_b=ln2_b,
             ffn_w_gate=ffn_w_gate, ffn_w_up=ffn_w_up, ffn_w_down=ffn_w_down,
             moe_router=moe_router, moe_w_gate=moe_w_gate, moe_w_up=moe_w_up, moe_w_down=moe_w_down)
    y_prompt, kept = _trunk(x_prompt, c_ctx[None, :], P, None)
    y_sample, _ = _trunk(x_sample, c, P, (cache_attn_k, cache_attn_v, state_rwkv, state_gla))
    new_attn_k = jnp.stack([t[0] for t in kept], axis=1)
    new_attn_v = jnp.stack([t[1] for t in kept], axis=1)
    new_state_rwkv = jnp.stack([t[2] for t in kept], axis=1)
    new_state_gla = jnp.stack([t[3] for t in kept], axis=1)
    return (y_prompt, y_sample, new_attn_k, new_attn_v, new_state_rwkv, new_state_gla)
```

```python
import functools

import numpy as np
import jax
import jax.numpy as jnp
from jax import lax
from jax.experimental import pallas as pl
from jax.experimental.pallas import tpu as pltpu

D = 1024
N_CTX_B, CTX_T = 32, 256
N_DEC_B, DEC_T = 2, 1024
N_CTX_TOK = N_CTX_B * CTX_T
N_TOK = N_CTX_TOK + N_DEC_B * DEC_T
DEPTH = 2
PAST = 256
GRID_W = 64
A_HEADS, A_KV, A_DH = 8, 2, 64
A_G = A_HEADS // A_KV
A_WIN, A_BLK = 128, 128
ROPE_BASE = 10000.0
B_HEADS, B_DH = 4, 64
B_W = B_HEADS * B_DH
B_GN_EPS = 64e-5
C_HEADS, C_DK, C_DV = 4, 32, 64
C_KW, C_VW = C_HEADS * C_DK, C_HEADS * C_DV
C_GATE_RANK = 16
C_GATE_NORM = 16.0
C_CHUNK = 64
C_SUB = 16
D_FF = 2816
N_EXP = 8
D_FFE = 3584
LN_EPS = 1e-5
DN_ALPHA = (2.0 * DEPTH) ** 0.25
NEG_INF = -1e30
IN_MAIN = 2304
IN_GATE = 3072
LR_W = 512
N_GROUPS = 8

F32 = jnp.float32
BF16 = jnp.bfloat16
VMEM_LIMIT = 56 * 1024 * 1024

NN = ((1,), (0,))
NT = ((1,), (1,))
TN = ((0,), (0,))


def _dg(a, b, dims=NN):
    return lax.dot_general(a, b, (dims, ((), ())), preferred_element_type=F32)


def _split2(x):
    hi = x.astype(BF16)
    lo = (x - hi.astype(F32)).astype(BF16)
    return hi, lo


def _split3(x):
    hi = x.astype(BF16)
    r = x - hi.astype(F32)
    mid = r.astype(BF16)
    lo = (r - mid.astype(F32)).astype(BF16)
    return hi, mid, lo


def _mm(a, b, dims=NN):
    return _dg(a.astype(BF16), b.astype(BF16), dims)


def _mm3(a, b, dims=NN):
    ah, al = _split2(a)
    bh, bl = _split2(b)
    return _dg(ah, bh, dims) + (_dg(ah, bl, dims) + _dg(al, bh, dims))


def _mm_xr(a, b_exact, passes, dims=NN):
    parts = (a.astype(BF16),) if passes == 1 else (_split2(a) if passes == 2 else _split3(a))
    out = _dg(parts[0], b_exact, dims)
    for p in parts[1:]:
        out = out + _dg(p, b_exact, dims)
    return out


def _mm_xl(a_exact, b, passes):
    parts = (b.astype(BF16),) if passes == 1 else (_split2(b) if passes == 2 else _split3(b))
    out = _dg(a_exact, parts[0])
    for p in parts[1:]:
        out = out + _dg(a_exact, p)
    return out


def _sigmoid(x):
    return 1.0 / (1.0 + jnp.exp(-x))


def _iota(shape, dim):
    return lax.broadcasted_iota(jnp.int32, shape, dim)


def _block_ones(n, blk):
    return (_iota((n, n), 0) // blk == _iota((n, n), 1) // blk).astype(BF16)


def _layer_norm(y, g, b):
    mu = jnp.mean(y, -1, keepdims=True)
    yc = y - mu
    var = jnp.mean(yc * yc, -1, keepdims=True)
    return yc * lax.rsqrt(var + LN_EPS) * g + b


def _group_of_tile(i, tm):
    n_ctx = N_CTX_TOK // tm
    per_dec = DEC_T // tm
    return jnp.where(i < n_ctx, 0, 1 + (i - n_ctx) // per_dec)


def _cparams(sem):
    return pltpu.CompilerParams(dimension_semantics=sem, vmem_limit_bytes=VMEM_LIMIT)


def _ada_kernel(c_ref, w_ref, b_ref, o_ref):
    c = c_ref[...]
    s = c * _sigmoid(c)
    o_ref[...] = _mm3(s, w_ref[...]) + b_ref[...]


def _ada_call(cvec, w_ada, b_ada):
    tn = 1536
    return pl.pallas_call(
        _ada_kernel,
        grid=(DEPTH, 6 * D // tn),
        in_specs=[
            pl.BlockSpec((N_GROUPS, D), lambda l, j: (0, 0)),
            pl.BlockSpec((None, D, tn), lambda l, j: (l, 0, j)),
            pl.BlockSpec((None, 1, tn), lambda l, j: (l, 0, j)),
        ],
        out_specs=pl.BlockSpec((None, N_GROUPS, tn), lambda l, j: (l, 0, j)),
        out_shape=jax.ShapeDtypeStruct((DEPTH, N_GROUPS, 6 * D), F32),
        compiler_params=_cparams(("arbitrary", "arbitrary")),
        name="ada",
    )(cvec, w_ada, b_ada.reshape(DEPTH, 1, 6 * D))


def _inproj_kernel(x_ref, mod_ref, w_ref, o_ref):
    sh = mod_ref[0:1, :]
    sc = mod_ref[1:2, :]
    h = x_ref[...] * (1.0 + sc) + sh
    o_ref[...] = _mm(h, w_ref[...])


def _inproj_call(x, mod_l, w, w_spec_fn, n_out, tn, name):
    tm = 512
    return pl.pallas_call(
        _inproj_kernel,
        grid=(n_out // tn, N_TOK // tm),
        in_specs=[
            pl.BlockSpec((tm, D), lambda j, i: (i, 0)),
            pl.BlockSpec((None, 6, D), lambda j, i: (_group_of_tile(i, tm), 0, 0)),
            w_spec_fn(tn),
        ],
        out_specs=pl.BlockSpec((tm, tn), lambda j, i: (i, j)),
        out_shape=jax.ShapeDtypeStruct((N_TOK, n_out), F32),
        compiler_params=_cparams(("arbitrary", "arbitrary")),
        name=name,
    )(x, mod_l, w)


def _sink_col(sink_ref, kvh, rows_per_head):
    n = A_G * rows_per_head
    r = _iota((n, 1), 0) // rows_per_head
    col = jnp.full((n, 1), sink_ref[kvh * A_G], F32)
    for g in range(1, A_G):
        col = jnp.where(r == g, sink_ref[kvh * A_G + g], col)
    return col


def _attn_ctx_kernel(sink_ref, q_ref, k_ref, v_ref, o_ref):
    scale = A_DH ** -0.5
    for kvh in range(A_KV):
        ks = k_ref[:, kvh * A_DH:(kvh + 1) * A_DH].astype(BF16)
        vs = v_ref[:, kvh * A_DH:(kvh + 1) * A_DH].astype(BF16)
        q4 = jnp.concatenate(
            [q_ref[:, (kvh * A_G + g) * A_DH:(kvh * A_G + g + 1) * A_DH] for g in range(A_G)], axis=0)
        s = _dg(q4.astype(BF16), ks, NT) * scale
        sink = _sink_col(sink_ref, kvh, CTX_T)
        m = jnp.maximum(jnp.max(s, -1, keepdims=True), sink)
        e = jnp.exp(s - m)
        p = e / (jnp.sum(e, -1, keepdims=True) + jnp.exp(sink - m))
        o = _dg(p.astype(BF16), vs)
        for g in range(A_G):
            h = kvh * A_G + g
            o_ref[:, h * A_DH:(h + 1) * A_DH] = o[g * CTX_T:(g + 1) * CTX_T, :]


def _attn_ctx_call(sink_l, pm):
    return pl.pallas_call(
        _attn_ctx_kernel,
        grid=(N_CTX_B,),
        in_specs=[
            pl.BlockSpec(memory_space=pltpu.SMEM),
            pl.BlockSpec((CTX_T, 512), lambda b: (b, 0)),
            pl.BlockSpec((CTX_T, 128), lambda b: (b, 4)),
            pl.BlockSpec((CTX_T, 128), lambda b: (b, 5)),
        ],
        out_specs=pl.BlockSpec((CTX_T, 512), lambda b: (b, 0)),
        out_shape=jax.ShapeDtypeStruct((N_CTX_TOK, 512), F32),
        compiler_params=_cparams(("arbitrary",)),
        name="attn_ctx",
    )(sink_l, pm, pm, pm)


def _rope(x, cos, sin_signed):
    w = x.shape[-1]
    lane = _iota(x.shape, 1)
    partner = jnp.where((lane % 32) < 16, pltpu.roll(x, w - 16, 1), pltpu.roll(x, 16, 1))
    return x * cos + partner * sin_signed


def _attn_lat_kernel(sink_ref, q_ref, k_ref, v_ref, kc_ref, vc_ref, cos_ref, sin_ref, o_ref, kr_ref):
    n = pl.program_id(1)
    scale = A_DH ** -0.5

    @pl.when(n == 0)
    def _():
        kr_ref[...] = _rope(k_ref[...], cos_ref[:, 0:128], sin_ref[:, 0:128]).astype(BF16)

    q0 = pl.multiple_of(n * A_BLK, A_BLK)
    qr = _rope(q_ref[...], cos_ref[pl.ds(q0, A_BLK), :], sin_ref[pl.ds(q0, A_BLK), :])
    kstart = pl.multiple_of(jnp.clip((n - 1) * A_BLK, 0, DEC_T - 3 * A_BLK), A_BLK)
    kwin = kr_ref[pl.ds(kstart, 3 * A_BLK), :]
    vwin = v_ref[pl.ds(kstart, 3 * A_BLK), :].astype(BF16)
    kc = kc_ref[...].astype(BF16)
    vc = vc_ref[...].astype(BF16)
    rows = A_G * A_BLK
    qpos = q0 + _iota((rows, 3 * A_BLK), 0) % A_BLK
    kpos = kstart + _iota((rows, 3 * A_BLK), 1)
    valid = jnp.abs(qpos - kpos) <= A_WIN
    for kvh in range(A_KV):
        cs = slice(kvh * A_DH, (kvh + 1) * A_DH)
        q4 = jnp.concatenate(
            [qr[:, (kvh * A_G + g) * A_DH:(kvh * A_G + g + 1) * A_DH] for g in range(A_G)], axis=0).astype(BF16)
        s_loc = jnp.where(valid, _dg(q4, kwin[:, cs], NT) * scale, NEG_INF)
        s_ctx = _dg(q4, kc[:, cs], NT) * scale
        sink = _sink_col(sink_ref, kvh, A_BLK)
        m = jnp.maximum(jnp.maximum(jnp.max(s_loc, -1, keepdims=True), jnp.max(s_ctx, -1, keepdims=True)), sink)
        e_loc = jnp.exp(s_loc - m)
        e_ctx = jnp.exp(s_ctx - m)
        inv = 1.0 / (jnp.sum(e_loc, -1, keepdims=True) + jnp.sum(e_ctx, -1, keepdims=True) + jnp.exp(sink - m))
        o = _dg((e_loc * inv).astype(BF16), vwin[:, cs]) + _dg((e_ctx * inv).astype(BF16), vc[:, cs])
        for g in range(A_G):
            h = kvh * A_G + g
            o_ref[:, h * A_DH:(h + 1) * A_DH] = o[g * A_BLK:(g + 1) * A_BLK, :]


def _rope_tables():
    half = A_DH // 2
    t = np.arange(DEC_T)
    rows = (t // GRID_W).astype(np.float32)
    cols = (t % GRID_W).astype(np.float32)
    inv_freq = (ROPE_BASE ** (-np.arange(0, half, 2, dtype=np.float32) / half)).astype(np.float32)
    ang_r = rows[:, None] * inv_freq[None, :]
    ang_c = cols[:, None] * inv_freq[None, :]
    cos = np.concatenate([np.cos(ang_r), np.cos(ang_r), np.cos(ang_c), np.cos(ang_c)], -1)
    sin = np.concatenate([-np.sin(ang_r), np.sin(ang_r), -np.sin(ang_c), np.sin(ang_c)], -1)
    return (jnp.asarray(np.tile(cos, (1, A_HEADS)), F32), jnp.asarray(np.tile(sin, (1, A_HEADS)), F32))


def _attn_lat_call(sink_l, pm, kc, vc, l, cos, sin):
    nb = DEC_T // A_BLK
    row0 = N_CTX_TOK // A_BLK
    seq0 = N_CTX_TOK // DEC_T
    return pl.pallas_call(
        _attn_lat_kernel,
        grid=(N_DEC_B, nb),
        in_specs=[
            pl.BlockSpec(memory_space=pltpu.SMEM),
            pl.BlockSpec((A_BLK, 512), lambda b, n: (row0 + b * nb + n, 0)),
            pl.BlockSpec((DEC_T, 128), lambda b, n: (seq0 + b, 4)),
            pl.BlockSpec((DEC_T, 128), lambda b, n: (seq0 + b, 5)),
            pl.BlockSpec((None, None, PAST, 128), lambda b, n: (b, l, 0, 0)),
            pl.BlockSpec((None, None, PAST, 128), lambda b, n: (b, l, 0, 0)),
            pl.BlockSpec((DEC_T, 512), lambda b, n: (0, 0)),
            pl.BlockSpec((DEC_T, 512), lambda b, n: (0, 0)),
        ],
        out_specs=pl.BlockSpec((A_BLK, 512), lambda b, n: (b * nb + n, 0)),
        out_shape=jax.ShapeDtypeStruct((N_DEC_B * DEC_T, 512), F32),
        scratch_shapes=[pltpu.VMEM((DEC_T, 128), BF16)],
        compiler_params=_cparams(("arbitrary", "arbitrary")),
        name="attn_lat",
    )(sink_l, pm, pm, pm, kc, vc, cos, sin)


def _rwkv_kernel(T, NS, r_ref, k_ref, v_ref, lr_ref, s0_ref, w0_ref, wb_ref, a0_ref, ab_ref, gb_ref,
                 kk_ref, ka_ref, rk_ref, lng_ref, lnb_ref, o_ref, sfin_ref,
                 KK, W, WR, AKK, KT, C1, C2, BON, Y, S):
    ones4 = _block_ones(B_W, B_DH)
    ones2 = _block_ones(128, B_DH)
    decay_c = float(np.exp(-0.5))
    RC = 256
    SUB = 8

    def prep(c, carry):
        r0 = pl.multiple_of(c * RC, RC)
        rs = pl.ds(r0, RC)
        r = r_ref[rs, :]
        k = k_ref[rs, :]
        v = v_ref[rs, :]
        lr = lr_ref[rs, :]
        kkr = k * kk_ref[...]
        kk = kkr * lax.rsqrt(_mm_xr(kkr * kkr, ones4, 3) + 1e-12)
        KK[rs, :] = kk
        bonus = jnp.zeros((RC, B_W), F32)
        for d in range(2):
            z = w0_ref[d] + _mm(jnp.tanh(lr[:, 64 * d:64 * d + 64]), wb_ref[d])
            w = jnp.exp(-decay_c * _sigmoid(z))
            a = _sigmoid(a0_ref[d] + _mm(lr[:, 128 + 64 * d:192 + 64 * d], ab_ref[d]))
            kt = k * (1.0 + (a - 1.0) * ka_ref[...])
            akk = a * kk
            W[d, rs, :] = w
            WR[d, rs, :] = w * r
            AKK[d, rs, :] = akk
            KT[d, rs, :] = kt
            C1[d, rs, :] = _mm_xr(akk * r, ones4, 3)
            C2[d, rs, :] = _mm_xr(kt * r, ones4, 3)
            bonus = bonus + _mm_xr(r * kt * rk_ref[...], ones4, 3) * v
        BON[rs, :] = bonus
        return carry

    lax.fori_loop(0, NS * T // RC, prep, 0)

    for s in range(NS):
        for d in range(2):
            for hp in range(2):
                S[s, d, hp] = jnp.concatenate([s0_ref[s, d, 2 * hp], s0_ref[s, d, 2 * hp + 1]], axis=1)

    eye2 = _iota((B_DH, 128), 0) == (_iota((B_DH, 128), 1) % B_DH)

    def steps(i, carry):
        for s in range(NS):
            for d in range(2):
                t0 = pl.multiple_of(s * T + (i * SUB if d == 0 else T - SUB - i * SUB), SUB)
                rows = pl.ds(t0, SUB)
                for hp in range(2):
                    cs = slice(hp * 128, (hp + 1) * 128)
                    kk8, wr8, v8 = KK[rows, cs], WR[d, rows, cs], v_ref[rows, cs]
                    w8, akk8, kt8 = W[d, rows, cs], AKK[d, rows, cs], KT[d, rows, cs]
                    c18, c28 = C1[d, rows, cs], C2[d, rows, cs]
                    st = S[s, d, hp]
                    ys = [None] * SUB
                    for jj in range(SUB):
                        j = jj if d == 0 else SUB - 1 - jj
                        r = slice(j, j + 1)
                        sk = _mm_xr(st * kk8[r], ones2, 2)
                        yp = _mm_xr(st * wr8[r], ones2, 1)
                        vcol = _mm_xr(jnp.where(eye2, v8[r], 0.0), ones2, 2)
                        st = st * w8[r] - sk * akk8[r] + vcol * kt8[r]
                        y = yp - sk * c18[r] + vcol * c28[r]
                        ys[j] = jnp.sum(jnp.where(eye2, y, 0.0), axis=0, keepdims=True)
                    S[s, d, hp] = st
                    Y[d, rows, cs] = jnp.concatenate(ys, axis=0)
        return carry

    lax.fori_loop(0, T // SUB, steps, 0)

    for s in range(NS):
        for d in range(2):
            for hp in range(2):
                st = S[s, d, hp]
                sfin_ref[s, d, 2 * hp] = st[:, 0:B_DH]
                sfin_ref[s, d, 2 * hp + 1] = st[:, B_DH:2 * B_DH]

    def post(c, carry):
        r0 = pl.multiple_of(c * RC, RC)
        rs = pl.ds(r0, RC)
        y = Y[0, rs, :] + Y[1, rs, :]
        mu = _mm_xr(y, ones4, 3) * (1.0 / B_DH)
        yc = y - mu
        var = _mm_xr(yc * yc, ones4, 3) * (1.0 / B_DH)
        yn = yc * lax.rsqrt(var + B_GN_EPS) * lng_ref[...] + lnb_ref[...] + BON[rs, :]
        g = _mm(_sigmoid(lr_ref[rs, 256:384]), gb_ref[...])
        o_ref[rs, :] = yn * g
        return carry

    lax.fori_loop(0, NS * T // RC, post, 0)


def _rwkv_call(T, NS, n_seq, tok0, pm, plr, s0, prm, name):
    rows = NS * T
    blk0 = tok0 // rows
    full = lambda shape: pl.BlockSpec(shape, lambda b: (0,) * len(shape))
    kern = functools.partial(_rwkv_kernel, T, NS)
    return pl.pallas_call(
        kern,
        grid=(n_seq // NS,),
        in_specs=[
            pl.BlockSpec((rows, B_W), lambda b: (blk0 + b, 3)),
            pl.BlockSpec((rows, B_W), lambda b: (blk0 + b, 4)),
            pl.BlockSpec((rows, B_W), lambda b: (blk0 + b, 5)),
            pl.BlockSpec((rows, LR_W), lambda b: (blk0 + b, 0)),
            pl.BlockSpec((NS, 2, B_HEADS, B_DH, B_DH), lambda b: (b, 0, 0, 0, 0)),
            full((2, 1, B_W)), full((2, 64, B_W)), full((2, 1, B_W)), full((2, 64, B_W)), full((128, B_W)),
            full((1, B_W)), full((1, B_W)), full((1, B_W)), full((1, B_W)), full((1, B_W)),
        ],
        out_specs=[
            pl.BlockSpec((rows, B_W), lambda b: (b, 0)),
            pl.BlockSpec((NS, 2, B_HEADS, B_DH, B_DH), lambda b: (b, 0, 0, 0, 0)),
        ],
        out_shape=[
            jax.ShapeDtypeStruct((n_seq * T, B_W), F32),
            jax.ShapeDtypeStruct((n_seq, 2, B_HEADS, B_DH, B_DH), F32),
        ],
        scratch_shapes=[
            pltpu.VMEM((rows, B_W), F32),
            pltpu.VMEM((2, rows, B_W), F32),
            pltpu.VMEM((2, rows, B_W), F32),
            pltpu.VMEM((2, rows, B_W), F32),
            pltpu.VMEM((2, rows, B_W), F32),
            pltpu.VMEM((2, rows, B_W), F32),
            pltpu.VMEM((2, rows, B_W), F32),
            pltpu.VMEM((rows, B_W), F32),
            pltpu.VMEM((2, rows, B_W), F32),
            pltpu.VMEM((NS, 2, 2, B_DH, 128), F32),
        ],
        compiler_params=_cparams(("arbitrary",)),
        name=name,
    )(pm, pm, pm, plr, s0, *prm)


def _gla_kernel(T, q_ref, k_ref, v_ref, og_ref, lr_ref, s0_ref, gb_ref, bias_ref, ng_ref,
                o_ref, sfin_ref, LA, O, S):
    n_chunks = T // C_CHUNK
    nsub = C_CHUNK // C_SUB
    qscale = C_DK ** -0.5
    lr = lr_ref[...]
    for d in range(2):
        gl = _mm(lr, gb_ref[d]) + bias_ref[d]
        LA[d] = (jnp.minimum(gl, 0.0) - jnp.log(1.0 + jnp.exp(-jnp.abs(gl)))) * (1.0 / C_GATE_NORM)
    bd_state = _iota((C_KW, C_VW), 0) // C_DK == _iota((C_KW, C_VW), 1) // C_DV
    for d in range(2):
        for h in range(C_HEADS):
            pad_l = h * C_DV
            pad_r = C_VW - (h + 1) * C_DV
            blk = s0_ref[d, h]
            parts = ([jnp.zeros((C_DK, pad_l), F32)] if pad_l else []) + [blk] + \
                    ([jnp.zeros((C_DK, pad_r), F32)] if pad_r else [])
            S[d, h * C_DK:(h + 1) * C_DK, :] = jnp.concatenate(parts, axis=1)

    ti = _iota((C_CHUNK, C_CHUNK), 0)
    si = _iota((C_CHUNK, C_CHUNK), 1)
    tri = ((si <= ti).astype(BF16), (si >= ti).astype(BF16))
    trow = _iota((C_CHUNK, 1), 0)
    mask_k = _iota((C_CHUNK, C_KW), 0) // C_SUB == _iota((C_CHUNK, C_KW), 1) // C_DK
    mask_v = _iota((C_CHUNK, C_VW), 0) // C_SUB == _iota((C_CHUNK, C_VW), 1) // C_DV
    t_att = _iota((C_CHUNK, C_CHUNK), 0)
    s_att = _iota((C_CHUNK, C_CHUNK), 1) % C_SUB
    eye_k = _iota((C_KW, C_KW), 0) == _iota((C_KW, C_KW), 1)

    def chunk(d, r0):
        rs = pl.ds(r0, C_CHUNK)
        b = _mm_xl(tri[d], LA[d, rs, :], 3)
        q = q_ref[rs, :] * qscale
        k = k_ref[rs, :]
        v = v_ref[rs, :]
        st = S[d]
        o = _mm3(q * jnp.exp(b), st)
        for j in range(nsub):
            lo, hi = j * C_SUB, (j + 1) * C_SUB
            if d == 0:
                gamma = b[hi - 1:hi, :]
                row_ok = trow >= lo
                att_ok = t_att >= lo + s_att
            else:
                gamma = b[lo:lo + 1, :]
                row_ok = trow < hi
                att_ok = t_att <= lo + s_att
            qj = q * jnp.exp(jnp.where(row_ok, b - gamma, NEG_INF))
            kj = k[lo:hi, :] * jnp.exp(gamma - b[lo:hi, :])
            kbd = jnp.where(mask_k, jnp.concatenate([kj] * C_HEADS, axis=0), 0.0)
            att = jnp.where(att_ok, _mm(qj, kbd, NT), 0.0)
            vbd = jnp.where(mask_v, jnp.concatenate([v[lo:hi, :]] * C_HEADS, axis=0), 0.0)
            o = o + _mm(att, vbd)
        O[d, rs, :] = o
        blast = b[C_CHUNK - 1:C_CHUNK, :] if d == 0 else b[0:1, :]
        kl = k * jnp.exp(blast - b)
        upd = jnp.where(bd_state, _mm3(kl.T, v), 0.0)
        dec = jnp.where(eye_k, jnp.exp(blast), 0.0)
        S[d] = _mm3(dec, st) + upd

    def body(c, carry):
        chunk(0, pl.multiple_of(c * C_CHUNK, C_CHUNK))
        chunk(1, pl.multiple_of((n_chunks - 1 - c) * C_CHUNK, C_CHUNK))
        return carry

    lax.fori_loop(0, n_chunks, body, 0)

    for d in range(2):
        st = S[d]
        for h in range(C_HEADS):
            sfin_ref[d, h] = st[h * C_DK:(h + 1) * C_DK, h * C_DV:(h + 1) * C_DV]

    ones4 = _block_ones(C_VW, C_DV)
    o = O[0] + O[1]
    ms = _mm_xr(o * o, ones4, 3) * (1.0 / C_DV)
    og = og_ref[...]
    o_ref[...] = o * lax.rsqrt(ms + LN_EPS) * ng_ref[...] * (og * _sigmoid(og))


def _gla_call(T, n_seq, tok0, pm, plr, s0, gb_pad, bias, ng, name):
    blk0 = tok0 // T
    full = lambda shape: pl.BlockSpec(shape, lambda b: (0,) * len(shape))
    return pl.pallas_call(
        functools.partial(_gla_kernel, T),
        grid=(n_seq,),
        in_specs=[
            pl.BlockSpec((T, C_KW), lambda b: (blk0 + b, 12)),
            pl.BlockSpec((T, C_KW), lambda b: (blk0 + b, 13)),
            pl.BlockSpec((T, C_VW), lambda b: (blk0 + b, 7)),
            pl.BlockSpec((T, C_VW), lambda b: (blk0 + b, 8)),
            pl.BlockSpec((T, 128), lambda b: (blk0 + b, 3)),
            pl.BlockSpec((None, 2, C_HEADS, C_DK, C_DV), lambda b: (b, 0, 0, 0, 0)),
            full((2, 128, C_KW)), full((2, 1, C_KW)), full((1, C_VW)),
        ],
        out_specs=[
            pl.BlockSpec((T, C_VW), lambda b: (b, 0)),
            pl.BlockSpec((None, 2, C_HEADS, C_DK, C_DV), lambda b: (b, 0, 0, 0, 0)),
        ],
        out_shape=[
            jax.ShapeDtypeStruct((n_seq * T, C_VW), F32),
            jax.ShapeDtypeStruct((n_seq, 2, C_HEADS, C_DK, C_DV), F32),
        ],
        scratch_shapes=[
            pltpu.VMEM((2, T, C_KW), F32),
            pltpu.VMEM((2, T, C_VW), F32),
            pltpu.VMEM((2, C_KW, C_VW), F32),
        ],
        compiler_params=_cparams(("arbitrary",)),
        name=name,
    )(pm, pm, pm, pm, plr, s0, gb_pad, bias, ng)


def _merge_kernel(x_ref, g_ref, oa_ref, ob_ref, oc_ref, mod_ref, wa_ref, wb_ref, wc_ref, wo_ref,
                  lg_ref, lb_ref, o_ref, wa_s, wb_s, wc_s, wo_s):
    @pl.when(pl.program_id(0) == 0)
    def _():
        wa_s[...] = wa_ref[...].astype(BF16)
        wb_s[...] = wb_ref[...].astype(BF16)
        wc_s[...] = wc_ref[...].astype(BF16)
        wo_s[...] = wo_ref[...].astype(BF16)

    merged = (_sigmoid(g_ref[:, 0:D]) * _dg(oa_ref[...].astype(BF16), wa_s[...])
              + _sigmoid(g_ref[:, D:2 * D]) * _dg(ob_ref[...].astype(BF16), wb_s[...])
              + _sigmoid(g_ref[:, 2 * D:3 * D]) * _dg(oc_ref[...].astype(BF16), wc_s[...]))
    mix = _dg(merged.astype(BF16), wo_s[...])
    y = DN_ALPHA * x_ref[...] + mod_ref[2:3, :] * mix
    o_ref[...] = _layer_norm(y, lg_ref[...], lb_ref[...])


def _merge_call(x, gates, oa, ob, oc, mod_l, wa, wb, wc, wo, lg, lb, l):
    tm = 512
    wspec = lambda r: pl.BlockSpec((None, r, D), lambda i: (l, 0, 0))
    vspec = pl.BlockSpec((None, 1, D), lambda i: (l, 0, 0))
    return pl.pallas_call(
        _merge_kernel,
        grid=(N_TOK // tm,),
        in_specs=[
            pl.BlockSpec((tm, D), lambda i: (i, 0)),
            pl.BlockSpec((tm, IN_GATE), lambda i: (i, 0)),
            pl.BlockSpec((tm, 512), lambda i: (i, 0)),
            pl.BlockSpec((tm, B_W), lambda i: (i, 0)),
            pl.BlockSpec((tm, C_VW), lambda i: (i, 0)),
            pl.BlockSpec((None, 6, D), lambda i: (_group_of_tile(i, tm), 0, 0)),
            wspec(512), wspec(B_W), wspec(C_VW), wspec(D), vspec, vspec,
        ],
        out_specs=pl.BlockSpec((tm, D), lambda i: (i, 0)),
        out_shape=jax.ShapeDtypeStruct((N_TOK, D), F32),
        scratch_shapes=[pltpu.VMEM((512, D), BF16), pltpu.VMEM((B_W, D), BF16),
                        pltpu.VMEM((C_VW, D), BF16), pltpu.VMEM((D, D), BF16)],
        compiler_params=_cparams(("arbitrary",)),
        name="merge",
    )(x, gates, oa, ob, oc, mod_l, wa, wb, wc, wo, lg.reshape(DEPTH, 1, D), lb.reshape(DEPTH, 1, D))


def _ffn_kernel(x_ref, mod_ref, wg_ref, wu_ref, wd_ref, lg_ref, lb_ref, o_ref, h_s, acc_s):
    f = pl.program_id(1)

    @pl.when(f == 0)
    def _():
        h_s[...] = (x_ref[...] * (1.0 + mod_ref[4:5, :]) + mod_ref[3:4, :]).astype(BF16)
        acc_s[...] = jnp.zeros_like(acc_s)

    h = h_s[...]
    gate = _dg(h, wg_ref[...].astype(BF16))
    up = _dg(h, wu_ref[...].astype(BF16))
    acc_s[...] += _dg((gate * _sigmoid(gate) * up).astype(BF16), wd_ref[...].astype(BF16))

    @pl.when(f == pl.num_programs(1) - 1)
    def _():
        y = DN_ALPHA * x_ref[...] + mod_ref[5:6, :] * acc_s[...]
        o_ref[...] = _layer_norm(y, lg_ref[...], lb_ref[...])


def _ffn_call(x, mod_l, wg, wu, wd, lg, lb, l, i_ffn):
    tm, tf = 1024, 256
    vspec = pl.BlockSpec((None, 1, D), lambda i, f: (l, 0, 0))
    return pl.pallas_call(
        _ffn_kernel,
        grid=(N_TOK // tm, D_FF // tf),
        in_specs=[
            pl.BlockSpec((tm, D), lambda i, f: (i, 0)),
            pl.BlockSpec((None, 6, D), lambda i, f: (_group_of_tile(i, tm), 0, 0)),
            pl.BlockSpec((None, D, tf), lambda i, f: (i_ffn, 0, f)),
            pl.BlockSpec((None, D, tf), lambda i, f: (i_ffn, 0, f)),
            pl.BlockSpec((None, tf, D), lambda i, f: (i_ffn, f, 0)),
            vspec, vspec,
        ],
        out_specs=pl.BlockSpec((tm, D), lambda i, f: (i, 0)),
        out_shape=jax.ShapeDtypeStruct((N_TOK, D), F32),
        scratch_shapes=[pltpu.VMEM((tm, D), BF16), pltpu.VMEM((tm, D), F32)],
        compiler_params=_cparams(("arbitrary", "arbitrary")),
        name="ffn",
    )(x, mod_l, wg, wu, wd, lg.reshape(DEPTH, 1, D), lb.reshape(DEPTH, 1, D))


def _moe_kernel(x_ref, mod_ref, wr_ref, wg_ref, wu_ref, wd_ref, lg_ref, lb_ref, o_ref, h_s, acc_s, gates_s):
    e = pl.program_id(1)
    f = pl.program_id(2)

    @pl.when((e == 0) & (f == 0))
    def _():
        h = x_ref[...] * (1.0 + mod_ref[4:5, :]) + mod_ref[3:4, :]
        h_s[...] = h.astype(BF16)
        acc_s[...] = jnp.zeros_like(acc_s)
        logits = _mm3(h, wr_ref[...])
        lane = _iota(logits.shape, 1)
        logits = jnp.where(lane < N_EXP, logits, NEG_INF)
        v1 = jnp.max(logits, -1, keepdims=True)
        i1 = jnp.min(jnp.where(logits == v1, lane, 128), -1, keepdims=True)
        rest = jnp.where(lane == i1, NEG_INF, logits)
        v2 = jnp.max(rest, -1, keepdims=True)
        i2 = jnp.min(jnp.where(rest == v2, lane, 128), -1, keepdims=True)
        e2 = jnp.exp(v2 - v1)
        w1 = 1.0 / (1.0 + e2)
        w2 = e2 / (1.0 + e2)
        gates_s[...] = jnp.where(lane == i1, w1, 0.0) + jnp.where(lane == i2, w2, 0.0)

    h = h_s[...]
    gate = _dg(h, wg_ref[...].astype(BF16))
    up = _dg(h, wu_ref[...].astype(BF16))
    y = _dg((gate * _sigmoid(gate) * up).astype(BF16), wd_ref[...].astype(BF16))
    lane = _iota(gates_s.shape, 1)
    gcol = jnp.sum(jnp.where(lane == e, gates_s[...], 0.0), -1, keepdims=True)
    acc_s[...] += gcol * y

    @pl.when((e == pl.num_programs(1) - 1) & (f == pl.num_programs(2) - 1))
    def _():
        y2 = DN_ALPHA * x_ref[...] + mod_ref[5:6, :] * acc_s[...]
        o_ref[...] = _layer_norm(y2, lg_ref[...], lb_ref[...])


def _moe_call(x, mod_l, wr_pad, wg, wu, wd, lg, lb, l, i_moe):
    tm, tf = 1024, 512
    vspec = pl.BlockSpec((None, 1, D), lambda i, e, f: (l, 0, 0))
    return pl.pallas_call(
        _moe_kernel,
        grid=(N_TOK // tm, N_EXP, D_FFE // tf),
        in_specs=[
            pl.BlockSpec((tm, D), lambda i, e, f: (i, 0)),
            pl.BlockSpec((None, 6, D), lambda i, e, f: (_group_of_tile(i, tm), 0, 0)),
            pl.BlockSpec((None, D, 128), lambda i, e, f: (i_moe, 0, 0)),
            pl.BlockSpec((None, None, D, tf), lambda i, e, f: (i_moe, e, 0, f)),
            pl.BlockSpec((None, None, D, tf), lambda i, e, f: (i_moe, e, 0, f)),
            pl.BlockSpec((None, None, tf, D), lambda i, e, f: (i_moe, e, f, 0)),
            vspec, vspec,
        ],
        out_specs=pl.BlockSpec((tm, D), lambda i, e, f: (i, 0)),
        out_shape=jax.ShapeDtypeStruct((N_TOK, D), F32),
        scratch_shapes=[pltpu.VMEM((tm, D), BF16), pltpu.VMEM((tm, D), F32), pltpu.VMEM((tm, 128), F32)],
        compiler_params=_cparams(("arbitrary", "arbitrary", "arbitrary")),
        name="moe",
    )(x, mod_l, wr_pad, wg, wu, wd, lg.reshape(DEPTH, 1, D), lb.reshape(DEPTH, 1, D))


def kernel(x_prompt, x_sample, cache_attn_k, cache_attn_v, state_rwkv, state_gla, c, c_ctx, w_ada, b_ada, w_in,
           attn_sink, rwkv_w0, rwkv_w_a, rwkv_w_b, rwkv_a0, rwkv_a_a, rwkv_a_b, rwkv_g_a, rwkv_g_b, rwkv_k_k,
           rwkv_k_a, rwkv_r_k, rwkv_ln_g, rwkv_ln_b, gla_gate_a, gla_gate_b, gla_gate_bias, gla_norm_g, w_up_a,
           w_up_b, w_up_c, w_out, ln1_g, ln1_b, ln2_g, ln2_b, ffn_w_gate, ffn_w_up, ffn_w_down, moe_router,
           moe_w_gate, moe_w_up, moe_w_down):
    cvec = jnp.concatenate([c_ctx[None, :], c, jnp.zeros((N_GROUPS - 1 - N_DEC_B, D), F32)], axis=0)
    mods = _ada_call(cvec, w_ada, b_ada).reshape(DEPTH, N_GROUPS, 6, D)
    x = jnp.concatenate([x_prompt.reshape(N_CTX_TOK, D), x_sample.reshape(N_DEC_B * DEC_T, D)], axis=0)
    cos, sin = _rope_tables()
    kc_all = cache_attn_k.reshape(N_DEC_B, DEPTH, PAST, A_KV * A_DH)
    vc_all = cache_attn_v.reshape(N_DEC_B, DEPTH, PAST, A_KV * A_DH)
    zeros_r = jnp.zeros((N_CTX_B, 2, B_HEADS, B_DH, B_DH), F32)
    zeros_g = jnp.zeros((N_CTX_B, 2, C_HEADS, C_DK, C_DV), F32)

    new_k, new_v, new_sr, new_sg = [], [], [], []
    for l in range(DEPTH):
        mod_l = mods[l]
        w_lr = jnp.concatenate(
            [rwkv_w_a[l, 0], rwkv_w_a[l, 1], rwkv_a_a[l, 0], rwkv_a_a[l, 1], rwkv_g_a[l],
             gla_gate_a[l, 0], gla_gate_a[l, 1], jnp.zeros((D, LR_W - 416), F32)], axis=1)
        pm = _inproj_call(x, mod_l, w_in, lambda tn: pl.BlockSpec((None, D, tn), lambda j, i: (l, 0, j)),
                          IN_MAIN, 768, "inproj_main")
        gates = _inproj_call(x, mod_l, w_in,
                             lambda tn: pl.BlockSpec((None, D, tn), lambda j, i: (l, 0, j + IN_MAIN // tn)),
                             IN_GATE, 768, "inproj_gate")
        plr = _inproj_call(x, mod_l, w_lr, lambda tn: pl.BlockSpec((D, tn), lambda j, i: (0, j)),
                           LR_W, LR_W, "inproj_lr")

        sink_l = attn_sink[l]
        oa = jnp.concatenate([_attn_ctx_call(sink_l, pm),
                              _attn_lat_call(sink_l, pm, kc_all, vc_all, l, cos, sin)], axis=0)

        rprm = (rwkv_w0[l].reshape(2, 1, B_W), rwkv_w_b[l], rwkv_a0[l].reshape(2, 1, B_W), rwkv_a_b[l],
                rwkv_g_b[l], rwkv_k_k[l].reshape(1, B_W), rwkv_k_a[l].reshape(1, B_W),
                rwkv_r_k[l].reshape(1, B_W), rwkv_ln_g[l].reshape(1, B_W), rwkv_ln_b[l].reshape(1, B_W))
        ob_c, sr_c = _rwkv_call(CTX_T, 2, N_CTX_B, 0, pm, plr, zeros_r, rprm, "rwkv_ctx")
        ob_d, _ = _rwkv_call(DEC_T, 1, N_DEC_B, N_CTX_TOK, pm, plr, state_rwkv[:, l], rprm, "rwkv_lat")
        ob = jnp.concatenate([ob_c, ob_d], axis=0)

        gb_pad = jnp.zeros((2, 128, C_KW), F32)
        gb_pad = gb_pad.at[0, 0:C_GATE_RANK].set(gla_gate_b[l, 0]).at[1, C_GATE_RANK:2 * C_GATE_RANK].set(
            gla_gate_b[l, 1])
        gbias = gla_gate_bias[l].reshape(2, 1, C_KW)
        ng = jnp.tile(gla_norm_g[l], C_HEADS).reshape(1, C_VW)
        oc_c, sg_c = _gla_call(CTX_T, N_CTX_B, 0, pm, plr, zeros_g, gb_pad, gbias, ng, "gla_ctx")
        oc_d, _ = _gla_call(DEC_T, N_DEC_B, N_CTX_TOK, pm, plr, state_gla[:, l], gb_pad, gbias, ng, "gla_lat")
        oc = jnp.concatenate([oc_c, oc_d], axis=0)

        x = _merge_call(x, gates, oa, ob, oc, mod_l, w_up_a, w_up_b, w_up_c, w_out, ln1_g, ln1_b, l)
        if l % 2 == 0:
            x = _ffn_call(x, mod_l, ffn_w_gate, ffn_w_up, ffn_w_down, ln2_g, ln2_b, l, l // 2)
        else:
            wr_pad = jnp.concatenate([moe_router, jnp.zeros((moe_router.shape[0], D, 128 - N_EXP), F32)], axis=2)
            x = _moe_call(x, mod_l, wr_pad, moe_w_gate, moe_w_up, moe_w_down, ln2_g, ln2_b, l, l // 2)

        new_k.append(pm[:N_CTX_TOK, 512:640].reshape(N_CTX_B, CTX_T, A_KV, A_DH))
        new_v.append(pm[:N_CTX_TOK, 640:768].reshape(N_CTX_B, CTX_T, A_KV, A_DH))
        new_sr.append(sr_c)
        new_sg.append(sg_c)

    y_prompt = x[:N_CTX_TOK].reshape(N_CTX_B, CTX_T, D)
    y_sample = x[N_CTX_TOK:].reshape(N_DEC_B, DEC_T, D)
    return (y_prompt, y_sample, jnp.stack(new_k, axis=1), jnp.stack(new_v, axis=1),
            jnp.stack(new_sr, axis=1), jnp.stack(new_sg, axis=1))
```

```python
import functools

import numpy as np
import jax
import jax.numpy as jnp
from jax import lax
from jax.experimental import pallas as pl
from jax.experimental.pallas import tpu as pltpu

D = 1024
N_CTX_B, CTX_T = 32, 256
N_DEC_B, DEC_T = 2, 1024
N_CTX_TOK = N_CTX_B * CTX_T
N_TOK = N_CTX_TOK + N_DEC_B * DEC_T
DEPTH = 2
PAST = 256
GRID_W = 64
A_HEADS, A_KV, A_DH = 8, 2, 64
A_G = A_HEADS // A_KV
A_WIN, A_BLK = 128, 128
ROPE_BASE = 10000.0
B_HEADS, B_DH = 4, 64
B_W = B_HEADS * B_DH
B_GN_EPS = 64e-5
C_HEADS, C_DK, C_DV = 4, 32, 64
C_KW, C_VW = C_HEADS * C_DK, C_HEADS * C_DV
C_GATE_RANK = 16
C_GATE_NORM = 16.0
C_CHUNK = 64
C_SUB = 16
D_FF = 2816
N_EXP = 8
D_FFE = 3584
LN_EPS = 1e-5
DN_ALPHA = (2.0 * DEPTH) ** 0.25
NEG_INF = -1e30
IN_MAIN = 2304
IN_GATE = 3072
LR_W = 512
N_GROUPS = 8

F32 = jnp.float32
BF16 = jnp.bfloat16
VMEM_LIMIT = 56 * 1024 * 1024

NN = ((1,), (0,))
NT = ((1,), (1,))
TN = ((0,), (0,))


def _dg(a, b, dims=NN):
    return lax.dot_general(a, b, (dims, ((), ())), preferred_element_type=F32)


def _split2(x):
    hi = x.astype(BF16)
    lo = (x - hi.astype(F32)).astype(BF16)
    return hi, lo


def _split3(x):
    hi = x.astype(BF16)
    r = x - hi.astype(F32)
    mid = r.astype(BF16)
    lo = (r - mid.astype(F32)).astype(BF16)
    return hi, mid, lo


def _mm(a, b, dims=NN):
    return _dg(a.astype(BF16), b.astype(BF16), dims)


def _mm3(a, b, dims=NN):
    ah, al = _split2(a)
    bh, bl = _split2(b)
    return _dg(ah, bh, dims) + (_dg(ah, bl, dims) + _dg(al, bh, dims))


def _mm_xr(a, b_exact, passes, dims=NN):
    parts = (a.astype(BF16),) if passes == 1 else (_split2(a) if passes == 2 else _split3(a))
    out = _dg(parts[0], b_exact, dims)
    for p in parts[1:]:
        out = out + _dg(p, b_exact, dims)
    return out


def _mm_xl(a_exact, b, passes):
    parts = (b.astype(BF16),) if passes == 1 else (_split2(b) if passes == 2 else _split3(b))
    out = _dg(a_exact, parts[0])
    for p in parts[1:]:
        out = out + _dg(a_exact, p)
    return out


def _sigmoid(x):
    return 1.0 / (1.0 + jnp.exp(-x))


def _iota(shape, dim):
    return lax.broadcasted_iota(jnp.int32, shape, dim)


def _block_ones(n, blk):
    return (_iota((n, n), 0) // blk == _iota((n, n), 1) // blk).astype(BF16)


def _layer_norm(y, g, b):
    mu = jnp.mean(y, -1, keepdims=True)
    yc = y - mu
    var = jnp.mean(yc * yc, -1, keepdims=True)
    return yc * lax.rsqrt(var + LN_EPS) * g + b


def _group_of_tile(i, tm):
    n_ctx = N_CTX_TOK // tm
    per_dec = DEC_T // tm
    return jnp.where(i < n_ctx, 0, 1 + (i - n_ctx) // per_dec)


def _cparams(sem):
    return pltpu.CompilerParams(dimension_semantics=sem, vmem_limit_bytes=VMEM_LIMIT)


def _ada_kernel(c_ref, w_ref, b_ref, o_ref):
    c = c_ref[...]
    s = c * _sigmoid(c)
    o_ref[...] = _mm3(s, w_ref[...]) + b_ref[...]


def _ada_call(cvec, w_ada, b_ada):
    tn = 1536
    return pl.pallas_call(
        _ada_kernel,
        grid=(DEPTH, 6 * D // tn),
        in_specs=[
            pl.BlockSpec((N_GROUPS, D), lambda l, j: (0, 0)),
            pl.BlockSpec((None, D, tn), lambda l, j: (l, 0, j)),
            pl.BlockSpec((None, 1, tn), lambda l, j: (l, 0, j)),
        ],
        out_specs=pl.BlockSpec((None, N_GROUPS, tn), lambda l, j: (l, 0, j)),
        out_shape=jax.ShapeDtypeStruct((DEPTH, N_GROUPS, 6 * D), F32),
        compiler_params=_cparams(("arbitrary", "arbitrary")),
        name="ada",
    )(cvec, w_ada, b_ada.reshape(DEPTH, 1, 6 * D))


def _inproj_kernel(x_ref, mod_ref, w_ref, o_ref):
    sh = mod_ref[0:1, :]
    sc = mod_ref[1:2, :]
    h = x_ref[...] * (1.0 + sc) + sh
    o_ref[...] = _mm(h, w_ref[...])


def _inproj_call(x, mod_l, w, w_spec_fn, n_out, tn, name):
    tm = 512
    return pl.pallas_call(
        _inproj_kernel,
        grid=(n_out // tn, N_TOK // tm),
        in_specs=[
            pl.BlockSpec((tm, D), lambda j, i: (i, 0)),
            pl.BlockSpec((None, 6, D), lambda j, i: (_group_of_tile(i, tm), 0, 0)),
            w_spec_fn(tn),
        ],
        out_specs=pl.BlockSpec((tm, tn), lambda j, i: (i, j)),
        out_shape=jax.ShapeDtypeStruct((N_TOK, n_out), F32),
        compiler_params=_cparams(("arbitrary", "arbitrary")),
        name=name,
    )(x, mod_l, w)


def _sink_col(sink_ref, kvh, rows_per_head):
    n = A_G * rows_per_head
    r = _iota((n, 1), 0) // rows_per_head
    col = jnp.full((n, 1), sink_ref[kvh * A_G], F32)
    for g in range(1, A_G):
        col = jnp.where(r == g, sink_ref[kvh * A_G + g], col)
    return col


def _attn_ctx_kernel(sink_ref, q_ref, k_ref, v_ref, o_ref):
    scale = A_DH ** -0.5
    for kvh in range(A_KV):
        ks = k_ref[:, kvh * A_DH:(kvh + 1) * A_DH].astype(BF16)
        vs = v_ref[:, kvh * A_DH:(kvh + 1) * A_DH].astype(BF16)
        q4 = jnp.concatenate(
            [q_ref[:, (kvh * A_G + g) * A_DH:(kvh * A_G + g + 1) * A_DH] for g in range(A_G)], axis=0)
        s = _dg(q4.astype(BF16), ks, NT) * scale
        sink = _sink_col(sink_ref, kvh, CTX_T)
        m = jnp.maximum(jnp.max(s, -1, keepdims=True), sink)
        e = jnp.exp(s - m)
        p = e / (jnp.sum(e, -1, keepdims=True) + jnp.exp(sink - m))
        o = _dg(p.astype(BF16), vs)
        for g in range(A_G):
            h = kvh * A_G + g
            o_ref[:, h * A_DH:(h + 1) * A_DH] = o[g * CTX_T:(g + 1) * CTX_T, :]


def _attn_ctx_call(sink_l, pm):
    return pl.pallas_call(
        _attn_ctx_kernel,
        grid=(N_CTX_B,),
        in_specs=[
            pl.BlockSpec(memory_space=pltpu.SMEM),
            pl.BlockSpec((CTX_T, 512), lambda b: (b, 0)),
            pl.BlockSpec((CTX_T, 128), lambda b: (b, 4)),
            pl.BlockSpec((CTX_T, 128), lambda b: (b, 5)),
        ],
        out_specs=pl.BlockSpec((CTX_T, 512), lambda b: (b, 0)),
        out_shape=jax.ShapeDtypeStruct((N_CTX_TOK, 512), F32),
        compiler_params=_cparams(("arbitrary",)),
        name="attn_ctx",
    )(sink_l, pm, pm, pm)


def _rope(x, cos, sin_signed):
    w = x.shape[-1]
    lane = _iota(x.shape, 1)
    partner = jnp.where((lane % 32) < 16, pltpu.roll(x, w - 16, 1), pltpu.roll(x, 16, 1))
    return x * cos + partner * sin_signed


def _attn_lat_kernel(sink_ref, q_ref, k_ref, v_ref, kc_ref, vc_ref, cos_ref, sin_ref, o_ref, kr_ref):
    n = pl.program_id(1)
    scale = A_DH ** -0.5

    @pl.when(n == 0)
    def _():
        kr_ref[...] = _rope(k_ref[...], cos_ref[:, 0:128], sin_ref[:, 0:128]).astype(BF16)

    q0 = pl.multiple_of(n * A_BLK, A_BLK)
    qr = _rope(q_ref[...], cos_ref[pl.ds(q0, A_BLK), :], sin_ref[pl.ds(q0, A_BLK), :])
    kstart = pl.multiple_of(jnp.clip((n - 1) * A_BLK, 0, DEC_T - 3 * A_BLK), A_BLK)
    kwin = kr_ref[pl.ds(kstart, 3 * A_BLK), :]
    vwin = v_ref[pl.ds(kstart, 3 * A_BLK), :].astype(BF16)
    kc = kc_ref[...].astype(BF16)
    vc = vc_ref[...].astype(BF16)
    rows = A_G * A_BLK
    qpos = q0 + _iota((rows, 3 * A_BLK), 0) % A_BLK
    kpos = kstart + _iota((rows, 3 * A_BLK), 1)
    valid = jnp.abs(qpos - kpos) <= A_WIN
    for kvh in range(A_KV):
        cs = slice(kvh * A_DH, (kvh + 1) * A_DH)
        q4 = jnp.concatenate(
            [qr[:, (kvh * A_G + g) * A_DH:(kvh * A_G + g + 1) * A_DH] for g in range(A_G)], axis=0).astype(BF16)
        s_loc = jnp.where(valid, _dg(q4, kwin[:, cs], NT) * scale, NEG_INF)
        s_ctx = _dg(q4, kc[:, cs], NT) * scale
        sink = _sink_col(sink_ref, kvh, A_BLK)
        m = jnp.maximum(jnp.maximum(jnp.max(s_loc, -1, keepdims=True), jnp.max(s_ctx, -1, keepdims=True)), sink)
        e_loc = jnp.exp(s_loc - m)
        e_ctx = jnp.exp(s_ctx - m)
        inv = 1.0 / (jnp.sum(e_loc, -1, keepdims=True) + jnp.sum(e_ctx, -1, keepdims=True) + jnp.exp(sink - m))
        o = _dg((e_loc * inv).astype(BF16), vwin[:, cs]) + _dg((e_ctx * inv).astype(BF16), vc[:, cs])
        for g in range(A_G):
            h = kvh * A_G + g
            o_ref[:, h * A_DH:(h + 1) * A_DH] = o[g * A_BLK:(g + 1) * A_BLK, :]


def _rope_tables():
    half = A_DH // 2
    t = np.arange(DEC_T)
    rows = (t // GRID_W).astype(np.float32)
    cols = (t % GRID_W).astype(np.float32)
    inv_freq = (ROPE_BASE ** (-np.arange(0, half, 2, dtype=np.float32) / half)).astype(np.float32)
    ang_r = rows[:, None] * inv_freq[None, :]
    ang_c = cols[:, None] * inv_freq[None, :]
    cos = np.concatenate([np.cos(ang_r), np.cos(ang_r), np.cos(ang_c), np.cos(ang_c)], -1)
    sin = np.concatenate([-np.sin(ang_r), np.sin(ang_r), -np.sin(ang_c), np.sin(ang_c)], -1)
    return (jnp.asarray(np.tile(cos, (1, A_HEADS)), F32), jnp.asarray(np.tile(sin, (1, A_HEADS)), F32))


def _attn_lat_call(sink_l, pm, kc, vc, l, cos, sin):
    nb = DEC_T // A_BLK
    row0 = N_CTX_TOK // A_BLK
    seq0 = N_CTX_TOK // DEC_T
    return pl.pallas_call(
        _attn_lat_kernel,
        grid=(N_DEC_B, nb),
        in_specs=[
            pl.BlockSpec(memory_space=pltpu.SMEM),
            pl.BlockSpec((A_BLK, 512), lambda b, n: (row0 + b * nb + n, 0)),
            pl.BlockSpec((DEC_T, 128), lambda b, n: (seq0 + b, 4)),
            pl.BlockSpec((DEC_T, 128), lambda b, n: (seq0 + b, 5)),
            pl.BlockSpec((None, None, PAST, 128), lambda b, n: (b, l, 0, 0)),
            pl.BlockSpec((None, None, PAST, 128), lambda b, n: (b, l, 0, 0)),
            pl.BlockSpec((DEC_T, 512), lambda b, n: (0, 0)),
            pl.BlockSpec((DEC_T, 512), lambda b, n: (0, 0)),
        ],
        out_specs=pl.BlockSpec((A_BLK, 512), lambda b, n: (b * nb + n, 0)),
        out_shape=jax.ShapeDtypeStruct((N_DEC_B * DEC_T, 512), F32),
        scratch_shapes=[pltpu.VMEM((DEC_T, 128), BF16)],
        compiler_params=_cparams(("arbitrary", "arbitrary")),
        name="attn_lat",
    )(sink_l, pm, pm, pm, kc, vc, cos, sin)


def _rwkv_kernel(T, NS, r_ref, k_ref, v_ref, lr_ref, s0_ref, w0_ref, wb_ref, a0_ref, ab_ref, gb_ref,
                 kk_ref, ka_ref, rk_ref, lng_ref, lnb_ref, o_ref, sfin_ref,
                 KK, W, WRP, AKK, KT, VC2, BON, Y, S):
    ones4 = _block_ones(B_W, B_DH)
    decay_c = float(np.exp(-0.5))
    RC = 256
    SUB = 8
    NP = 5

    def prep(c, carry):
        r0 = pl.multiple_of(c * RC, RC)
        rs = pl.ds(r0, RC)
        r = r_ref[rs, :]
        k = k_ref[rs, :]
        v = v_ref[rs, :]
        lr = lr_ref[rs, :]
        kkr = k * kk_ref[...]
        kk = kkr * lax.rsqrt(_mm_xr(kkr * kkr, ones4, 3) + 1e-12)
        KK[rs, :] = kk
        bonus = jnp.zeros((RC, B_W), F32)
        vc2 = jnp.zeros((RC, B_W), F32)
        for d in range(2):
            z = w0_ref[d] + _mm(jnp.tanh(lr[:, 64 * d:64 * d + 64]), wb_ref[d])
            w = jnp.exp(-decay_c * _sigmoid(z))
            a = _sigmoid(a0_ref[d] + _mm(lr[:, 128 + 64 * d:192 + 64 * d], ab_ref[d]))
            kt = k * (1.0 + (a - 1.0) * ka_ref[...])
            akk = a * kk
            W[d, rs, :] = w
            WRP[d, rs, :] = w * r - _mm_xr(akk * r, ones4, 3) * kk
            AKK[d, rs, :] = akk
            KT[d, rs, :] = kt
            vc2 = vc2 + _mm_xr(kt * r, ones4, 3) * v
            bonus = bonus + _mm_xr(r * kt * rk_ref[...], ones4, 3) * v
        VC2[rs, :] = vc2
        BON[rs, :] = bonus
        return carry

    lax.fori_loop(0, NS * T // RC, prep, 0)

    chains = [(s, d) for s in range(NS) for d in range(2)]
    for s, d in chains:
        S[s, d] = jnp.concatenate([s0_ref[s, d, h] for h in range(B_HEADS)], axis=1)

    eye4 = _iota((B_DH, B_W), 0) == (_iota((B_DH, B_W), 1) % B_DH)

    def steps(i, carry):
        tiles = []
        for s, d in chains:
            t0 = pl.multiple_of(s * T + (i * SUB if d == 0 else T - SUB - i * SUB), SUB)
            rows = pl.ds(t0, SUB)
            v8 = v_ref[rows, :]
            vh8 = v8.astype(BF16).astype(F32)
            tiles.append((rows, KK[rows, :], WRP[d, rows, :], vh8, v8 - vh8,
                          W[d, rows, :], AKK[d, rows, :], KT[d, rows, :]))
        ys = [[None] * SUB for _ in chains]
        for jj in range(SUB):
            lhs = []
            for (s, d), (_, kk8, wrp8, vh8, vl8, _, _, _) in zip(chains, tiles):
                j = jj if d == 0 else SUB - 1 - jj
                r = slice(j, j + 1)
                st = S[s, d]
                x1 = st * kk8[r]
                x1h = x1.astype(BF16)
                lhs += [x1h, (x1 - x1h.astype(F32)).astype(BF16), (st * wrp8[r]).astype(BF16),
                        jnp.where(eye4, vh8[r], 0.0).astype(BF16), jnp.where(eye4, vl8[r], 0.0).astype(BF16)]
            res = _dg(jnp.concatenate(lhs, axis=0), ones4)
            for g, ((s, d), (_, _, _, _, _, w8, akk8, kt8)) in enumerate(zip(chains, tiles)):
                j = jj if d == 0 else SUB - 1 - jj
                r = slice(j, j + 1)
                p = [res[(g * NP + n) * B_DH:(g * NP + n + 1) * B_DH] for n in range(NP)]
                sk = p[0] + p[1]
                vcol = p[3] + p[4]
                S[s, d] = S[s, d] * w8[r] - sk * akk8[r] + vcol * kt8[r]
                ys[g][j] = jnp.sum(jnp.where(eye4, p[2], 0.0), axis=0, keepdims=True)
        for g, ((s, d), tl) in enumerate(zip(chains, tiles)):
            Y[d, tl[0], :] = jnp.concatenate(ys[g], axis=0)
        return carry

    lax.fori_loop(0, T // SUB, steps, 0)

    for s, d in chains:
        st = S[s, d]
        for h in range(B_HEADS):
            sfin_ref[s, d, h] = st[:, h * B_DH:(h + 1) * B_DH]

    def post(c, carry):
        r0 = pl.multiple_of(c * RC, RC)
        rs = pl.ds(r0, RC)
        y = Y[0, rs, :] + Y[1, rs, :] + VC2[rs, :]
        mu = _mm_xr(y, ones4, 3) * (1.0 / B_DH)
        yc = y - mu
        var = _mm_xr(yc * yc, ones4, 3) * (1.0 / B_DH)
        yn = yc * lax.rsqrt(var + B_GN_EPS) * lng_ref[...] + lnb_ref[...] + BON[rs, :]
        g = _mm(_sigmoid(lr_ref[rs, 256:384]), gb_ref[...])
        o_ref[rs, :] = yn * g
        return carry

    lax.fori_loop(0, NS * T // RC, post, 0)


def _rwkv_call(T, NS, n_seq, tok0, pm, plr, s0, prm, name):
    rows = NS * T
    blk0 = tok0 // rows
    n_steps = n_seq // NS
    full = lambda shape: pl.BlockSpec(shape, lambda b: (0,) * len(shape))
    big = lambda shape, imap: (pl.BlockSpec(shape, imap, pipeline_mode=pl.Buffered(1)) if n_steps == 1
                               else pl.BlockSpec(shape, imap))
    kern = functools.partial(_rwkv_kernel, T, NS)
    return pl.pallas_call(
        kern,
        grid=(n_steps,),
        in_specs=[
            big((rows, B_W), lambda b: (blk0 + b, 3)),
            big((rows, B_W), lambda b: (blk0 + b, 4)),
            big((rows, B_W), lambda b: (blk0 + b, 5)),
            big((rows, LR_W), lambda b: (blk0 + b, 0)),
            pl.BlockSpec((NS, 2, B_HEADS, B_DH, B_DH), lambda b: (b, 0, 0, 0, 0)),
            full((2, 1, B_W)), full((2, 64, B_W)), full((2, 1, B_W)), full((2, 64, B_W)), full((128, B_W)),
            full((1, B_W)), full((1, B_W)), full((1, B_W)), full((1, B_W)), full((1, B_W)),
        ],
        out_specs=[
            big((rows, B_W), lambda b: (b, 0)),
            pl.BlockSpec((NS, 2, B_HEADS, B_DH, B_DH), lambda b: (b, 0, 0, 0, 0)),
        ],
        out_shape=[
            jax.ShapeDtypeStruct((n_seq * T, B_W), F32),
            jax.ShapeDtypeStruct((n_seq, 2, B_HEADS, B_DH, B_DH), F32),
        ],
        scratch_shapes=[
            pltpu.VMEM((rows, B_W), F32),
            pltpu.VMEM((2, rows, B_W), F32),
            pltpu.VMEM((2, rows, B_W), F32),
            pltpu.VMEM((2, rows, B_W), F32),
            pltpu.VMEM((2, rows, B_W), F32),
            pltpu.VMEM((rows, B_W), F32),
            pltpu.VMEM((rows, B_W), F32),
            pltpu.VMEM((2, rows, B_W), F32),
            pltpu.VMEM((NS, 2, B_DH, B_W), F32),
        ],
        compiler_params=_cparams(("arbitrary",)),
        name=name,
    )(pm, pm, pm, plr, s0, *prm)


def _gla_kernel(T, q_ref, k_ref, v_ref, og_ref, lr_ref, s0_ref, gb_ref, bias_ref, ng_ref,
                o_ref, sfin_ref, LA, O, S):
    n_chunks = T // C_CHUNK
    nsub = C_CHUNK // C_SUB
    qscale = C_DK ** -0.5
    lr = lr_ref[...]
    for d in range(2):
        gl = _mm(lr, gb_ref[d]) + bias_ref[d]
        LA[d] = (jnp.minimum(gl, 0.0) - jnp.log(1.0 + jnp.exp(-jnp.abs(gl)))) * (1.0 / C_GATE_NORM)
    bd_state = _iota((C_KW, C_VW), 0) // C_DK == _iota((C_KW, C_VW), 1) // C_DV
    for d in range(2):
        for h in range(C_HEADS):
            pad_l = h * C_DV
            pad_r = C_VW - (h + 1) * C_DV
            blk = s0_ref[d, h]
            parts = ([jnp.zeros((C_DK, pad_l), F32)] if pad_l else []) + [blk] + \
                    ([jnp.zeros((C_DK, pad_r), F32)] if pad_r else [])
            S[d, h * C_DK:(h + 1) * C_DK, :] = jnp.concatenate(parts, axis=1)

    ti = _iota((C_CHUNK, C_CHUNK), 0)
    si = _iota((C_CHUNK, C_CHUNK), 1)
    tri = ((si <= ti).astype(BF16), (si >= ti).astype(BF16))
    trow = _iota((C_CHUNK, 1), 0)
    mask_k = _iota((C_CHUNK, C_KW), 0) // C_SUB == _iota((C_CHUNK, C_KW), 1) // C_DK
    mask_v = _iota((C_CHUNK, C_VW), 0) // C_SUB == _iota((C_CHUNK, C_VW), 1) // C_DV
    t_att = _iota((C_CHUNK, C_CHUNK), 0)
    s_att = _iota((C_CHUNK, C_CHUNK), 1) % C_SUB
    eye_k = _iota((C_KW, C_KW), 0) == _iota((C_KW, C_KW), 1)

    def chunk(d, r0):
        rs = pl.ds(r0, C_CHUNK)
        b = _mm_xl(tri[d], LA[d, rs, :], 3)
        q = q_ref[rs, :] * qscale
        k = k_ref[rs, :]
        v = v_ref[rs, :]
        st = S[d]
        o = _mm3(q * jnp.exp(b), st)
        for j in range(nsub):
            lo, hi = j * C_SUB, (j + 1) * C_SUB
            if d == 0:
                gamma = b[hi - 1:hi, :]
                row_ok = trow >= lo
                att_ok = t_att >= lo + s_att
            else:
                gamma = b[lo:lo + 1, :]
                row_ok = trow < hi
                att_ok = t_att <= lo + s_att
            qj = q * jnp.exp(jnp.where(row_ok, b - gamma, NEG_INF))
            kj = k[lo:hi, :] * jnp.exp(gamma - b[lo:hi, :])
            kbd = jnp.where(mask_k, jnp.concatenate([kj] * C_HEADS, axis=0), 0.0)
            att = jnp.where(att_ok, _mm(qj, kbd, NT), 0.0)
            vbd = jnp.where(mask_v, jnp.concatenate([v[lo:hi, :]] * C_HEADS, axis=0), 0.0)
            o = o + _mm(att, vbd)
        O[d, rs, :] = o
        blast = b[C_CHUNK - 1:C_CHUNK, :] if d == 0 else b[0:1, :]
        kl = k * jnp.exp(blast - b)
        upd = jnp.where(bd_state, _mm3(kl.T, v), 0.0)
        dec = jnp.where(eye_k, jnp.exp(blast), 0.0)
        S[d] = _mm3(dec, st) + upd

    def body(c, carry):
        chunk(0, pl.multiple_of(c * C_CHUNK, C_CHUNK))
        chunk(1, pl.multiple_of((n_chunks - 1 - c) * C_CHUNK, C_CHUNK))
        return carry

    lax.fori_loop(0, n_chunks, body, 0)

    for d in range(2):
        st = S[d]
        for h in range(C_HEADS):
            sfin_ref[d, h] = st[h * C_DK:(h + 1) * C_DK, h * C_DV:(h + 1) * C_DV]

    ones4 = _block_ones(C_VW, C_DV)
    o = O[0] + O[1]
    ms = _mm_xr(o * o, ones4, 3) * (1.0 / C_DV)
    og = og_ref[...]
    o_ref[...] = o * lax.rsqrt(ms + LN_EPS) * ng_ref[...] * (og * _sigmoid(og))


def _gla_call(T, n_seq, tok0, pm, plr, s0, gb_pad, bias, ng, name):
    blk0 = tok0 // T
    full = lambda shape: pl.BlockSpec(shape, lambda b: (0,) * len(shape))
    return pl.pallas_call(
        functools.partial(_gla_kernel, T),
        grid=(n_seq,),
        in_specs=[
            pl.BlockSpec((T, C_KW), lambda b: (blk0 + b, 12)),
            pl.BlockSpec((T, C_KW), lambda b: (blk0 + b, 13)),
            pl.BlockSpec((T, C_VW), lambda b: (blk0 + b, 7)),
            pl.BlockSpec((T, C_VW), lambda b: (blk0 + b, 8)),
            pl.BlockSpec((T, 128), lambda b: (blk0 + b, 3)),
            pl.BlockSpec((None, 2, C_HEADS, C_DK, C_DV), lambda b: (b, 0, 0, 0, 0)),
            full((2, 128, C_KW)), full((2, 1, C_KW)), full((1, C_VW)),
        ],
        out_specs=[
            pl.BlockSpec((T, C_VW), lambda b: (b, 0)),
            pl.BlockSpec((None, 2, C_HEADS, C_DK, C_DV), lambda b: (b, 0, 0, 0, 0)),
        ],
        out_shape=[
            jax.ShapeDtypeStruct((n_seq * T, C_VW), F32),
            jax.ShapeDtypeStruct((n_seq, 2, C_HEADS, C_DK, C_DV), F32),
        ],
        scratch_shapes=[
            pltpu.VMEM((2, T, C_KW), F32),
            pltpu.VMEM((2, T, C_VW), F32),
            pltpu.VMEM((2, C_KW, C_VW), F32),
        ],
        compiler_params=_cparams(("arbitrary",)),
        name=name,
    )(pm, pm, pm, pm, plr, s0, gb_pad, bias, ng)


def _merge_kernel(x_ref, g_ref, oa_ref, ob_ref, oc_ref, mod_ref, wa_ref, wb_ref, wc_ref, wo_ref,
                  lg_ref, lb_ref, o_ref, wa_s, wb_s, wc_s, wo_s):
    @pl.when(pl.program_id(0) == 0)
    def _():
        wa_s[...] = wa_ref[...].astype(BF16)
        wb_s[...] = wb_ref[...].astype(BF16)
        wc_s[...] = wc_ref[...].astype(BF16)
        wo_s[...] = wo_ref[...].astype(BF16)

    merged = (_sigmoid(g_ref[:, 0:D]) * _dg(oa_ref[...].astype(BF16), wa_s[...])
              + _sigmoid(g_ref[:, D:2 * D]) * _dg(ob_ref[...].astype(BF16), wb_s[...])
              + _sigmoid(g_ref[:, 2 * D:3 * D]) * _dg(oc_ref[...].astype(BF16), wc_s[...]))
    mix = _dg(merged.astype(BF16), wo_s[...])
    y = DN_ALPHA * x_ref[...] + mod_ref[2:3, :] * mix
    o_ref[...] = _layer_norm(y, lg_ref[...], lb_ref[...])


def _merge_call(x, gates, oa, ob, oc, mod_l, wa, wb, wc, wo, lg, lb, l):
    tm = 512
    wspec = lambda r: pl.BlockSpec((None, r, D), lambda i: (l, 0, 0))
    vspec = pl.BlockSpec((None, 1, D), lambda i: (l, 0, 0))
    return pl.pallas_call(
        _merge_kernel,
        grid=(N_TOK // tm,),
        in_specs=[
            pl.BlockSpec((tm, D), lambda i: (i, 0)),
            pl.BlockSpec((tm, IN_GATE), lambda i: (i, 0)),
            pl.BlockSpec((tm, 512), lambda i: (i, 0)),
            pl.BlockSpec((tm, B_W), lambda i: (i, 0)),
            pl.BlockSpec((tm, C_VW), lambda i: (i, 0)),
            pl.BlockSpec((None, 6, D), lambda i: (_group_of_tile(i, tm), 0, 0)),
            wspec(512), wspec(B_W), wspec(C_VW), wspec(D), vspec, vspec,
        ],
        out_specs=pl.BlockSpec((tm, D), lambda i: (i, 0)),
        out_shape=jax.ShapeDtypeStruct((N_TOK, D), F32),
        scratch_shapes=[pltpu.VMEM((512, D), BF16), pltpu.VMEM((B_W, D), BF16),
                        pltpu.VMEM((C_VW, D), BF16), pltpu.VMEM((D, D), BF16)],
        compiler_params=_cparams(("arbitrary",)),
        name="merge",
    )(x, gates, oa, ob, oc, mod_l, wa, wb, wc, wo, lg.reshape(DEPTH, 1, D), lb.reshape(DEPTH, 1, D))


def _ffn_kernel(x_ref, mod_ref, wg_ref, wu_ref, wd_ref, lg_ref, lb_ref, o_ref, h_s, acc_s):
    f = pl.program_id(1)

    @pl.when(f == 0)
    def _():
        h_s[...] = (x_ref[...] * (1.0 + mod_ref[4:5, :]) + mod_ref[3:4, :]).astype(BF16)
        acc_s[...] = jnp.zeros_like(acc_s)

    h = h_s[...]
    gate = _dg(h, wg_ref[...].astype(BF16))
    up = _dg(h, wu_ref[...].astype(BF16))
    acc_s[...] += _dg((gate * _sigmoid(gate) * up).astype(BF16), wd_ref[...].astype(BF16))

    @pl.when(f == pl.num_programs(1) - 1)
    def _():
        y = DN_ALPHA * x_ref[...] + mod_ref[5:6, :] * acc_s[...]
        o_ref[...] = _layer_norm(y, lg_ref[...], lb_ref[...])


def _ffn_call(x, mod_l, wg, wu, wd, lg, lb, l, i_ffn):
    tm, tf = 1024, 256
    vspec = pl.BlockSpec((None, 1, D), lambda i, f: (l, 0, 0))
    return pl.pallas_call(
        _ffn_kernel,
        grid=(N_TOK // tm, D_FF // tf),
        in_specs=[
            pl.BlockSpec((tm, D), lambda i, f: (i, 0)),
            pl.BlockSpec((None, 6, D), lambda i, f: (_group_of_tile(i, tm), 0, 0)),
            pl.BlockSpec((None, D, tf), lambda i, f: (i_ffn, 0, f)),
            pl.BlockSpec((None, D, tf), lambda i, f: (i_ffn, 0, f)),
            pl.BlockSpec((None, tf, D), lambda i, f: (i_ffn, f, 0)),
            vspec, vspec,
        ],
        out_specs=pl.BlockSpec((tm, D), lambda i, f: (i, 0)),
        out_shape=jax.ShapeDtypeStruct((N_TOK, D), F32),
        scratch_shapes=[pltpu.VMEM((tm, D), BF16), pltpu.VMEM((tm, D), F32)],
        compiler_params=_cparams(("arbitrary", "arbitrary")),
        name="ffn",
    )(x, mod_l, wg, wu, wd, lg.reshape(DEPTH, 1, D), lb.reshape(DEPTH, 1, D))


def _moe_kernel(x_ref, mod_ref, wr_ref, wg_ref, wu_ref, wd_ref, lg_ref, lb_ref, o_ref, h_s, acc_s, gates_s):
    e = pl.program_id(1)
    f = pl.program_id(2)

    @pl.when((e == 0) & (f == 0))
    def _():
        h = x_ref[...] * (1.0 + mod_ref[4:5, :]) + mod_ref[3:4, :]
        h_s[...] = h.astype(BF16)
        acc_s[...] = jnp.zeros_like(acc_s)
        logits = _mm3(h, wr_ref[...])
        lane = _iota(logits.shape, 1)
        logits = jnp.where(lane < N_EXP, logits, NEG_INF)
        v1 = jnp.max(logits, -1, keepdims=True)
        i1 = jnp.min(jnp.where(logits == v1, lane, 128), -1, keepdims=True)
        rest = jnp.where(lane == i1, NEG_INF, logits)
        v2 = jnp.max(rest, -1, keepdims=True)
        i2 = jnp.min(jnp.where(rest == v2, lane, 128), -1, keepdims=True)
        e2 = jnp.exp(v2 - v1)
        w1 = 1.0 / (1.0 + e2)
        w2 = e2 / (1.0 + e2)
        gates_s[...] = jnp.where(lane == i1, w1, 0.0) + jnp.where(lane == i2, w2, 0.0)

    h = h_s[...]
    gate = _dg(h, wg_ref[...].astype(BF16))
    up = _dg(h, wu_ref[...].astype(BF16))
    y = _dg((gate * _sigmoid(gate) * up).astype(BF16), wd_ref[...].astype(BF16))
    lane = _iota(gates_s.shape, 1)
    gcol = jnp.sum(jnp.where(lane == e, gates_s[...], 0.0), -1, keepdims=True)
    acc_s[...] += gcol * y

    @pl.when((e == pl.num_programs(1) - 1) & (f == pl.num_programs(2) - 1))
    def _():
        y2 = DN_ALPHA * x_ref[...] + mod_ref[5:6, :] * acc_s[...]
        o_ref[...] = _layer_norm(y2, lg_ref[...], lb_ref[...])


def _moe_call(x, mod_l, wr_pad, wg, wu, wd, lg, lb, l, i_moe):
    tm, tf = 1024, 512
    vspec = pl.BlockSpec((None, 1, D), lambda i, e, f: (l, 0, 0))
    return pl.pallas_call(
        _moe_kernel,
        grid=(N_TOK // tm, N_EXP, D_FFE // tf),
        in_specs=[
            pl.BlockSpec((tm, D), lambda i, e, f: (i, 0)),
            pl.BlockSpec((None, 6, D), lambda i, e, f: (_group_of_tile(i, tm), 0, 0)),
            pl.BlockSpec((None, D, 128), lambda i, e, f: (i_moe, 0, 0)),
            pl.BlockSpec((None, None, D, tf), lambda i, e, f: (i_moe, e, 0, f)),
            pl.BlockSpec((None, None, D, tf), lambda i, e, f: (i_moe, e, 0, f)),
            pl.BlockSpec((None, None, tf, D), lambda i, e, f: (i_moe, e, f, 0)),
            vspec, vspec,
        ],
        out_specs=pl.BlockSpec((tm, D), lambda i, e, f: (i, 0)),
        out_shape=jax.ShapeDtypeStruct((N_TOK, D), F32),
        scratch_shapes=[pltpu.VMEM((tm, D), BF16), pltpu.VMEM((tm, D), F32), pltpu.VMEM((tm, 128), F32)],
        compiler_params=_cparams(("arbitrary", "arbitrary", "arbitrary")),
        name="moe",
    )(x, mod_l, wr_pad, wg, wu, wd, lg.reshape(DEPTH, 1, D), lb.reshape(DEPTH, 1, D))


def kernel(x_prompt, x_sample, cache_attn_k, cache_attn_v, state_rwkv, state_gla, c, c_ctx, w_ada, b_ada, w_in,
           attn_sink, rwkv_w0, rwkv_w_a, rwkv_w_b, rwkv_a0, rwkv_a_a, rwkv_a_b, rwkv_g_a, rwkv_g_b, rwkv_k_k,
           rwkv_k_a, rwkv_r_k, rwkv_ln_g, rwkv_ln_b, gla_gate_a, gla_gate_b, gla_gate_bias, gla_norm_g, w_up_a,
           w_up_b, w_up_c, w_out, ln1_g, ln1_b, ln2_g, ln2_b, ffn_w_gate, ffn_w_up, ffn_w_down, moe_router,
           moe_w_gate, moe_w_up, moe_w_down):
    cvec = jnp.concatenate([c_ctx[None, :], c, jnp.zeros((N_GROUPS - 1 - N_DEC_B, D), F32)], axis=0)
    mods = _ada_call(cvec, w_ada, b_ada).reshape(DEPTH, N_GROUPS, 6, D)
    x = jnp.concatenate([x_prompt.reshape(N_CTX_TOK, D), x_sample.reshape(N_DEC_B * DEC_T, D)], axis=0)
    cos, sin = _rope_tables()
    kc_all = cache_attn_k.reshape(N_DEC_B, DEPTH, PAST, A_KV * A_DH)
    vc_all = cache_attn_v.reshape(N_DEC_B, DEPTH, PAST, A_KV * A_DH)
    zeros_r = jnp.zeros((N_CTX_B, 2, B_HEADS, B_DH, B_DH), F32)
    zeros_g = jnp.zeros((N_CTX_B, 2, C_HEADS, C_DK, C_DV), F32)

    new_k, new_v, new_sr, new_sg = [], [], [], []
    for l in range(DEPTH):
        mod_l = mods[l]
        w_lr = jnp.concatenate(
            [rwkv_w_a[l, 0], rwkv_w_a[l, 1], rwkv_a_a[l, 0], rwkv_a_a[l, 1], rwkv_g_a[l],
             gla_gate_a[l, 0], gla_gate_a[l, 1], jnp.zeros((D, LR_W - 416), F32)], axis=1)
        pm = _inproj_call(x, mod_l, w_in, lambda tn: pl.BlockSpec((None, D, tn), lambda j, i: (l, 0, j)),
                          IN_MAIN, 768, "inproj_main")
        gates = _inproj_call(x, mod_l, w_in,
                             lambda tn: pl.BlockSpec((None, D, tn), lambda j, i: (l, 0, j + IN_MAIN // tn)),
                             IN_GATE, 768, "inproj_gate")
        plr = _inproj_call(x, mod_l, w_lr, lambda tn: pl.BlockSpec((D, tn), lambda j, i: (0, j)),
                           LR_W, LR_W, "inproj_lr")

        sink_l = attn_sink[l]
        oa = jnp.concatenate([_attn_ctx_call(sink_l, pm),
                              _attn_lat_call(sink_l, pm, kc_all, vc_all, l, cos, sin)], axis=0)

        rprm = (rwkv_w0[l].reshape(2, 1, B_W), rwkv_w_b[l], rwkv_a0[l].reshape(2, 1, B_W), rwkv_a_b[l],
                rwkv_g_b[l], rwkv_k_k[l].reshape(1, B_W), rwkv_k_a[l].reshape(1, B_W),
                rwkv_r_k[l].reshape(1, B_W), rwkv_ln_g[l].reshape(1, B_W), rwkv_ln_b[l].reshape(1, B_W))
        ob_c, sr_c = _rwkv_call(CTX_T, 4, N_CTX_B, 0, pm, plr, zeros_r, rprm, "rwkv_ctx")
        ob_d, _ = _rwkv_call(DEC_T, 2, N_DEC_B, N_CTX_TOK, pm, plr, state_rwkv[:, l], rprm, "rwkv_lat")
        ob = jnp.concatenate([ob_c, ob_d], axis=0)

        gb_pad = jnp.zeros((2, 128, C_KW), F32)
        gb_pad = gb_pad.at[0, 0:C_GATE_RANK].set(gla_gate_b[l, 0]).at[1, C_GATE_RANK:2 * C_GATE_RANK].set(
            gla_gate_b[l, 1])
        gbias = gla_gate_bias[l].reshape(2, 1, C_KW)
        ng = jnp.tile(gla_norm_g[l], C_HEADS).reshape(1, C_VW)
        oc_c, sg_c = _gla_call(CTX_T, N_CTX_B, 0, pm, plr, zeros_g, gb_pad, gbias, ng, "gla_ctx")
        oc_d, _ = _gla_call(DEC_T, N_DEC_B, N_CTX_TOK, pm, plr, state_gla[:, l], gb_pad, gbias, ng, "gla_lat")
        oc = jnp.concatenate([oc_c, oc_d], axis=0)

        x = _merge_call(x, gates, oa, ob, oc, mod_l, w_up_a, w_up_b, w_up_c, w_out, ln1_g, ln1_b, l)
        if l % 2 == 0:
            x = _ffn_call(x, mod_l, ffn_w_gate, ffn_w_up, ffn_w_down, ln2_g, ln2_b, l, l // 2)
        else:
            wr_pad = jnp.concatenate([moe_router, jnp.zeros((moe_router.shape[0], D, 128 - N_EXP), F32)], axis=2)
            x = _moe_call(x, mod_l, wr_pad, moe_w_gate, moe_w_up, moe_w_down, ln2_g, ln2_b, l, l // 2)

        new_k.append(pm[:N_CTX_TOK, 512:640].reshape(N_CTX_B, CTX_T, A_KV, A_DH))
        new_v.append(pm[:N_CTX_TOK, 640:768].reshape(N_CTX_B, CTX_T, A_KV, A_DH))
        new_sr.append(sr_c)
        new_sg.append(sg_c)

    y_prompt = x[:N_CTX_TOK].reshape(N_CTX_B, CTX_T, D)
    y_sample = x[N_CTX_TOK:].reshape(N_DEC_B, DEC_T, D)
    return (y_prompt, y_sample, jnp.stack(new_k, axis=1), jnp.stack(new_v, axis=1),
            jnp.stack(new_sr, axis=1), jnp.stack(new_sg, axis=1))
```

```python
import functools

import numpy as np
import jax
import jax.numpy as jnp
from jax import lax
from jax.experimental import pallas as pl
from jax.experimental.pallas import tpu as pltpu

D = 1024
N_CTX_B, CTX_T = 32, 256
N_DEC_B, DEC_T = 2, 1024
N_CTX_TOK = N_CTX_B * CTX_T
N_TOK = N_CTX_TOK + N_DEC_B * DEC_T
DEPTH = 2
PAST = 256
GRID_W = 64
A_HEADS, A_KV, A_DH = 8, 2, 64
A_G = A_HEADS // A_KV
A_WIN, A_BLK = 128, 128
ROPE_BASE = 10000.0
B_HEADS, B_DH = 4, 64
B_W = B_HEADS * B_DH
B_GN_EPS = 64e-5
C_HEADS, C_DK, C_DV = 4, 32, 64
C_KW, C_VW = C_HEADS * C_DK, C_HEADS * C_DV
C_GATE_RANK = 16
C_GATE_NORM = 16.0
C_CHUNK = 64
C_SUB = 16
D_FF = 2816
N_EXP = 8
D_FFE = 3584
LN_EPS = 1e-5
DN_ALPHA = (2.0 * DEPTH) ** 0.25
NEG_INF = -1e30
IN_MAIN = 2304
IN_GATE = 3072
LR_W = 512
N_GROUPS = 8

F32 = jnp.float32
BF16 = jnp.bfloat16
VMEM_LIMIT = 56 * 1024 * 1024

NN = ((1,), (0,))
NT = ((1,), (1,))
TN = ((0,), (0,))


def _dg(a, b, dims=NN):
    return lax.dot_general(a, b, (dims, ((), ())), preferred_element_type=F32)


def _split2(x):
    hi = x.astype(BF16)
    lo = (x - hi.astype(F32)).astype(BF16)
    return hi, lo


def _split3(x):
    hi = x.astype(BF16)
    r = x - hi.astype(F32)
    mid = r.astype(BF16)
    lo = (r - mid.astype(F32)).astype(BF16)
    return hi, mid, lo


def _mm(a, b, dims=NN):
    return _dg(a.astype(BF16), b.astype(BF16), dims)


def _mm3(a, b, dims=NN):
    ah, al = _split2(a)
    bh, bl = _split2(b)
    return _dg(ah, bh, dims) + (_dg(ah, bl, dims) + _dg(al, bh, dims))


def _mm_xr(a, b_exact, passes, dims=NN):
    parts = (a.astype(BF16),) if passes == 1 else (_split2(a) if passes == 2 else _split3(a))
    out = _dg(parts[0], b_exact, dims)
    for p in parts[1:]:
        out = out + _dg(p, b_exact, dims)
    return out


def _mm_xl(a_exact, b, passes):
    parts = (b.astype(BF16),) if passes == 1 else (_split2(b) if passes == 2 else _split3(b))
    out = _dg(a_exact, parts[0])
    for p in parts[1:]:
        out = out + _dg(a_exact, p)
    return out


def _sigmoid(x):
    return 1.0 / (1.0 + jnp.exp(-x))


def _iota(shape, dim):
    return lax.broadcasted_iota(jnp.int32, shape, dim)


def _block_ones(n, blk):
    return (_iota((n, n), 0) // blk == _iota((n, n), 1) // blk).astype(BF16)


def _layer_norm(y, g, b):
    mu = jnp.mean(y, -1, keepdims=True)
    yc = y - mu
    var = jnp.mean(yc * yc, -1, keepdims=True)
    return yc * lax.rsqrt(var + LN_EPS) * g + b


def _group_of_tile(i, tm):
    n_ctx = N_CTX_TOK // tm
    per_dec = DEC_T // tm
    return jnp.where(i < n_ctx, 0, 1 + (i - n_ctx) // per_dec)


def _cparams(sem):
    return pltpu.CompilerParams(dimension_semantics=sem, vmem_limit_bytes=VMEM_LIMIT)


def _ada_kernel(c_ref, w_ref, b_ref, o_ref):
    c = c_ref[...]
    s = c * _sigmoid(c)
    o_ref[...] = _mm3(s, w_ref[...]) + b_ref[...]


def _ada_call(cvec, w_ada, b_ada):
    tn = 1536
    return pl.pallas_call(
        _ada_kernel,
        grid=(DEPTH, 6 * D // tn),
        in_specs=[
            pl.BlockSpec((N_GROUPS, D), lambda l, j: (0, 0)),
            pl.BlockSpec((None, D, tn), lambda l, j: (l, 0, j)),
            pl.BlockSpec((None, 1, tn), lambda l, j: (l, 0, j)),
        ],
        out_specs=pl.BlockSpec((None, N_GROUPS, tn), lambda l, j: (l, 0, j)),
        out_shape=jax.ShapeDtypeStruct((DEPTH, N_GROUPS, 6 * D), F32),
        compiler_params=_cparams(("arbitrary", "arbitrary")),
        name="ada",
    )(cvec, w_ada, b_ada.reshape(DEPTH, 1, 6 * D))


def _inproj_kernel(x_ref, mod_ref, w_ref, o_ref):
    sh = mod_ref[0:1, :]
    sc = mod_ref[1:2, :]
    h = x_ref[...] * (1.0 + sc) + sh
    o_ref[...] = _mm(h, w_ref[...])


def _inproj_call(x, mod_l, w, w_spec_fn, n_out, tn, name):
    tm = 512
    return pl.pallas_call(
        _inproj_kernel,
        grid=(n_out // tn, N_TOK // tm),
        in_specs=[
            pl.BlockSpec((tm, D), lambda j, i: (i, 0)),
            pl.BlockSpec((None, 6, D), lambda j, i: (_group_of_tile(i, tm), 0, 0)),
            w_spec_fn(tn),
        ],
        out_specs=pl.BlockSpec((tm, tn), lambda j, i: (i, j)),
        out_shape=jax.ShapeDtypeStruct((N_TOK, n_out), F32),
        compiler_params=_cparams(("arbitrary", "arbitrary")),
        name=name,
    )(x, mod_l, w)


def _sink_col(sink_ref, kvh, rows_per_head):
    n = A_G * rows_per_head
    r = _iota((n, 1), 0) // rows_per_head
    col = jnp.full((n, 1), sink_ref[kvh * A_G], F32)
    for g in range(1, A_G):
        col = jnp.where(r == g, sink_ref[kvh * A_G + g], col)
    return col


def _attn_ctx_kernel(sink_ref, q_ref, k_ref, v_ref, o_ref):
    scale = A_DH ** -0.5
    for kvh in range(A_KV):
        ks = k_ref[:, kvh * A_DH:(kvh + 1) * A_DH].astype(BF16)
        vs = v_ref[:, kvh * A_DH:(kvh + 1) * A_DH].astype(BF16)
        q4 = jnp.concatenate(
            [q_ref[:, (kvh * A_G + g) * A_DH:(kvh * A_G + g + 1) * A_DH] for g in range(A_G)], axis=0)
        s = _dg(q4.astype(BF16), ks, NT) * scale
        sink = _sink_col(sink_ref, kvh, CTX_T)
        m = jnp.maximum(jnp.max(s, -1, keepdims=True), sink)
        e = jnp.exp(s - m)
        p = e / (jnp.sum(e, -1, keepdims=True) + jnp.exp(sink - m))
        o = _dg(p.astype(BF16), vs)
        for g in range(A_G):
            h = kvh * A_G + g
            o_ref[:, h * A_DH:(h + 1) * A_DH] = o[g * CTX_T:(g + 1) * CTX_T, :]


def _attn_ctx_call(sink_l, pm):
    return pl.pallas_call(
        _attn_ctx_kernel,
        grid=(N_CTX_B,),
        in_specs=[
            pl.BlockSpec(memory_space=pltpu.SMEM),
            pl.BlockSpec((CTX_T, 512), lambda b: (b, 0)),
            pl.BlockSpec((CTX_T, 128), lambda b: (b, 4)),
            pl.BlockSpec((CTX_T, 128), lambda b: (b, 5)),
        ],
        out_specs=pl.BlockSpec((CTX_T, 512), lambda b: (b, 0)),
        out_shape=jax.ShapeDtypeStruct((N_CTX_TOK, 512), F32),
        compiler_params=_cparams(("arbitrary",)),
        name="attn_ctx",
    )(sink_l, pm, pm, pm)


def _rope(x, cos, sin_signed):
    w = x.shape[-1]
    lane = _iota(x.shape, 1)
    partner = jnp.where((lane % 32) < 16, pltpu.roll(x, w - 16, 1), pltpu.roll(x, 16, 1))
    return x * cos + partner * sin_signed


def _attn_lat_kernel(sink_ref, q_ref, k_ref, v_ref, kc_ref, vc_ref, cos_ref, sin_ref, o_ref, kr_ref):
    n = pl.program_id(1)
    scale = A_DH ** -0.5

    @pl.when(n == 0)
    def _():
        kr_ref[...] = _rope(k_ref[...], cos_ref[:, 0:128], sin_ref[:, 0:128]).astype(BF16)

    q0 = pl.multiple_of(n * A_BLK, A_BLK)
    qr = _rope(q_ref[...], cos_ref[pl.ds(q0, A_BLK), :], sin_ref[pl.ds(q0, A_BLK), :])
    kstart = pl.multiple_of(jnp.clip((n - 1) * A_BLK, 0, DEC_T - 3 * A_BLK), A_BLK)
    kwin = kr_ref[pl.ds(kstart, 3 * A_BLK), :]
    vwin = v_ref[pl.ds(kstart, 3 * A_BLK), :].astype(BF16)
    kc = kc_ref[...].astype(BF16)
    vc = vc_ref[...].astype(BF16)
    rows = A_G * A_BLK
    qpos = q0 + _iota((rows, 3 * A_BLK), 0) % A_BLK
    kpos = kstart + _iota((rows, 3 * A_BLK), 1)
    valid = jnp.abs(qpos - kpos) <= A_WIN
    for kvh in range(A_KV):
        cs = slice(kvh * A_DH, (kvh + 1) * A_DH)
        q4 = jnp.concatenate(
            [qr[:, (kvh * A_G + g) * A_DH:(kvh * A_G + g + 1) * A_DH] for g in range(A_G)], axis=0).astype(BF16)
        s_loc = jnp.where(valid, _dg(q4, kwin[:, cs], NT) * scale, NEG_INF)
        s_ctx = _dg(q4, kc[:, cs], NT) * scale
        sink = _sink_col(sink_ref, kvh, A_BLK)
        m = jnp.maximum(jnp.maximum(jnp.max(s_loc, -1, keepdims=True), jnp.max(s_ctx, -1, keepdims=True)), sink)
        e_loc = jnp.exp(s_loc - m)
        e_ctx = jnp.exp(s_ctx - m)
        inv = 1.0 / (jnp.sum(e_loc, -1, keepdims=True) + jnp.sum(e_ctx, -1, keepdims=True) + jnp.exp(sink - m))
        o = _dg((e_loc * inv).astype(BF16), vwin[:, cs]) + _dg((e_ctx * inv).astype(BF16), vc[:, cs])
        for g in range(A_G):
            h = kvh * A_G + g
            o_ref[:, h * A_DH:(h + 1) * A_DH] = o[g * A_BLK:(g + 1) * A_BLK, :]


def _rope_tables():
    half = A_DH // 2
    t = np.arange(DEC_T)
    rows = (t // GRID_W).astype(np.float32)
    cols = (t % GRID_W).astype(np.float32)
    inv_freq = (ROPE_BASE ** (-np.arange(0, half, 2, dtype=np.float32) / half)).astype(np.float32)
    ang_r = rows[:, None] * inv_freq[None, :]
    ang_c = cols[:, None] * inv_freq[None, :]
    cos = np.concatenate([np.cos(ang_r), np.cos(ang_r), np.cos(ang_c), np.cos(ang_c)], -1)
    sin = np.concatenate([-np.sin(ang_r), np.sin(ang_r), -np.sin(ang_c), np.sin(ang_c)], -1)
    return (jnp.asarray(np.tile(cos, (1, A_HEADS)), F32), jnp.asarray(np.tile(sin, (1, A_HEADS)), F32))


def _attn_lat_call(sink_l, pm, kc, vc, l, cos, sin):
    nb = DEC_T // A_BLK
    row0 = N_CTX_TOK // A_BLK
    seq0 = N_CTX_TOK // DEC_T
    return pl.pallas_call(
        _attn_lat_kernel,
        grid=(N_DEC_B, nb),
        in_specs=[
            pl.BlockSpec(memory_space=pltpu.SMEM),
            pl.BlockSpec((A_BLK, 512), lambda b, n: (row0 + b * nb + n, 0)),
            pl.BlockSpec((DEC_T, 128), lambda b, n: (seq0 + b, 4)),
            pl.BlockSpec((DEC_T, 128), lambda b, n: (seq0 + b, 5)),
            pl.BlockSpec((None, None, PAST, 128), lambda b, n: (b, l, 0, 0)),
            pl.BlockSpec((None, None, PAST, 128), lambda b, n: (b, l, 0, 0)),
            pl.BlockSpec((DEC_T, 512), lambda b, n: (0, 0)),
            pl.BlockSpec((DEC_T, 512), lambda b, n: (0, 0)),
        ],
        out_specs=pl.BlockSpec((A_BLK, 512), lambda b, n: (b * nb + n, 0)),
        out_shape=jax.ShapeDtypeStruct((N_DEC_B * DEC_T, 512), F32),
        scratch_shapes=[pltpu.VMEM((DEC_T, 128), BF16)],
        compiler_params=_cparams(("arbitrary", "arbitrary")),
        name="attn_lat",
    )(sink_l, pm, pm, pm, kc, vc, cos, sin)


def _rwkv_kernel(T, NS, r_ref, k_ref, v_ref, lr_ref, s0_ref, w0_ref, wb_ref, a0_ref, ab_ref, gb_ref,
                 kk_ref, ka_ref, rk_ref, lng_ref, lnb_ref, o_ref, sfin_ref,
                 KK, W, WRP, AKK, KT, VC2, BON, Y, S):
    ones4 = _block_ones(B_W, B_DH)
    decay_c = float(np.exp(-0.5))
    RC = 256
    SUB = 8
    NP = 3

    def prep(c, carry):
        r0 = pl.multiple_of(c * RC, RC)
        rs = pl.ds(r0, RC)
        r = r_ref[rs, :]
        k = k_ref[rs, :]
        v = v_ref[rs, :]
        lr = lr_ref[rs, :]
        kkr = k * kk_ref[...]
        kk = kkr * lax.rsqrt(_mm_xr(kkr * kkr, ones4, 3) + 1e-12)
        KK[rs, :] = kk
        bonus = jnp.zeros((RC, B_W), F32)
        vc2 = jnp.zeros((RC, B_W), F32)
        for d in range(2):
            z = w0_ref[d] + _mm(jnp.tanh(lr[:, 64 * d:64 * d + 64]), wb_ref[d])
            w = jnp.exp(-decay_c * _sigmoid(z))
            a = _sigmoid(a0_ref[d] + _mm(lr[:, 128 + 64 * d:192 + 64 * d], ab_ref[d]))
            kt = k * (1.0 + (a - 1.0) * ka_ref[...])
            akk = a * kk
            W[d, rs, :] = w
            WRP[d, rs, :] = w * r - _mm_xr(akk * r, ones4, 3) * kk
            AKK[d, rs, :] = akk
            KT[d, rs, :] = kt
            vc2 = vc2 + _mm_xr(kt * r, ones4, 3) * v
            bonus = bonus + _mm_xr(r * kt * rk_ref[...], ones4, 3) * v
        VC2[rs, :] = vc2
        BON[rs, :] = bonus
        return carry

    lax.fori_loop(0, NS * T // RC, prep, 0)

    chains = [(s, d) for s in range(NS) for d in range(2)]
    for s, d in chains:
        S[s, d] = jnp.concatenate([s0_ref[s, d, h] for h in range(B_HEADS)], axis=1)

    eye4 = _iota((B_DH, B_W), 0) == (_iota((B_DH, B_W), 1) % B_DH)

    def steps(i, carry):
        tiles = []
        for s, d in chains:
            t0 = pl.multiple_of(s * T + (i * SUB if d == 0 else T - SUB - i * SUB), SUB)
            rows = pl.ds(t0, SUB)
            tiles.append((rows, KK[rows, :], WRP[d, rows, :], v_ref[rows, :],
                          W[d, rows, :], AKK[d, rows, :], KT[d, rows, :]))
        ys = [[None] * SUB for _ in chains]
        for jj in range(SUB):
            lhs = []
            for (s, d), (_, kk8, wrp8, v8, _, _, _) in zip(chains, tiles):
                j = jj if d == 0 else SUB - 1 - jj
                r = slice(j, j + 1)
                st = S[s, d]
                lhs += [(st * kk8[r]).astype(BF16), (st * wrp8[r]).astype(BF16),
                        jnp.where(eye4, v8[r], 0.0).astype(BF16)]
            res = _dg(jnp.concatenate(lhs, axis=0), ones4)
            for g, ((s, d), (_, _, _, _, w8, akk8, kt8)) in enumerate(zip(chains, tiles)):
                j = jj if d == 0 else SUB - 1 - jj
                r = slice(j, j + 1)
                sk, yp, vcol = [res[(g * NP + n) * B_DH:(g * NP + n + 1) * B_DH] for n in range(NP)]
                S[s, d] = S[s, d] * w8[r] - sk * akk8[r] + vcol * kt8[r]
                ys[g][j] = jnp.sum(jnp.where(eye4, yp, 0.0), axis=0, keepdims=True)
        for g, ((s, d), tl) in enumerate(zip(chains, tiles)):
            Y[d, tl[0], :] = jnp.concatenate(ys[g], axis=0)
        return carry

    lax.fori_loop(0, T // SUB, steps, 0)

    for s, d in chains:
        st = S[s, d]
        for h in range(B_HEADS):
            sfin_ref[s, d, h] = st[:, h * B_DH:(h + 1) * B_DH]

    def post(c, carry):
        r0 = pl.multiple_of(c * RC, RC)
        rs = pl.ds(r0, RC)
        y = Y[0, rs, :] + Y[1, rs, :] + VC2[rs, :]
        mu = _mm_xr(y, ones4, 3) * (1.0 / B_DH)
        yc = y - mu
        var = _mm_xr(yc * yc, ones4, 3) * (1.0 / B_DH)
        yn = yc * lax.rsqrt(var + B_GN_EPS) * lng_ref[...] + lnb_ref[...] + BON[rs, :]
        g = _mm(_sigmoid(lr_ref[rs, 256:384]), gb_ref[...])
        o_ref[rs, :] = yn * g
        return carry

    lax.fori_loop(0, NS * T // RC, post, 0)


def _rwkv_call(T, NS, n_seq, tok0, pm, plr, s0, prm, name):
    rows = NS * T
    blk0 = tok0 // rows
    n_steps = n_seq // NS
    full = lambda shape: pl.BlockSpec(shape, lambda b: (0,) * len(shape))
    big = lambda shape, imap: (pl.BlockSpec(shape, imap, pipeline_mode=pl.Buffered(1)) if n_steps == 1
                               else pl.BlockSpec(shape, imap))
    kern = functools.partial(_rwkv_kernel, T, NS)
    return pl.pallas_call(
        kern,
        grid=(n_steps,),
        in_specs=[
            big((rows, B_W), lambda b: (blk0 + b, 3)),
            big((rows, B_W), lambda b: (blk0 + b, 4)),
            big((rows, B_W), lambda b: (blk0 + b, 5)),
            big((rows, LR_W), lambda b: (blk0 + b, 0)),
            pl.BlockSpec((NS, 2, B_HEADS, B_DH, B_DH), lambda b: (b, 0, 0, 0, 0)),
            full((2, 1, B_W)), full((2, 64, B_W)), full((2, 1, B_W)), full((2, 64, B_W)), full((128, B_W)),
            full((1, B_W)), full((1, B_W)), full((1, B_W)), full((1, B_W)), full((1, B_W)),
        ],
        out_specs=[
            big((rows, B_W), lambda b: (b, 0)),
            pl.BlockSpec((NS, 2, B_HEADS, B_DH, B_DH), lambda b: (b, 0, 0, 0, 0)),
        ],
        out_shape=[
            jax.ShapeDtypeStruct((n_seq * T, B_W), F32),
            jax.ShapeDtypeStruct((n_seq, 2, B_HEADS, B_DH, B_DH), F32),
        ],
        scratch_shapes=[
            pltpu.VMEM((rows, B_W), F32),
            pltpu.VMEM((2, rows, B_W), F32),
            pltpu.VMEM((2, rows, B_W), F32),
            pltpu.VMEM((2, rows, B_W), F32),
            pltpu.VMEM((2, rows, B_W), F32),
            pltpu.VMEM((rows, B_W), F32),
            pltpu.VMEM((rows, B_W), F32),
            pltpu.VMEM((2, rows, B_W), F32),
            pltpu.VMEM((NS, 2, B_DH, B_W), F32),
        ],
        compiler_params=_cparams(("arbitrary",)),
        name=name,
    )(pm, pm, pm, plr, s0, *prm)


def _gla_kernel(T, q_ref, k_ref, v_ref, og_ref, lr_ref, s0_ref, gb_ref, bias_ref, ng_ref,
                o_ref, sfin_ref, LA, O, S):
    n_chunks = T // C_CHUNK
    nsub = C_CHUNK // C_SUB
    qscale = C_DK ** -0.5
    lr = lr_ref[...]
    for d in range(2):
        gl = _mm(lr, gb_ref[d]) + bias_ref[d]
        LA[d] = (jnp.minimum(gl, 0.0) - jnp.log(1.0 + jnp.exp(-jnp.abs(gl)))) * (1.0 / C_GATE_NORM)
    bd_state = _iota((C_KW, C_VW), 0) // C_DK == _iota((C_KW, C_VW), 1) // C_DV
    for d in range(2):
        for h in range(C_HEADS):
            pad_l = h * C_DV
            pad_r = C_VW - (h + 1) * C_DV
            blk = s0_ref[d, h]
            parts = ([jnp.zeros((C_DK, pad_l), F32)] if pad_l else []) + [blk] + \
                    ([jnp.zeros((C_DK, pad_r), F32)] if pad_r else [])
            S[d, h * C_DK:(h + 1) * C_DK, :] = jnp.concatenate(parts, axis=1)

    ti = _iota((C_CHUNK, C_CHUNK), 0)
    si = _iota((C_CHUNK, C_CHUNK), 1)
    tri = ((si <= ti).astype(BF16), (si >= ti).astype(BF16))
    trow = _iota((C_CHUNK, 1), 0)
    mask_k = _iota((C_CHUNK, C_KW), 0) // C_SUB == _iota((C_CHUNK, C_KW), 1) // C_DK
    mask_v = _iota((C_CHUNK, C_VW), 0) // C_SUB == _iota((C_CHUNK, C_VW), 1) // C_DV
    t_att = _iota((C_CHUNK, C_CHUNK), 0)
    s_att = _iota((C_CHUNK, C_CHUNK), 1) % C_SUB
    eye_k = _iota((C_KW, C_KW), 0) == _iota((C_KW, C_KW), 1)

    def chunk(d, r0):
        rs = pl.ds(r0, C_CHUNK)
        b = _mm_xl(tri[d], LA[d, rs, :], 3)
        q = q_ref[rs, :] * qscale
        k = k_ref[rs, :]
        v = v_ref[rs, :]
        st = S[d]
        o = _mm3(q * jnp.exp(b), st)
        for j in range(nsub):
            lo, hi = j * C_SUB, (j + 1) * C_SUB
            if d == 0:
                gamma = b[hi - 1:hi, :]
                row_ok = trow >= lo
                att_ok = t_att >= lo + s_att
            else:
                gamma = b[lo:lo + 1, :]
                row_ok = trow < hi
                att_ok = t_att <= lo + s_att
            qj = q * jnp.exp(jnp.where(row_ok, b - gamma, NEG_INF))
            kj = k[lo:hi, :] * jnp.exp(gamma - b[lo:hi, :])
            kbd = jnp.where(mask_k, jnp.concatenate([kj] * C_HEADS, axis=0), 0.0)
            att = jnp.where(att_ok, _mm(qj, kbd, NT), 0.0)
            vbd = jnp.where(mask_v, jnp.concatenate([v[lo:hi, :]] * C_HEADS, axis=0), 0.0)
            o = o + _mm(att, vbd)
        O[d, rs, :] = o
        blast = b[C_CHUNK - 1:C_CHUNK, :] if d == 0 else b[0:1, :]
        kl = k * jnp.exp(blast - b)
        upd = jnp.where(bd_state, _mm3(kl.T, v), 0.0)
        dec = jnp.where(eye_k, jnp.exp(blast), 0.0)
        S[d] = _mm3(dec, st) + upd

    def body(c, carry):
        chunk(0, pl.multiple_of(c * C_CHUNK, C_CHUNK))
        chunk(1, pl.multiple_of((n_chunks - 1 - c) * C_CHUNK, C_CHUNK))
        return carry

    lax.fori_loop(0, n_chunks, body, 0)

    for d in range(2):
        st = S[d]
        for h in range(C_HEADS):
            sfin_ref[d, h] = st[h * C_DK:(h + 1) * C_DK, h * C_DV:(h + 1) * C_DV]

    ones4 = _block_ones(C_VW, C_DV)
    o = O[0] + O[1]
    ms = _mm_xr(o * o, ones4, 3) * (1.0 / C_DV)
    og = og_ref[...]
    o_ref[...] = o * lax.rsqrt(ms + LN_EPS) * ng_ref[...] * (og * _sigmoid(og))


def _gla_call(T, n_seq, tok0, pm, plr, s0, gb_pad, bias, ng, name):
    blk0 = tok0 // T
    full = lambda shape: pl.BlockSpec(shape, lambda b: (0,) * len(shape))
    return pl.pallas_call(
        functools.partial(_gla_kernel, T),
        grid=(n_seq,),
        in_specs=[
            pl.BlockSpec((T, C_KW), lambda b: (blk0 + b, 12)),
            pl.BlockSpec((T, C_KW), lambda b: (blk0 + b, 13)),
            pl.BlockSpec((T, C_VW), lambda b: (blk0 + b, 7)),
            pl.BlockSpec((T, C_VW), lambda b: (blk0 + b, 8)),
            pl.BlockSpec((T, 128), lambda b: (blk0 + b, 3)),
            pl.BlockSpec((None, 2, C_HEADS, C_DK, C_DV), lambda b: (b, 0, 0, 0, 0)),
            full((2, 128, C_KW)), full((2, 1, C_KW)), full((1, C_VW)),
        ],
        out_specs=[
            pl.BlockSpec((T, C_VW), lambda b: (b, 0)),
            pl.BlockSpec((None, 2, C_HEADS, C_DK, C_DV), lambda b: (b, 0, 0, 0, 0)),
        ],
        out_shape=[
            jax.ShapeDtypeStruct((n_seq * T, C_VW), F32),
            jax.ShapeDtypeStruct((n_seq, 2, C_HEADS, C_DK, C_DV), F32),
        ],
        scratch_shapes=[
            pltpu.VMEM((2, T, C_KW), F32),
            pltpu.VMEM((2, T, C_VW), F32),
            pltpu.VMEM((2, C_KW, C_VW), F32),
        ],
        compiler_params=_cparams(("arbitrary",)),
        name=name,
    )(pm, pm, pm, pm, plr, s0, gb_pad, bias, ng)


def _merge_kernel(x_ref, g_ref, oa_ref, ob_ref, oc_ref, mod_ref, wa_ref, wb_ref, wc_ref, wo_ref,
                  lg_ref, lb_ref, o_ref, wa_s, wb_s, wc_s, wo_s):
    @pl.when(pl.program_id(0) == 0)
    def _():
        wa_s[...] = wa_ref[...].astype(BF16)
        wb_s[...] = wb_ref[...].astype(BF16)
        wc_s[...] = wc_ref[...].astype(BF16)
        wo_s[...] = wo_ref[...].astype(BF16)

    merged = (_sigmoid(g_ref[:, 0:D]) * _dg(oa_ref[...].astype(BF16), wa_s[...])
              + _sigmoid(g_ref[:, D:2 * D]) * _dg(ob_ref[...].astype(BF16), wb_s[...])
              + _sigmoid(g_ref[:, 2 * D:3 * D]) * _dg(oc_ref[...].astype(BF16), wc_s[...]))
    mix = _dg(merged.astype(BF16), wo_s[...])
    y = DN_ALPHA * x_ref[...] + mod_ref[2:3, :] * mix
    o_ref[...] = _layer_norm(y, lg_ref[...], lb_ref[...])


def _merge_call(x, gates, oa, ob, oc, mod_l, wa, wb, wc, wo, lg, lb, l):
    tm = 512
    wspec = lambda r: pl.BlockSpec((None, r, D), lambda i: (l, 0, 0))
    vspec = pl.BlockSpec((None, 1, D), lambda i: (l, 0, 0))
    return pl.pallas_call(
        _merge_kernel,
        grid=(N_TOK // tm,),
        in_specs=[
            pl.BlockSpec((tm, D), lambda i: (i, 0)),
            pl.BlockSpec((tm, IN_GATE), lambda i: (i, 0)),
            pl.BlockSpec((tm, 512), lambda i: (i, 0)),
            pl.BlockSpec((tm, B_W), lambda i: (i, 0)),
            pl.BlockSpec((tm, C_VW), lambda i: (i, 0)),
            pl.BlockSpec((None, 6, D), lambda i: (_group_of_tile(i, tm), 0, 0)),
            wspec(512), wspec(B_W), wspec(C_VW), wspec(D), vspec, vspec,
        ],
        out_specs=pl.BlockSpec((tm, D), lambda i: (i, 0)),
        out_shape=jax.ShapeDtypeStruct((N_TOK, D), F32),
        scratch_shapes=[pltpu.VMEM((512, D), BF16), pltpu.VMEM((B_W, D), BF16),
                        pltpu.VMEM((C_VW, D), BF16), pltpu.VMEM((D, D), BF16)],
        compiler_params=_cparams(("arbitrary",)),
        name="merge",
    )(x, gates, oa, ob, oc, mod_l, wa, wb, wc, wo, lg.reshape(DEPTH, 1, D), lb.reshape(DEPTH, 1, D))


def _ffn_kernel(x_ref, mod_ref, wg_ref, wu_ref, wd_ref, lg_ref, lb_ref, o_ref, h_s, acc_s):
    f = pl.program_id(1)

    @pl.when(f == 0)
    def _():
        h_s[...] = (x_ref[...] * (1.0 + mod_ref[4:5, :]) + mod_ref[3:4, :]).astype(BF16)
        acc_s[...] = jnp.zeros_like(acc_s)

    h = h_s[...]
    gate = _dg(h, wg_ref[...].astype(BF16))
    up = _dg(h, wu_ref[...].astype(BF16))
    acc_s[...] += _dg((gate * _sigmoid(gate) * up).astype(BF16), wd_ref[...].astype(BF16))

    @pl.when(f == pl.num_programs(1) - 1)
    def _():
        y = DN_ALPHA * x_ref[...] + mod_ref[5:6, :] * acc_s[...]
        o_ref[...] = _layer_norm(y, lg_ref[...], lb_ref[...])


def _ffn_call(x, mod_l, wg, wu, wd, lg, lb, l, i_ffn):
    tm, tf = 1024, 256
    vspec = pl.BlockSpec((None, 1, D), lambda i, f: (l, 0, 0))
    return pl.pallas_call(
        _ffn_kernel,
        grid=(N_TOK // tm, D_FF // tf),
        in_specs=[
            pl.BlockSpec((tm, D), lambda i, f: (i, 0)),
            pl.BlockSpec((None, 6, D), lambda i, f: (_group_of_tile(i, tm), 0, 0)),
            pl.BlockSpec((None, D, tf), lambda i, f: (i_ffn, 0, f)),
            pl.BlockSpec((None, D, tf), lambda i, f: (i_ffn, 0, f)),
            pl.BlockSpec((None, tf, D), lambda i, f: (i_ffn, f, 0)),
            vspec, vspec,
        ],
        out_specs=pl.BlockSpec((tm, D), lambda i, f: (i, 0)),
        out_shape=jax.ShapeDtypeStruct((N_TOK, D), F32),
        scratch_shapes=[pltpu.VMEM((tm, D), BF16), pltpu.VMEM((tm, D), F32)],
        compiler_params=_cparams(("arbitrary", "arbitrary")),
        name="ffn",
    )(x, mod_l, wg, wu, wd, lg.reshape(DEPTH, 1, D), lb.reshape(DEPTH, 1, D))


MOE_TM = 1024
MOE_TR = 256
MOE_RS = 3072
MOE_TF = 512
MOE_ROWS = 2 * N_TOK + N_EXP * MOE_TR
MOE_NST = -(-MOE_ROWS // MOE_RS) + N_EXP
R_I1, R_I2, R_W1, R_W2, R_RANK1, R_RANK2 = range(6)


def _moe_route_kernel(x_ref, mod_ref, wr_ref, h_ref, info_ref, cnt_ref, carry_s):
    tm = MOE_TM

    @pl.when(pl.program_id(0) == 0)
    def _():
        carry_s[...] = jnp.zeros_like(carry_s)

    h = x_ref[...] * (1.0 + mod_ref[4:5, :]) + mod_ref[3:4, :]
    h_ref[...] = h
    logits = _mm3(h, wr_ref[...])
    lane = _iota(logits.shape, 1)
    logits = jnp.where(lane < N_EXP, logits, NEG_INF)
    v1 = jnp.max(logits, -1, keepdims=True)
    i1 = jnp.min(jnp.where(logits == v1, lane, 128), -1, keepdims=True)
    rest = jnp.where(lane == i1, NEG_INF, logits)
    v2 = jnp.max(rest, -1, keepdims=True)
    i2 = jnp.min(jnp.where(rest == v2, lane, 128), -1, keepdims=True)
    e2 = jnp.exp(v2 - v1)
    w1 = 1.0 / (1.0 + e2)
    w2 = e2 / (1.0 + e2)
    oh1 = lane == i1
    oh2 = lane == i2
    cnt = oh1.astype(F32) + oh2.astype(F32)
    earlier = (_iota((tm, tm), 1) < _iota((tm, tm), 0)).astype(BF16)
    before = _dg(earlier, cnt.astype(BF16)) + carry_s[...]
    rank1 = jnp.sum(jnp.where(oh1, before, 0.0), -1, keepdims=True)
    rank2 = jnp.sum(jnp.where(oh2, before, 0.0), -1, keepdims=True)
    info = jnp.zeros(logits.shape, F32)
    for ln, val in ((R_I1, i1.astype(F32)), (R_I2, i2.astype(F32)), (R_W1, w1), (R_W2, w2),
                    (R_RANK1, rank1), (R_RANK2, rank2)):
        info = jnp.where(lane == ln, val, info)
    info_ref[...] = info
    carry_s[...] += jnp.sum(cnt, axis=0, keepdims=True)
    cnt_ref[...] = carry_s[...]


def _moe_route_call(x, mod_l, wr_pad, i_moe):
    tm = MOE_TM
    return pl.pallas_call(
        _moe_route_kernel,
        grid=(N_TOK // tm,),
        in_specs=[
            pl.BlockSpec((tm, D), lambda i: (i, 0)),
            pl.BlockSpec((None, 6, D), lambda i: (_group_of_tile(i, tm), 0, 0)),
            pl.BlockSpec((None, D, 128), lambda i: (i_moe, 0, 0)),
        ],
        out_specs=[
            pl.BlockSpec((tm, D), lambda i: (i, 0)),
            pl.BlockSpec((tm, 128), lambda i: (i, 0)),
            pl.BlockSpec((1, 128), lambda i: (0, 0)),
        ],
        out_shape=[
            jax.ShapeDtypeStruct((N_TOK, D), F32),
            jax.ShapeDtypeStruct((N_TOK, 128), F32),
            jax.ShapeDtypeStruct((1, 128), F32),
        ],
        scratch_shapes=[pltpu.VMEM((1, 128), F32)],
        compiler_params=_cparams(("arbitrary",)),
        name="moe_route",
    )(x, mod_l, wr_pad)


def _moe_plan(info, cnt):
    i32 = jnp.int32
    i1 = info[:, R_I1].astype(i32)
    i2 = info[:, R_I2].astype(i32)
    counts = cnt[0, :N_EXP].astype(i32)
    padded = (counts + MOE_TR - 1) // MOE_TR * MOE_TR
    seg_start = jnp.cumsum(padded) - padded
    pos1 = seg_start[i1] + info[:, R_RANK1].astype(i32)
    pos2 = seg_start[i2] + info[:, R_RANK2].astype(i32)
    tok = jnp.arange(N_TOK, dtype=i32)
    src = jnp.zeros((MOE_ROWS,), i32).at[pos1].set(tok).at[pos2].set(tok)
    dst = jnp.zeros((MOE_ROWS,), i32).at[pos1].set(tok).at[pos2].set(N_TOK + tok)
    n_pass = (padded + MOE_RS - 1) // MOE_RS
    pass_end = jnp.cumsum(n_pass)
    total = pass_end[-1]
    sidx = jnp.arange(MOE_NST, dtype=i32)
    used = sidx < total
    e_of = jnp.minimum(jnp.searchsorted(pass_end, jnp.minimum(sidx, total - 1), side="right"), N_EXP - 1).astype(i32)
    k = jnp.minimum(sidx, total - 1) - (pass_end - n_pass)[e_of]
    row0 = seg_start[e_of] + k * MOE_RS
    nrows = jnp.where(used, jnp.clip(padded[e_of] - k * MOE_RS, 0, MOE_RS), 0)
    nvalid = jnp.where(used, jnp.clip(counts[e_of] - k * MOE_RS, 0, MOE_RS), 0)
    return src, dst, e_of, row0.astype(i32), nrows.astype(i32), nvalid.astype(i32)


def _moe_expert_kernel(src_ref, dst_ref, exp_ref, row0_ref, nrows_ref, nvalid_ref,
                       h_hbm, wg_ref, wu_ref, wd_ref, yo_hbm, xs, xb, acc, wgb, wub, wdb, gsem, ssem):
    s = pl.program_id(0)
    f = pl.program_id(1)
    nf = pl.num_programs(1)
    nrows = pl.multiple_of(nrows_ref[s], MOE_TR)
    nvalid = nvalid_ref[s]
    row0 = row0_ref[s]
    n_chunks = nrows // MOE_TR

    def row_copy(src, dst, sem):
        return pltpu.make_async_copy(src, dst, sem)

    @pl.when((f == 0) & (nrows > 0))
    def _gather():
        def issue(r, carry):
            tok = src_ref[row0 + r]
            row_copy(h_hbm.at[pl.ds(tok, 1), :], xs.at[pl.ds(r, 1), :], gsem).start()
            return carry

        lax.fori_loop(0, nrows, issue, 0)
        row_copy(h_hbm.at[pl.ds(0, nrows), :], xs.at[pl.ds(0, nrows), :], gsem).wait()

        def cvt(c, carry):
            rs = pl.ds(pl.multiple_of(c * MOE_TR, MOE_TR), MOE_TR)
            xb[rs, :] = xs[rs, :].astype(BF16)
            return carry

        lax.fori_loop(0, n_chunks, cvt, 0)

    @pl.when(nrows > 0)
    def _compute():
        wgb[...] = wg_ref[...].astype(BF16)
        wub[...] = wu_ref[...].astype(BF16)
        wdb[...] = wd_ref[...].astype(BF16)

        def chunk(c, carry):
            rs = pl.ds(pl.multiple_of(c * MOE_TR, MOE_TR), MOE_TR)
            x = xb[rs, :]
            gate = _dg(x, wgb[...])
            up = _dg(x, wub[...])
            y = _dg((gate * _sigmoid(gate) * up).astype(BF16), wdb[...])

            @pl.when(f == 0)
            def _():
                acc[rs, :] = y

            @pl.when(f > 0)
            def _():
                acc[rs, :] += y

            return carry

        lax.fori_loop(0, n_chunks, chunk, 0)

    @pl.when((f == nf - 1) & (nvalid > 0))
    def _scatter():
        def issue(r, carry):
            d = dst_ref[row0 + r]
            row_copy(acc.at[pl.ds(r, 1), :], yo_hbm.at[pl.ds(d, 1), :], ssem).start()
            return carry

        lax.fori_loop(0, nvalid, issue, 0)
        n8 = pl.multiple_of(nvalid // 8 * 8, 8)

        @pl.when(n8 > 0)
        def _():
            row_copy(acc.at[pl.ds(0, n8), :], yo_hbm.at[pl.ds(0, n8), :], ssem).wait()

        def wait_one(r, carry):
            row_copy(acc.at[pl.ds(0, 1), :], yo_hbm.at[pl.ds(0, 1), :], ssem).wait()
            return carry

        lax.fori_loop(n8, nvalid, wait_one, 0)


def _moe_expert_call(h, plan, wg, wu, wd, i_moe):
    nf = D_FFE // MOE_TF

    def wspec(shape, fdim):
        def imap(s, f, src, dst, exp, row0, nrows, nvalid):
            fe = jnp.where(nrows[s] > 0, f, nf - 1)
            return (i_moe, exp[s], 0, fe) if fdim == 3 else (i_moe, exp[s], fe, 0)
        return pl.BlockSpec(shape, imap)

    grid_spec = pltpu.PrefetchScalarGridSpec(
        num_scalar_prefetch=6,
        grid=(MOE_NST, nf),
        in_specs=[
            pl.BlockSpec(memory_space=pl.ANY),
            wspec((None, None, D, MOE_TF), 3),
            wspec((None, None, D, MOE_TF), 3),
            wspec((None, None, MOE_TF, D), 2),
        ],
        out_specs=pl.BlockSpec(memory_space=pl.ANY),
        scratch_shapes=[
            pltpu.VMEM((MOE_RS, D), F32),
            pltpu.VMEM((MOE_RS, D), BF16),
            pltpu.VMEM((MOE_RS, D), F32),
            pltpu.VMEM((D, MOE_TF), BF16), pltpu.VMEM((D, MOE_TF), BF16), pltpu.VMEM((MOE_TF, D), BF16),
            pltpu.SemaphoreType.DMA(()), pltpu.SemaphoreType.DMA(()),
        ],
    )
    return pl.pallas_call(
        _moe_expert_kernel,
        grid_spec=grid_spec,
        out_shape=jax.ShapeDtypeStruct((2 * N_TOK, D), F32),
        compiler_params=pltpu.CompilerParams(dimension_semantics=("arbitrary", "arbitrary"),
                                             vmem_limit_bytes=VMEM_LIMIT, disable_bounds_checks=True),
        name="moe_experts",
    )(*plan, h, wg, wu, wd)


def _moe_combine_kernel(x_ref, y1_ref, y2_ref, info_ref, mod_ref, lg_ref, lb_ref, o_ref):
    f = info_ref[:, R_W1:R_W1 + 1] * y1_ref[...] + info_ref[:, R_W2:R_W2 + 1] * y2_ref[...]
    y = DN_ALPHA * x_ref[...] + mod_ref[5:6, :] * f
    o_ref[...] = _layer_norm(y, lg_ref[...], lb_ref[...])


def _moe_combine_call(x, yo, info, mod_l, lg, lb, l):
    tm = MOE_TM
    nt = N_TOK // tm
    vspec = pl.BlockSpec((None, 1, D), lambda i: (l, 0, 0))
    return pl.pallas_call(
        _moe_combine_kernel,
        grid=(nt,),
        in_specs=[
            pl.BlockSpec((tm, D), lambda i: (i, 0)),
            pl.BlockSpec((tm, D), lambda i: (i, 0)),
            pl.BlockSpec((tm, D), lambda i: (nt + i, 0)),
            pl.BlockSpec((tm, 128), lambda i: (i, 0)),
            pl.BlockSpec((None, 6, D), lambda i: (_group_of_tile(i, tm), 0, 0)),
            vspec, vspec,
        ],
        out_specs=pl.BlockSpec((tm, D), lambda i: (i, 0)),
        out_shape=jax.ShapeDtypeStruct((N_TOK, D), F32),
        compiler_params=_cparams(("arbitrary",)),
        name="moe_combine",
    )(x, yo, yo, info, mod_l, lg.reshape(DEPTH, 1, D), lb.reshape(DEPTH, 1, D))


def _moe_call(x, mod_l, wr_pad, wg, wu, wd, lg, lb, l, i_moe):
    h, info, cnt = _moe_route_call(x, mod_l, wr_pad, i_moe)
    yo = _moe_expert_call(h, _moe_plan(info, cnt), wg, wu, wd, i_moe)
    return _moe_combine_call(x, yo, info, mod_l, lg, lb, l)


def kernel(x_prompt, x_sample, cache_attn_k, cache_attn_v, state_rwkv, state_gla, c, c_ctx, w_ada, b_ada, w_in,
           attn_sink, rwkv_w0, rwkv_w_a, rwkv_w_b, rwkv_a0, rwkv_a_a, rwkv_a_b, rwkv_g_a, rwkv_g_b, rwkv_k_k,
           rwkv_k_a, rwkv_r_k, rwkv_ln_g, rwkv_ln_b, gla_gate_a, gla_gate_b, gla_gate_bias, gla_norm_g, w_up_a,
           w_up_b, w_up_c, w_out, ln1_g, ln1_b, ln2_g, ln2_b, ffn_w_gate, ffn_w_up, ffn_w_down, moe_router,
           moe_w_gate, moe_w_up, moe_w_down):
    cvec = jnp.concatenate([c_ctx[None, :], c, jnp.zeros((N_GROUPS - 1 - N_DEC_B, D), F32)], axis=0)
    mods = _ada_call(cvec, w_ada, b_ada).reshape(DEPTH, N_GROUPS, 6, D)
    x = jnp.concatenate([x_prompt.reshape(N_CTX_TOK, D), x_sample.reshape(N_DEC_B * DEC_T, D)], axis=0)
    cos, sin = _rope_tables()
    kc_all = cache_attn_k.reshape(N_DEC_B, DEPTH, PAST, A_KV * A_DH)
    vc_all = cache_attn_v.reshape(N_DEC_B, DEPTH, PAST, A_KV * A_DH)
    zeros_r = jnp.zeros((N_CTX_B, 2, B_HEADS, B_DH, B_DH), F32)
    zeros_g = jnp.zeros((N_CTX_B, 2, C_HEADS, C_DK, C_DV), F32)

    new_k, new_v, new_sr, new_sg = [], [], [], []
    for l in range(DEPTH):
        mod_l = mods[l]
        w_lr = jnp.concatenate(
            [rwkv_w_a[l, 0], rwkv_w_a[l, 1], rwkv_a_a[l, 0], rwkv_a_a[l, 1], rwkv_g_a[l],
             gla_gate_a[l, 0], gla_gate_a[l, 1], jnp.zeros((D, LR_W - 416), F32)], axis=1)
        pm = _inproj_call(x, mod_l, w_in, lambda tn: pl.BlockSpec((None, D, tn), lambda j, i: (l, 0, j)),
                          IN_MAIN, 768, "inproj_main")
        gates = _inproj_call(x, mod_l, w_in,
                             lambda tn: pl.BlockSpec((None, D, tn), lambda j, i: (l, 0, j + IN_MAIN // tn)),
                             IN_GATE, 768, "inproj_gate")
        plr = _inproj_call(x, mod_l, w_lr, lambda tn: pl.BlockSpec((D, tn), lambda j, i: (0, j)),
                           LR_W, LR_W, "inproj_lr")

        sink_l = attn_sink[l]
        oa = jnp.concatenate([_attn_ctx_call(sink_l, pm),
                              _attn_lat_call(sink_l, pm, kc_all, vc_all, l, cos, sin)], axis=0)

        rprm = (rwkv_w0[l].reshape(2, 1, B_W), rwkv_w_b[l], rwkv_a0[l].reshape(2, 1, B_W), rwkv_a_b[l],
                rwkv_g_b[l], rwkv_k_k[l].reshape(1, B_W), rwkv_k_a[l].reshape(1, B_W),
                rwkv_r_k[l].reshape(1, B_W), rwkv_ln_g[l].reshape(1, B_W), rwkv_ln_b[l].reshape(1, B_W))
        ob_c, sr_c = _rwkv_call(CTX_T, 4, N_CTX_B, 0, pm, plr, zeros_r, rprm, "rwkv_ctx")
        ob_d, _ = _rwkv_call(DEC_T, 2, N_DEC_B, N_CTX_TOK, pm, plr, state_rwkv[:, l], rprm, "rwkv_lat")
        ob = jnp.concatenate([ob_c, ob_d], axis=0)

        gb_pad = jnp.zeros((2, 128, C_KW), F32)
        gb_pad = gb_pad.at[0, 0:C_GATE_RANK].set(gla_gate_b[l, 0]).at[1, C_GATE_RANK:2 * C_GATE_RANK].set(
            gla_gate_b[l, 1])
        gbias = gla_gate_bias[l].reshape(2, 1, C_KW)
        ng = jnp.tile(gla_norm_g[l], C_HEADS).reshape(1, C_VW)
        oc_c, sg_c = _gla_call(CTX_T, N_CTX_B, 0, pm, plr, zeros_g, gb_pad, gbias, ng, "gla_ctx")
        oc_d, _ = _gla_call(DEC_T, N_DEC_B, N_CTX_TOK, pm, plr, state_gla[:, l], gb_pad, gbias, ng, "gla_lat")
        oc = jnp.concatenate([oc_c, oc_d], axis=0)

        x = _merge_call(x, gates, oa, ob, oc, mod_l, w_up_a, w_up_b, w_up_c, w_out, ln1_g, ln1_b, l)
        if l % 2 == 0:
            x = _ffn_call(x, mod_l, ffn_w_gate, ffn_w_up, ffn_w_down, ln2_g, ln2_b, l, l // 2)
        else:
            wr_pad = jnp.concatenate([moe_router, jnp.zeros((moe_router.shape[0], D, 128 - N_EXP), F32)], axis=2)
            x = _moe_call(x, mod_l, wr_pad, moe_w_gate, moe_w_up, moe_w_down, ln2_g, ln2_b, l, l // 2)

        new_k.append(pm[:N_CTX_TOK, 512:640].reshape(N_CTX_B, CTX_T, A_KV, A_DH))
        new_v.append(pm[:N_CTX_TOK, 640:768].reshape(N_CTX_B, CTX_T, A_KV, A_DH))
        new_sr.append(sr_c)
        new_sg.append(sg_c)

    y_prompt = x[:N_CTX_TOK].reshape(N_CTX_B, CTX_T, D)
    y_sample = x[N_CTX_TOK:].reshape(N_DEC_B, DEC_T, D)
    return (y_prompt, y_sample, jnp.stack(new_k, axis=1), jnp.stack(new_v, axis=1),
            jnp.stack(new_sr, axis=1), jnp.stack(new_sg, axis=1))
```

```python
import functools

import numpy as np
import jax
import jax.numpy as jnp
from jax import lax
from jax.experimental import pallas as pl
from jax.experimental.pallas import tpu as pltpu

D = 1024
N_CTX_B, CTX_T = 32, 256
N_DEC_B, DEC_T = 2, 1024
N_CTX_TOK = N_CTX_B * CTX_T
N_TOK = N_CTX_TOK + N_DEC_B * DEC_T
DEPTH = 2
PAST = 256
GRID_W = 64
A_HEADS, A_KV, A_DH = 8, 2, 64
A_G = A_HEADS // A_KV
A_WIN, A_BLK = 128, 128
ROPE_BASE = 10000.0
B_HEADS, B_DH = 4, 64
B_W = B_HEADS * B_DH
B_GN_EPS = 64e-5
C_HEADS, C_DK, C_DV = 4, 32, 64
C_KW, C_VW = C_HEADS * C_DK, C_HEADS * C_DV
C_GATE_RANK = 16
C_GATE_NORM = 16.0
C_CHUNK = 64
C_SUB = 16
D_FF = 2816
N_EXP = 8
D_FFE = 3584
LN_EPS = 1e-5
DN_ALPHA = (2.0 * DEPTH) ** 0.25
NEG_INF = -1e30
IN_MAIN = 2304
IN_GATE = 3072
LR_W = 512
N_GROUPS = 8

F32 = jnp.float32
BF16 = jnp.bfloat16
VMEM_LIMIT = 56 * 1024 * 1024

NN = ((1,), (0,))
NT = ((1,), (1,))
TN = ((0,), (0,))


def _dg(a, b, dims=NN):
    return lax.dot_general(a, b, (dims, ((), ())), preferred_element_type=F32)


def _split2(x):
    hi = x.astype(BF16)
    lo = (x - hi.astype(F32)).astype(BF16)
    return hi, lo


def _split3(x):
    hi = x.astype(BF16)
    r = x - hi.astype(F32)
    mid = r.astype(BF16)
    lo = (r - mid.astype(F32)).astype(BF16)
    return hi, mid, lo


def _mm(a, b, dims=NN):
    return _dg(a.astype(BF16), b.astype(BF16), dims)


def _mm3(a, b, dims=NN):
    ah, al = _split2(a)
    bh, bl = _split2(b)
    return _dg(ah, bh, dims) + (_dg(ah, bl, dims) + _dg(al, bh, dims))


def _mm_xr(a, b_exact, passes, dims=NN):
    parts = (a.astype(BF16),) if passes == 1 else (_split2(a) if passes == 2 else _split3(a))
    out = _dg(parts[0], b_exact, dims)
    for p in parts[1:]:
        out = out + _dg(p, b_exact, dims)
    return out


def _mm_xl(a_exact, b, passes):
    parts = (b.astype(BF16),) if passes == 1 else (_split2(b) if passes == 2 else _split3(b))
    out = _dg(a_exact, parts[0])
    for p in parts[1:]:
        out = out + _dg(a_exact, p)
    return out


def _sigmoid(x):
    return 1.0 / (1.0 + jnp.exp(-x))


def _iota(shape, dim):
    return lax.broadcasted_iota(jnp.int32, shape, dim)


def _block_ones(n, blk):
    return (_iota((n, n), 0) // blk == _iota((n, n), 1) // blk).astype(BF16)


def _layer_norm(y, g, b):
    mu = jnp.mean(y, -1, keepdims=True)
    yc = y - mu
    var = jnp.mean(yc * yc, -1, keepdims=True)
    return yc * lax.rsqrt(var + LN_EPS) * g + b


def _group_of_tile(i, tm):
    n_ctx = N_CTX_TOK // tm
    per_dec = DEC_T // tm
    return jnp.where(i < n_ctx, 0, 1 + (i - n_ctx) // per_dec)


def _cparams(sem):
    return pltpu.CompilerParams(dimension_semantics=sem, vmem_limit_bytes=VMEM_LIMIT)


def _ada_kernel(c_ref, w_ref, b_ref, o_ref):
    c = c_ref[...]
    s = c * _sigmoid(c)
    o_ref[...] = _mm3(s, w_ref[...]) + b_ref[...]


def _ada_call(cvec, w_ada, b_ada):
    tn = 1536
    return pl.pallas_call(
        _ada_kernel,
        grid=(DEPTH, 6 * D // tn),
        in_specs=[
            pl.BlockSpec((N_GROUPS, D), lambda l, j: (0, 0)),
            pl.BlockSpec((None, D, tn), lambda l, j: (l, 0, j)),
            pl.BlockSpec((None, 1, tn), lambda l, j: (l, 0, j)),
        ],
        out_specs=pl.BlockSpec((None, N_GROUPS, tn), lambda l, j: (l, 0, j)),
        out_shape=jax.ShapeDtypeStruct((DEPTH, N_GROUPS, 6 * D), F32),
        compiler_params=_cparams(("arbitrary", "arbitrary")),
        name="ada",
    )(cvec, w_ada, b_ada.reshape(DEPTH, 1, 6 * D))


def _inproj_kernel(x_ref, mod_ref, wm_ref, wg_ref, wl_ref, om_ref, og_ref, ol_ref):
    sh = mod_ref[0:1, :]
    sc = mod_ref[1:2, :]
    h = (x_ref[...] * (1.0 + sc) + sh).astype(BF16)
    om_ref[...] = _dg(h, wm_ref[...])
    og_ref[...] = _dg(h, wg_ref[...])
    ol_ref[...] = _dg(h, wl_ref[...])


def _resident(shape):
    return pl.BlockSpec(shape, lambda *_: (0,) * len(shape), pipeline_mode=pl.Buffered(1))


def _inproj_call(x, mod_l, w_main, w_gate, w_lr):
    tm = 256
    return pl.pallas_call(
        _inproj_kernel,
        grid=(N_TOK // tm,),
        in_specs=[
            pl.BlockSpec((tm, D), lambda i: (i, 0)),
            pl.BlockSpec((None, 6, D), lambda i: (_group_of_tile(i, tm), 0, 0)),
            _resident((D, IN_MAIN)), _resident((D, IN_GATE)), _resident((D, LR_W)),
        ],
        out_specs=[
            pl.BlockSpec((tm, IN_MAIN), lambda i: (i, 0)),
            pl.BlockSpec((tm, IN_GATE), lambda i: (i, 0)),
            pl.BlockSpec((tm, LR_W), lambda i: (i, 0)),
        ],
        out_shape=[
            jax.ShapeDtypeStruct((N_TOK, IN_MAIN), F32),
            jax.ShapeDtypeStruct((N_TOK, IN_GATE), F32),
            jax.ShapeDtypeStruct((N_TOK, LR_W), F32),
        ],
        compiler_params=_cparams(("arbitrary",)),
        name="inproj",
    )(x, mod_l, w_main, w_gate, w_lr)


def _sink_col(sink_ref, kvh, rows_per_head):
    n = A_G * rows_per_head
    r = _iota((n, 1), 0) // rows_per_head
    col = jnp.full((n, 1), sink_ref[kvh * A_G], F32)
    for g in range(1, A_G):
        col = jnp.where(r == g, sink_ref[kvh * A_G + g], col)
    return col


def _attn_ctx_kernel(sink_ref, q_ref, k_ref, v_ref, o_ref):
    scale = A_DH ** -0.5
    for kvh in range(A_KV):
        ks = k_ref[:, kvh * A_DH:(kvh + 1) * A_DH].astype(BF16)
        vs = v_ref[:, kvh * A_DH:(kvh + 1) * A_DH].astype(BF16)
        q4 = jnp.concatenate(
            [q_ref[:, (kvh * A_G + g) * A_DH:(kvh * A_G + g + 1) * A_DH] for g in range(A_G)], axis=0)
        s = _dg(q4.astype(BF16), ks, NT) * scale
        sink = _sink_col(sink_ref, kvh, CTX_T)
        m = jnp.maximum(jnp.max(s, -1, keepdims=True), sink)
        e = jnp.exp(s - m)
        p = e / (jnp.sum(e, -1, keepdims=True) + jnp.exp(sink - m))
        o = _dg(p.astype(BF16), vs)
        for g in range(A_G):
            h = kvh * A_G + g
            o_ref[:, h * A_DH:(h + 1) * A_DH] = o[g * CTX_T:(g + 1) * CTX_T, :]


def _attn_ctx_call(sink_l, pm):
    return pl.pallas_call(
        _attn_ctx_kernel,
        grid=(N_CTX_B,),
        in_specs=[
            pl.BlockSpec(memory_space=pltpu.SMEM),
            pl.BlockSpec((CTX_T, 512), lambda b: (b, 0)),
            pl.BlockSpec((CTX_T, 128), lambda b: (b, 4)),
            pl.BlockSpec((CTX_T, 128), lambda b: (b, 5)),
        ],
        out_specs=pl.BlockSpec((CTX_T, 512), lambda b: (b, 0)),
        out_shape=jax.ShapeDtypeStruct((N_CTX_TOK, 512), F32),
        compiler_params=_cparams(("arbitrary",)),
        name="attn_ctx",
    )(sink_l, pm, pm, pm)


def _rope(x, cos, sin_signed):
    w = x.shape[-1]
    lane = _iota(x.shape, 1)
    partner = jnp.where((lane % 32) < 16, pltpu.roll(x, w - 16, 1), pltpu.roll(x, 16, 1))
    return x * cos + partner * sin_signed


def _attn_lat_kernel(sink_ref, q_ref, k_ref, v_ref, kc_ref, vc_ref, cos_ref, sin_ref, o_ref, kr_ref):
    n = pl.program_id(1)
    scale = A_DH ** -0.5

    @pl.when(n == 0)
    def _():
        kr_ref[...] = _rope(k_ref[...], cos_ref[:, 0:128], sin_ref[:, 0:128]).astype(BF16)

    q0 = pl.multiple_of(n * A_BLK, A_BLK)
    qr = _rope(q_ref[...], cos_ref[pl.ds(q0, A_BLK), :], sin_ref[pl.ds(q0, A_BLK), :])
    kstart = pl.multiple_of(jnp.clip((n - 1) * A_BLK, 0, DEC_T - 3 * A_BLK), A_BLK)
    kwin = kr_ref[pl.ds(kstart, 3 * A_BLK), :]
    vwin = v_ref[pl.ds(kstart, 3 * A_BLK), :].astype(BF16)
    kc = kc_ref[...].astype(BF16)
    vc = vc_ref[...].astype(BF16)
    rows = A_G * A_BLK
    qpos = q0 + _iota((rows, 3 * A_BLK), 0) % A_BLK
    kpos = kstart + _iota((rows, 3 * A_BLK), 1)
    valid = jnp.abs(qpos - kpos) <= A_WIN
    for kvh in range(A_KV):
        cs = slice(kvh * A_DH, (kvh + 1) * A_DH)
        q4 = jnp.concatenate(
            [qr[:, (kvh * A_G + g) * A_DH:(kvh * A_G + g + 1) * A_DH] for g in range(A_G)], axis=0).astype(BF16)
        s_loc = jnp.where(valid, _dg(q4, kwin[:, cs], NT) * scale, NEG_INF)
        s_ctx = _dg(q4, kc[:, cs], NT) * scale
        sink = _sink_col(sink_ref, kvh, A_BLK)
        m = jnp.maximum(jnp.maximum(jnp.max(s_loc, -1, keepdims=True), jnp.max(s_ctx, -1, keepdims=True)), sink)
        e_loc = jnp.exp(s_loc - m)
        e_ctx = jnp.exp(s_ctx - m)
        inv = 1.0 / (jnp.sum(e_loc, -1, keepdims=True) + jnp.sum(e_ctx, -1, keepdims=True) + jnp.exp(sink - m))
        o = _dg((e_loc * inv).astype(BF16), vwin[:, cs]) + _dg((e_ctx * inv).astype(BF16), vc[:, cs])
        for g in range(A_G):
            h = kvh * A_G + g
            o_ref[:, h * A_DH:(h + 1) * A_DH] = o[g * A_BLK:(g + 1) * A_BLK, :]


def _rope_tables():
    half = A_DH // 2
    t = np.arange(DEC_T)
    rows = (t // GRID_W).astype(np.float32)
    cols = (t % GRID_W).astype(np.float32)
    inv_freq = (ROPE_BASE ** (-np.arange(0, half, 2, dtype=np.float32) / half)).astype(np.float32)
    ang_r = rows[:, None] * inv_freq[None, :]
    ang_c = cols[:, None] * inv_freq[None, :]
    cos = np.concatenate([np.cos(ang_r), np.cos(ang_r), np.cos(ang_c), np.cos(ang_c)], -1)
    sin = np.concatenate([-np.sin(ang_r), np.sin(ang_r), -np.sin(ang_c), np.sin(ang_c)], -1)
    return (jnp.asarray(np.tile(cos, (1, A_HEADS)), F32), jnp.asarray(np.tile(sin, (1, A_HEADS)), F32))


def _attn_lat_call(sink_l, pm, kc, vc, l, cos, sin):
    nb = DEC_T // A_BLK
    row0 = N_CTX_TOK // A_BLK
    seq0 = N_CTX_TOK // DEC_T
    return pl.pallas_call(
        _attn_lat_kernel,
        grid=(N_DEC_B, nb),
        in_specs=[
            pl.BlockSpec(memory_space=pltpu.SMEM),
            pl.BlockSpec((A_BLK, 512), lambda b, n: (row0 + b * nb + n, 0)),
            pl.BlockSpec((DEC_T, 128), lambda b, n: (seq0 + b, 4)),
            pl.BlockSpec((DEC_T, 128), lambda b, n: (seq0 + b, 5)),
            pl.BlockSpec((None, None, PAST, 128), lambda b, n: (b, l, 0, 0)),
            pl.BlockSpec((None, None, PAST, 128), lambda b, n: (b, l, 0, 0)),
            pl.BlockSpec((DEC_T, 512), lambda b, n: (0, 0)),
            pl.BlockSpec((DEC_T, 512), lambda b, n: (0, 0)),
        ],
        out_specs=pl.BlockSpec((A_BLK, 512), lambda b, n: (b * nb + n, 0)),
        out_shape=jax.ShapeDtypeStruct((N_DEC_B * DEC_T, 512), F32),
        scratch_shapes=[pltpu.VMEM((DEC_T, 128), BF16)],
        compiler_params=_cparams(("arbitrary", "arbitrary")),
        name="attn_lat",
    )(sink_l, pm, pm, pm, kc, vc, cos, sin)


def _rwkv_kernel(T, NS, r_ref, k_ref, v_ref, lr_ref, s0_ref, w0_ref, wb_ref, a0_ref, ab_ref, gb_ref,
                 kk_ref, ka_ref, rk_ref, lng_ref, lnb_ref, o_ref, sfin_ref,
                 KK, W, WRP, AKK, KT, VC2, BON, Y, S):
    ones4 = _block_ones(B_W, B_DH)
    decay_c = float(np.exp(-0.5))
    RC = 256
    SUB = 8
    NP = 3

    def prep(c, carry):
        r0 = pl.multiple_of(c * RC, RC)
        rs = pl.ds(r0, RC)
        r = r_ref[rs, :]
        k = k_ref[rs, :]
        v = v_ref[rs, :]
        lr = lr_ref[rs, :]
        kkr = k * kk_ref[...]
        kk = kkr * lax.rsqrt(_mm_xr(kkr * kkr, ones4, 3) + 1e-12)
        KK[rs, :] = kk
        bonus = jnp.zeros((RC, B_W), F32)
        vc2 = jnp.zeros((RC, B_W), F32)
        for d in range(2):
            z = w0_ref[d] + _mm(jnp.tanh(lr[:, 64 * d:64 * d + 64]), wb_ref[d])
            w = jnp.exp(-decay_c * _sigmoid(z))
            a = _sigmoid(a0_ref[d] + _mm(lr[:, 128 + 64 * d:192 + 64 * d], ab_ref[d]))
            kt = k * (1.0 + (a - 1.0) * ka_ref[...])
            akk = a * kk
            W[d, rs, :] = w
            WRP[d, rs, :] = w * r - _mm_xr(akk * r, ones4, 3) * kk
            AKK[d, rs, :] = akk
            KT[d, rs, :] = kt
            vc2 = vc2 + _mm_xr(kt * r, ones4, 3) * v
            bonus = bonus + _mm_xr(r * kt * rk_ref[...], ones4, 3) * v
        VC2[rs, :] = vc2
        BON[rs, :] = bonus
        return carry

    lax.fori_loop(0, NS * T // RC, prep, 0)

    chains = [(s, d) for s in range(NS) for d in range(2)]
    for s, d in chains:
        S[s, d] = jnp.concatenate([s0_ref[s, d, h] for h in range(B_HEADS)], axis=1)

    eye4 = _iota((B_DH, B_W), 0) == (_iota((B_DH, B_W), 1) % B_DH)

    def steps(i, carry):
        tiles = []
        for s, d in chains:
            t0 = pl.multiple_of(s * T + (i * SUB if d == 0 else T - SUB - i * SUB), SUB)
            rows = pl.ds(t0, SUB)
            tiles.append((rows, KK[rows, :], WRP[d, rows, :], v_ref[rows, :],
                          W[d, rows, :], AKK[d, rows, :], KT[d, rows, :]))
        ys = [[None] * SUB for _ in chains]
        for jj in range(SUB):
            lhs = []
            for (s, d), (_, kk8, wrp8, v8, _, _, _) in zip(chains, tiles):
                j = jj if d == 0 else SUB - 1 - jj
                r = slice(j, j + 1)
                st = S[s, d]
                lhs += [(st * kk8[r]).astype(BF16), (st * wrp8[r]).astype(BF16),
                        jnp.where(eye4, v8[r], 0.0).astype(BF16)]
            res = _dg(jnp.concatenate(lhs, axis=0), ones4)
            for g, ((s, d), (_, _, _, _, w8, akk8, kt8)) in enumerate(zip(chains, tiles)):
                j = jj if d == 0 else SUB - 1 - jj
                r = slice(j, j + 1)
                sk, yp, vcol = [res[(g * NP + n) * B_DH:(g * NP + n + 1) * B_DH] for n in range(NP)]
                S[s, d] = S[s, d] * w8[r] - sk * akk8[r] + vcol * kt8[r]
                ys[g][j] = jnp.sum(jnp.where(eye4, yp, 0.0), axis=0, keepdims=True)
        for g, ((s, d), tl) in enumerate(zip(chains, tiles)):
            Y[d, tl[0], :] = jnp.concatenate(ys[g], axis=0)
        return carry

    lax.fori_loop(0, T // SUB, steps, 0)

    for s, d in chains:
        st = S[s, d]
        for h in range(B_HEADS):
            sfin_ref[s, d, h] = st[:, h * B_DH:(h + 1) * B_DH]

    def post(c, carry):
        r0 = pl.multiple_of(c * RC, RC)
        rs = pl.ds(r0, RC)
        y = Y[0, rs, :] + Y[1, rs, :] + VC2[rs, :]
        mu = _mm_xr(y, ones4, 3) * (1.0 / B_DH)
        yc = y - mu
        var = _mm_xr(yc * yc, ones4, 3) * (1.0 / B_DH)
        yn = yc * lax.rsqrt(var + B_GN_EPS) * lng_ref[...] + lnb_ref[...] + BON[rs, :]
        g = _mm(_sigmoid(lr_ref[rs, 256:384]), gb_ref[...])
        o_ref[rs, :] = yn * g
        return carry

    lax.fori_loop(0, NS * T // RC, post, 0)


def _rwkv_call(T, NS, n_seq, tok0, pm, plr, s0, prm, name):
    rows = NS * T
    blk0 = tok0 // rows
    n_steps = n_seq // NS
    full = lambda shape: pl.BlockSpec(shape, lambda b: (0,) * len(shape))
    big = lambda shape, imap: (pl.BlockSpec(shape, imap, pipeline_mode=pl.Buffered(1)) if n_steps == 1
                               else pl.BlockSpec(shape, imap))
    kern = functools.partial(_rwkv_kernel, T, NS)
    return pl.pallas_call(
        kern,
        grid=(n_steps,),
        in_specs=[
            big((rows, B_W), lambda b: (blk0 + b, 3)),
            big((rows, B_W), lambda b: (blk0 + b, 4)),
            big((rows, B_W), lambda b: (blk0 + b, 5)),
            big((rows, LR_W), lambda b: (blk0 + b, 0)),
            pl.BlockSpec((NS, 2, B_HEADS, B_DH, B_DH), lambda b: (b, 0, 0, 0, 0)),
            full((2, 1, B_W)), full((2, 64, B_W)), full((2, 1, B_W)), full((2, 64, B_W)), full((128, B_W)),
            full((1, B_W)), full((1, B_W)), full((1, B_W)), full((1, B_W)), full((1, B_W)),
        ],
        out_specs=[
            big((rows, B_W), lambda b: (b, 0)),
            pl.BlockSpec((NS, 2, B_HEADS, B_DH, B_DH), lambda b: (b, 0, 0, 0, 0)),
        ],
        out_shape=[
            jax.ShapeDtypeStruct((n_seq * T, B_W), F32),
            jax.ShapeDtypeStruct((n_seq, 2, B_HEADS, B_DH, B_DH), F32),
        ],
        scratch_shapes=[
            pltpu.VMEM((rows, B_W), F32),
            pltpu.VMEM((2, rows, B_W), F32),
            pltpu.VMEM((2, rows, B_W), F32),
            pltpu.VMEM((2, rows, B_W), F32),
            pltpu.VMEM((2, rows, B_W), F32),
            pltpu.VMEM((rows, B_W), F32),
            pltpu.VMEM((rows, B_W), F32),
            pltpu.VMEM((2, rows, B_W), F32),
            pltpu.VMEM((NS, 2, B_DH, B_W), F32),
        ],
        compiler_params=_cparams(("arbitrary",)),
        name=name,
    )(pm, pm, pm, plr, s0, *prm)


def _gla_kernel(T, NS, q_ref, k_ref, v_ref, og_ref, lr_ref, s0_ref, gb_ref, bias_ref, ng_ref,
                o_ref, sfin_ref, LA, O, S):
    n_chunks = T // C_CHUNK
    nsub = C_CHUNK // C_SUB
    qscale = C_DK ** -0.5
    lr = lr_ref[...]
    for d in range(2):
        gl = _mm(lr, gb_ref[d]) + bias_ref[d]
        LA[d] = (jnp.minimum(gl, 0.0) - jnp.log(1.0 + jnp.exp(-jnp.abs(gl)))) * (1.0 / C_GATE_NORM)
    bd_state = _iota((C_KW, C_VW), 0) // C_DK == _iota((C_KW, C_VW), 1) // C_DV
    chains = [(s, d) for s in range(NS) for d in range(2)]
    for s, d in chains:
        for h in range(C_HEADS):
            pad_l = h * C_DV
            pad_r = C_VW - (h + 1) * C_DV
            blk = s0_ref[s, d, h]
            parts = ([jnp.zeros((C_DK, pad_l), F32)] if pad_l else []) + [blk] + \
                    ([jnp.zeros((C_DK, pad_r), F32)] if pad_r else [])
            S[s, d, h * C_DK:(h + 1) * C_DK, :] = jnp.concatenate(parts, axis=1)

    ti = _iota((C_CHUNK, C_CHUNK), 0)
    si = _iota((C_CHUNK, C_CHUNK), 1)
    tri = ((si <= ti).astype(BF16), (si >= ti).astype(BF16))
    trow = _iota((C_CHUNK, 1), 0)
    mask_k = _iota((C_CHUNK, C_KW), 0) // C_SUB == _iota((C_CHUNK, C_KW), 1) // C_DK
    mask_v = _iota((C_CHUNK, C_VW), 0) // C_SUB == _iota((C_CHUNK, C_VW), 1) // C_DV
    t_att = _iota((C_CHUNK, C_CHUNK), 0)
    s_att = _iota((C_CHUNK, C_CHUNK), 1) % C_SUB
    eye_k = _iota((C_KW, C_KW), 0) == _iota((C_KW, C_KW), 1)

    def body(c, carry):
        cx = []
        for s, d in chains:
            cc = c if d == 0 else n_chunks - 1 - c
            rs = pl.ds(pl.multiple_of(s * T + cc * C_CHUNK, C_CHUNK), C_CHUNK)
            b = _mm_xl(tri[d], LA[d, rs, :], 3)
            cx.append(dict(s=s, d=d, rs=rs, b=b, q=q_ref[rs, :] * qscale, k=k_ref[rs, :], v=v_ref[rs, :]))
        for x in cx:
            x["o"] = _mm3(x["q"] * jnp.exp(x["b"]), S[x["s"], x["d"]])
        for j in range(nsub):
            lo, hi = j * C_SUB, (j + 1) * C_SUB
            for x in cx:
                b, q, k = x["b"], x["q"], x["k"]
                if x["d"] == 0:
                    gamma = b[hi - 1:hi, :]
                    row_ok = trow >= lo
                    att_ok = t_att >= lo + s_att
                else:
                    gamma = b[lo:lo + 1, :]
                    row_ok = trow < hi
                    att_ok = t_att <= lo + s_att
                qj = q * jnp.exp(jnp.where(row_ok, b - gamma, NEG_INF))
                kj = k[lo:hi, :] * jnp.exp(gamma - b[lo:hi, :])
                kbd = jnp.where(mask_k, jnp.concatenate([kj] * C_HEADS, axis=0), 0.0)
                x["att"] = jnp.where(att_ok, _mm(qj, kbd, NT), 0.0)
            for x in cx:
                vbd = jnp.where(mask_v, jnp.concatenate([x["v"][lo:hi, :]] * C_HEADS, axis=0), 0.0)
                x["o"] = x["o"] + _mm(x["att"], vbd)
        for x in cx:
            s, d, b = x["s"], x["d"], x["b"]
            O[d, x["rs"], :] = x["o"]
            blast = b[C_CHUNK - 1:C_CHUNK, :] if d == 0 else b[0:1, :]
            kl = x["k"] * jnp.exp(blast - b)
            upd = jnp.where(bd_state, _mm3(kl.T, x["v"]), 0.0)
            dec = jnp.where(eye_k, jnp.exp(blast), 0.0)
            S[s, d] = _mm3(dec, S[s, d]) + upd
        return carry

    lax.fori_loop(0, n_chunks, body, 0)

    for s, d in chains:
        st = S[s, d]
        for h in range(C_HEADS):
            sfin_ref[s, d, h] = st[h * C_DK:(h + 1) * C_DK, h * C_DV:(h + 1) * C_DV]

    ones4 = _block_ones(C_VW, C_DV)
    o = O[0] + O[1]
    ms = _mm_xr(o * o, ones4, 3) * (1.0 / C_DV)
    og = og_ref[...]
    o_ref[...] = o * lax.rsqrt(ms + LN_EPS) * ng_ref[...] * (og * _sigmoid(og))


def _gla_call(T, NS, n_seq, tok0, pm, plr, s0, gb_pad, bias, ng, name):
    rows = NS * T
    blk0 = tok0 // rows
    full = lambda shape: pl.BlockSpec(shape, lambda b: (0,) * len(shape))
    return pl.pallas_call(
        functools.partial(_gla_kernel, T, NS),
        grid=(n_seq // NS,),
        in_specs=[
            pl.BlockSpec((rows, C_KW), lambda b: (blk0 + b, 12)),
            pl.BlockSpec((rows, C_KW), lambda b: (blk0 + b, 13)),
            pl.BlockSpec((rows, C_VW), lambda b: (blk0 + b, 7)),
            pl.BlockSpec((rows, C_VW), lambda b: (blk0 + b, 8)),
            pl.BlockSpec((rows, 128), lambda b: (blk0 + b, 3)),
            pl.BlockSpec((NS, 2, C_HEADS, C_DK, C_DV), lambda b: (b, 0, 0, 0, 0)),
            full((2, 128, C_KW)), full((2, 1, C_KW)), full((1, C_VW)),
        ],
        out_specs=[
            pl.BlockSpec((rows, C_VW), lambda b: (b, 0)),
            pl.BlockSpec((NS, 2, C_HEADS, C_DK, C_DV), lambda b: (b, 0, 0, 0, 0)),
        ],
        out_shape=[
            jax.ShapeDtypeStruct((n_seq * T, C_VW), F32),
            jax.ShapeDtypeStruct((n_seq, 2, C_HEADS, C_DK, C_DV), F32),
        ],
        scratch_shapes=[
            pltpu.VMEM((2, rows, C_KW), F32),
            pltpu.VMEM((2, rows, C_VW), F32),
            pltpu.VMEM((NS, 2, C_KW, C_VW), F32),
        ],
        compiler_params=_cparams(("arbitrary",)),
        name=name,
    )(pm, pm, pm, pm, plr, s0, gb_pad, bias, ng)


MERGE_TM = 512


def _merge_kernel(x_ref, g_ref, oac_ref, oad_ref, obc_ref, obd_ref, occ_ref, ocd_ref, mod_ref,
                  wa_ref, wb_ref, wc_ref, wo_ref, lg_ref, lb_ref, o_ref):
    is_ctx = pl.program_id(0) < N_CTX_TOK // MERGE_TM
    oa = jnp.where(is_ctx, oac_ref[...], oad_ref[...]).astype(BF16)
    ob = jnp.where(is_ctx, obc_ref[...], obd_ref[...]).astype(BF16)
    oc = jnp.where(is_ctx, occ_ref[...], ocd_ref[...]).astype(BF16)
    merged = (_sigmoid(g_ref[:, 0:D]) * _dg(oa, wa_ref[...])
              + _sigmoid(g_ref[:, D:2 * D]) * _dg(ob, wb_ref[...])
              + _sigmoid(g_ref[:, 2 * D:3 * D]) * _dg(oc, wc_ref[...]))
    mix = _dg(merged.astype(BF16), wo_ref[...])
    y = DN_ALPHA * x_ref[...] + mod_ref[2:3, :] * mix
    o_ref[...] = _layer_norm(y, lg_ref[...], lb_ref[...])


def _merge_call(x, gates, oa, ob, oc, mod_l, wa, wb, wc, wo, lg, lb):
    tm = MERGE_TM
    n_ctx = N_CTX_TOK // tm
    pair = lambda w: [pl.BlockSpec((tm, w), lambda i: (jnp.minimum(i, n_ctx - 1), 0)),
                      pl.BlockSpec((tm, w), lambda i: (jnp.maximum(i - n_ctx, 0), 0))]
    return pl.pallas_call(
        _merge_kernel,
        grid=(N_TOK // tm,),
        in_specs=[
            pl.BlockSpec((tm, D), lambda i: (i, 0)),
            pl.BlockSpec((tm, IN_GATE), lambda i: (i, 0)),
            *pair(512), *pair(B_W), *pair(C_VW),
            pl.BlockSpec((None, 6, D), lambda i: (_group_of_tile(i, tm), 0, 0)),
            _resident((512, D)), _resident((B_W, D)), _resident((C_VW, D)), _resident((D, D)),
            _resident((1, D)), _resident((1, D)),
        ],
        out_specs=pl.BlockSpec((tm, D), lambda i: (i, 0)),
        out_shape=jax.ShapeDtypeStruct((N_TOK, D), F32),
        compiler_params=_cparams(("arbitrary",)),
        name="merge",
    )(x, gates, *oa, *ob, *oc, mod_l, wa, wb, wc, wo, lg.reshape(1, D), lb.reshape(1, D))


def _ffn_kernel(x_ref, mod_ref, wg_ref, wu_ref, wd_ref, lg_ref, lb_ref, o_ref):
    x = x_ref[...]
    h = (x * (1.0 + mod_ref[4:5, :]) + mod_ref[3:4, :]).astype(BF16)
    gate = _dg(h, wg_ref[...])
    up = _dg(h, wu_ref[...])
    f = _dg((gate * _sigmoid(gate) * up).astype(BF16), wd_ref[...])
    y = DN_ALPHA * x + mod_ref[5:6, :] * f
    o_ref[...] = _layer_norm(y, lg_ref[...], lb_ref[...])


def _ffn_call(x, mod_l, wg, wu, wd, lg, lb):
    tm = 512
    return pl.pallas_call(
        _ffn_kernel,
        grid=(N_TOK // tm,),
        in_specs=[
            pl.BlockSpec((tm, D), lambda i: (i, 0)),
            pl.BlockSpec((None, 6, D), lambda i: (_group_of_tile(i, tm), 0, 0)),
            _resident((D, D_FF)), _resident((D, D_FF)), _resident((D_FF, D)),
            _resident((1, D)), _resident((1, D)),
        ],
        out_specs=pl.BlockSpec((tm, D), lambda i: (i, 0)),
        out_shape=jax.ShapeDtypeStruct((N_TOK, D), F32),
        compiler_params=_cparams(("arbitrary",)),
        name="ffn",
    )(x, mod_l, wg, wu, wd, lg.reshape(1, D), lb.reshape(1, D))


MOE_TM = 1024
MOE_TR = 256
MOE_RS = 3072
MOE_TF = 512
MOE_ROWS = 2 * N_TOK + N_EXP * MOE_TR
MOE_NST = -(-MOE_ROWS // MOE_RS) + N_EXP
R_I1, R_I2, R_W1, R_W2, R_RANK1, R_RANK2 = range(6)


def _moe_route_kernel(x_ref, mod_ref, wr_ref, h_ref, info_ref, cnt_ref, carry_s):
    tm = MOE_TM

    @pl.when(pl.program_id(0) == 0)
    def _():
        carry_s[...] = jnp.zeros_like(carry_s)

    h = x_ref[...] * (1.0 + mod_ref[4:5, :]) + mod_ref[3:4, :]
    h_ref[...] = h
    logits = _mm3(h, wr_ref[...])
    lane = _iota(logits.shape, 1)
    logits = jnp.where(lane < N_EXP, logits, NEG_INF)
    v1 = jnp.max(logits, -1, keepdims=True)
    i1 = jnp.min(jnp.where(logits == v1, lane, 128), -1, keepdims=True)
    rest = jnp.where(lane == i1, NEG_INF, logits)
    v2 = jnp.max(rest, -1, keepdims=True)
    i2 = jnp.min(jnp.where(rest == v2, lane, 128), -1, keepdims=True)
    e2 = jnp.exp(v2 - v1)
    w1 = 1.0 / (1.0 + e2)
    w2 = e2 / (1.0 + e2)
    oh1 = lane == i1
    oh2 = lane == i2
    cnt = oh1.astype(F32) + oh2.astype(F32)
    earlier = (_iota((tm, tm), 1) < _iota((tm, tm), 0)).astype(BF16)
    before = _dg(earlier, cnt.astype(BF16)) + carry_s[...]
    rank1 = jnp.sum(jnp.where(oh1, before, 0.0), -1, keepdims=True)
    rank2 = jnp.sum(jnp.where(oh2, before, 0.0), -1, keepdims=True)
    info = jnp.zeros(logits.shape, F32)
    for ln, val in ((R_I1, i1.astype(F32)), (R_I2, i2.astype(F32)), (R_W1, w1), (R_W2, w2),
                    (R_RANK1, rank1), (R_RANK2, rank2)):
        info = jnp.where(lane == ln, val, info)
    info_ref[...] = info
    carry_s[...] += jnp.sum(cnt, axis=0, keepdims=True)
    cnt_ref[...] = carry_s[...]


def _moe_route_call(x, mod_l, wr_pad, i_moe):
    tm = MOE_TM
    return pl.pallas_call(
        _moe_route_kernel,
        grid=(N_TOK // tm,),
        in_specs=[
            pl.BlockSpec((tm, D), lambda i: (i, 0)),
            pl.BlockSpec((None, 6, D), lambda i: (_group_of_tile(i, tm), 0, 0)),
            pl.BlockSpec((None, D, 128), lambda i: (i_moe, 0, 0)),
        ],
        out_specs=[
            pl.BlockSpec((tm, D), lambda i: (i, 0)),
            pl.BlockSpec((tm, 128), lambda i: (i, 0)),
            pl.BlockSpec((1, 128), lambda i: (0, 0)),
        ],
        out_shape=[
            jax.ShapeDtypeStruct((N_TOK, D), F32),
            jax.ShapeDtypeStruct((N_TOK, 128), F32),
            jax.ShapeDtypeStruct((1, 128), F32),
        ],
        scratch_shapes=[pltpu.VMEM((1, 128), F32)],
        compiler_params=_cparams(("arbitrary",)),
        name="moe_route",
    )(x, mod_l, wr_pad)


def _moe_plan(info, cnt):
    i32 = jnp.int32
    i1 = info[:, R_I1].astype(i32)
    i2 = info[:, R_I2].astype(i32)
    counts = cnt[0, :N_EXP].astype(i32)
    padded = (counts + MOE_TR - 1) // MOE_TR * MOE_TR
    seg_start = jnp.cumsum(padded) - padded
    pos1 = seg_start[i1] + info[:, R_RANK1].astype(i32)
    pos2 = seg_start[i2] + info[:, R_RANK2].astype(i32)
    dst = jnp.zeros((MOE_ROWS,), i32).at[jnp.concatenate([pos1, pos2])].set(jnp.arange(2 * N_TOK, dtype=i32))
    n_pass = (padded + MOE_RS - 1) // MOE_RS
    pass_end = jnp.cumsum(n_pass)
    total = pass_end[-1]
    sidx = jnp.arange(MOE_NST, dtype=i32)
    used = sidx < total
    e_of = jnp.minimum(jnp.searchsorted(pass_end, jnp.minimum(sidx, total - 1), side="right"), N_EXP - 1).astype(i32)
    k = jnp.minimum(sidx, total - 1) - (pass_end - n_pass)[e_of]
    row0 = seg_start[e_of] + k * MOE_RS
    nrows = jnp.where(used, jnp.clip(padded[e_of] - k * MOE_RS, 0, MOE_RS), 0)
    nvalid = jnp.where(used, jnp.clip(counts[e_of] - k * MOE_RS, 0, MOE_RS), 0)
    return dst, e_of, row0.astype(i32), nrows.astype(i32), nvalid.astype(i32)


MOE_UNROLL = 8


def _moe_expert_kernel(dst_ref, exp_ref, row0_ref, nrows_ref, nvalid_ref,
                       h_hbm, wg_ref, wu_ref, wd_ref, yo_hbm, xs, xb, acc, wgb, wub, wdb, gsem, ssem):
    s = pl.program_id(0)
    f = pl.program_id(1)
    nf = pl.num_programs(1)
    nrows = pl.multiple_of(nrows_ref[s], MOE_TR)
    nvalid = nvalid_ref[s]
    row0 = row0_ref[s]
    n_chunks = nrows // MOE_TR

    def row_copy(src, dst, sem):
        return pltpu.make_async_copy(src, dst, sem)

    @pl.when((f == 0) & (nrows > 0))
    def _gather():
        def issue(g, carry):
            for u in range(MOE_UNROLL):
                r = g * MOE_UNROLL + u
                d = dst_ref[row0 + r]
                tok = jnp.where(d >= N_TOK, d - N_TOK, d)
                row_copy(h_hbm.at[pl.ds(tok, 1), :], xs.at[pl.ds(r, 1), :], gsem).start()
            return carry

        lax.fori_loop(0, nrows // MOE_UNROLL, issue, 0)
        row_copy(h_hbm.at[pl.ds(0, nrows), :], xs.at[pl.ds(0, nrows), :], gsem).wait()

        def cvt(c, carry):
            rs = pl.ds(pl.multiple_of(c * MOE_TR, MOE_TR), MOE_TR)
            xb[rs, :] = xs[rs, :].astype(BF16)
            return carry

        lax.fori_loop(0, n_chunks, cvt, 0)

    @pl.when(nrows > 0)
    def _compute():
        wgb[...] = wg_ref[...].astype(BF16)
        wub[...] = wu_ref[...].astype(BF16)
        wdb[...] = wd_ref[...].astype(BF16)

        def chunk(start, n):
            rs = pl.ds(pl.multiple_of(start, MOE_TR), n)
            x = xb[rs, :]
            gate = _dg(x, wgb[...])
            up = _dg(x, wub[...])
            y = _dg((gate * _sigmoid(gate) * up).astype(BF16), wdb[...])

            @pl.when(f == 0)
            def _():
                acc[rs, :] = y

            @pl.when(f > 0)
            def _():
                acc[rs, :] += y

        def chunk_pair(c2, carry):
            chunk(c2 * (2 * MOE_TR), 2 * MOE_TR)
            return carry

        lax.fori_loop(0, n_chunks // 2, chunk_pair, 0)

        @pl.when(n_chunks % 2 == 1)
        def _():
            chunk((n_chunks - 1) * MOE_TR, MOE_TR)

    @pl.when((f == nf - 1) & (nvalid > 0))
    def _scatter():
        def issue_row(r):
            d = dst_ref[row0 + r]
            row_copy(acc.at[pl.ds(r, 1), :], yo_hbm.at[pl.ds(d, 1), :], ssem).start()

        def issue(g, carry):
            for u in range(MOE_UNROLL):
                issue_row(g * MOE_UNROLL + u)
            return carry

        def issue_tail(r, carry):
            issue_row(r)
            return carry

        lax.fori_loop(0, nvalid // MOE_UNROLL, issue, 0)
        lax.fori_loop(nvalid // MOE_UNROLL * MOE_UNROLL, nvalid, issue_tail, 0)
        n8 = pl.multiple_of(nvalid // 8 * 8, 8)

        @pl.when(n8 > 0)
        def _():
            row_copy(acc.at[pl.ds(0, n8), :], yo_hbm.at[pl.ds(0, n8), :], ssem).wait()

        def wait_one(r, carry):
            row_copy(acc.at[pl.ds(0, 1), :], yo_hbm.at[pl.ds(0, 1), :], ssem).wait()
            return carry

        lax.fori_loop(n8, nvalid, wait_one, 0)


def _moe_expert_call(h, plan, wg, wu, wd, i_moe):
    nf = D_FFE // MOE_TF

    def wspec(shape, fdim):
        def imap(s, f, dst, exp, row0, nrows, nvalid):
            fe = jnp.where(nrows[s] > 0, f, nf - 1)
            return (i_moe, exp[s], 0, fe) if fdim == 3 else (i_moe, exp[s], fe, 0)
        return pl.BlockSpec(shape, imap)

    grid_spec = pltpu.PrefetchScalarGridSpec(
        num_scalar_prefetch=5,
        grid=(MOE_NST, nf),
        in_specs=[
            pl.BlockSpec(memory_space=pl.ANY),
            wspec((None, None, D, MOE_TF), 3),
            wspec((None, None, D, MOE_TF), 3),
            wspec((None, None, MOE_TF, D), 2),
        ],
        out_specs=pl.BlockSpec(memory_space=pl.ANY),
        scratch_shapes=[
            pltpu.VMEM((MOE_RS, D), F32),
            pltpu.VMEM((MOE_RS, D), BF16),
            pltpu.VMEM((MOE_RS, D), F32),
            pltpu.VMEM((D, MOE_TF), BF16), pltpu.VMEM((D, MOE_TF), BF16), pltpu.VMEM((MOE_TF, D), BF16),
            pltpu.SemaphoreType.DMA(()), pltpu.SemaphoreType.DMA(()),
        ],
    )
    return pl.pallas_call(
        _moe_expert_kernel,
        grid_spec=grid_spec,
        out_shape=jax.ShapeDtypeStruct((2 * N_TOK, D), F32),
        compiler_params=pltpu.CompilerParams(dimension_semantics=("arbitrary", "arbitrary"),
                                             vmem_limit_bytes=VMEM_LIMIT, disable_bounds_checks=True),
        name="moe_experts",
    )(*plan, h, wg, wu, wd)


def _moe_combine_kernel(x_ref, y1_ref, y2_ref, info_ref, mod_ref, lg_ref, lb_ref, o_ref):
    f = info_ref[:, R_W1:R_W1 + 1] * y1_ref[...] + info_ref[:, R_W2:R_W2 + 1] * y2_ref[...]
    y = DN_ALPHA * x_ref[...] + mod_ref[5:6, :] * f
    o_ref[...] = _layer_norm(y, lg_ref[...], lb_ref[...])


def _moe_combine_call(x, yo, info, mod_l, lg, lb, l):
    tm = MOE_TM
    nt = N_TOK // tm
    vspec = pl.BlockSpec((None, 1, D), lambda i: (l, 0, 0))
    return pl.pallas_call(
        _moe_combine_kernel,
        grid=(nt,),
        in_specs=[
            pl.BlockSpec((tm, D), lambda i: (i, 0)),
            pl.BlockSpec((tm, D), lambda i: (i, 0)),
            pl.BlockSpec((tm, D), lambda i: (nt + i, 0)),
            pl.BlockSpec((tm, 128), lambda i: (i, 0)),
            pl.BlockSpec((None, 6, D), lambda i: (_group_of_tile(i, tm), 0, 0)),
            vspec, vspec,
        ],
        out_specs=pl.BlockSpec((tm, D), lambda i: (i, 0)),
        out_shape=jax.ShapeDtypeStruct((N_TOK, D), F32),
        compiler_params=_cparams(("arbitrary",)),
        name="moe_combine",
    )(x, yo, yo, info, mod_l, lg.reshape(DEPTH, 1, D), lb.reshape(DEPTH, 1, D))


def _moe_call(x, mod_l, wr_pad, wg, wu, wd, lg, lb, l, i_moe):
    h, info, cnt = _moe_route_call(x, mod_l, wr_pad, i_moe)
    yo = _moe_expert_call(h, _moe_plan(info, cnt), wg, wu, wd, i_moe)
    return _moe_combine_call(x, yo, info, mod_l, lg, lb, l)


def kernel(x_prompt, x_sample, cache_attn_k, cache_attn_v, state_rwkv, state_gla, c, c_ctx, w_ada, b_ada, w_in,
           attn_sink, rwkv_w0, rwkv_w_a, rwkv_w_b, rwkv_a0, rwkv_a_a, rwkv_a_b, rwkv_g_a, rwkv_g_b, rwkv_k_k,
           rwkv_k_a, rwkv_r_k, rwkv_ln_g, rwkv_ln_b, gla_gate_a, gla_gate_b, gla_gate_bias, gla_norm_g, w_up_a,
           w_up_b, w_up_c, w_out, ln1_g, ln1_b, ln2_g, ln2_b, ffn_w_gate, ffn_w_up, ffn_w_down, moe_router,
           moe_w_gate, moe_w_up, moe_w_down):
    cvec = jnp.concatenate([c_ctx[None, :], c, jnp.zeros((N_GROUPS - 1 - N_DEC_B, D), F32)], axis=0)
    mods = _ada_call(cvec, w_ada, b_ada).reshape(DEPTH, N_GROUPS, 6, D)
    x = jnp.concatenate([x_prompt.reshape(N_CTX_TOK, D), x_sample.reshape(N_DEC_B * DEC_T, D)], axis=0)
    cos, sin = _rope_tables()
    kc_all = cache_attn_k.reshape(N_DEC_B, DEPTH, PAST, A_KV * A_DH)
    vc_all = cache_attn_v.reshape(N_DEC_B, DEPTH, PAST, A_KV * A_DH)
    zeros_r = jnp.zeros((N_CTX_B, 2, B_HEADS, B_DH, B_DH), F32)
    zeros_g = jnp.zeros((N_CTX_B, 2, C_HEADS, C_DK, C_DV), F32)

    new_k, new_v, new_sr, new_sg = [], [], [], []
    for l in range(DEPTH):
        mod_l = mods[l]
        w_lr = jnp.concatenate(
            [rwkv_w_a[l, 0], rwkv_w_a[l, 1], rwkv_a_a[l, 0], rwkv_a_a[l, 1], rwkv_g_a[l],
             gla_gate_a[l, 0], gla_gate_a[l, 1], jnp.zeros((D, LR_W - 416), F32)], axis=1).astype(BF16)
        pm, gates, plr = _inproj_call(x, mod_l, w_in[l, :, :IN_MAIN].astype(BF16),
                                      w_in[l, :, IN_MAIN:].astype(BF16), w_lr)

        sink_l = attn_sink[l]
        oa = (_attn_ctx_call(sink_l, pm), _attn_lat_call(sink_l, pm, kc_all, vc_all, l, cos, sin))

        rprm = (rwkv_w0[l].reshape(2, 1, B_W), rwkv_w_b[l], rwkv_a0[l].reshape(2, 1, B_W), rwkv_a_b[l],
                rwkv_g_b[l], rwkv_k_k[l].reshape(1, B_W), rwkv_k_a[l].reshape(1, B_W),
                rwkv_r_k[l].reshape(1, B_W), rwkv_ln_g[l].reshape(1, B_W), rwkv_ln_b[l].reshape(1, B_W))
        ob_c, sr_c = _rwkv_call(CTX_T, 4, N_CTX_B, 0, pm, plr, zeros_r, rprm, "rwkv_ctx")
        ob_d, _ = _rwkv_call(DEC_T, 2, N_DEC_B, N_CTX_TOK, pm, plr, state_rwkv[:, l], rprm, "rwkv_lat")

        gb_pad = jnp.zeros((2, 128, C_KW), F32)
        gb_pad = gb_pad.at[0, 0:C_GATE_RANK].set(gla_gate_b[l, 0]).at[1, C_GATE_RANK:2 * C_GATE_RANK].set(
            gla_gate_b[l, 1])
        gbias = gla_gate_bias[l].reshape(2, 1, C_KW)
        ng = jnp.tile(gla_norm_g[l], C_HEADS).reshape(1, C_VW)
        oc_c, sg_c = _gla_call(CTX_T, 2, N_CTX_B, 0, pm, plr, zeros_g, gb_pad, gbias, ng, "gla_ctx")
        oc_d, _ = _gla_call(DEC_T, 2, N_DEC_B, N_CTX_TOK, pm, plr, state_gla[:, l], gb_pad, gbias, ng, "gla_lat")

        x = _merge_call(x, gates, oa, (ob_c, ob_d), (oc_c, oc_d), mod_l, w_up_a[l].astype(BF16),
                        w_up_b[l].astype(BF16), w_up_c[l].astype(BF16), w_out[l].astype(BF16), ln1_g[l], ln1_b[l])
        if l % 2 == 0:
            i_ffn = l // 2
            x = _ffn_call(x, mod_l, ffn_w_gate[i_ffn].astype(BF16), ffn_w_up[i_ffn].astype(BF16),
                          ffn_w_down[i_ffn].astype(BF16), ln2_g[l], ln2_b[l])
        else:
            wr_pad = jnp.concatenate([moe_router, jnp.zeros((moe_router.shape[0], D, 128 - N_EXP), F32)], axis=2)
            x = _moe_call(x, mod_l, wr_pad, moe_w_gate, moe_w_up, moe_w_down, ln2_g, ln2_b, l, l // 2)

        new_k.append(pm[:N_CTX_TOK, 512:640].reshape(N_CTX_B, CTX_T, A_KV, A_DH))
        new_v.append(pm[:N_CTX_TOK, 640:768].reshape(N_CTX_B, CTX_T, A_KV, A_DH))
        new_sr.append(sr_c)
        new_sg.append(sg_c)

    y_prompt = x[:N_CTX_TOK].reshape(N_CTX_B, CTX_T, D)
    y_sample = x[N_CTX_TOK:].reshape(N_DEC_B, DEC_T, D)
    return (y_prompt, y_sample, jnp.stack(new_k, axis=1), jnp.stack(new_v, axis=1),
            jnp.stack(new_sr, axis=1), jnp.stack(new_sg, axis=1))
```

```python
import functools

import numpy as np
import jax
import jax.numpy as jnp
from jax import lax
from jax.experimental import pallas as pl
from jax.experimental.pallas import tpu as pltpu

D = 1024
N_CTX_B, CTX_T = 32, 256
N_DEC_B, DEC_T = 2, 1024
N_CTX_TOK = N_CTX_B * CTX_T
N_TOK = N_CTX_TOK + N_DEC_B * DEC_T
DEPTH = 2
PAST = 256
GRID_W = 64
A_HEADS, A_KV, A_DH = 8, 2, 64
A_G = A_HEADS // A_KV
A_WIN, A_BLK = 128, 128
ROPE_BASE = 10000.0
B_HEADS, B_DH = 4, 64
B_W = B_HEADS * B_DH
B_GN_EPS = 64e-5
C_HEADS, C_DK, C_DV = 4, 32, 64
C_KW, C_VW = C_HEADS * C_DK, C_HEADS * C_DV
C_GATE_RANK = 16
C_GATE_NORM = 16.0
C_CHUNK = 64
C_SUB = 16
D_FF = 2816
N_EXP = 8
D_FFE = 3584
LN_EPS = 1e-5
DN_ALPHA = (2.0 * DEPTH) ** 0.25
NEG_INF = -1e30
IN_MAIN = 2304
IN_GATE = 3072
LR_W = 512
N_GROUPS = 8

F32 = jnp.float32
BF16 = jnp.bfloat16
VMEM_LIMIT = 56 * 1024 * 1024
SUBLANES = 8

NN = ((1,), (0,))
NT = ((1,), (1,))
TN = ((0,), (0,))


def _dg(a, b, dims=NN):
    return lax.dot_general(a, b, (dims, ((), ())), preferred_element_type=F32)


def _split2(x):
    hi = x.astype(BF16)
    lo = (x - hi.astype(F32)).astype(BF16)
    return hi, lo


def _split3(x):
    hi = x.astype(BF16)
    r = x - hi.astype(F32)
    mid = r.astype(BF16)
    lo = (r - mid.astype(F32)).astype(BF16)
    return hi, mid, lo


def _mm(a, b, dims=NN):
    return _dg(a.astype(BF16), b.astype(BF16), dims)


def _mm3(a, b, dims=NN):
    ah, al = _split2(a)
    bh, bl = _split2(b)
    return _dg(ah, bh, dims) + (_dg(ah, bl, dims) + _dg(al, bh, dims))


def _mm_xr(a, b_exact, passes, dims=NN):
    parts = (a.astype(BF16),) if passes == 1 else (_split2(a) if passes == 2 else _split3(a))
    out = _dg(parts[0], b_exact, dims)
    for p in parts[1:]:
        out = out + _dg(p, b_exact, dims)
    return out


def _mm_xl(a_exact, b, passes):
    parts = (b.astype(BF16),) if passes == 1 else (_split2(b) if passes == 2 else _split3(b))
    out = _dg(a_exact, parts[0])
    for p in parts[1:]:
        out = out + _dg(a_exact, p)
    return out


def _sigmoid(x):
    return 1.0 / (1.0 + jnp.exp(-x))


def _iota(shape, dim):
    return lax.broadcasted_iota(jnp.int32, shape, dim)


def _block_ones(n, blk):
    return (_iota((n, n), 0) // blk == _iota((n, n), 1) // blk).astype(BF16)


def _layer_norm(y, g, b):
    mu = jnp.mean(y, -1, keepdims=True)
    yc = y - mu
    var = jnp.mean(yc * yc, -1, keepdims=True)
    return yc * lax.rsqrt(var + LN_EPS) * g + b


def _group_of_tile(i, tm):
    n_ctx = N_CTX_TOK // tm
    per_dec = DEC_T // tm
    return jnp.where(i < n_ctx, 0, 1 + (i - n_ctx) // per_dec)


def _cparams(sem):
    return pltpu.CompilerParams(dimension_semantics=sem, vmem_limit_bytes=VMEM_LIMIT)


def _ada_kernel(c_ref, w_ref, b_ref, o_ref):
    c = c_ref[...]
    s = c * _sigmoid(c)
    o_ref[...] = _mm3(s, w_ref[...]) + b_ref[...]


def _ada_call(cvec, w_ada, b_ada):
    tn = 1536
    return pl.pallas_call(
        _ada_kernel,
        grid=(DEPTH, 6 * D // tn),
        in_specs=[
            pl.BlockSpec((N_GROUPS, D), lambda l, j: (0, 0)),
            pl.BlockSpec((None, D, tn), lambda l, j: (l, 0, j)),
            pl.BlockSpec((None, 1, tn), lambda l, j: (l, 0, j)),
        ],
        out_specs=pl.BlockSpec((None, N_GROUPS, tn), lambda l, j: (l, 0, j)),
        out_shape=jax.ShapeDtypeStruct((DEPTH, N_GROUPS, 6 * D), F32),
        compiler_params=_cparams(("arbitrary", "arbitrary")),
        name="ada",
    )(cvec, w_ada, b_ada.reshape(DEPTH, 1, 6 * D))


def _inproj_kernel(x_ref, mod_ref, wm_ref, wg_ref, wl_ref, om_ref, og_ref, ol_ref):
    sh = mod_ref[0:1, :]
    sc = mod_ref[1:2, :]
    h = (x_ref[...] * (1.0 + sc) + sh).astype(BF16)
    om_ref[...] = _dg(h, wm_ref[...])
    og_ref[...] = _sigmoid(_dg(h, wg_ref[...])).astype(BF16)
    ol_ref[...] = _dg(h, wl_ref[...])


def _resident(shape):
    return pl.BlockSpec(shape, lambda *_: (0,) * len(shape), pipeline_mode=pl.Buffered(1))


def _inproj_call(x, mod_l, w_main, w_gate, w_lr):
    tm = 256
    return pl.pallas_call(
        _inproj_kernel,
        grid=(N_TOK // tm,),
        in_specs=[
            pl.BlockSpec((tm, D), lambda i: (i, 0)),
            pl.BlockSpec((None, 6, D), lambda i: (_group_of_tile(i, tm), 0, 0)),
            _resident((D, IN_MAIN)), _resident((D, IN_GATE)), _resident((D, LR_W)),
        ],
        out_specs=[
            pl.BlockSpec((tm, IN_MAIN), lambda i: (i, 0)),
            pl.BlockSpec((tm, IN_GATE), lambda i: (i, 0)),
            pl.BlockSpec((tm, LR_W), lambda i: (i, 0)),
        ],
        out_shape=[
            jax.ShapeDtypeStruct((N_TOK, IN_MAIN), F32),
            jax.ShapeDtypeStruct((N_TOK, IN_GATE), BF16),
            jax.ShapeDtypeStruct((N_TOK, LR_W), F32),
        ],
        compiler_params=_cparams(("arbitrary",)),
        name="inproj",
    )(x, mod_l, w_main, w_gate, w_lr)


def _sink_col(sink_ref, kvh, rows_per_head):
    n = A_G * rows_per_head
    r = _iota((n, 1), 0) // rows_per_head
    col = jnp.full((n, 1), sink_ref[kvh * A_G], F32)
    for g in range(1, A_G):
        col = jnp.where(r == g, sink_ref[kvh * A_G + g], col)
    return col


def _attn_ctx_kernel(sink_ref, q_ref, k_ref, v_ref, o_ref):
    scale = A_DH ** -0.5
    for kvh in range(A_KV):
        ks = k_ref[:, kvh * A_DH:(kvh + 1) * A_DH].astype(BF16)
        vs = v_ref[:, kvh * A_DH:(kvh + 1) * A_DH].astype(BF16)
        q4 = jnp.concatenate(
            [q_ref[:, (kvh * A_G + g) * A_DH:(kvh * A_G + g + 1) * A_DH] for g in range(A_G)], axis=0)
        s = _dg(q4.astype(BF16), ks, NT) * scale
        sink = _sink_col(sink_ref, kvh, CTX_T)
        m = jnp.maximum(jnp.max(s, -1, keepdims=True), sink)
        e = jnp.exp(s - m)
        p = e / (jnp.sum(e, -1, keepdims=True) + jnp.exp(sink - m))
        o = _dg(p.astype(BF16), vs)
        for g in range(A_G):
            h = kvh * A_G + g
            o_ref[:, h * A_DH:(h + 1) * A_DH] = o[g * CTX_T:(g + 1) * CTX_T, :]


def _attn_ctx_call(sink_l, pm):
    return pl.pallas_call(
        _attn_ctx_kernel,
        grid=(N_CTX_B,),
        in_specs=[
            pl.BlockSpec(memory_space=pltpu.SMEM),
            pl.BlockSpec((CTX_T, 512), lambda b: (b, 0)),
            pl.BlockSpec((CTX_T, 128), lambda b: (b, 4)),
            pl.BlockSpec((CTX_T, 128), lambda b: (b, 5)),
        ],
        out_specs=pl.BlockSpec((CTX_T, 512), lambda b: (b, 0)),
        out_shape=jax.ShapeDtypeStruct((N_CTX_TOK, 512), F32),
        compiler_params=_cparams(("arbitrary",)),
        name="attn_ctx",
    )(sink_l, pm, pm, pm)


def _rope(x, cos, sin_signed):
    w = x.shape[-1]
    lane = _iota(x.shape, 1)
    partner = jnp.where((lane % 32) < 16, pltpu.roll(x, w - 16, 1), pltpu.roll(x, 16, 1))
    return x * cos + partner * sin_signed


def _attn_lat_kernel(sink_ref, q_ref, k_ref, v_ref, kc_ref, vc_ref, cos_ref, sin_ref, o_ref, kr_ref):
    n = pl.program_id(1)
    scale = A_DH ** -0.5

    @pl.when(n == 0)
    def _():
        kr_ref[...] = _rope(k_ref[...], cos_ref[:, 0:128], sin_ref[:, 0:128]).astype(BF16)

    q0 = pl.multiple_of(n * A_BLK, A_BLK)
    qr = _rope(q_ref[...], cos_ref[pl.ds(q0, A_BLK), :], sin_ref[pl.ds(q0, A_BLK), :])
    kstart = pl.multiple_of(jnp.clip((n - 1) * A_BLK, 0, DEC_T - 3 * A_BLK), A_BLK)
    kwin = kr_ref[pl.ds(kstart, 3 * A_BLK), :]
    vwin = v_ref[pl.ds(kstart, 3 * A_BLK), :].astype(BF16)
    kc = kc_ref[...].astype(BF16)
    vc = vc_ref[...].astype(BF16)
    rows = A_G * A_BLK
    qpos = q0 + _iota((rows, 3 * A_BLK), 0) % A_BLK
    kpos = kstart + _iota((rows, 3 * A_BLK), 1)
    valid = jnp.abs(qpos - kpos) <= A_WIN
    for kvh in range(A_KV):
        cs = slice(kvh * A_DH, (kvh + 1) * A_DH)
        q4 = jnp.concatenate(
            [qr[:, (kvh * A_G + g) * A_DH:(kvh * A_G + g + 1) * A_DH] for g in range(A_G)], axis=0).astype(BF16)
        s_loc = jnp.where(valid, _dg(q4, kwin[:, cs], NT) * scale, NEG_INF)
        s_ctx = _dg(q4, kc[:, cs], NT) * scale
        sink = _sink_col(sink_ref, kvh, A_BLK)
        m = jnp.maximum(jnp.maximum(jnp.max(s_loc, -1, keepdims=True), jnp.max(s_ctx, -1, keepdims=True)), sink)
        e_loc = jnp.exp(s_loc - m)
        e_ctx = jnp.exp(s_ctx - m)
        inv = 1.0 / (jnp.sum(e_loc, -1, keepdims=True) + jnp.sum(e_ctx, -1, keepdims=True) + jnp.exp(sink - m))
        o = _dg((e_loc * inv).astype(BF16), vwin[:, cs]) + _dg((e_ctx * inv).astype(BF16), vc[:, cs])
        for g in range(A_G):
            h = kvh * A_G + g
            o_ref[:, h * A_DH:(h + 1) * A_DH] = o[g * A_BLK:(g + 1) * A_BLK, :]


def _rope_tables():
    half = A_DH // 2
    t = np.arange(DEC_T)
    rows = (t // GRID_W).astype(np.float32)
    cols = (t % GRID_W).astype(np.float32)
    inv_freq = (ROPE_BASE ** (-np.arange(0, half, 2, dtype=np.float32) / half)).astype(np.float32)
    ang_r = rows[:, None] * inv_freq[None, :]
    ang_c = cols[:, None] * inv_freq[None, :]
    cos = np.concatenate([np.cos(ang_r), np.cos(ang_r), np.cos(ang_c), np.cos(ang_c)], -1)
    sin = np.concatenate([-np.sin(ang_r), np.sin(ang_r), -np.sin(ang_c), np.sin(ang_c)], -1)
    return (jnp.asarray(np.tile(cos, (1, A_HEADS)), F32), jnp.asarray(np.tile(sin, (1, A_HEADS)), F32))


def _attn_lat_call(sink_l, pm, kc, vc, l, cos, sin):
    nb = DEC_T // A_BLK
    row0 = N_CTX_TOK // A_BLK
    seq0 = N_CTX_TOK // DEC_T
    return pl.pallas_call(
        _attn_lat_kernel,
        grid=(N_DEC_B, nb),
        in_specs=[
            pl.BlockSpec(memory_space=pltpu.SMEM),
            pl.BlockSpec((A_BLK, 512), lambda b, n: (row0 + b * nb + n, 0)),
            pl.BlockSpec((DEC_T, 128), lambda b, n: (seq0 + b, 4)),
            pl.BlockSpec((DEC_T, 128), lambda b, n: (seq0 + b, 5)),
            pl.BlockSpec((None, None, PAST, 128), lambda b, n: (b, l, 0, 0)),
            pl.BlockSpec((None, None, PAST, 128), lambda b, n: (b, l, 0, 0)),
            pl.BlockSpec((DEC_T, 512), lambda b, n: (0, 0)),
            pl.BlockSpec((DEC_T, 512), lambda b, n: (0, 0)),
        ],
        out_specs=pl.BlockSpec((A_BLK, 512), lambda b, n: (b * nb + n, 0)),
        out_shape=jax.ShapeDtypeStruct((N_DEC_B * DEC_T, 512), F32),
        scratch_shapes=[pltpu.VMEM((DEC_T, 128), BF16)],
        compiler_params=_cparams(("arbitrary", "arbitrary")),
        name="attn_lat",
    )(sink_l, pm, pm, pm, kc, vc, cos, sin)


def _rwkv_kernel(T, NS, r_ref, k_ref, v_ref, lr_ref, s0_ref, w0_ref, wb_ref, a0_ref, ab_ref, gb_ref,
                 kk_ref, ka_ref, rk_ref, lng_ref, lnb_ref, o_ref, sfin_ref,
                 KK, W, WRP, AKK, KT, VC2, BON, Y, S):
    ones4 = _block_ones(B_W, B_DH)
    decay_c = float(np.exp(-0.5))
    RC = 256
    SUB = 16
    NP = 3

    def prep(c, carry):
        r0 = pl.multiple_of(c * RC, RC)
        rs = pl.ds(r0, RC)
        r = r_ref[rs, :]
        k = k_ref[rs, :]
        v = v_ref[rs, :]
        lr = lr_ref[rs, :]
        kkr = k * kk_ref[...]
        kk = kkr * lax.rsqrt(_mm_xr(kkr * kkr, ones4, 2) + 1e-12)
        KK[rs, :] = kk
        bonus = jnp.zeros((RC, B_W), F32)
        vc2 = jnp.zeros((RC, B_W), F32)
        for d in range(2):
            z = w0_ref[d] + _mm(jnp.tanh(lr[:, 64 * d:64 * d + 64]), wb_ref[d])
            w = jnp.exp(-decay_c * _sigmoid(z))
            a = _sigmoid(a0_ref[d] + _mm(lr[:, 128 + 64 * d:192 + 64 * d], ab_ref[d]))
            kt = k * (1.0 + (a - 1.0) * ka_ref[...])
            akk = a * kk
            W[d, rs, :] = w
            WRP[d, rs, :] = w * r - _mm_xr(akk * r, ones4, 2) * kk
            AKK[d, rs, :] = akk
            KT[d, rs, :] = kt
            vc2 = vc2 + _mm_xr(kt * r, ones4, 2) * v
            bonus = bonus + _mm_xr(r * kt * rk_ref[...], ones4, 2) * v
        VC2[rs, :] = vc2
        BON[rs, :] = bonus
        return carry

    lax.fori_loop(0, NS * T // RC, prep, 0)

    chains = [(s, d) for s in range(NS) for d in range(2)]
    for s, d in chains:
        S[s, d] = jnp.concatenate([s0_ref[s, d, h] for h in range(B_HEADS)], axis=1)

    eye4 = _iota((B_DH, B_W), 0) == (_iota((B_DH, B_W), 1) % B_DH)

    def steps(i, carry):
        tiles = []
        for s, d in chains:
            t0 = pl.multiple_of(s * T + (i * SUB if d == 0 else T - SUB - i * SUB), SUB)
            rows = pl.ds(t0, SUB)
            tiles.append((rows, KK[rows, :], WRP[d, rows, :], v_ref[rows, :],
                          W[d, rows, :], AKK[d, rows, :], KT[d, rows, :]))
        ys = [[None] * SUB for _ in chains]
        for jj in range(SUB):
            lhs = []
            for (s, d), (_, kk8, wrp8, v8, _, _, _) in zip(chains, tiles):
                j = jj if d == 0 else SUB - 1 - jj
                r = slice(j, j + 1)
                st = S[s, d]
                lhs += [(st * kk8[r]).astype(BF16), (st * wrp8[r]).astype(BF16),
                        jnp.where(eye4, v8[r], 0.0).astype(BF16)]
            res = _dg(jnp.concatenate(lhs, axis=0), ones4)
            for g, ((s, d), (_, _, _, _, w8, akk8, kt8)) in enumerate(zip(chains, tiles)):
                j = jj if d == 0 else SUB - 1 - jj
                r = slice(j, j + 1)
                sk, yp, vcol = [res[(g * NP + n) * B_DH:(g * NP + n + 1) * B_DH] for n in range(NP)]
                S[s, d] = S[s, d] * w8[r] - sk * akk8[r] + vcol * kt8[r]
                ys[g][j] = jnp.sum(jnp.where(eye4, yp, 0.0), axis=0, keepdims=True)
        for g, ((s, d), tl) in enumerate(zip(chains, tiles)):
            Y[d, tl[0], :] = jnp.concatenate(ys[g], axis=0)
        return carry

    lax.fori_loop(0, T // SUB, steps, 0)

    for s, d in chains:
        st = S[s, d]
        for h in range(B_HEADS):
            sfin_ref[s, d, h] = st[:, h * B_DH:(h + 1) * B_DH]

    def post(c, carry):
        r0 = pl.multiple_of(c * RC, RC)
        rs = pl.ds(r0, RC)
        y = Y[0, rs, :] + Y[1, rs, :] + VC2[rs, :]
        mu = _mm_xr(y, ones4, 2) * (1.0 / B_DH)
        yc = y - mu
        var = _mm_xr(yc * yc, ones4, 2) * (1.0 / B_DH)
        yn = yc * lax.rsqrt(var + B_GN_EPS) * lng_ref[...] + lnb_ref[...] + BON[rs, :]
        g = _mm(_sigmoid(lr_ref[rs, 256:384]), gb_ref[...])
        o_ref[rs, :] = yn * g
        return carry

    lax.fori_loop(0, NS * T // RC, post, 0)


def _rwkv_call(T, NS, n_seq, tok0, pm, plr, s0, prm, name):
    rows = NS * T
    blk0 = tok0 // rows
    n_steps = n_seq // NS
    full = lambda shape: pl.BlockSpec(shape, lambda b: (0,) * len(shape))
    big = lambda shape, imap: (pl.BlockSpec(shape, imap, pipeline_mode=pl.Buffered(1)) if n_steps == 1
                               else pl.BlockSpec(shape, imap))
    kern = functools.partial(_rwkv_kernel, T, NS)
    return pl.pallas_call(
        kern,
        grid=(n_steps,),
        in_specs=[
            big((rows, B_W), lambda b: (blk0 + b, 3)),
            big((rows, B_W), lambda b: (blk0 + b, 4)),
            big((rows, B_W), lambda b: (blk0 + b, 5)),
            big((rows, LR_W), lambda b: (blk0 + b, 0)),
            pl.BlockSpec((NS, 2, B_HEADS, B_DH, B_DH), lambda b: (b, 0, 0, 0, 0)),
            full((2, 1, B_W)), full((2, 64, B_W)), full((2, 1, B_W)), full((2, 64, B_W)), full((128, B_W)),
            full((1, B_W)), full((1, B_W)), full((1, B_W)), full((1, B_W)), full((1, B_W)),
        ],
        out_specs=[
            big((rows, B_W), lambda b: (b, 0)),
            pl.BlockSpec((NS, 2, B_HEADS, B_DH, B_DH), lambda b: (b, 0, 0, 0, 0)),
        ],
        out_shape=[
            jax.ShapeDtypeStruct((n_seq * T, B_W), F32),
            jax.ShapeDtypeStruct((n_seq, 2, B_HEADS, B_DH, B_DH), F32),
        ],
        scratch_shapes=[
            pltpu.VMEM((rows, B_W), F32),
            pltpu.VMEM((2, rows, B_W), F32),
            pltpu.VMEM((2, rows, B_W), F32),
            pltpu.VMEM((2, rows, B_W), F32),
            pltpu.VMEM((2, rows, B_W), F32),
            pltpu.VMEM((rows, B_W), F32),
            pltpu.VMEM((rows, B_W), F32),
            pltpu.VMEM((2, rows, B_W), F32),
            pltpu.VMEM((NS, 2, B_DH, B_W), F32),
        ],
        compiler_params=_cparams(("arbitrary",)),
        name=name,
    )(pm, pm, pm, plr, s0, *prm)


def _gla_kernel(T, NS, q_ref, k_ref, v_ref, og_ref, lr_ref, s0_ref, gb_ref, bias_ref, ng_ref,
                o_ref, sfin_ref, LA, O, S):
    n_chunks = T // C_CHUNK
    nsub = C_CHUNK // C_SUB
    qscale = C_DK ** -0.5
    lr = lr_ref[...]
    for d in range(2):
        gl = _mm(lr, gb_ref[d]) + bias_ref[d]
        LA[d] = (jnp.minimum(gl, 0.0) - jnp.log(1.0 + jnp.exp(-jnp.abs(gl)))) * (1.0 / C_GATE_NORM)
    bd_state = _iota((C_KW, C_VW), 0) // C_DK == _iota((C_KW, C_VW), 1) // C_DV
    chains = [(s, d) for s in range(NS) for d in range(2)]
    for s, d in chains:
        for h in range(C_HEADS):
            pad_l = h * C_DV
            pad_r = C_VW - (h + 1) * C_DV
            blk = s0_ref[s, d, h]
            parts = ([jnp.zeros((C_DK, pad_l), F32)] if pad_l else []) + [blk] + \
                    ([jnp.zeros((C_DK, pad_r), F32)] if pad_r else [])
            S[s, d, h * C_DK:(h + 1) * C_DK, :] = jnp.concatenate(parts, axis=1)

    ti = _iota((C_CHUNK, C_CHUNK), 0)
    si = _iota((C_CHUNK, C_CHUNK), 1)
    tri = ((si <= ti).astype(BF16), (si >= ti).astype(BF16))
    trow = _iota((C_CHUNK, 1), 0)
    mask_k = _iota((C_CHUNK, C_KW), 0) // C_SUB == _iota((C_CHUNK, C_KW), 1) // C_DK
    mask_v = _iota((C_CHUNK, C_VW), 0) // C_SUB == _iota((C_CHUNK, C_VW), 1) // C_DV
    t_att = _iota((C_CHUNK, C_CHUNK), 0)
    s_att = _iota((C_CHUNK, C_CHUNK), 1) % C_SUB
    eye_k = _iota((C_KW, C_KW), 0) == _iota((C_KW, C_KW), 1)

    def body(c, carry):
        cx = []
        for s, d in chains:
            cc = c if d == 0 else n_chunks - 1 - c
            rs = pl.ds(pl.multiple_of(s * T + cc * C_CHUNK, C_CHUNK), C_CHUNK)
            b = _mm_xl(tri[d], LA[d, rs, :], 3)
            cx.append(dict(s=s, d=d, rs=rs, b=b, q=q_ref[rs, :] * qscale, k=k_ref[rs, :], v=v_ref[rs, :]))
        for x in cx:
            x["o"] = _mm(x["q"] * jnp.exp(x["b"]), S[x["s"], x["d"]])
        for j in range(nsub):
            lo, hi = j * C_SUB, (j + 1) * C_SUB
            for x in cx:
                b, q, k = x["b"], x["q"], x["k"]
                if x["d"] == 0:
                    gamma = b[hi - 1:hi, :]
                    row_ok = trow >= lo
                    att_ok = t_att >= lo + s_att
                else:
                    gamma = b[lo:lo + 1, :]
                    row_ok = trow < hi
                    att_ok = t_att <= lo + s_att
                qj = q * jnp.exp(jnp.where(row_ok, b - gamma, NEG_INF))
                kj = k[lo:hi, :] * jnp.exp(gamma - b[lo:hi, :])
                kbd = jnp.where(mask_k, jnp.concatenate([kj] * C_HEADS, axis=0), 0.0)
                x["att"] = jnp.where(att_ok, _mm(qj, kbd, NT), 0.0)
            for x in cx:
                vbd = jnp.where(mask_v, jnp.concatenate([x["v"][lo:hi, :]] * C_HEADS, axis=0), 0.0)
                x["o"] = x["o"] + _mm(x["att"], vbd)
        for x in cx:
            s, d, b = x["s"], x["d"], x["b"]
            O[d, x["rs"], :] = x["o"]
            blast = b[C_CHUNK - 1:C_CHUNK, :] if d == 0 else b[0:1, :]
            kl = x["k"] * jnp.exp(blast - b)
            upd = jnp.where(bd_state, _mm3(kl.T, x["v"]), 0.0)
            dec = jnp.where(eye_k, jnp.exp(blast), 0.0)
            S[s, d] = _mm3(dec, S[s, d]) + upd
        return carry

    lax.fori_loop(0, n_chunks, body, 0)

    for s, d in chains:
        st = S[s, d]
        for h in range(C_HEADS):
            sfin_ref[s, d, h] = st[h * C_DK:(h + 1) * C_DK, h * C_DV:(h + 1) * C_DV]

    ones4 = _block_ones(C_VW, C_DV)
    o = O[0] + O[1]
    ms = _mm_xr(o * o, ones4, 2) * (1.0 / C_DV)
    og = og_ref[...]
    o_ref[...] = o * lax.rsqrt(ms + LN_EPS) * ng_ref[...] * (og * _sigmoid(og))


def _gla_call(T, NS, n_seq, tok0, pm, plr, s0, gb_pad, bias, ng, name):
    rows = NS * T
    blk0 = tok0 // rows
    full = lambda shape: pl.BlockSpec(shape, lambda b: (0,) * len(shape))
    return pl.pallas_call(
        functools.partial(_gla_kernel, T, NS),
        grid=(n_seq // NS,),
        in_specs=[
            pl.BlockSpec((rows, C_KW), lambda b: (blk0 + b, 12)),
            pl.BlockSpec((rows, C_KW), lambda b: (blk0 + b, 13)),
            pl.BlockSpec((rows, C_VW), lambda b: (blk0 + b, 7)),
            pl.BlockSpec((rows, C_VW), lambda b: (blk0 + b, 8)),
            pl.BlockSpec((rows, 128), lambda b: (blk0 + b, 3)),
            pl.BlockSpec((NS, 2, C_HEADS, C_DK, C_DV), lambda b: (b, 0, 0, 0, 0)),
            full((2, 128, C_KW)), full((2, 1, C_KW)), full((1, C_VW)),
        ],
        out_specs=[
            pl.BlockSpec((rows, C_VW), lambda b: (b, 0)),
            pl.BlockSpec((NS, 2, C_HEADS, C_DK, C_DV), lambda b: (b, 0, 0, 0, 0)),
        ],
        out_shape=[
            jax.ShapeDtypeStruct((n_seq * T, C_VW), F32),
            jax.ShapeDtypeStruct((n_seq, 2, C_HEADS, C_DK, C_DV), F32),
        ],
        scratch_shapes=[
            pltpu.VMEM((2, rows, C_KW), F32),
            pltpu.VMEM((2, rows, C_VW), F32),
            pltpu.VMEM((NS, 2, C_KW, C_VW), F32),
        ],
        compiler_params=_cparams(("arbitrary",)),
        name=name,
    )(pm, pm, pm, pm, plr, s0, gb_pad, bias, ng)


MERGE_TM = 512


def _merge_kernel(x_ref, g_ref, oac_ref, oad_ref, obc_ref, obd_ref, occ_ref, ocd_ref, mod_ref,
                  wa_ref, wb_ref, wc_ref, wo_ref, lg_ref, lb_ref, o_ref):
    is_ctx = pl.program_id(0) < N_CTX_TOK // MERGE_TM
    oa = jnp.where(is_ctx, oac_ref[...], oad_ref[...]).astype(BF16)
    ob = jnp.where(is_ctx, obc_ref[...], obd_ref[...]).astype(BF16)
    oc = jnp.where(is_ctx, occ_ref[...], ocd_ref[...]).astype(BF16)
    merged = (g_ref[:, 0:D].astype(F32) * _dg(oa, wa_ref[...])
              + g_ref[:, D:2 * D].astype(F32) * _dg(ob, wb_ref[...])
              + g_ref[:, 2 * D:3 * D].astype(F32) * _dg(oc, wc_ref[...]))
    mix = _dg(merged.astype(BF16), wo_ref[...])
    y = DN_ALPHA * x_ref[...] + mod_ref[2:3, :] * mix
    o_ref[...] = _layer_norm(y, lg_ref[...], lb_ref[...])


def _merge_call(x, gates, oa, ob, oc, mod_l, wa, wb, wc, wo, lg, lb):
    tm = MERGE_TM
    n_ctx = N_CTX_TOK // tm
    pair = lambda w: [pl.BlockSpec((tm, w), lambda i: (jnp.minimum(i, n_ctx - 1), 0)),
                      pl.BlockSpec((tm, w), lambda i: (jnp.maximum(i - n_ctx, 0), 0))]
    return pl.pallas_call(
        _merge_kernel,
        grid=(N_TOK // tm,),
        in_specs=[
            pl.BlockSpec((tm, D), lambda i: (i, 0)),
            pl.BlockSpec((tm, IN_GATE), lambda i: (i, 0)),
            *pair(512), *pair(B_W), *pair(C_VW),
            pl.BlockSpec((None, 6, D), lambda i: (_group_of_tile(i, tm), 0, 0)),
            _resident((512, D)), _resident((B_W, D)), _resident((C_VW, D)), _resident((D, D)),
            _resident((1, D)), _resident((1, D)),
        ],
        out_specs=pl.BlockSpec((tm, D), lambda i: (i, 0)),
        out_shape=jax.ShapeDtypeStruct((N_TOK, D), F32),
        compiler_params=_cparams(("arbitrary",)),
        name="merge",
    )(x, gates, *oa, *ob, *oc, mod_l, wa, wb, wc, wo, lg.reshape(1, D), lb.reshape(1, D))


def _ffn_kernel(x_ref, mod_ref, wg_ref, wu_ref, wd_ref, lg_ref, lb_ref, o_ref):
    x = x_ref[...]
    h = (x * (1.0 + mod_ref[4:5, :]) + mod_ref[3:4, :]).astype(BF16)
    gate = _dg(h, wg_ref[...])
    up = _dg(h, wu_ref[...])
    f = _dg((gate * _sigmoid(gate) * up).astype(BF16), wd_ref[...])
    y = DN_ALPHA * x + mod_ref[5:6, :] * f
    o_ref[...] = _layer_norm(y, lg_ref[...], lb_ref[...])


def _ffn_call(x, mod_l, wg, wu, wd, lg, lb):
    tm = 512
    return pl.pallas_call(
        _ffn_kernel,
        grid=(N_TOK // tm,),
        in_specs=[
            pl.BlockSpec((tm, D), lambda i: (i, 0)),
            pl.BlockSpec((None, 6, D), lambda i: (_group_of_tile(i, tm), 0, 0)),
            _resident((D, D_FF)), _resident((D, D_FF)), _resident((D_FF, D)),
            _resident((1, D)), _resident((1, D)),
        ],
        out_specs=pl.BlockSpec((tm, D), lambda i: (i, 0)),
        out_shape=jax.ShapeDtypeStruct((N_TOK, D), F32),
        compiler_params=_cparams(("arbitrary",)),
        name="ffn",
    )(x, mod_l, wg, wu, wd, lg.reshape(1, D), lb.reshape(1, D))


MOE_TM = 1024
MOE_TR = 256
MOE_RS = 3072
MOE_TF = 512
MOE_ROWS = 2 * N_TOK + N_EXP * MOE_TR
MOE_NST = -(-MOE_ROWS // MOE_RS) + N_EXP
R_I1, R_I2, R_W1, R_W2, R_RANK1, R_RANK2 = range(6)


def _moe_route_kernel(x_ref, mod_ref, wr_ref, h_ref, info_ref, cnt_ref, carry_s):
    tm = MOE_TM

    @pl.when(pl.program_id(0) == 0)
    def _():
        carry_s[...] = jnp.zeros_like(carry_s)

    h = x_ref[...] * (1.0 + mod_ref[4:5, :]) + mod_ref[3:4, :]
    h_ref[...] = h
    logits = _mm3(h, wr_ref[...])
    lane = _iota(logits.shape, 1)
    logits = jnp.where(lane < N_EXP, logits, NEG_INF)
    v1 = jnp.max(logits, -1, keepdims=True)
    i1 = jnp.min(jnp.where(logits == v1, lane, 128), -1, keepdims=True)
    rest = jnp.where(lane == i1, NEG_INF, logits)
    v2 = jnp.max(rest, -1, keepdims=True)
    i2 = jnp.min(jnp.where(rest == v2, lane, 128), -1, keepdims=True)
    e2 = jnp.exp(v2 - v1)
    w1 = 1.0 / (1.0 + e2)
    w2 = e2 / (1.0 + e2)
    oh1 = lane == i1
    oh2 = lane == i2
    cnt = oh1.astype(F32) + oh2.astype(F32)
    earlier = (_iota((tm, tm), 1) < _iota((tm, tm), 0)).astype(BF16)
    before = _dg(earlier, cnt.astype(BF16)) + carry_s[...]
    rank1 = jnp.sum(jnp.where(oh1, before, 0.0), -1, keepdims=True)
    rank2 = jnp.sum(jnp.where(oh2, before, 0.0), -1, keepdims=True)
    info = jnp.zeros(logits.shape, F32)
    for ln, val in ((R_I1, i1.astype(F32)), (R_I2, i2.astype(F32)), (R_W1, w1), (R_W2, w2),
                    (R_RANK1, rank1), (R_RANK2, rank2)):
        info = jnp.where(lane == ln, val, info)
    info_ref[...] = info
    carry_s[...] += jnp.sum(cnt, axis=0, keepdims=True)
    cnt_ref[...] = carry_s[...]


def _moe_route_call(x, mod_l, wr_pad, i_moe):
    tm = MOE_TM
    return pl.pallas_call(
        _moe_route_kernel,
        grid=(N_TOK // tm,),
        in_specs=[
            pl.BlockSpec((tm, D), lambda i: (i, 0)),
            pl.BlockSpec((None, 6, D), lambda i: (_group_of_tile(i, tm), 0, 0)),
            pl.BlockSpec((None, D, 128), lambda i: (i_moe, 0, 0)),
        ],
        out_specs=[
            pl.BlockSpec((tm, D), lambda i: (i, 0)),
            pl.BlockSpec((tm, 128), lambda i: (i, 0)),
            pl.BlockSpec((1, 128), lambda i: (0, 0)),
        ],
        out_shape=[
            jax.ShapeDtypeStruct((N_TOK, D), F32),
            jax.ShapeDtypeStruct((N_TOK, 128), F32),
            jax.ShapeDtypeStruct((1, 128), F32),
        ],
        scratch_shapes=[pltpu.VMEM((1, 128), F32)],
        compiler_params=_cparams(("arbitrary",)),
        name="moe_route",
    )(x, mod_l, wr_pad)


def _moe_plan(info, cnt):
    i32 = jnp.int32
    i1 = info[:, R_I1].astype(i32)
    i2 = info[:, R_I2].astype(i32)
    counts = cnt[0, :N_EXP].astype(i32)
    padded = (counts + MOE_TR - 1) // MOE_TR * MOE_TR
    seg_start = jnp.cumsum(padded) - padded
    pos1 = seg_start[i1] + info[:, R_RANK1].astype(i32)
    pos2 = seg_start[i2] + info[:, R_RANK2].astype(i32)
    dst = jnp.zeros((MOE_ROWS,), i32).at[jnp.concatenate([pos1, pos2])].set(jnp.arange(2 * N_TOK, dtype=i32))
    n_pass = (padded + MOE_RS - 1) // MOE_RS
    pass_end = jnp.cumsum(n_pass)
    total = pass_end[-1]
    sidx = jnp.arange(MOE_NST, dtype=i32)
    used = sidx < total
    e_of = jnp.minimum(jnp.searchsorted(pass_end, jnp.minimum(sidx, total - 1), side="right"), N_EXP - 1).astype(i32)
    k = jnp.minimum(sidx, total - 1) - (pass_end - n_pass)[e_of]
    row0 = seg_start[e_of] + k * MOE_RS
    nrows = jnp.where(used, jnp.clip(padded[e_of] - k * MOE_RS, 0, MOE_RS), 0)
    nvalid = jnp.where(used, jnp.clip(counts[e_of] - k * MOE_RS, 0, MOE_RS), 0)
    return dst, e_of, row0.astype(i32), nrows.astype(i32), nvalid.astype(i32)


def _moe_expert_kernel(dst_ref, exp_ref, row0_ref, nrows_ref, nvalid_ref,
                       h_hbm, wg_ref, wu_ref, wd_ref, yo_hbm, xs, xb, acc, wgb, wub, wdb, gsem, ssem):
    s = pl.program_id(0)
    f = pl.program_id(1)
    nf = pl.num_programs(1)
    nrows = pl.multiple_of(nrows_ref[s], MOE_TR)
    nvalid = nvalid_ref[s]
    row0 = row0_ref[s]
    n_chunks = nrows // MOE_TR

    def row_copy(src, dst, sem):
        return pltpu.make_async_copy(src, dst, sem)

    def hbm_row(ref, i):
        return ref.at[pl.ds(i, 1), :]

    def wait_rows(buf, n_groups, sem):
        pltpu.make_async_copy(buf.at[pl.ds(0, n_groups)], buf.at[pl.ds(0, n_groups)], sem).wait()

    @pl.when((f == 0) & (nrows > 0))
    def _gather():
        def issue(g, carry):
            for u in range(SUBLANES):
                d = dst_ref[row0 + g * SUBLANES + u]
                tok = jnp.where(d >= N_TOK, d - N_TOK, d)
                row_copy(hbm_row(h_hbm, tok), xs.at[g, pl.ds(u, 1), :], gsem).start()
            return carry

        ng = nrows // SUBLANES
        lax.fori_loop(0, ng, issue, 0)
        wait_rows(xs, ng, gsem)

        def cvt(c, carry):
            g0 = pl.multiple_of(c * (MOE_TR // SUBLANES), MOE_TR // SUBLANES)
            rs = pl.ds(pl.multiple_of(c * MOE_TR, MOE_TR), MOE_TR)
            xb[rs, :] = xs[pl.ds(g0, MOE_TR // SUBLANES)].reshape(MOE_TR, D).astype(BF16)
            return carry

        lax.fori_loop(0, n_chunks, cvt, 0)

    @pl.when(nrows > 0)
    def _compute():
        wgb[...] = wg_ref[...].astype(BF16)
        wub[...] = wu_ref[...].astype(BF16)
        wdb[...] = wd_ref[...].astype(BF16)

        def chunk(start, n):
            x = xb[pl.ds(pl.multiple_of(start, MOE_TR), n), :]
            gate = _dg(x, wgb[...])
            up = _dg(x, wub[...])
            y = _dg((gate * _sigmoid(gate) * up).astype(BF16), wdb[...]).reshape(n // SUBLANES, SUBLANES, D)
            gs = pl.ds(pl.multiple_of(start // SUBLANES, MOE_TR // SUBLANES), n // SUBLANES)

            @pl.when(f == 0)
            def _():
                acc[gs] = y

            @pl.when(f > 0)
            def _():
                acc[gs] += y

        def chunk_pair(c2, carry):
            chunk(c2 * (2 * MOE_TR), 2 * MOE_TR)
            return carry

        lax.fori_loop(0, n_chunks // 2, chunk_pair, 0)

        @pl.when(n_chunks % 2 == 1)
        def _():
            chunk((n_chunks - 1) * MOE_TR, MOE_TR)

    @pl.when((f == nf - 1) & (nvalid > 0))
    def _scatter():
        n8 = nvalid // SUBLANES

        def issue(g, carry):
            for u in range(SUBLANES):
                d = dst_ref[row0 + g * SUBLANES + u]
                row_copy(acc.at[g, pl.ds(u, 1), :], hbm_row(yo_hbm, d), ssem).start()
            return carry

        def issue_tail(r, carry):
            d = dst_ref[row0 + r]
            row_copy(acc.at[n8, pl.ds(r - n8 * SUBLANES, 1), :], hbm_row(yo_hbm, d), ssem).start()
            return carry

        lax.fori_loop(0, n8, issue, 0)
        lax.fori_loop(n8 * SUBLANES, nvalid, issue_tail, 0)

        @pl.when(n8 > 0)
        def _():
            wait_rows(acc, n8, ssem)

        def wait_one(r, carry):
            row_copy(acc.at[0, pl.ds(0, 1), :], hbm_row(yo_hbm, 0), ssem).wait()
            return carry

        lax.fori_loop(n8 * SUBLANES, nvalid, wait_one, 0)


def _moe_expert_call(h, plan, wg, wu, wd, i_moe):
    nf = D_FFE // MOE_TF

    def wspec(shape, fdim):
        def imap(s, f, dst, exp, row0, nrows, nvalid):
            fe = jnp.where(nrows[s] > 0, f, nf - 1)
            return (i_moe, exp[s], 0, fe) if fdim == 3 else (i_moe, exp[s], fe, 0)
        return pl.BlockSpec(shape, imap)

    grid_spec = pltpu.PrefetchScalarGridSpec(
        num_scalar_prefetch=5,
        grid=(MOE_NST, nf),
        in_specs=[
            pl.BlockSpec(memory_space=pl.ANY),
            wspec((None, None, D, MOE_TF), 3),
            wspec((None, None, D, MOE_TF), 3),
            wspec((None, None, MOE_TF, D), 2),
        ],
        out_specs=pl.BlockSpec(memory_space=pl.ANY),
        scratch_shapes=[
            pltpu.VMEM((MOE_RS // SUBLANES, SUBLANES, D), F32),
            pltpu.VMEM((MOE_RS, D), BF16),
            pltpu.VMEM((MOE_RS // SUBLANES, SUBLANES, D), F32),
            pltpu.VMEM((D, MOE_TF), BF16), pltpu.VMEM((D, MOE_TF), BF16), pltpu.VMEM((MOE_TF, D), BF16),
            pltpu.SemaphoreType.DMA(()), pltpu.SemaphoreType.DMA(()),
        ],
    )
    return pl.pallas_call(
        _moe_expert_kernel,
        grid_spec=grid_spec,
        out_shape=jax.ShapeDtypeStruct((2 * N_TOK, D), F32),
        compiler_params=pltpu.CompilerParams(dimension_semantics=("arbitrary", "arbitrary"),
                                             vmem_limit_bytes=VMEM_LIMIT, disable_bounds_checks=True),
        name="moe_experts",
    )(*plan, h, wg, wu, wd)


def _moe_combine_kernel(x_ref, y1_ref, y2_ref, info_ref, mod_ref, lg_ref, lb_ref, o_ref):
    f = info_ref[:, R_W1:R_W1 + 1] * y1_ref[...] + info_ref[:, R_W2:R_W2 + 1] * y2_ref[...]
    y = DN_ALPHA * x_ref[...] + mod_ref[5:6, :] * f
    o_ref[...] = _layer_norm(y, lg_ref[...], lb_ref[...])


def _moe_combine_call(x, yo, info, mod_l, lg, lb, l):
    tm = MOE_TM
    nt = N_TOK // tm
    vspec = pl.BlockSpec((None, 1, D), lambda i: (l, 0, 0))
    return pl.pallas_call(
        _moe_combine_kernel,
        grid=(nt,),
        in_specs=[
            pl.BlockSpec((tm, D), lambda i: (i, 0)),
            pl.BlockSpec((tm, D), lambda i: (i, 0)),
            pl.BlockSpec((tm, D), lambda i: (nt + i, 0)),
            pl.BlockSpec((tm, 128), lambda i: (i, 0)),
            pl.BlockSpec((None, 6, D), lambda i: (_group_of_tile(i, tm), 0, 0)),
            vspec, vspec,
        ],
        out_specs=pl.BlockSpec((tm, D), lambda i: (i, 0)),
        out_shape=jax.ShapeDtypeStruct((N_TOK, D), F32),
        compiler_params=_cparams(("arbitrary",)),
        name="moe_combine",
    )(x, yo, yo, info, mod_l, lg.reshape(DEPTH, 1, D), lb.reshape(DEPTH, 1, D))


def _moe_call(x, mod_l, wr_pad, wg, wu, wd, lg, lb, l, i_moe):
    h, info, cnt = _moe_route_call(x, mod_l, wr_pad, i_moe)
    yo = _moe_expert_call(h, _moe_plan(info, cnt), wg, wu, wd, i_moe)
    return _moe_combine_call(x, yo, info, mod_l, lg, lb, l)


def kernel(x_prompt, x_sample, cache_attn_k, cache_attn_v, state_rwkv, state_gla, c, c_ctx, w_ada, b_ada, w_in,
           attn_sink, rwkv_w0, rwkv_w_a, rwkv_w_b, rwkv_a0, rwkv_a_a, rwkv_a_b, rwkv_g_a, rwkv_g_b, rwkv_k_k,
           rwkv_k_a, rwkv_r_k, rwkv_ln_g, rwkv_ln_b, gla_gate_a, gla_gate_b, gla_gate_bias, gla_norm_g, w_up_a,
           w_up_b, w_up_c, w_out, ln1_g, ln1_b, ln2_g, ln2_b, ffn_w_gate, ffn_w_up, ffn_w_down, moe_router,
           moe_w_gate, moe_w_up, moe_w_down):
    cvec = jnp.concatenate([c_ctx[None, :], c, jnp.zeros((N_GROUPS - 1 - N_DEC_B, D), F32)], axis=0)
    mods = _ada_call(cvec, w_ada, b_ada).reshape(DEPTH, N_GROUPS, 6, D)
    x = jnp.concatenate([x_prompt.reshape(N_CTX_TOK, D), x_sample.reshape(N_DEC_B * DEC_T, D)], axis=0)
    cos, sin = _rope_tables()
    kc_all = cache_attn_k.reshape(N_DEC_B, DEPTH, PAST, A_KV * A_DH)
    vc_all = cache_attn_v.reshape(N_DEC_B, DEPTH, PAST, A_KV * A_DH)
    zeros_r = jnp.zeros((N_CTX_B, 2, B_HEADS, B_DH, B_DH), F32)
    zeros_g = jnp.zeros((N_CTX_B, 2, C_HEADS, C_DK, C_DV), F32)

    new_k, new_v, new_sr, new_sg = [], [], [], []
    for l in range(DEPTH):
        mod_l = mods[l]
        w_lr = jnp.concatenate(
            [rwkv_w_a[l, 0], rwkv_w_a[l, 1], rwkv_a_a[l, 0], rwkv_a_a[l, 1], rwkv_g_a[l],
             gla_gate_a[l, 0], gla_gate_a[l, 1], jnp.zeros((D, LR_W - 416), F32)], axis=1).astype(BF16)
        pm, gates, plr = _inproj_call(x, mod_l, w_in[l, :, :IN_MAIN].astype(BF16),
                                      w_in[l, :, IN_MAIN:].astype(BF16), w_lr)

        sink_l = attn_sink[l]
        oa = (_attn_ctx_call(sink_l, pm), _attn_lat_call(sink_l, pm, kc_all, vc_all, l, cos, sin))

        rprm = (rwkv_w0[l].reshape(2, 1, B_W), rwkv_w_b[l], rwkv_a0[l].reshape(2, 1, B_W), rwkv_a_b[l],
                rwkv_g_b[l], rwkv_k_k[l].reshape(1, B_W), rwkv_k_a[l].reshape(1, B_W),
                rwkv_r_k[l].reshape(1, B_W), rwkv_ln_g[l].reshape(1, B_W), rwkv_ln_b[l].reshape(1, B_W))
        ob_c, sr_c = _rwkv_call(CTX_T, 4, N_CTX_B, 0, pm, plr, zeros_r, rprm, "rwkv_ctx")
        ob_d, _ = _rwkv_call(DEC_T, 2, N_DEC_B, N_CTX_TOK, pm, plr, state_rwkv[:, l], rprm, "rwkv_lat")

        gb_pad = jnp.zeros((2, 128, C_KW), F32)
        gb_pad = gb_pad.at[0, 0:C_GATE_RANK].set(gla_gate_b[l, 0]).at[1, C_GATE_RANK:2 * C_GATE_RANK].set(
            gla_gate_b[l, 1])
        gbias = gla_gate_bias[l].reshape(2, 1, C_KW)
        ng = jnp.tile(gla_norm_g[l], C_HEADS).reshape(1, C_VW)
        oc_c, sg_c = _gla_call(CTX_T, 2, N_CTX_B, 0, pm, plr, zeros_g, gb_pad, gbias, ng, "gla_ctx")
        oc_d, _ = _gla_call(DEC_T, 2, N_DEC_B, N_CTX_TOK, pm, plr, state_gla[:, l], gb_pad, gbias, ng, "gla_lat")

        x = _merge_call(x, gates, oa, (ob_c, ob_d), (oc_c, oc_d), mod_l, w_up_a[l].astype(BF16),
                        w_up_b[l].astype(BF16), w_up_c[l].astype(BF16), w_out[l].astype(BF16), ln1_g[l], ln1_b[l])
        if l % 2 == 0:
            i_ffn = l // 2
            x = _ffn_call(x, mod_l, ffn_w_gate[i_ffn].astype(BF16), ffn_w_up[i_ffn].astype(BF16),
                          ffn_w_down[i_ffn].astype(BF16), ln2_g[l], ln2_b[l])
        else:
            wr_pad = jnp.concatenate([moe_router, jnp.zeros((moe_router.shape[0], D, 128 - N_EXP), F32)], axis=2)
            x = _moe_call(x, mod_l, wr_pad, moe_w_gate, moe_w_up, moe_w_down, ln2_g, ln2_b, l, l // 2)

        new_k.append(pm[:N_CTX_TOK, 512:640].reshape(N_CTX_B, CTX_T, A_KV, A_DH))
        new_v.append(pm[:N_CTX_TOK, 640:768].reshape(N_CTX_B, CTX_T, A_KV, A_DH))
        new_sr.append(sr_c)
        new_sg.append(sg_c)

    y_prompt = x[:N_CTX_TOK].reshape(N_CTX_B, CTX_T, D)
    y_sample = x[N_CTX_TOK:].reshape(N_DEC_B, DEC_T, D)
    return (y_prompt, y_sample, jnp.stack(new_k, axis=1), jnp.stack(new_v, axis=1),
            jnp.stack(new_sr, axis=1), jnp.stack(new_sg, axis=1))
```

```python
import functools

import numpy as np
import jax
import jax.numpy as jnp
from jax import lax
from jax.experimental import pallas as pl
from jax.experimental.pallas import tpu as pltpu

D = 1024
N_CTX_B, CTX_T = 32, 256
N_DEC_B, DEC_T = 2, 1024
N_CTX_TOK = N_CTX_B * CTX_T
N_TOK = N_CTX_TOK + N_DEC_B * DEC_T
DEPTH = 2
PAST = 256
GRID_W = 64
A_HEADS, A_KV, A_DH = 8, 2, 64
A_G = A_HEADS // A_KV
A_WIN, A_BLK = 128, 128
ROPE_BASE = 10000.0
B_HEADS, B_DH = 4, 64
B_W = B_HEADS * B_DH
B_GN_EPS = 64e-5
C_HEADS, C_DK, C_DV = 4, 32, 64
C_KW, C_VW = C_HEADS * C_DK, C_HEADS * C_DV
C_GATE_RANK = 16
C_GATE_NORM = 16.0
C_CHUNK = 64
C_SUB = 16
D_FF = 2816
N_EXP = 8
D_FFE = 3584
LN_EPS = 1e-5
DN_ALPHA = (2.0 * DEPTH) ** 0.25
NEG_INF = -1e30
IN_MAIN = 2304
IN_GATE = 3072
LR_W = 512
N_GROUPS = 8

F32 = jnp.float32
BF16 = jnp.bfloat16
VMEM_LIMIT = 56 * 1024 * 1024
SUBLANES = 8

NN = ((1,), (0,))
NT = ((1,), (1,))
TN = ((0,), (0,))


def _dg(a, b, dims=NN):
    return lax.dot_general(a, b, (dims, ((), ())), preferred_element_type=F32)


def _split2(x):
    hi = x.astype(BF16)
    lo = (x - hi.astype(F32)).astype(BF16)
    return hi, lo


def _split3(x):
    hi = x.astype(BF16)
    r = x - hi.astype(F32)
    mid = r.astype(BF16)
    lo = (r - mid.astype(F32)).astype(BF16)
    return hi, mid, lo


def _mm(a, b, dims=NN):
    return _dg(a.astype(BF16), b.astype(BF16), dims)


def _mm3(a, b, dims=NN):
    ah, al = _split2(a)
    bh, bl = _split2(b)
    return _dg(ah, bh, dims) + (_dg(ah, bl, dims) + _dg(al, bh, dims))


def _mm_xr(a, b_exact, passes, dims=NN):
    parts = (a.astype(BF16),) if passes == 1 else (_split2(a) if passes == 2 else _split3(a))
    out = _dg(parts[0], b_exact, dims)
    for p in parts[1:]:
        out = out + _dg(p, b_exact, dims)
    return out


def _mm_xl(a_exact, b, passes):
    parts = (b.astype(BF16),) if passes == 1 else (_split2(b) if passes == 2 else _split3(b))
    out = _dg(a_exact, parts[0])
    for p in parts[1:]:
        out = out + _dg(a_exact, p)
    return out


def _sigmoid(x):
    return 1.0 / (1.0 + jnp.exp(-x))


def _iota(shape, dim):
    return lax.broadcasted_iota(jnp.int32, shape, dim)


def _block_ones(n, blk):
    return (_iota((n, n), 0) // blk == _iota((n, n), 1) // blk).astype(BF16)


def _layer_norm(y, g, b):
    mu = jnp.mean(y, -1, keepdims=True)
    yc = y - mu
    var = jnp.mean(yc * yc, -1, keepdims=True)
    return yc * lax.rsqrt(var + LN_EPS) * g + b


def _group_of_tile(i, tm):
    n_ctx = N_CTX_TOK // tm
    per_dec = DEC_T // tm
    return jnp.where(i < n_ctx, 0, 1 + (i - n_ctx) // per_dec)


def _cparams(sem):
    return pltpu.CompilerParams(dimension_semantics=sem, vmem_limit_bytes=VMEM_LIMIT)


def _ada_kernel(c_ref, w_ref, b_ref, o_ref):
    c = c_ref[...]
    s = c * _sigmoid(c)
    o_ref[...] = _mm3(s, w_ref[...]) + b_ref[...]


def _ada_call(cvec, w_ada, b_ada):
    tn = 1536
    return pl.pallas_call(
        _ada_kernel,
        grid=(DEPTH, 6 * D // tn),
        in_specs=[
            pl.BlockSpec((N_GROUPS, D), lambda l, j: (0, 0)),
            pl.BlockSpec((None, D, tn), lambda l, j: (l, 0, j)),
            pl.BlockSpec((None, 1, tn), lambda l, j: (l, 0, j)),
        ],
        out_specs=pl.BlockSpec((None, N_GROUPS, tn), lambda l, j: (l, 0, j)),
        out_shape=jax.ShapeDtypeStruct((DEPTH, N_GROUPS, 6 * D), F32),
        compiler_params=_cparams(("arbitrary", "arbitrary")),
        name="ada",
    )(cvec, w_ada, b_ada.reshape(DEPTH, 1, 6 * D))


def _pair_specs(tm, width):
    n_ctx = N_CTX_TOK // tm
    return [pl.BlockSpec((tm, width), lambda i: (jnp.minimum(i, n_ctx - 1), 0)),
            pl.BlockSpec((tm, width), lambda i: (jnp.maximum(i - n_ctx, 0), 0))]


def _read_x(x_refs, tm):
    if len(x_refs) == 1:
        return x_refs[0][...]
    return jnp.where(pl.program_id(0) < N_CTX_TOK // tm, x_refs[0][...], x_refs[1][...])


INPROJ_TM = 256


def _inproj_kernel(n_x, *refs):
    x_refs = refs[:n_x]
    mod_ref, wm_ref, wg_ref, wl_ref, om_ref, og_ref, ol_ref, ok_ref, ov_ref = refs[n_x:]
    sh = mod_ref[0:1, :]
    sc = mod_ref[1:2, :]
    h = (_read_x(x_refs, INPROJ_TM) * (1.0 + sc) + sh).astype(BF16)
    main = _dg(h, wm_ref[...])
    om_ref[...] = main
    og_ref[...] = _sigmoid(_dg(h, wg_ref[...])).astype(BF16)
    ol_ref[...] = _dg(h, wl_ref[...])

    @pl.when(pl.program_id(0) < N_CTX_TOK // INPROJ_TM)
    def _():
        ok_ref[...] = main[:, 512:640]
        ov_ref[...] = main[:, 640:768]


def _resident(shape):
    return pl.BlockSpec(shape, lambda *_: (0,) * len(shape), pipeline_mode=pl.Buffered(1))


def _inproj_call(xs, mod_l, w_main, w_gate, w_lr):
    tm = INPROJ_TM
    n_ctx = N_CTX_TOK // tm
    x_specs = [pl.BlockSpec((tm, D), lambda i: (i, 0))] if len(xs) == 1 else _pair_specs(tm, D)
    kv_spec = pl.BlockSpec((tm, A_KV * A_DH), lambda i: (jnp.minimum(i, n_ctx - 1), 0))
    return pl.pallas_call(
        functools.partial(_inproj_kernel, len(xs)),
        grid=(N_TOK // tm,),
        in_specs=[
            *x_specs,
            pl.BlockSpec((None, 6, D), lambda i: (_group_of_tile(i, tm), 0, 0)),
            _resident((D, IN_MAIN)), _resident((D, IN_GATE)), _resident((D, LR_W)),
        ],
        out_specs=[
            pl.BlockSpec((tm, IN_MAIN), lambda i: (i, 0)),
            pl.BlockSpec((tm, IN_GATE), lambda i: (i, 0)),
            pl.BlockSpec((tm, LR_W), lambda i: (i, 0)),
            kv_spec, kv_spec,
        ],
        out_shape=[
            jax.ShapeDtypeStruct((N_TOK, IN_MAIN), F32),
            jax.ShapeDtypeStruct((N_TOK, IN_GATE), BF16),
            jax.ShapeDtypeStruct((N_TOK, LR_W), F32),
            jax.ShapeDtypeStruct((N_CTX_TOK, A_KV * A_DH), F32),
            jax.ShapeDtypeStruct((N_CTX_TOK, A_KV * A_DH), F32),
        ],
        compiler_params=_cparams(("arbitrary",)),
        name="inproj",
    )(*xs, mod_l, w_main, w_gate, w_lr)


def _sink_col(sink_ref, kvh, rows_per_head):
    n = A_G * rows_per_head
    r = _iota((n, 1), 0) // rows_per_head
    col = jnp.full((n, 1), sink_ref[kvh * A_G], F32)
    for g in range(1, A_G):
        col = jnp.where(r == g, sink_ref[kvh * A_G + g], col)
    return col


def _attn_ctx_kernel(sink_ref, q_ref, k_ref, v_ref, o_ref):
    scale = A_DH ** -0.5
    for kvh in range(A_KV):
        ks = k_ref[:, kvh * A_DH:(kvh + 1) * A_DH].astype(BF16)
        vs = v_ref[:, kvh * A_DH:(kvh + 1) * A_DH].astype(BF16)
        q4 = jnp.concatenate(
            [q_ref[:, (kvh * A_G + g) * A_DH:(kvh * A_G + g + 1) * A_DH] for g in range(A_G)], axis=0)
        s = _dg(q4.astype(BF16), ks, NT) * scale
        sink = _sink_col(sink_ref, kvh, CTX_T)
        m = jnp.maximum(jnp.max(s, -1, keepdims=True), sink)
        e = jnp.exp(s - m)
        p = e / (jnp.sum(e, -1, keepdims=True) + jnp.exp(sink - m))
        o = _dg(p.astype(BF16), vs)
        for g in range(A_G):
            h = kvh * A_G + g
            o_ref[:, h * A_DH:(h + 1) * A_DH] = o[g * CTX_T:(g + 1) * CTX_T, :]


def _attn_ctx_call(sink_l, pm):
    return pl.pallas_call(
        _attn_ctx_kernel,
        grid=(N_CTX_B,),
        in_specs=[
            pl.BlockSpec(memory_space=pltpu.SMEM),
            pl.BlockSpec((CTX_T, 512), lambda b: (b, 0)),
            pl.BlockSpec((CTX_T, 128), lambda b: (b, 4)),
            pl.BlockSpec((CTX_T, 128), lambda b: (b, 5)),
        ],
        out_specs=pl.BlockSpec((CTX_T, 512), lambda b: (b, 0)),
        out_shape=jax.ShapeDtypeStruct((N_CTX_TOK, 512), F32),
        compiler_params=_cparams(("arbitrary",)),
        name="attn_ctx",
    )(sink_l, pm, pm, pm)


def _rope(x, cos, sin_signed):
    w = x.shape[-1]
    lane = _iota(x.shape, 1)
    partner = jnp.where((lane % 32) < 16, pltpu.roll(x, w - 16, 1), pltpu.roll(x, 16, 1))
    return x * cos + partner * sin_signed


def _attn_lat_kernel(sink_ref, q_ref, k_ref, v_ref, kc_ref, vc_ref, cos_ref, sin_ref, o_ref, kr_ref):
    n = pl.program_id(1)
    scale = A_DH ** -0.5

    @pl.when(n == 0)
    def _():
        kr_ref[...] = _rope(k_ref[...], cos_ref[:, 0:128], sin_ref[:, 0:128]).astype(BF16)

    q0 = pl.multiple_of(n * A_BLK, A_BLK)
    qr = _rope(q_ref[...], cos_ref[pl.ds(q0, A_BLK), :], sin_ref[pl.ds(q0, A_BLK), :])
    kstart = pl.multiple_of(jnp.clip((n - 1) * A_BLK, 0, DEC_T - 3 * A_BLK), A_BLK)
    kwin = kr_ref[pl.ds(kstart, 3 * A_BLK), :]
    vwin = v_ref[pl.ds(kstart, 3 * A_BLK), :].astype(BF16)
    kc = kc_ref[...].astype(BF16)
    vc = vc_ref[...].astype(BF16)
    rows = A_G * A_BLK
    qpos = q0 + _iota((rows, 3 * A_BLK), 0) % A_BLK
    kpos = kstart + _iota((rows, 3 * A_BLK), 1)
    valid = jnp.abs(qpos - kpos) <= A_WIN
    for kvh in range(A_KV):
        cs = slice(kvh * A_DH, (kvh + 1) * A_DH)
        q4 = jnp.concatenate(
            [qr[:, (kvh * A_G + g) * A_DH:(kvh * A_G + g + 1) * A_DH] for g in range(A_G)], axis=0).astype(BF16)
        s_loc = jnp.where(valid, _dg(q4, kwin[:, cs], NT) * scale, NEG_INF)
        s_ctx = _dg(q4, kc[:, cs], NT) * scale
        sink = _sink_col(sink_ref, kvh, A_BLK)
        m = jnp.maximum(jnp.maximum(jnp.max(s_loc, -1, keepdims=True), jnp.max(s_ctx, -1, keepdims=True)), sink)
        e_loc = jnp.exp(s_loc - m)
        e_ctx = jnp.exp(s_ctx - m)
        inv = 1.0 / (jnp.sum(e_loc, -1, keepdims=True) + jnp.sum(e_ctx, -1, keepdims=True) + jnp.exp(sink - m))
        o = _dg((e_loc * inv).astype(BF16), vwin[:, cs]) + _dg((e_ctx * inv).astype(BF16), vc[:, cs])
        for g in range(A_G):
            h = kvh * A_G + g
            o_ref[:, h * A_DH:(h + 1) * A_DH] = o[g * A_BLK:(g + 1) * A_BLK, :]


def _rope_tables():
    half = A_DH // 2
    t = np.arange(DEC_T)
    rows = (t // GRID_W).astype(np.float32)
    cols = (t % GRID_W).astype(np.float32)
    inv_freq = (ROPE_BASE ** (-np.arange(0, half, 2, dtype=np.float32) / half)).astype(np.float32)
    ang_r = rows[:, None] * inv_freq[None, :]
    ang_c = cols[:, None] * inv_freq[None, :]
    cos = np.concatenate([np.cos(ang_r), np.cos(ang_r), np.cos(ang_c), np.cos(ang_c)], -1)
    sin = np.concatenate([-np.sin(ang_r), np.sin(ang_r), -np.sin(ang_c), np.sin(ang_c)], -1)
    return (jnp.asarray(np.tile(cos, (1, A_HEADS)), F32), jnp.asarray(np.tile(sin, (1, A_HEADS)), F32))


def _attn_lat_call(sink_l, pm, kc, vc, l, cos, sin):
    nb = DEC_T // A_BLK
    row0 = N_CTX_TOK // A_BLK
    seq0 = N_CTX_TOK // DEC_T
    return pl.pallas_call(
        _attn_lat_kernel,
        grid=(N_DEC_B, nb),
        in_specs=[
            pl.BlockSpec(memory_space=pltpu.SMEM),
            pl.BlockSpec((A_BLK, 512), lambda b, n: (row0 + b * nb + n, 0)),
            pl.BlockSpec((DEC_T, 128), lambda b, n: (seq0 + b, 4)),
            pl.BlockSpec((DEC_T, 128), lambda b, n: (seq0 + b, 5)),
            pl.BlockSpec((None, None, PAST, 128), lambda b, n: (b, l, 0, 0)),
            pl.BlockSpec((None, None, PAST, 128), lambda b, n: (b, l, 0, 0)),
            pl.BlockSpec((DEC_T, 512), lambda b, n: (0, 0)),
            pl.BlockSpec((DEC_T, 512), lambda b, n: (0, 0)),
        ],
        out_specs=pl.BlockSpec((A_BLK, 512), lambda b, n: (b * nb + n, 0)),
        out_shape=jax.ShapeDtypeStruct((N_DEC_B * DEC_T, 512), F32),
        scratch_shapes=[pltpu.VMEM((DEC_T, 128), BF16)],
        compiler_params=_cparams(("arbitrary", "arbitrary")),
        name="attn_lat",
    )(sink_l, pm, pm, pm, kc, vc, cos, sin)


def _rwkv_kernel(T, NS, r_ref, k_ref, v_ref, lr_ref, s0_ref, w0_ref, wb_ref, a0_ref, ab_ref, gb_ref,
                 kk_ref, ka_ref, rk_ref, lng_ref, lnb_ref, o_ref, sfin_ref,
                 KK, W, WRP, AKK, KT, VC2, BON, Y, S):
    ones4 = _block_ones(B_W, B_DH)
    decay_c = float(np.exp(-0.5))
    RC = 256
    SUB = 16
    NP = 3

    def prep(c, carry):
        r0 = pl.multiple_of(c * RC, RC)
        rs = pl.ds(r0, RC)
        r = r_ref[rs, :]
        k = k_ref[rs, :]
        v = v_ref[rs, :]
        lr = lr_ref[rs, :]
        kkr = k * kk_ref[...]
        kk = kkr * lax.rsqrt(_mm_xr(kkr * kkr, ones4, 2) + 1e-12)
        KK[rs, :] = kk
        bonus = jnp.zeros((RC, B_W), F32)
        vc2 = jnp.zeros((RC, B_W), F32)
        for d in range(2):
            z = w0_ref[d] + _mm(jnp.tanh(lr[:, 64 * d:64 * d + 64]), wb_ref[d])
            w = jnp.exp(-decay_c * _sigmoid(z))
            a = _sigmoid(a0_ref[d] + _mm(lr[:, 128 + 64 * d:192 + 64 * d], ab_ref[d]))
            kt = k * (1.0 + (a - 1.0) * ka_ref[...])
            akk = a * kk
            W[d, rs, :] = w
            WRP[d, rs, :] = w * r - _mm_xr(akk * r, ones4, 2) * kk
            AKK[d, rs, :] = akk
            KT[d, rs, :] = kt
            vc2 = vc2 + _mm_xr(kt * r, ones4, 2) * v
            bonus = bonus + _mm_xr(r * kt * rk_ref[...], ones4, 2) * v
        VC2[rs, :] = vc2
        BON[rs, :] = bonus
        return carry

    lax.fori_loop(0, NS * T // RC, prep, 0)

    chains = [(s, d) for s in range(NS) for d in range(2)]
    for s, d in chains:
        S[s, d] = jnp.concatenate([s0_ref[s, d, h] for h in range(B_HEADS)], axis=1)

    eye4 = _iota((B_DH, B_W), 0) == (_iota((B_DH, B_W), 1) % B_DH)

    def steps(i, carry):
        tiles = []
        for s, d in chains:
            t0 = pl.multiple_of(s * T + (i * SUB if d == 0 else T - SUB - i * SUB), SUB)
            rows = pl.ds(t0, SUB)
            tiles.append((rows, KK[rows, :], WRP[d, rows, :], v_ref[rows, :],
                          W[d, rows, :], AKK[d, rows, :], KT[d, rows, :]))
        ys = [[None] * SUB for _ in chains]
        for jj in range(SUB):
            lhs = []
            for (s, d), (_, kk8, wrp8, v8, _, _, _) in zip(chains, tiles):
                j = jj if d == 0 else SUB - 1 - jj
                r = slice(j, j + 1)
                st = S[s, d]
                lhs += [(st * kk8[r]).astype(BF16), (st * wrp8[r]).astype(BF16),
                        jnp.where(eye4, v8[r], 0.0).astype(BF16)]
            res = _dg(jnp.concatenate(lhs, axis=0), ones4)
            for g, ((s, d), (_, _, _, _, w8, akk8, kt8)) in enumerate(zip(chains, tiles)):
                j = jj if d == 0 else SUB - 1 - jj
                r = slice(j, j + 1)
                sk, yp, vcol = [res[(g * NP + n) * B_DH:(g * NP + n + 1) * B_DH] for n in range(NP)]
                S[s, d] = S[s, d] * w8[r] - sk * akk8[r] + vcol * kt8[r]
                ys[g][j] = jnp.sum(jnp.where(eye4, yp, 0.0), axis=0, keepdims=True)
        for g, ((s, d), tl) in enumerate(zip(chains, tiles)):
            Y[d, tl[0], :] = jnp.concatenate(ys[g], axis=0)
        return carry

    lax.fori_loop(0, T // SUB, steps, 0)

    for s, d in chains:
        st = S[s, d]
        for h in range(B_HEADS):
            sfin_ref[s, d, h] = st[:, h * B_DH:(h + 1) * B_DH]

    def post(c, carry):
        r0 = pl.multiple_of(c * RC, RC)
        rs = pl.ds(r0, RC)
        y = Y[0, rs, :] + Y[1, rs, :] + VC2[rs, :]
        mu = _mm_xr(y, ones4, 2) * (1.0 / B_DH)
        yc = y - mu
        var = _mm_xr(yc * yc, ones4, 2) * (1.0 / B_DH)
        yn = yc * lax.rsqrt(var + B_GN_EPS) * lng_ref[...] + lnb_ref[...] + BON[rs, :]
        g = _mm(_sigmoid(lr_ref[rs, 256:384]), gb_ref[...])
        o_ref[rs, :] = yn * g
        return carry

    lax.fori_loop(0, NS * T // RC, post, 0)


def _rwkv_call(T, NS, n_seq, tok0, pm, plr, s0, prm, name):
    rows = NS * T
    blk0 = tok0 // rows
    n_steps = n_seq // NS
    full = lambda shape: pl.BlockSpec(shape, lambda b: (0,) * len(shape))
    big = lambda shape, imap: (pl.BlockSpec(shape, imap, pipeline_mode=pl.Buffered(1)) if n_steps == 1
                               else pl.BlockSpec(shape, imap))
    kern = functools.partial(_rwkv_kernel, T, NS)
    return pl.pallas_call(
        kern,
        grid=(n_steps,),
        in_specs=[
            big((rows, B_W), lambda b: (blk0 + b, 3)),
            big((rows, B_W), lambda b: (blk0 + b, 4)),
            big((rows, B_W), lambda b: (blk0 + b, 5)),
            big((rows, LR_W), lambda b: (blk0 + b, 0)),
            pl.BlockSpec((NS, 2, B_HEADS, B_DH, B_DH), lambda b: (b, 0, 0, 0, 0)),
            full((2, 1, B_W)), full((2, 64, B_W)), full((2, 1, B_W)), full((2, 64, B_W)), full((128, B_W)),
            full((1, B_W)), full((1, B_W)), full((1, B_W)), full((1, B_W)), full((1, B_W)),
        ],
        out_specs=[
            big((rows, B_W), lambda b: (b, 0)),
            pl.BlockSpec((NS, 2, B_HEADS, B_DH, B_DH), lambda b: (b, 0, 0, 0, 0)),
        ],
        out_shape=[
            jax.ShapeDtypeStruct((n_seq * T, B_W), F32),
            jax.ShapeDtypeStruct((n_seq, 2, B_HEADS, B_DH, B_DH), F32),
        ],
        scratch_shapes=[
            pltpu.VMEM((rows, B_W), F32),
            pltpu.VMEM((2, rows, B_W), F32),
            pltpu.VMEM((2, rows, B_W), F32),
            pltpu.VMEM((2, rows, B_W), F32),
            pltpu.VMEM((2, rows, B_W), F32),
            pltpu.VMEM((rows, B_W), F32),
            pltpu.VMEM((rows, B_W), F32),
            pltpu.VMEM((2, rows, B_W), F32),
            pltpu.VMEM((NS, 2, B_DH, B_W), F32),
        ],
        compiler_params=_cparams(("arbitrary",)),
        name=name,
    )(pm, pm, pm, plr, s0, *prm)


def _gla_kernel(T, NS, q_ref, k_ref, v_ref, og_ref, lr_ref, s0_ref, gb_ref, bias_ref, ng_ref,
                o_ref, sfin_ref, LA, O, S):
    n_chunks = T // C_CHUNK
    nsub = C_CHUNK // C_SUB
    qscale = C_DK ** -0.5
    lr = lr_ref[...]
    for d in range(2):
        gl = _mm(lr, gb_ref[d]) + bias_ref[d]
        LA[d] = (jnp.minimum(gl, 0.0) - jnp.log(1.0 + jnp.exp(-jnp.abs(gl)))) * (1.0 / C_GATE_NORM)
    bd_state = _iota((C_KW, C_VW), 0) // C_DK == _iota((C_KW, C_VW), 1) // C_DV
    chains = [(s, d) for s in range(NS) for d in range(2)]
    for s, d in chains:
        for h in range(C_HEADS):
            pad_l = h * C_DV
            pad_r = C_VW - (h + 1) * C_DV
            blk = s0_ref[s, d, h]
            parts = ([jnp.zeros((C_DK, pad_l), F32)] if pad_l else []) + [blk] + \
                    ([jnp.zeros((C_DK, pad_r), F32)] if pad_r else [])
            S[s, d, h * C_DK:(h + 1) * C_DK, :] = jnp.concatenate(parts, axis=1)

    ti = _iota((C_CHUNK, C_CHUNK), 0)
    si = _iota((C_CHUNK, C_CHUNK), 1)
    tri = ((si <= ti).astype(BF16), (si >= ti).astype(BF16))
    trow = _iota((C_CHUNK, 1), 0)
    mask_k = _iota((C_CHUNK, C_KW), 0) // C_SUB == _iota((C_CHUNK, C_KW), 1) // C_DK
    mask_v = _iota((C_CHUNK, C_VW), 0) // C_SUB == _iota((C_CHUNK, C_VW), 1) // C_DV
    t_att = _iota((C_CHUNK, C_CHUNK), 0)
    s_att = _iota((C_CHUNK, C_CHUNK), 1) % C_SUB
    eye_k = _iota((C_KW, C_KW), 0) == _iota((C_KW, C_KW), 1)

    def body(c, carry):
        cx = []
        for s, d in chains:
            cc = c if d == 0 else n_chunks - 1 - c
            rs = pl.ds(pl.multiple_of(s * T + cc * C_CHUNK, C_CHUNK), C_CHUNK)
            b = _mm_xl(tri[d], LA[d, rs, :], 3)
            cx.append(dict(s=s, d=d, rs=rs, b=b, q=q_ref[rs, :] * qscale, k=k_ref[rs, :], v=v_ref[rs, :]))
        for x in cx:
            x["o"] = _mm(x["q"] * jnp.exp(x["b"]), S[x["s"], x["d"]])
        for j in range(nsub):
            lo, hi = j * C_SUB, (j + 1) * C_SUB
            for x in cx:
                b, q, k = x["b"], x["q"], x["k"]
                if x["d"] == 0:
                    gamma = b[hi - 1:hi, :]
                    row_ok = trow >= lo
                    att_ok = t_att >= lo + s_att
                else:
                    gamma = b[lo:lo + 1, :]
                    row_ok = trow < hi
                    att_ok = t_att <= lo + s_att
                qj = q * jnp.exp(jnp.where(row_ok, b - gamma, NEG_INF))
                kj = k[lo:hi, :] * jnp.exp(gamma - b[lo:hi, :])
                kbd = jnp.where(mask_k, jnp.concatenate([kj] * C_HEADS, axis=0), 0.0)
                x["att"] = jnp.where(att_ok, _mm(qj, kbd, NT), 0.0)
            for x in cx:
                vbd = jnp.where(mask_v, jnp.concatenate([x["v"][lo:hi, :]] * C_HEADS, axis=0), 0.0)
                x["o"] = x["o"] + _mm(x["att"], vbd)
        for x in cx:
            s, d, b = x["s"], x["d"], x["b"]
            O[d, x["rs"], :] = x["o"]
            blast = b[C_CHUNK - 1:C_CHUNK, :] if d == 0 else b[0:1, :]
            kl = x["k"] * jnp.exp(blast - b)
            upd = jnp.where(bd_state, _mm3(kl.T, x["v"]), 0.0)
            dec = jnp.where(eye_k, jnp.exp(blast), 0.0)
            S[s, d] = _mm3(dec, S[s, d]) + upd
        return carry

    lax.fori_loop(0, n_chunks, body, 0)

    for s, d in chains:
        st = S[s, d]
        for h in range(C_HEADS):
            sfin_ref[s, d, h] = st[h * C_DK:(h + 1) * C_DK, h * C_DV:(h + 1) * C_DV]

    ones4 = _block_ones(C_VW, C_DV)
    o = O[0] + O[1]
    ms = _mm_xr(o * o, ones4, 2) * (1.0 / C_DV)
    og = og_ref[...]
    o_ref[...] = o * lax.rsqrt(ms + LN_EPS) * ng_ref[...] * (og * _sigmoid(og))


def _gla_call(T, NS, n_seq, tok0, pm, plr, s0, gb_pad, bias, ng, name):
    rows = NS * T
    blk0 = tok0 // rows
    full = lambda shape: pl.BlockSpec(shape, lambda b: (0,) * len(shape))
    return pl.pallas_call(
        functools.partial(_gla_kernel, T, NS),
        grid=(n_seq // NS,),
        in_specs=[
            pl.BlockSpec((rows, C_KW), lambda b: (blk0 + b, 12)),
            pl.BlockSpec((rows, C_KW), lambda b: (blk0 + b, 13)),
            pl.BlockSpec((rows, C_VW), lambda b: (blk0 + b, 7)),
            pl.BlockSpec((rows, C_VW), lambda b: (blk0 + b, 8)),
            pl.BlockSpec((rows, 128), lambda b: (blk0 + b, 3)),
            pl.BlockSpec((NS, 2, C_HEADS, C_DK, C_DV), lambda b: (b, 0, 0, 0, 0)),
            full((2, 128, C_KW)), full((2, 1, C_KW)), full((1, C_VW)),
        ],
        out_specs=[
            pl.BlockSpec((rows, C_VW), lambda b: (b, 0)),
            pl.BlockSpec((NS, 2, C_HEADS, C_DK, C_DV), lambda b: (b, 0, 0, 0, 0)),
        ],
        out_shape=[
            jax.ShapeDtypeStruct((n_seq * T, C_VW), F32),
            jax.ShapeDtypeStruct((n_seq, 2, C_HEADS, C_DK, C_DV), F32),
        ],
        scratch_shapes=[
            pltpu.VMEM((2, rows, C_KW), F32),
            pltpu.VMEM((2, rows, C_VW), F32),
            pltpu.VMEM((NS, 2, C_KW, C_VW), F32),
        ],
        compiler_params=_cparams(("arbitrary",)),
        name=name,
    )(pm, pm, pm, pm, plr, s0, gb_pad, bias, ng)


MERGE_TM = 512


def _merge_kernel(n_x, *refs):
    x_refs = refs[:n_x]
    (g_ref, oac_ref, oad_ref, obc_ref, obd_ref, occ_ref, ocd_ref, mod_ref,
     wa_ref, wb_ref, wc_ref, wo_ref, lg_ref, lb_ref, o_ref) = refs[n_x:]
    is_ctx = pl.program_id(0) < N_CTX_TOK // MERGE_TM
    oa = jnp.where(is_ctx, oac_ref[...], oad_ref[...]).astype(BF16)
    ob = jnp.where(is_ctx, obc_ref[...], obd_ref[...]).astype(BF16)
    oc = jnp.where(is_ctx, occ_ref[...], ocd_ref[...]).astype(BF16)
    merged = (g_ref[:, 0:D].astype(F32) * _dg(oa, wa_ref[...])
              + g_ref[:, D:2 * D].astype(F32) * _dg(ob, wb_ref[...])
              + g_ref[:, 2 * D:3 * D].astype(F32) * _dg(oc, wc_ref[...]))
    mix = _dg(merged.astype(BF16), wo_ref[...])
    y = DN_ALPHA * _read_x(x_refs, MERGE_TM) + mod_ref[2:3, :] * mix
    o_ref[...] = _layer_norm(y, lg_ref[...], lb_ref[...])


def _merge_call(xs, gates, oa, ob, oc, mod_l, wa, wb, wc, wo, lg, lb):
    tm = MERGE_TM
    x_specs = [pl.BlockSpec((tm, D), lambda i: (i, 0))] if len(xs) == 1 else _pair_specs(tm, D)
    return pl.pallas_call(
        functools.partial(_merge_kernel, len(xs)),
        grid=(N_TOK // tm,),
        in_specs=[
            *x_specs,
            pl.BlockSpec((tm, IN_GATE), lambda i: (i, 0)),
            *_pair_specs(tm, 512), *_pair_specs(tm, B_W), *_pair_specs(tm, C_VW),
            pl.BlockSpec((None, 6, D), lambda i: (_group_of_tile(i, tm), 0, 0)),
            _resident((512, D)), _resident((B_W, D)), _resident((C_VW, D)), _resident((D, D)),
            _resident((1, D)), _resident((1, D)),
        ],
        out_specs=pl.BlockSpec((tm, D), lambda i: (i, 0)),
        out_shape=jax.ShapeDtypeStruct((N_TOK, D), F32),
        compiler_params=_cparams(("arbitrary",)),
        name="merge",
    )(*xs, gates, *oa, *ob, *oc, mod_l, wa, wb, wc, wo, lg.reshape(1, D), lb.reshape(1, D))


def _ffn_kernel(x_ref, mod_ref, wg_ref, wu_ref, wd_ref, lg_ref, lb_ref, o_ref):
    x = x_ref[...]
    h = (x * (1.0 + mod_ref[4:5, :]) + mod_ref[3:4, :]).astype(BF16)
    gate = _dg(h, wg_ref[...])
    up = _dg(h, wu_ref[...])
    f = _dg((gate * _sigmoid(gate) * up).astype(BF16), wd_ref[...])
    y = DN_ALPHA * x + mod_ref[5:6, :] * f
    o_ref[...] = _layer_norm(y, lg_ref[...], lb_ref[...])


def _ffn_call(x, mod_l, wg, wu, wd, lg, lb):
    tm = 512
    return pl.pallas_call(
        _ffn_kernel,
        grid=(N_TOK // tm,),
        in_specs=[
            pl.BlockSpec((tm, D), lambda i: (i, 0)),
            pl.BlockSpec((None, 6, D), lambda i: (_group_of_tile(i, tm), 0, 0)),
            _resident((D, D_FF)), _resident((D, D_FF)), _resident((D_FF, D)),
            _resident((1, D)), _resident((1, D)),
        ],
        out_specs=pl.BlockSpec((tm, D), lambda i: (i, 0)),
        out_shape=jax.ShapeDtypeStruct((N_TOK, D), F32),
        compiler_params=_cparams(("arbitrary",)),
        name="ffn",
    )(x, mod_l, wg, wu, wd, lg.reshape(1, D), lb.reshape(1, D))


MOE_TM = 1024
MOE_TR = 256
MOE_RS = 3072
MOE_TF = 512
MOE_ROWS = 2 * N_TOK + N_EXP * MOE_TR
MOE_NST = -(-MOE_ROWS // MOE_RS) + N_EXP
R_I1, R_I2, R_W1, R_W2, R_RANK1, R_RANK2 = range(6)


def _moe_route_kernel(x_ref, mod_ref, wr_ref, h_ref, info_ref, cnt_ref, carry_s):
    tm = MOE_TM

    @pl.when(pl.program_id(0) == 0)
    def _():
        carry_s[...] = jnp.zeros_like(carry_s)

    h = x_ref[...] * (1.0 + mod_ref[4:5, :]) + mod_ref[3:4, :]
    h_ref[...] = h
    logits = _mm3(h, wr_ref[...])
    lane = _iota(logits.shape, 1)
    logits = jnp.where(lane < N_EXP, logits, NEG_INF)
    v1 = jnp.max(logits, -1, keepdims=True)
    i1 = jnp.min(jnp.where(logits == v1, lane, 128), -1, keepdims=True)
    rest = jnp.where(lane == i1, NEG_INF, logits)
    v2 = jnp.max(rest, -1, keepdims=True)
    i2 = jnp.min(jnp.where(rest == v2, lane, 128), -1, keepdims=True)
    e2 = jnp.exp(v2 - v1)
    w1 = 1.0 / (1.0 + e2)
    w2 = e2 / (1.0 + e2)
    oh1 = lane == i1
    oh2 = lane == i2
    cnt = oh1.astype(F32) + oh2.astype(F32)
    earlier = (_iota((tm, tm), 1) < _iota((tm, tm), 0)).astype(BF16)
    before = _dg(earlier, cnt.astype(BF16)) + carry_s[...]
    rank1 = jnp.sum(jnp.where(oh1, before, 0.0), -1, keepdims=True)
    rank2 = jnp.sum(jnp.where(oh2, before, 0.0), -1, keepdims=True)
    info = jnp.zeros(logits.shape, F32)
    for ln, val in ((R_I1, i1.astype(F32)), (R_I2, i2.astype(F32)), (R_W1, w1), (R_W2, w2),
                    (R_RANK1, rank1), (R_RANK2, rank2)):
        info = jnp.where(lane == ln, val, info)
    info_ref[...] = info
    carry_s[...] += jnp.sum(cnt, axis=0, keepdims=True)
    cnt_ref[...] = carry_s[...]


def _moe_route_call(x, mod_l, wr_pad, i_moe):
    tm = MOE_TM
    return pl.pallas_call(
        _moe_route_kernel,
        grid=(N_TOK // tm,),
        in_specs=[
            pl.BlockSpec((tm, D), lambda i: (i, 0)),
            pl.BlockSpec((None, 6, D), lambda i: (_group_of_tile(i, tm), 0, 0)),
            pl.BlockSpec((None, D, 128), lambda i: (i_moe, 0, 0)),
        ],
        out_specs=[
            pl.BlockSpec((tm, D), lambda i: (i, 0)),
            pl.BlockSpec((tm, 128), lambda i: (i, 0)),
            pl.BlockSpec((1, 128), lambda i: (0, 0)),
        ],
        out_shape=[
            jax.ShapeDtypeStruct((N_TOK, D), F32),
            jax.ShapeDtypeStruct((N_TOK, 128), F32),
            jax.ShapeDtypeStruct((1, 128), F32),
        ],
        scratch_shapes=[pltpu.VMEM((1, 128), F32)],
        compiler_params=_cparams(("arbitrary",)),
        name="moe_route",
    )(x, mod_l, wr_pad)


def _moe_plan(info, cnt):
    i32 = jnp.int32
    i1 = info[:, R_I1].astype(i32)
    i2 = info[:, R_I2].astype(i32)
    counts = cnt[0, :N_EXP].astype(i32)
    padded = (counts + MOE_TR - 1) // MOE_TR * MOE_TR
    seg_start = jnp.cumsum(padded) - padded
    pos1 = seg_start[i1] + info[:, R_RANK1].astype(i32)
    pos2 = seg_start[i2] + info[:, R_RANK2].astype(i32)
    dst = jnp.zeros((MOE_ROWS,), i32).at[jnp.concatenate([pos1, pos2])].set(jnp.arange(2 * N_TOK, dtype=i32))
    n_pass = (padded + MOE_RS - 1) // MOE_RS
    pass_end = jnp.cumsum(n_pass)
    total = pass_end[-1]
    sidx = jnp.arange(MOE_NST, dtype=i32)
    used = sidx < total
    e_of = jnp.minimum(jnp.searchsorted(pass_end, jnp.minimum(sidx, total - 1), side="right"), N_EXP - 1).astype(i32)
    k = jnp.minimum(sidx, total - 1) - (pass_end - n_pass)[e_of]
    row0 = seg_start[e_of] + k * MOE_RS
    nrows = jnp.where(used, jnp.clip(padded[e_of] - k * MOE_RS, 0, MOE_RS), 0)
    nvalid = jnp.where(used, jnp.clip(counts[e_of] - k * MOE_RS, 0, MOE_RS), 0)
    return dst, e_of, row0.astype(i32), nrows.astype(i32), nvalid.astype(i32)


def _moe_expert_kernel(dst_ref, exp_ref, row0_ref, nrows_ref, nvalid_ref,
                       h_hbm, wg_ref, wu_ref, wd_ref, yo_hbm, xs, xb, acc, wgb, wub, wdb, gsem, ssem):
    s = pl.program_id(0)
    f = pl.program_id(1)
    nf = pl.num_programs(1)
    nrows = pl.multiple_of(nrows_ref[s], MOE_TR)
    nvalid = nvalid_ref[s]
    row0 = row0_ref[s]
    n_chunks = nrows // MOE_TR

    def row_copy(src, dst, sem):
        return pltpu.make_async_copy(src, dst, sem)

    def hbm_row(ref, i):
        return ref.at[pl.ds(i, 1), :]

    def wait_rows(buf, n_groups, sem):
        pltpu.make_async_copy(buf.at[pl.ds(0, n_groups)], buf.at[pl.ds(0, n_groups)], sem).wait()

    @pl.when((f == 0) & (nrows > 0))
    def _gather():
        def issue(g, carry):
            for u in range(SUBLANES):
                d = dst_ref[row0 + g * SUBLANES + u]
                tok = jnp.where(d >= N_TOK, d - N_TOK, d)
                row_copy(hbm_row(h_hbm, tok), xs.at[g, pl.ds(u, 1), :], gsem).start()
            return carry

        ng = nrows // SUBLANES
        lax.fori_loop(0, ng, issue, 0)
        wait_rows(xs, ng, gsem)

        def cvt(c, carry):
            g0 = pl.multiple_of(c * (MOE_TR // SUBLANES), MOE_TR // SUBLANES)
            rs = pl.ds(pl.multiple_of(c * MOE_TR, MOE_TR), MOE_TR)
            xb[rs, :] = xs[pl.ds(g0, MOE_TR // SUBLANES)].reshape(MOE_TR, D).astype(BF16)
            return carry

        lax.fori_loop(0, n_chunks, cvt, 0)

    @pl.when(nrows > 0)
    def _compute():
        wgb[...] = wg_ref[...].astype(BF16)
        wub[...] = wu_ref[...].astype(BF16)
        wdb[...] = wd_ref[...].astype(BF16)

        def chunk(start, n):
            x = xb[pl.ds(pl.multiple_of(start, MOE_TR), n), :]
            gate = _dg(x, wgb[...])
            up = _dg(x, wub[...])
            y = _dg((gate * _sigmoid(gate) * up).astype(BF16), wdb[...]).reshape(n // SUBLANES, SUBLANES, D)
            gs = pl.ds(pl.multiple_of(start // SUBLANES, MOE_TR // SUBLANES), n // SUBLANES)

            @pl.when(f == 0)
            def _():
                acc[gs] = y

            @pl.when(f > 0)
            def _():
                acc[gs] += y

        def chunk_quad(c4, carry):
            chunk(c4 * (4 * MOE_TR), 4 * MOE_TR)
            return carry

        lax.fori_loop(0, n_chunks // 4, chunk_quad, 0)
        done = n_chunks // 4 * 4

        @pl.when(n_chunks % 4 >= 2)
        def _():
            chunk(done * MOE_TR, 2 * MOE_TR)

        @pl.when(n_chunks % 2 == 1)
        def _():
            chunk((n_chunks - 1) * MOE_TR, MOE_TR)

    @pl.when((f == nf - 1) & (nvalid > 0))
    def _scatter():
        n8 = nvalid // SUBLANES

        def issue(g, carry):
            for u in range(SUBLANES):
                d = dst_ref[row0 + g * SUBLANES + u]
                row_copy(acc.at[g, pl.ds(u, 1), :], hbm_row(yo_hbm, d), ssem).start()
            return carry

        def issue_tail(r, carry):
            d = dst_ref[row0 + r]
            row_copy(acc.at[n8, pl.ds(r - n8 * SUBLANES, 1), :], hbm_row(yo_hbm, d), ssem).start()
            return carry

        lax.fori_loop(0, n8, issue, 0)
        lax.fori_loop(n8 * SUBLANES, nvalid, issue_tail, 0)

        @pl.when(n8 > 0)
        def _():
            wait_rows(acc, n8, ssem)

        def wait_one(r, carry):
            row_copy(acc.at[0, pl.ds(0, 1), :], hbm_row(yo_hbm, 0), ssem).wait()
            return carry

        lax.fori_loop(n8 * SUBLANES, nvalid, wait_one, 0)


def _moe_expert_call(h, plan, wg, wu, wd, i_moe):
    nf = D_FFE // MOE_TF

    def wspec(shape, fdim):
        def imap(s, f, dst, exp, row0, nrows, nvalid):
            fe = jnp.where(nrows[s] > 0, f, nf - 1)
            return (i_moe, exp[s], 0, fe) if fdim == 3 else (i_moe, exp[s], fe, 0)
        return pl.BlockSpec(shape, imap)

    grid_spec = pltpu.PrefetchScalarGridSpec(
        num_scalar_prefetch=5,
        grid=(MOE_NST, nf),
        in_specs=[
            pl.BlockSpec(memory_space=pl.ANY),
            wspec((None, None, D, MOE_TF), 3),
            wspec((None, None, D, MOE_TF), 3),
            wspec((None, None, MOE_TF, D), 2),
        ],
        out_specs=pl.BlockSpec(memory_space=pl.ANY),
        scratch_shapes=[
            pltpu.VMEM((MOE_RS // SUBLANES, SUBLANES, D), F32),
            pltpu.VMEM((MOE_RS, D), BF16),
            pltpu.VMEM((MOE_RS // SUBLANES, SUBLANES, D), F32),
            pltpu.VMEM((D, MOE_TF), BF16), pltpu.VMEM((D, MOE_TF), BF16), pltpu.VMEM((MOE_TF, D), BF16),
            pltpu.SemaphoreType.DMA(()), pltpu.SemaphoreType.DMA(()),
        ],
    )
    return pl.pallas_call(
        _moe_expert_kernel,
        grid_spec=grid_spec,
        out_shape=jax.ShapeDtypeStruct((2 * N_TOK, D), F32),
        compiler_params=pltpu.CompilerParams(dimension_semantics=("arbitrary", "arbitrary"),
                                             vmem_limit_bytes=VMEM_LIMIT, disable_bounds_checks=True),
        name="moe_experts",
    )(*plan, h, wg, wu, wd)


def _moe_combine_kernel(x_ref, y1_ref, y2_ref, info_ref, mod_ref, lg_ref, lb_ref, oc_ref, od_ref):
    f = info_ref[:, R_W1:R_W1 + 1] * y1_ref[...] + info_ref[:, R_W2:R_W2 + 1] * y2_ref[...]
    y = DN_ALPHA * x_ref[...] + mod_ref[5:6, :] * f
    out = _layer_norm(y, lg_ref[...], lb_ref[...])
    is_ctx = pl.program_id(0) < N_CTX_TOK // MOE_TM

    @pl.when(is_ctx)
    def _():
        oc_ref[...] = out

    @pl.when(jnp.logical_not(is_ctx))
    def _():
        od_ref[...] = out


def _moe_combine_call(x, yo, info, mod_l, lg, lb):
    tm = MOE_TM
    nt = N_TOK // tm
    return pl.pallas_call(
        _moe_combine_kernel,
        grid=(nt,),
        in_specs=[
            pl.BlockSpec((tm, D), lambda i: (i, 0)),
            pl.BlockSpec((tm, D), lambda i: (i, 0)),
            pl.BlockSpec((tm, D), lambda i: (nt + i, 0)),
            pl.BlockSpec((tm, 128), lambda i: (i, 0)),
            pl.BlockSpec((None, 6, D), lambda i: (_group_of_tile(i, tm), 0, 0)),
            _resident((1, D)), _resident((1, D)),
        ],
        out_specs=_pair_specs(tm, D),
        out_shape=[jax.ShapeDtypeStruct((N_CTX_TOK, D), F32), jax.ShapeDtypeStruct((N_DEC_B * DEC_T, D), F32)],
        compiler_params=_cparams(("arbitrary",)),
        name="moe_combine",
    )(x, yo, yo, info, mod_l, lg.reshape(1, D), lb.reshape(1, D))


def _moe_call(x, mod_l, wr_pad, wg, wu, wd, lg, lb, i_moe):
    h, info, cnt = _moe_route_call(x, mod_l, wr_pad, i_moe)
    yo = _moe_expert_call(h, _moe_plan(info, cnt), wg, wu, wd, i_moe)
    return _moe_combine_call(x, yo, info, mod_l, lg, lb)


def kernel(x_prompt, x_sample, cache_attn_k, cache_attn_v, state_rwkv, state_gla, c, c_ctx, w_ada, b_ada, w_in,
           attn_sink, rwkv_w0, rwkv_w_a, rwkv_w_b, rwkv_a0, rwkv_a_a, rwkv_a_b, rwkv_g_a, rwkv_g_b, rwkv_k_k,
           rwkv_k_a, rwkv_r_k, rwkv_ln_g, rwkv_ln_b, gla_gate_a, gla_gate_b, gla_gate_bias, gla_norm_g, w_up_a,
           w_up_b, w_up_c, w_out, ln1_g, ln1_b, ln2_g, ln2_b, ffn_w_gate, ffn_w_up, ffn_w_down, moe_router,
           moe_w_gate, moe_w_up, moe_w_down):
    cvec = jnp.concatenate([c_ctx[None, :], c, jnp.zeros((N_GROUPS - 1 - N_DEC_B, D), F32)], axis=0)
    mods = _ada_call(cvec, w_ada, b_ada).reshape(DEPTH, N_GROUPS, 6, D)
    xs = (x_prompt.reshape(N_CTX_TOK, D), x_sample.reshape(N_DEC_B * DEC_T, D))
    cos, sin = _rope_tables()
    kc_all = cache_attn_k.reshape(N_DEC_B, DEPTH, PAST, A_KV * A_DH)
    vc_all = cache_attn_v.reshape(N_DEC_B, DEPTH, PAST, A_KV * A_DH)
    zeros_r = jnp.zeros((N_CTX_B, 2, B_HEADS, B_DH, B_DH), F32)
    zeros_g = jnp.zeros((N_CTX_B, 2, C_HEADS, C_DK, C_DV), F32)

    new_k, new_v, new_sr, new_sg = [], [], [], []
    for l in range(DEPTH):
        mod_l = mods[l]
        w_lr = jnp.concatenate(
            [rwkv_w_a[l, 0], rwkv_w_a[l, 1], rwkv_a_a[l, 0], rwkv_a_a[l, 1], rwkv_g_a[l],
             gla_gate_a[l, 0], gla_gate_a[l, 1], jnp.zeros((D, LR_W - 416), F32)], axis=1).astype(BF16)
        pm, gates, plr, k_ctx, v_ctx = _inproj_call(xs, mod_l, w_in[l, :, :IN_MAIN].astype(BF16),
                                                    w_in[l, :, IN_MAIN:].astype(BF16), w_lr)

        sink_l = attn_sink[l]
        oa = (_attn_ctx_call(sink_l, pm), _attn_lat_call(sink_l, pm, kc_all, vc_all, l, cos, sin))

        rprm = (rwkv_w0[l].reshape(2, 1, B_W), rwkv_w_b[l], rwkv_a0[l].reshape(2, 1, B_W), rwkv_a_b[l],
                rwkv_g_b[l], rwkv_k_k[l].reshape(1, B_W), rwkv_k_a[l].reshape(1, B_W),
                rwkv_r_k[l].reshape(1, B_W), rwkv_ln_g[l].reshape(1, B_W), rwkv_ln_b[l].reshape(1, B_W))
        ob_c, sr_c = _rwkv_call(CTX_T, 4, N_CTX_B, 0, pm, plr, zeros_r, rprm, "rwkv_ctx")
        ob_d, _ = _rwkv_call(DEC_T, 2, N_DEC_B, N_CTX_TOK, pm, plr, state_rwkv[:, l], rprm, "rwkv_lat")

        gb_pad = jnp.zeros((2, 128, C_KW), F32)
        gb_pad = gb_pad.at[0, 0:C_GATE_RANK].set(gla_gate_b[l, 0]).at[1, C_GATE_RANK:2 * C_GATE_RANK].set(
            gla_gate_b[l, 1])
        gbias = gla_gate_bias[l].reshape(2, 1, C_KW)
        ng = jnp.tile(gla_norm_g[l], C_HEADS).reshape(1, C_VW)
        oc_c, sg_c = _gla_call(CTX_T, 4, N_CTX_B, 0, pm, plr, zeros_g, gb_pad, gbias, ng, "gla_ctx")
        oc_d, _ = _gla_call(DEC_T, 2, N_DEC_B, N_CTX_TOK, pm, plr, state_gla[:, l], gb_pad, gbias, ng, "gla_lat")

        x = _merge_call(xs, gates, oa, (ob_c, ob_d), (oc_c, oc_d), mod_l, w_up_a[l].astype(BF16),
                        w_up_b[l].astype(BF16), w_up_c[l].astype(BF16), w_out[l].astype(BF16), ln1_g[l], ln1_b[l])
        if l % 2 == 0:
            i_ffn = l // 2
            xs = (_ffn_call(x, mod_l, ffn_w_gate[i_ffn].astype(BF16), ffn_w_up[i_ffn].astype(BF16),
                            ffn_w_down[i_ffn].astype(BF16), ln2_g[l], ln2_b[l]),)
        else:
            wr_pad = jnp.concatenate([moe_router, jnp.zeros((moe_router.shape[0], D, 128 - N_EXP), F32)], axis=2)
            xs = _moe_call(x, mod_l, wr_pad, moe_w_gate, moe_w_up, moe_w_down, ln2_g[l], ln2_b[l], l // 2)

        new_k.append(k_ctx.reshape(N_CTX_B, CTX_T, A_KV, A_DH))
        new_v.append(v_ctx.reshape(N_CTX_B, CTX_T, A_KV, A_DH))
        new_sr.append(sr_c)
        new_sg.append(sg_c)

    if len(xs) == 1:
        xs = (xs[0][:N_CTX_TOK], xs[0][N_CTX_TOK:])
    y_prompt = xs[0].reshape(N_CTX_B, CTX_T, D)
    y_sample = xs[1].reshape(N_DEC_B, DEC_T, D)
    return (y_prompt, y_sample, jnp.stack(new_k, axis=1), jnp.stack(new_v, axis=1),
            jnp.stack(new_sr, axis=1), jnp.stack(new_sg, axis=1))
```

```python
import functools

import numpy as np
import jax
import jax.numpy as jnp
from jax import lax
from jax.experimental import pallas as pl
from jax.experimental.pallas import tpu as pltpu

D = 1024
N_CTX_B, CTX_T = 32, 256
N_DEC_B, DEC_T = 2, 1024
N_CTX_TOK = N_CTX_B * CTX_T
N_TOK = N_CTX_TOK + N_DEC_B * DEC_T
DEPTH = 2
PAST = 256
GRID_W = 64
A_HEADS, A_KV, A_DH = 8, 2, 64
A_G = A_HEADS // A_KV
A_WIN, A_BLK = 128, 128
ROPE_BASE = 10000.0
B_HEADS, B_DH = 4, 64
B_W = B_HEADS * B_DH
B_GN_EPS = 64e-5
C_HEADS, C_DK, C_DV = 4, 32, 64
C_KW, C_VW = C_HEADS * C_DK, C_HEADS * C_DV
C_GATE_RANK = 16
C_GATE_NORM = 16.0
C_CHUNK = 64
C_SUB = 16
D_FF = 2816
N_EXP = 8
D_FFE = 3584
LN_EPS = 1e-5
DN_ALPHA = (2.0 * DEPTH) ** 0.25
NEG_INF = -1e30
IN_MAIN = 2304
IN_GATE = 3072
LR_W = 512
N_GROUPS = 8

F32 = jnp.float32
BF16 = jnp.bfloat16
VMEM_LIMIT = 56 * 1024 * 1024
SUBLANES = 8

NN = ((1,), (0,))
NT = ((1,), (1,))
TN = ((0,), (0,))


def _dg(a, b, dims=NN):
    return lax.dot_general(a, b, (dims, ((), ())), preferred_element_type=F32)


def _split2(x):
    hi = x.astype(BF16)
    lo = (x - hi.astype(F32)).astype(BF16)
    return hi, lo


def _split3(x):
    hi = x.astype(BF16)
    r = x - hi.astype(F32)
    mid = r.astype(BF16)
    lo = (r - mid.astype(F32)).astype(BF16)
    return hi, mid, lo


def _mm(a, b, dims=NN):
    return _dg(a.astype(BF16), b.astype(BF16), dims)


def _mm3(a, b, dims=NN):
    ah, al = _split2(a)
    bh, bl = _split2(b)
    return _dg(ah, bh, dims) + (_dg(ah, bl, dims) + _dg(al, bh, dims))


def _mm_xr(a, b_exact, passes, dims=NN):
    parts = (a.astype(BF16),) if passes == 1 else (_split2(a) if passes == 2 else _split3(a))
    out = _dg(parts[0], b_exact, dims)
    for p in parts[1:]:
        out = out + _dg(p, b_exact, dims)
    return out


def _mm_xl(a_exact, b, passes):
    parts = (b.astype(BF16),) if passes == 1 else (_split2(b) if passes == 2 else _split3(b))
    out = _dg(a_exact, parts[0])
    for p in parts[1:]:
        out = out + _dg(a_exact, p)
    return out


def _sigmoid(x):
    return 1.0 / (1.0 + jnp.exp(-x))


def _iota(shape, dim):
    return lax.broadcasted_iota(jnp.int32, shape, dim)


def _block_ones(n, blk):
    return (_iota((n, n), 0) // blk == _iota((n, n), 1) // blk).astype(BF16)


def _layer_norm(y, g, b):
    mu = jnp.mean(y, -1, keepdims=True)
    yc = y - mu
    var = jnp.mean(yc * yc, -1, keepdims=True)
    return yc * lax.rsqrt(var + LN_EPS) * g + b


def _group_of_tile(i, tm):
    n_ctx = N_CTX_TOK // tm
    per_dec = DEC_T // tm
    return jnp.where(i < n_ctx, 0, 1 + (i - n_ctx) // per_dec)


def _cparams(sem):
    return pltpu.CompilerParams(dimension_semantics=sem, vmem_limit_bytes=VMEM_LIMIT)


def _ada_kernel(c_ref, w_ref, b_ref, o_ref):
    c = c_ref[...]
    s = c * _sigmoid(c)
    o_ref[...] = _mm3(s, w_ref[...]) + b_ref[...]


def _ada_call(cvec, w_ada, b_ada):
    tn = 1536
    return pl.pallas_call(
        _ada_kernel,
        grid=(DEPTH, 6 * D // tn),
        in_specs=[
            pl.BlockSpec((N_GROUPS, D), lambda l, j: (0, 0)),
            pl.BlockSpec((None, D, tn), lambda l, j: (l, 0, j)),
            pl.BlockSpec((None, 1, tn), lambda l, j: (l, 0, j)),
        ],
        out_specs=pl.BlockSpec((None, N_GROUPS, tn), lambda l, j: (l, 0, j)),
        out_shape=jax.ShapeDtypeStruct((DEPTH, N_GROUPS, 6 * D), F32),
        compiler_params=_cparams(("arbitrary", "arbitrary")),
        name="ada",
    )(cvec, w_ada, b_ada.reshape(DEPTH, 1, 6 * D))


def _pair_specs(tm, width):
    n_ctx = N_CTX_TOK // tm
    return [pl.BlockSpec((tm, width), lambda i: (jnp.minimum(i, n_ctx - 1), 0)),
            pl.BlockSpec((tm, width), lambda i: (jnp.maximum(i - n_ctx, 0), 0))]


def _read_x(x_refs, tm):
    if len(x_refs) == 1:
        return x_refs[0][...]
    return jnp.where(pl.program_id(0) < N_CTX_TOK // tm, x_refs[0][...], x_refs[1][...])


INPROJ_TM = 256


def _inproj_kernel(n_x, *refs):
    x_refs = refs[:n_x]
    mod_ref, wm_ref, wg_ref, wl_ref, om_ref, og_ref, ol_ref, ok_ref, ov_ref = refs[n_x:]
    sh = mod_ref[0:1, :]
    sc = mod_ref[1:2, :]
    h = (_read_x(x_refs, INPROJ_TM) * (1.0 + sc) + sh).astype(BF16)
    main = _dg(h, wm_ref[...])
    om_ref[...] = main
    og_ref[...] = _sigmoid(_dg(h, wg_ref[...])).astype(BF16)
    ol_ref[...] = _dg(h, wl_ref[...])

    @pl.when(pl.program_id(0) < N_CTX_TOK // INPROJ_TM)
    def _():
        ok_ref[...] = main[:, 512:640]
        ov_ref[...] = main[:, 640:768]


def _resident(shape):
    return pl.BlockSpec(shape, lambda *_: (0,) * len(shape), pipeline_mode=pl.Buffered(1))


def _inproj_call(xs, mod_l, w_main, w_gate, w_lr):
    tm = INPROJ_TM
    n_ctx = N_CTX_TOK // tm
    x_specs = [pl.BlockSpec((tm, D), lambda i: (i, 0))] if len(xs) == 1 else _pair_specs(tm, D)
    kv_spec = pl.BlockSpec((tm, A_KV * A_DH), lambda i: (jnp.minimum(i, n_ctx - 1), 0))
    return pl.pallas_call(
        functools.partial(_inproj_kernel, len(xs)),
        grid=(N_TOK // tm,),
        in_specs=[
            *x_specs,
            pl.BlockSpec((None, 6, D), lambda i: (_group_of_tile(i, tm), 0, 0)),
            _resident((D, IN_MAIN)), _resident((D, IN_GATE)), _resident((D, LR_W)),
        ],
        out_specs=[
            pl.BlockSpec((tm, IN_MAIN), lambda i: (i, 0)),
            pl.BlockSpec((tm, IN_GATE), lambda i: (i, 0)),
            pl.BlockSpec((tm, LR_W), lambda i: (i, 0)),
            kv_spec, kv_spec,
        ],
        out_shape=[
            jax.ShapeDtypeStruct((N_TOK, IN_MAIN), F32),
            jax.ShapeDtypeStruct((N_TOK, IN_GATE), BF16),
            jax.ShapeDtypeStruct((N_TOK, LR_W), F32),
            jax.ShapeDtypeStruct((N_CTX_TOK, A_KV * A_DH), F32),
            jax.ShapeDtypeStruct((N_CTX_TOK, A_KV * A_DH), F32),
        ],
        compiler_params=_cparams(("arbitrary",)),
        name="inproj",
    )(*xs, mod_l, w_main, w_gate, w_lr)


def _sink_col(sink_ref, kvh, rows_per_head):
    n = A_G * rows_per_head
    r = _iota((n, 1), 0) // rows_per_head
    col = jnp.full((n, 1), sink_ref[kvh * A_G], F32)
    for g in range(1, A_G):
        col = jnp.where(r == g, sink_ref[kvh * A_G + g], col)
    return col


def _attn_ctx_kernel(sink_ref, q_ref, k_ref, v_ref, o_ref):
    scale = A_DH ** -0.5
    for kvh in range(A_KV):
        ks = k_ref[:, kvh * A_DH:(kvh + 1) * A_DH].astype(BF16)
        vs = v_ref[:, kvh * A_DH:(kvh + 1) * A_DH].astype(BF16)
        q4 = jnp.concatenate(
            [q_ref[:, (kvh * A_G + g) * A_DH:(kvh * A_G + g + 1) * A_DH] for g in range(A_G)], axis=0)
        s = _dg(q4.astype(BF16), ks, NT) * scale
        sink = _sink_col(sink_ref, kvh, CTX_T)
        m = jnp.maximum(jnp.max(s, -1, keepdims=True), sink)
        e = jnp.exp(s - m)
        p = e / (jnp.sum(e, -1, keepdims=True) + jnp.exp(sink - m))
        o = _dg(p.astype(BF16), vs)
        for g in range(A_G):
            h = kvh * A_G + g
            o_ref[:, h * A_DH:(h + 1) * A_DH] = o[g * CTX_T:(g + 1) * CTX_T, :]


def _attn_ctx_call(sink_l, pm):
    return pl.pallas_call(
        _attn_ctx_kernel,
        grid=(N_CTX_B,),
        in_specs=[
            pl.BlockSpec(memory_space=pltpu.SMEM),
            pl.BlockSpec((CTX_T, 512), lambda b: (b, 0)),
            pl.BlockSpec((CTX_T, 128), lambda b: (b, 4)),
            pl.BlockSpec((CTX_T, 128), lambda b: (b, 5)),
        ],
        out_specs=pl.BlockSpec((CTX_T, 512), lambda b: (b, 0)),
        out_shape=jax.ShapeDtypeStruct((N_CTX_TOK, 512), F32),
        compiler_params=_cparams(("arbitrary",)),
        name="attn_ctx",
    )(sink_l, pm, pm, pm)


def _rope(x, cos, sin_signed):
    w = x.shape[-1]
    lane = _iota(x.shape, 1)
    partner = jnp.where((lane % 32) < 16, pltpu.roll(x, w - 16, 1), pltpu.roll(x, 16, 1))
    return x * cos + partner * sin_signed


def _attn_lat_kernel(sink_ref, q_ref, k_ref, v_ref, kc_ref, vc_ref, cos_ref, sin_ref, o_ref, kr_ref):
    n = pl.program_id(1)
    scale = A_DH ** -0.5

    @pl.when(n == 0)
    def _():
        kr_ref[...] = _rope(k_ref[...], cos_ref[:, 0:128], sin_ref[:, 0:128]).astype(BF16)

    q0 = pl.multiple_of(n * A_BLK, A_BLK)
    qr = _rope(q_ref[...], cos_ref[pl.ds(q0, A_BLK), :], sin_ref[pl.ds(q0, A_BLK), :])
    kstart = pl.multiple_of(jnp.clip((n - 1) * A_BLK, 0, DEC_T - 3 * A_BLK), A_BLK)
    kwin = kr_ref[pl.ds(kstart, 3 * A_BLK), :]
    vwin = v_ref[pl.ds(kstart, 3 * A_BLK), :].astype(BF16)
    kc = kc_ref[...].astype(BF16)
    vc = vc_ref[...].astype(BF16)
    rows = A_G * A_BLK
    qpos = q0 + _iota((rows, 3 * A_BLK), 0) % A_BLK
    kpos = kstart + _iota((rows, 3 * A_BLK), 1)
    valid = jnp.abs(qpos - kpos) <= A_WIN
    for kvh in range(A_KV):
        cs = slice(kvh * A_DH, (kvh + 1) * A_DH)
        q4 = jnp.concatenate(
            [qr[:, (kvh * A_G + g) * A_DH:(kvh * A_G + g + 1) * A_DH] for g in range(A_G)], axis=0).astype(BF16)
        s_loc = jnp.where(valid, _dg(q4, kwin[:, cs], NT) * scale, NEG_INF)
        s_ctx = _dg(q4, kc[:, cs], NT) * scale
        sink = _sink_col(sink_ref, kvh, A_BLK)
        m = jnp.maximum(jnp.maximum(jnp.max(s_loc, -1, keepdims=True), jnp.max(s_ctx, -1, keepdims=True)), sink)
        e_loc = jnp.exp(s_loc - m)
        e_ctx = jnp.exp(s_ctx - m)
        inv = 1.0 / (jnp.sum(e_loc, -1, keepdims=True) + jnp.sum(e_ctx, -1, keepdims=True) + jnp.exp(sink - m))
        o = _dg((e_loc * inv).astype(BF16), vwin[:, cs]) + _dg((e_ctx * inv).astype(BF16), vc[:, cs])
        for g in range(A_G):
            h = kvh * A_G + g
            o_ref[:, h * A_DH:(h + 1) * A_DH] = o[g * A_BLK:(g + 1) * A_BLK, :]


def _rope_tables():
    half = A_DH // 2
    t = np.arange(DEC_T)
    rows = (t // GRID_W).astype(np.float32)
    cols = (t % GRID_W).astype(np.float32)
    inv_freq = (ROPE_BASE ** (-np.arange(0, half, 2, dtype=np.float32) / half)).astype(np.float32)
    ang_r = rows[:, None] * inv_freq[None, :]
    ang_c = cols[:, None] * inv_freq[None, :]
    cos = np.concatenate([np.cos(ang_r), np.cos(ang_r), np.cos(ang_c), np.cos(ang_c)], -1)
    sin = np.concatenate([-np.sin(ang_r), np.sin(ang_r), -np.sin(ang_c), np.sin(ang_c)], -1)
    return (jnp.asarray(np.tile(cos, (1, A_HEADS)), F32), jnp.asarray(np.tile(sin, (1, A_HEADS)), F32))


def _attn_lat_call(sink_l, pm, kc, vc, l, cos, sin):
    nb = DEC_T // A_BLK
    row0 = N_CTX_TOK // A_BLK
    seq0 = N_CTX_TOK // DEC_T
    return pl.pallas_call(
        _attn_lat_kernel,
        grid=(N_DEC_B, nb),
        in_specs=[
            pl.BlockSpec(memory_space=pltpu.SMEM),
            pl.BlockSpec((A_BLK, 512), lambda b, n: (row0 + b * nb + n, 0)),
            pl.BlockSpec((DEC_T, 128), lambda b, n: (seq0 + b, 4)),
            pl.BlockSpec((DEC_T, 128), lambda b, n: (seq0 + b, 5)),
            pl.BlockSpec((None, None, PAST, 128), lambda b, n: (b, l, 0, 0)),
            pl.BlockSpec((None, None, PAST, 128), lambda b, n: (b, l, 0, 0)),
            pl.BlockSpec((DEC_T, 512), lambda b, n: (0, 0)),
            pl.BlockSpec((DEC_T, 512), lambda b, n: (0, 0)),
        ],
        out_specs=pl.BlockSpec((A_BLK, 512), lambda b, n: (b * nb + n, 0)),
        out_shape=jax.ShapeDtypeStruct((N_DEC_B * DEC_T, 512), F32),
        scratch_shapes=[pltpu.VMEM((DEC_T, 128), BF16)],
        compiler_params=_cparams(("arbitrary", "arbitrary")),
        name="attn_lat",
    )(sink_l, pm, pm, pm, kc, vc, cos, sin)


def _rwkv_kernel(T, NS, r_ref, k_ref, v_ref, lr_ref, s0_ref, w0_ref, wb_ref, a0_ref, ab_ref, gb_ref,
                 kk_ref, ka_ref, rk_ref, lng_ref, lnb_ref, o_ref, sfin_ref,
                 KK, W, WRP, AKK, KT, VC2, BON, Y, S):
    ones4 = _block_ones(B_W, B_DH)
    decay_c = float(np.exp(-0.5))
    RC = 256
    SUB = 16
    NP = 3

    def prep(c, carry):
        r0 = pl.multiple_of(c * RC, RC)
        rs = pl.ds(r0, RC)
        r = r_ref[rs, :]
        k = k_ref[rs, :]
        v = v_ref[rs, :]
        lr = lr_ref[rs, :]
        kkr = k * kk_ref[...]
        kk = kkr * lax.rsqrt(_mm_xr(kkr * kkr, ones4, 2) + 1e-12)
        KK[rs, :] = kk
        bonus = jnp.zeros((RC, B_W), F32)
        vc2 = jnp.zeros((RC, B_W), F32)
        for d in range(2):
            z = w0_ref[d] + _mm(jnp.tanh(lr[:, 64 * d:64 * d + 64]), wb_ref[d])
            w = jnp.exp(-decay_c * _sigmoid(z))
            a = _sigmoid(a0_ref[d] + _mm(lr[:, 128 + 64 * d:192 + 64 * d], ab_ref[d]))
            kt = k * (1.0 + (a - 1.0) * ka_ref[...])
            akk = a * kk
            W[d, rs, :] = w
            WRP[d, rs, :] = w * r - _mm_xr(akk * r, ones4, 2) * kk
            AKK[d, rs, :] = akk
            KT[d, rs, :] = kt
            vc2 = vc2 + _mm_xr(kt * r, ones4, 2) * v
            bonus = bonus + _mm_xr(r * kt * rk_ref[...], ones4, 2) * v
        VC2[rs, :] = vc2
        BON[rs, :] = bonus
        return carry

    lax.fori_loop(0, NS * T // RC, prep, 0)

    chains = [(s, d) for s in range(NS) for d in range(2)]
    for s, d in chains:
        S[s, d] = jnp.concatenate([s0_ref[s, d, h] for h in range(B_HEADS)], axis=1)

    eye4 = _iota((B_DH, B_W), 0) == (_iota((B_DH, B_W), 1) % B_DH)

    def steps(i, carry):
        t0s = [pl.multiple_of(s * T + (i * SUB if d == 0 else T - SUB - i * SUB), SUB) for s, d in chains]

        def row(ref, g, j, d=None):
            tile = pl.ds(t0s[g] + (j // SUBLANES) * SUBLANES, SUBLANES)
            vals = ref[tile, :] if d is None else ref[d, tile, :]
            return vals[j % SUBLANES:j % SUBLANES + 1]

        ys = [[None] * SUB for _ in chains]
        for jj in range(SUB):
            lhs = []
            for g, (s, d) in enumerate(chains):
                j = jj if d == 0 else SUB - 1 - jj
                stb = S[s, d].astype(BF16)
                lhs += [stb * row(KK, g, j).astype(BF16), stb * row(WRP, g, j, d).astype(BF16),
                        jnp.where(eye4, row(v_ref, g, j), 0.0).astype(BF16)]
            res = _dg(jnp.concatenate(lhs, axis=0), ones4)
            for g, (s, d) in enumerate(chains):
                j = jj if d == 0 else SUB - 1 - jj
                sk, yp, vcol = [res[(g * NP + n) * B_DH:(g * NP + n + 1) * B_DH] for n in range(NP)]
                S[s, d] = S[s, d] * row(W, g, j, d) - sk * row(AKK, g, j, d) + vcol * row(KT, g, j, d)
                ys[g][j] = jnp.sum(jnp.where(eye4, yp, 0.0), axis=0, keepdims=True)
        for g, (s, d) in enumerate(chains):
            Y[d, pl.ds(t0s[g], SUB), :] = jnp.concatenate(ys[g], axis=0)
        return carry

    lax.fori_loop(0, T // SUB, steps, 0)

    for s, d in chains:
        st = S[s, d]
        for h in range(B_HEADS):
            sfin_ref[s, d, h] = st[:, h * B_DH:(h + 1) * B_DH]

    def post(c, carry):
        r0 = pl.multiple_of(c * RC, RC)
        rs = pl.ds(r0, RC)
        y = Y[0, rs, :] + Y[1, rs, :] + VC2[rs, :]
        mu = _mm_xr(y, ones4, 2) * (1.0 / B_DH)
        yc = y - mu
        var = _mm_xr(yc * yc, ones4, 2) * (1.0 / B_DH)
        yn = yc * lax.rsqrt(var + B_GN_EPS) * lng_ref[...] + lnb_ref[...] + BON[rs, :]
        g = _mm(_sigmoid(lr_ref[rs, 256:384]), gb_ref[...])
        o_ref[rs, :] = yn * g
        return carry

    lax.fori_loop(0, NS * T // RC, post, 0)


def _rwkv_call(T, NS, n_seq, tok0, pm, plr, s0, prm, name):
    rows = NS * T
    blk0 = tok0 // rows
    n_steps = n_seq // NS
    full = lambda shape: pl.BlockSpec(shape, lambda b: (0,) * len(shape))
    big = lambda shape, imap: (pl.BlockSpec(shape, imap, pipeline_mode=pl.Buffered(1)) if n_steps == 1
                               else pl.BlockSpec(shape, imap))
    kern = functools.partial(_rwkv_kernel, T, NS)
    return pl.pallas_call(
        kern,
        grid=(n_steps,),
        in_specs=[
            big((rows, B_W), lambda b: (blk0 + b, 3)),
            big((rows, B_W), lambda b: (blk0 + b, 4)),
            big((rows, B_W), lambda b: (blk0 + b, 5)),
            big((rows, LR_W), lambda b: (blk0 + b, 0)),
            pl.BlockSpec((NS, 2, B_HEADS, B_DH, B_DH), lambda b: (b, 0, 0, 0, 0)),
            full((2, 1, B_W)), full((2, 64, B_W)), full((2, 1, B_W)), full((2, 64, B_W)), full((128, B_W)),
            full((1, B_W)), full((1, B_W)), full((1, B_W)), full((1, B_W)), full((1, B_W)),
        ],
        out_specs=[
            big((rows, B_W), lambda b: (b, 0)),
            pl.BlockSpec((NS, 2, B_HEADS, B_DH, B_DH), lambda b: (b, 0, 0, 0, 0)),
        ],
        out_shape=[
            jax.ShapeDtypeStruct((n_seq * T, B_W), F32),
            jax.ShapeDtypeStruct((n_seq, 2, B_HEADS, B_DH, B_DH), F32),
        ],
        scratch_shapes=[
            pltpu.VMEM((rows, B_W), F32),
            pltpu.VMEM((2, rows, B_W), F32),
            pltpu.VMEM((2, rows, B_W), F32),
            pltpu.VMEM((2, rows, B_W), F32),
            pltpu.VMEM((2, rows, B_W), F32),
            pltpu.VMEM((rows, B_W), F32),
            pltpu.VMEM((rows, B_W), F32),
            pltpu.VMEM((2, rows, B_W), F32),
            pltpu.VMEM((NS, 2, B_DH, B_W), F32),
        ],
        compiler_params=_cparams(("arbitrary",)),
        name=name,
    )(pm, pm, pm, plr, s0, *prm)


def _gla_kernel(T, NS, q_ref, k_ref, v_ref, og_ref, lr_ref, s0_ref, gb_ref, bias_ref, ng_ref,
                o_ref, sfin_ref, LA, O, S):
    n_chunks = T // C_CHUNK
    nsub = C_CHUNK // C_SUB
    qscale = C_DK ** -0.5
    lr = lr_ref[...]
    for d in range(2):
        gl = _mm(lr, gb_ref[d]) + bias_ref[d]
        LA[d] = (jnp.minimum(gl, 0.0) - jnp.log(1.0 + jnp.exp(-jnp.abs(gl)))) * (1.0 / C_GATE_NORM)
    bd_state = _iota((C_KW, C_VW), 0) // C_DK == _iota((C_KW, C_VW), 1) // C_DV
    chains = [(s, d) for s in range(NS) for d in range(2)]
    for s, d in chains:
        for h in range(C_HEADS):
            pad_l = h * C_DV
            pad_r = C_VW - (h + 1) * C_DV
            blk = s0_ref[s, d, h]
            parts = ([jnp.zeros((C_DK, pad_l), F32)] if pad_l else []) + [blk] + \
                    ([jnp.zeros((C_DK, pad_r), F32)] if pad_r else [])
            S[s, d, h * C_DK:(h + 1) * C_DK, :] = jnp.concatenate(parts, axis=1)

    ti = _iota((C_CHUNK, C_CHUNK), 0)
    si = _iota((C_CHUNK, C_CHUNK), 1)
    tri = ((si <= ti).astype(BF16), (si >= ti).astype(BF16))
    trow = _iota((C_CHUNK, 1), 0)
    mask_k = _iota((C_CHUNK, C_KW), 0) // C_SUB == _iota((C_CHUNK, C_KW), 1) // C_DK
    mask_v = _iota((C_CHUNK, C_VW), 0) // C_SUB == _iota((C_CHUNK, C_VW), 1) // C_DV
    t_att = _iota((C_CHUNK, C_CHUNK), 0)
    s_att = _iota((C_CHUNK, C_CHUNK), 1) % C_SUB
    eye_k = _iota((C_KW, C_KW), 0) == _iota((C_KW, C_KW), 1)

    def body(c, carry):
        cx = []
        for s, d in chains:
            cc = c if d == 0 else n_chunks - 1 - c
            rs = pl.ds(pl.multiple_of(s * T + cc * C_CHUNK, C_CHUNK), C_CHUNK)
            b = _mm_xl(tri[d], LA[d, rs, :], 3)
            cx.append(dict(s=s, d=d, rs=rs, b=b, q=q_ref[rs, :] * qscale, k=k_ref[rs, :], v=v_ref[rs, :]))
        for x in cx:
            x["o"] = _mm(x["q"] * jnp.exp(x["b"]), S[x["s"], x["d"]])
        for j in range(nsub):
            lo, hi = j * C_SUB, (j + 1) * C_SUB
            for x in cx:
                b, q, k = x["b"], x["q"], x["k"]
                if x["d"] == 0:
                    gamma = b[hi - 1:hi, :]
                    row_ok = trow >= lo
                    att_ok = t_att >= lo + s_att
                else:
                    gamma = b[lo:lo + 1, :]
                    row_ok = trow < hi
                    att_ok = t_att <= lo + s_att
                qj = q * jnp.exp(jnp.where(row_ok, b - gamma, NEG_INF))
                kj = k[lo:hi, :] * jnp.exp(gamma - b[lo:hi, :])
                kbd = jnp.where(mask_k, jnp.concatenate([kj] * C_HEADS, axis=0), 0.0)
                x["att"] = jnp.where(att_ok, _mm(qj, kbd, NT), 0.0)
            for x in cx:
                vbd = jnp.where(mask_v, jnp.concatenate([x["v"][lo:hi, :]] * C_HEADS, axis=0), 0.0)
                x["o"] = x["o"] + _mm(x["att"], vbd)
        for x in cx:
            s, d, b = x["s"], x["d"], x["b"]
            O[d, x["rs"], :] = x["o"]
            blast = b[C_CHUNK - 1:C_CHUNK, :] if d == 0 else b[0:1, :]
            kl = x["k"] * jnp.exp(blast - b)
            upd = jnp.where(bd_state, _mm3(kl.T, x["v"]), 0.0)
            dec = jnp.where(eye_k, jnp.exp(blast), 0.0)
            S[s, d] = _mm3(dec, S[s, d]) + upd
        return carry

    lax.fori_loop(0, n_chunks, body, 0)

    for s, d in chains:
        st = S[s, d]
        for h in range(C_HEADS):
            sfin_ref[s, d, h] = st[h * C_DK:(h + 1) * C_DK, h * C_DV:(h + 1) * C_DV]

    ones4 = _block_ones(C_VW, C_DV)
    o = O[0] + O[1]
    ms = _mm_xr(o * o, ones4, 2) * (1.0 / C_DV)
    og = og_ref[...]
    o_ref[...] = o * lax.rsqrt(ms + LN_EPS) * ng_ref[...] * (og * _sigmoid(og))


def _gla_call(T, NS, n_seq, tok0, pm, plr, s0, gb_pad, bias, ng, name):
    rows = NS * T
    blk0 = tok0 // rows
    full = lambda shape: pl.BlockSpec(shape, lambda b: (0,) * len(shape))
    return pl.pallas_call(
        functools.partial(_gla_kernel, T, NS),
        grid=(n_seq // NS,),
        in_specs=[
            pl.BlockSpec((rows, C_KW), lambda b: (blk0 + b, 12)),
            pl.BlockSpec((rows, C_KW), lambda b: (blk0 + b, 13)),
            pl.BlockSpec((rows, C_VW), lambda b: (blk0 + b, 7)),
            pl.BlockSpec((rows, C_VW), lambda b: (blk0 + b, 8)),
            pl.BlockSpec((rows, 128), lambda b: (blk0 + b, 3)),
            pl.BlockSpec((NS, 2, C_HEADS, C_DK, C_DV), lambda b: (b, 0, 0, 0, 0)),
            full((2, 128, C_KW)), full((2, 1, C_KW)), full((1, C_VW)),
        ],
        out_specs=[
            pl.BlockSpec((rows, C_VW), lambda b: (b, 0)),
            pl.BlockSpec((NS, 2, C_HEADS, C_DK, C_DV), lambda b: (b, 0, 0, 0, 0)),
        ],
        out_shape=[
            jax.ShapeDtypeStruct((n_seq * T, C_VW), F32),
            jax.ShapeDtypeStruct((n_seq, 2, C_HEADS, C_DK, C_DV), F32),
        ],
        scratch_shapes=[
            pltpu.VMEM((2, rows, C_KW), F32),
            pltpu.VMEM((2, rows, C_VW), F32),
            pltpu.VMEM((NS, 2, C_KW, C_VW), F32),
        ],
        compiler_params=_cparams(("arbitrary",)),
        name=name,
    )(pm, pm, pm, pm, plr, s0, gb_pad, bias, ng)


MERGE_TM = 512


def _merge_kernel(n_x, *refs):
    x_refs = refs[:n_x]
    (g_ref, oac_ref, oad_ref, obc_ref, obd_ref, occ_ref, ocd_ref, mod_ref,
     wa_ref, wb_ref, wc_ref, wo_ref, lg_ref, lb_ref, o_ref) = refs[n_x:]
    is_ctx = pl.program_id(0) < N_CTX_TOK // MERGE_TM
    oa = jnp.where(is_ctx, oac_ref[...], oad_ref[...]).astype(BF16)
    ob = jnp.where(is_ctx, obc_ref[...], obd_ref[...]).astype(BF16)
    oc = jnp.where(is_ctx, occ_ref[...], ocd_ref[...]).astype(BF16)
    merged = (g_ref[:, 0:D].astype(F32) * _dg(oa, wa_ref[...])
              + g_ref[:, D:2 * D].astype(F32) * _dg(ob, wb_ref[...])
              + g_ref[:, 2 * D:3 * D].astype(F32) * _dg(oc, wc_ref[...]))
    mix = _dg(merged.astype(BF16), wo_ref[...])
    y = DN_ALPHA * _read_x(x_refs, MERGE_TM) + mod_ref[2:3, :] * mix
    o_ref[...] = _layer_norm(y, lg_ref[...], lb_ref[...])


def _merge_call(xs, gates, oa, ob, oc, mod_l, wa, wb, wc, wo, lg, lb):
    tm = MERGE_TM
    x_specs = [pl.BlockSpec((tm, D), lambda i: (i, 0))] if len(xs) == 1 else _pair_specs(tm, D)
    return pl.pallas_call(
        functools.partial(_merge_kernel, len(xs)),
        grid=(N_TOK // tm,),
        in_specs=[
            *x_specs,
            pl.BlockSpec((tm, IN_GATE), lambda i: (i, 0)),
            *_pair_specs(tm, 512), *_pair_specs(tm, B_W), *_pair_specs(tm, C_VW),
            pl.BlockSpec((None, 6, D), lambda i: (_group_of_tile(i, tm), 0, 0)),
            _resident((512, D)), _resident((B_W, D)), _resident((C_VW, D)), _resident((D, D)),
            _resident((1, D)), _resident((1, D)),
        ],
        out_specs=pl.BlockSpec((tm, D), lambda i: (i, 0)),
        out_shape=jax.ShapeDtypeStruct((N_TOK, D), F32),
        compiler_params=_cparams(("arbitrary",)),
        name="merge",
    )(*xs, gates, *oa, *ob, *oc, mod_l, wa, wb, wc, wo, lg.reshape(1, D), lb.reshape(1, D))


def _ffn_kernel(x_ref, mod_ref, wg_ref, wu_ref, wd_ref, lg_ref, lb_ref, o_ref):
    x = x_ref[...]
    h = (x * (1.0 + mod_ref[4:5, :]) + mod_ref[3:4, :]).astype(BF16)
    gate = _dg(h, wg_ref[...])
    up = _dg(h, wu_ref[...])
    f = _dg((gate * _sigmoid(gate) * up).astype(BF16), wd_ref[...])
    y = DN_ALPHA * x + mod_ref[5:6, :] * f
    o_ref[...] = _layer_norm(y, lg_ref[...], lb_ref[...])


def _ffn_call(x, mod_l, wg, wu, wd, lg, lb):
    tm = 512
    return pl.pallas_call(
        _ffn_kernel,
        grid=(N_TOK // tm,),
        in_specs=[
            pl.BlockSpec((tm, D), lambda i: (i, 0)),
            pl.BlockSpec((None, 6, D), lambda i: (_group_of_tile(i, tm), 0, 0)),
            _resident((D, D_FF)), _resident((D, D_FF)), _resident((D_FF, D)),
            _resident((1, D)), _resident((1, D)),
        ],
        out_specs=pl.BlockSpec((tm, D), lambda i: (i, 0)),
        out_shape=jax.ShapeDtypeStruct((N_TOK, D), F32),
        compiler_params=_cparams(("arbitrary",)),
        name="ffn",
    )(x, mod_l, wg, wu, wd, lg.reshape(1, D), lb.reshape(1, D))


MOE_TM = 1024
MOE_TR = 256
MOE_RS = 3072
MOE_TF = 512
MOE_ROWS = 2 * N_TOK + N_EXP * MOE_TR
MOE_NST = -(-MOE_ROWS // MOE_RS) + N_EXP
R_I1, R_I2, R_W1, R_W2, R_RANK1, R_RANK2 = range(6)


def _moe_route_kernel(x_ref, mod_ref, wr_ref, h_ref, info_ref, cnt_ref, carry_s):
    tm = MOE_TM

    @pl.when(pl.program_id(0) == 0)
    def _():
        carry_s[...] = jnp.zeros_like(carry_s)

    h = x_ref[...] * (1.0 + mod_ref[4:5, :]) + mod_ref[3:4, :]
    h_ref[...] = h
    logits = _mm3(h, wr_ref[...])
    lane = _iota(logits.shape, 1)
    logits = jnp.where(lane < N_EXP, logits, NEG_INF)
    v1 = jnp.max(logits, -1, keepdims=True)
    i1 = jnp.min(jnp.where(logits == v1, lane, 128), -1, keepdims=True)
    rest = jnp.where(lane == i1, NEG_INF, logits)
    v2 = jnp.max(rest, -1, keepdims=True)
    i2 = jnp.min(jnp.where(rest == v2, lane, 128), -1, keepdims=True)
    e2 = jnp.exp(v2 - v1)
    w1 = 1.0 / (1.0 + e2)
    w2 = e2 / (1.0 + e2)
    oh1 = lane == i1
    oh2 = lane == i2
    cnt = oh1.astype(F32) + oh2.astype(F32)
    earlier = (_iota((tm, tm), 1) < _iota((tm, tm), 0)).astype(BF16)
    before = _dg(earlier, cnt.astype(BF16)) + carry_s[...]
    rank1 = jnp.sum(jnp.where(oh1, before, 0.0), -1, keepdims=True)
    rank2 = jnp.sum(jnp.where(oh2, before, 0.0), -1, keepdims=True)
    info = jnp.zeros(logits.shape, F32)
    for ln, val in ((R_I1, i1.astype(F32)), (R_I2, i2.astype(F32)), (R_W1, w1), (R_W2, w2),
                    (R_RANK1, rank1), (R_RANK2, rank2)):
        info = jnp.where(lane == ln, val, info)
    info_ref[...] = info
    carry_s[...] += jnp.sum(cnt, axis=0, keepdims=True)
    cnt_ref[...] = carry_s[...]


def _moe_route_call(x, mod_l, wr_pad, i_moe):
    tm = MOE_TM
    return pl.pallas_call(
        _moe_route_kernel,
        grid=(N_TOK // tm,),
        in_specs=[
            pl.BlockSpec((tm, D), lambda i: (i, 0)),
            pl.BlockSpec((None, 6, D), lambda i: (_group_of_tile(i, tm), 0, 0)),
            pl.BlockSpec((None, D, 128), lambda i: (i_moe, 0, 0)),
        ],
        out_specs=[
            pl.BlockSpec((tm, D), lambda i: (i, 0)),
            pl.BlockSpec((tm, 128), lambda i: (i, 0)),
            pl.BlockSpec((1, 128), lambda i: (0, 0)),
        ],
        out_shape=[
            jax.ShapeDtypeStruct((N_TOK, D), F32),
            jax.ShapeDtypeStruct((N_TOK, 128), F32),
            jax.ShapeDtypeStruct((1, 128), F32),
        ],
        scratch_shapes=[pltpu.VMEM((1, 128), F32)],
        compiler_params=_cparams(("arbitrary",)),
        name="moe_route",
    )(x, mod_l, wr_pad)


def _moe_plan(info, cnt):
    i32 = jnp.int32
    i1 = info[:, R_I1].astype(i32)
    i2 = info[:, R_I2].astype(i32)
    counts = cnt[0, :N_EXP].astype(i32)
    padded = (counts + MOE_TR - 1) // MOE_TR * MOE_TR
    seg_start = jnp.cumsum(padded) - padded
    pos1 = seg_start[i1] + info[:, R_RANK1].astype(i32)
    pos2 = seg_start[i2] + info[:, R_RANK2].astype(i32)
    dst = jnp.zeros((MOE_ROWS,), i32).at[jnp.concatenate([pos1, pos2])].set(jnp.arange(2 * N_TOK, dtype=i32))
    n_pass = (padded + MOE_RS - 1) // MOE_RS
    pass_end = jnp.cumsum(n_pass)
    total = pass_end[-1]
    sidx = jnp.arange(MOE_NST, dtype=i32)
    used = sidx < total
    e_of = jnp.minimum(jnp.searchsorted(pass_end, jnp.minimum(sidx, total - 1), side="right"), N_EXP - 1).astype(i32)
    k = jnp.minimum(sidx, total - 1) - (pass_end - n_pass)[e_of]
    row0 = seg_start[e_of] + k * MOE_RS
    nrows = jnp.where(used, jnp.clip(padded[e_of] - k * MOE_RS, 0, MOE_RS), 0)
    nvalid = jnp.where(used, jnp.clip(counts[e_of] - k * MOE_RS, 0, MOE_RS), 0)
    return dst, e_of, row0.astype(i32), nrows.astype(i32), nvalid.astype(i32)


def _moe_expert_kernel(dst_ref, exp_ref, row0_ref, nrows_ref, nvalid_ref,
                       h_hbm, wg_ref, wu_ref, wd_ref, yo_hbm, xs, xb, acc, wgb, wub, wdb, gsem, ssem):
    s = pl.program_id(0)
    f = pl.program_id(1)
    nf = pl.num_programs(1)
    nrows = pl.multiple_of(nrows_ref[s], MOE_TR)
    nvalid = nvalid_ref[s]
    row0 = row0_ref[s]
    n_chunks = nrows // MOE_TR

    def row_copy(src, dst, sem):
        return pltpu.make_async_copy(src, dst, sem)

    def hbm_row(ref, i):
        return ref.at[pl.ds(i, 1), :]

    def wait_rows(buf, n_groups, sem):
        pltpu.make_async_copy(buf.at[pl.ds(0, n_groups)], buf.at[pl.ds(0, n_groups)], sem).wait()

    @pl.when((f == 0) & (nrows > 0))
    def _gather():
        def issue(g, carry):
            for u in range(SUBLANES):
                d = dst_ref[row0 + g * SUBLANES + u]
                tok = jnp.where(d >= N_TOK, d - N_TOK, d)
                row_copy(hbm_row(h_hbm, tok), xs.at[g, pl.ds(u, 1), :], gsem).start()
            return carry

        ng = nrows // SUBLANES
        lax.fori_loop(0, ng, issue, 0)
        wait_rows(xs, ng, gsem)

        def cvt(c, carry):
            g0 = pl.multiple_of(c * (MOE_TR // SUBLANES), MOE_TR // SUBLANES)
            rs = pl.ds(pl.multiple_of(c * MOE_TR, MOE_TR), MOE_TR)
            xb[rs, :] = xs[pl.ds(g0, MOE_TR // SUBLANES)].reshape(MOE_TR, D).astype(BF16)
            return carry

        lax.fori_loop(0, n_chunks, cvt, 0)

    @pl.when(nrows > 0)
    def _compute():
        wgb[...] = wg_ref[...].astype(BF16)
        wub[...] = wu_ref[...].astype(BF16)
        wdb[...] = wd_ref[...].astype(BF16)

        def chunk(start, n):
            x = xb[pl.ds(pl.multiple_of(start, MOE_TR), n), :]
            gate = _dg(x, wgb[...])
            up = _dg(x, wub[...])
            y = _dg((gate * _sigmoid(gate) * up).astype(BF16), wdb[...]).reshape(n // SUBLANES, SUBLANES, D)
            gs = pl.ds(pl.multiple_of(start // SUBLANES, MOE_TR // SUBLANES), n // SUBLANES)

            @pl.when(f == 0)
            def _():
                acc[gs] = y

            @pl.when(f > 0)
            def _():
                acc[gs] += y

        def chunk_quad(c4, carry):
            chunk(c4 * (4 * MOE_TR), 4 * MOE_TR)
            return carry

        lax.fori_loop(0, n_chunks // 4, chunk_quad, 0)
        done = n_chunks // 4 * 4

        @pl.when(n_chunks % 4 >= 2)
        def _():
            chunk(done * MOE_TR, 2 * MOE_TR)

        @pl.when(n_chunks % 2 == 1)
        def _():
            chunk((n_chunks - 1) * MOE_TR, MOE_TR)

    @pl.when((f == nf - 1) & (nvalid > 0))
    def _scatter():
        n8 = nvalid // SUBLANES

        def issue(g, carry):
            for u in range(SUBLANES):
                d = dst_ref[row0 + g * SUBLANES + u]
                row_copy(acc.at[g, pl.ds(u, 1), :], hbm_row(yo_hbm, d), ssem).start()
            return carry

        def issue_tail(r, carry):
            d = dst_ref[row0 + r]
            row_copy(acc.at[n8, pl.ds(r - n8 * SUBLANES, 1), :], hbm_row(yo_hbm, d), ssem).start()
            return carry

        lax.fori_loop(0, n8, issue, 0)
        lax.fori_loop(n8 * SUBLANES, nvalid, issue_tail, 0)

        @pl.when(n8 > 0)
        def _():
            wait_rows(acc, n8, ssem)

        def wait_one(r, carry):
            row_copy(acc.at[0, pl.ds(0, 1), :], hbm_row(yo_hbm, 0), ssem).wait()
            return carry

        lax.fori_loop(n8 * SUBLANES, nvalid, wait_one, 0)


def _moe_expert_call(h, plan, wg, wu, wd, i_moe):
    nf = D_FFE // MOE_TF

    def wspec(shape, fdim):
        def imap(s, f, dst, exp, row0, nrows, nvalid):
            fe = jnp.where(nrows[s] > 0, f, nf - 1)
            return (i_moe, exp[s], 0, fe) if fdim == 3 else (i_moe, exp[s], fe, 0)
        return pl.BlockSpec(shape, imap)

    grid_spec = pltpu.PrefetchScalarGridSpec(
        num_scalar_prefetch=5,
        grid=(MOE_NST, nf),
        in_specs=[
            pl.BlockSpec(memory_space=pl.ANY),
            wspec((None, None, D, MOE_TF), 3),
            wspec((None, None, D, MOE_TF), 3),
            wspec((None, None, MOE_TF, D), 2),
        ],
        out_specs=pl.BlockSpec(memory_space=pl.ANY),
        scratch_shapes=[
            pltpu.VMEM((MOE_RS // SUBLANES, SUBLANES, D), F32),
            pltpu.VMEM((MOE_RS, D), BF16),
            pltpu.VMEM((MOE_RS // SUBLANES, SUBLANES, D), F32),
            pltpu.VMEM((D, MOE_TF), BF16), pltpu.VMEM((D, MOE_TF), BF16), pltpu.VMEM((MOE_TF, D), BF16),
            pltpu.SemaphoreType.DMA(()), pltpu.SemaphoreType.DMA(()),
        ],
    )
    return pl.pallas_call(
        _moe_expert_kernel,
        grid_spec=grid_spec,
        out_shape=jax.ShapeDtypeStruct((2 * N_TOK, D), F32),
        compiler_params=pltpu.CompilerParams(dimension_semantics=("arbitrary", "arbitrary"),
                                             vmem_limit_bytes=VMEM_LIMIT, disable_bounds_checks=True),
        name="moe_experts",
    )(*plan, h, wg, wu, wd)


def _moe_combine_kernel(x_ref, y1_ref, y2_ref, info_ref, mod_ref, lg_ref, lb_ref, oc_ref, od_ref):
    f = info_ref[:, R_W1:R_W1 + 1] * y1_ref[...] + info_ref[:, R_W2:R_W2 + 1] * y2_ref[...]
    y = DN_ALPHA * x_ref[...] + mod_ref[5:6, :] * f
    out = _layer_norm(y, lg_ref[...], lb_ref[...])
    is_ctx = pl.program_id(0) < N_CTX_TOK // MOE_TM

    @pl.when(is_ctx)
    def _():
        oc_ref[...] = out

    @pl.when(jnp.logical_not(is_ctx))
    def _():
        od_ref[...] = out


def _moe_combine_call(x, yo, info, mod_l, lg, lb):
    tm = MOE_TM
    nt = N_TOK // tm
    return pl.pallas_call(
        _moe_combine_kernel,
        grid=(nt,),
        in_specs=[
            pl.BlockSpec((tm, D), lambda i: (i, 0)),
            pl.BlockSpec((tm, D), lambda i: (i, 0)),
            pl.BlockSpec((tm, D), lambda i: (nt + i, 0)),
            pl.BlockSpec((tm, 128), lambda i: (i, 0)),
            pl.BlockSpec((None, 6, D), lambda i: (_group_of_tile(i, tm), 0, 0)),
            _resident((1, D)), _resident((1, D)),
        ],
        out_specs=_pair_specs(tm, D),
        out_shape=[jax.ShapeDtypeStruct((N_CTX_TOK, D), F32), jax.ShapeDtypeStruct((N_DEC_B * DEC_T, D), F32)],
        compiler_params=_cparams(("arbitrary",)),
        name="moe_combine",
    )(x, yo, yo, info, mod_l, lg.reshape(1, D), lb.reshape(1, D))


def _moe_call(x, mod_l, wr_pad, wg, wu, wd, lg, lb, i_moe):
    h, info, cnt = _moe_route_call(x, mod_l, wr_pad, i_moe)
    yo = _moe_expert_call(h, _moe_plan(info, cnt), wg, wu, wd, i_moe)
    return _moe_combine_call(x, yo, info, mod_l, lg, lb)


def kernel(x_prompt, x_sample, cache_attn_k, cache_attn_v, state_rwkv, state_gla, c, c_ctx, w_ada, b_ada, w_in,
           attn_sink, rwkv_w0, rwkv_w_a, rwkv_w_b, rwkv_a0, rwkv_a_a, rwkv_a_b, rwkv_g_a, rwkv_g_b, rwkv_k_k,
           rwkv_k_a, rwkv_r_k, rwkv_ln_g, rwkv_ln_b, gla_gate_a, gla_gate_b, gla_gate_bias, gla_norm_g, w_up_a,
           w_up_b, w_up_c, w_out, ln1_g, ln1_b, ln2_g, ln2_b, ffn_w_gate, ffn_w_up, ffn_w_down, moe_router,
           moe_w_gate, moe_w_up, moe_w_down):
    cvec = jnp.concatenate([c_ctx[None, :], c, jnp.zeros((N_GROUPS - 1 - N_DEC_B, D), F32)], axis=0)
    mods = _ada_call(cvec, w_ada, b_ada).reshape(DEPTH, N_GROUPS, 6, D)
    xs = (x_prompt.reshape(N_CTX_TOK, D), x_sample.reshape(N_DEC_B * DEC_T, D))
    cos, sin = _rope_tables()
    kc_all = cache_attn_k.reshape(N_DEC_B, DEPTH, PAST, A_KV * A_DH)
    vc_all = cache_attn_v.reshape(N_DEC_B, DEPTH, PAST, A_KV * A_DH)
    zeros_r = jnp.zeros((N_CTX_B, 2, B_HEADS, B_DH, B_DH), F32)
    zeros_g = jnp.zeros((N_CTX_B, 2, C_HEADS, C_DK, C_DV), F32)

    new_k, new_v, new_sr, new_sg = [], [], [], []
    for l in range(DEPTH):
        mod_l = mods[l]
        w_lr = jnp.concatenate(
            [rwkv_w_a[l, 0], rwkv_w_a[l, 1], rwkv_a_a[l, 0], rwkv_a_a[l, 1], rwkv_g_a[l],
             gla_gate_a[l, 0], gla_gate_a[l, 1], jnp.zeros((D, LR_W - 416), F32)], axis=1).astype(BF16)
        pm, gates, plr, k_ctx, v_ctx = _inproj_call(xs, mod_l, w_in[l, :, :IN_MAIN].astype(BF16),
                                                    w_in[l, :, IN_MAIN:].astype(BF16), w_lr)

        sink_l = attn_sink[l]
        oa = (_attn_ctx_call(sink_l, pm), _attn_lat_call(sink_l, pm, kc_all, vc_all, l, cos, sin))

        rprm = (rwkv_w0[l].reshape(2, 1, B_W), rwkv_w_b[l], rwkv_a0[l].reshape(2, 1, B_W), rwkv_a_b[l],
                rwkv_g_b[l], rwkv_k_k[l].reshape(1, B_W), rwkv_k_a[l].reshape(1, B_W),
                rwkv_r_k[l].reshape(1, B_W), rwkv_ln_g[l].reshape(1, B_W), rwkv_ln_b[l].reshape(1, B_W))
        ob_c, sr_c = _rwkv_call(CTX_T, 4, N_CTX_B, 0, pm, plr, zeros_r, rprm, "rwkv_ctx")
        ob_d, _ = _rwkv_call(DEC_T, 2, N_DEC_B, N_CTX_TOK, pm, plr, state_rwkv[:, l], rprm, "rwkv_lat")

        gb_pad = jnp.zeros((2, 128, C_KW), F32)
        gb_pad = gb_pad.at[0, 0:C_GATE_RANK].set(gla_gate_b[l, 0]).at[1, C_GATE_RANK:2 * C_GATE_RANK].set(
            gla_gate_b[l, 1])
        gbias = gla_gate_bias[l].reshape(2, 1, C_KW)
        ng = jnp.tile(gla_norm_g[l], C_HEADS).reshape(1, C_VW)
        oc_c, sg_c = _gla_call(CTX_T, 4, N_CTX_B, 0, pm, plr, zeros_g, gb_pad, gbias, ng, "gla_ctx")
        oc_d, _ = _gla_call(DEC_T, 2, N_DEC_B, N_CTX_TOK, pm, plr, state_gla[:, l], gb_pad, gbias, ng, "gla_lat")

        x = _merge_call(xs, gates, oa, (ob_c, ob_d), (oc_c, oc_d), mod_l, w_up_a[l].astype(BF16),
                        w_up_b[l].astype(BF16), w_up_c[l].astype(BF16), w_out[l].astype(BF16), ln1_g[l], ln1_b[l])
        if l % 2 == 0:
            i_ffn = l // 2
            xs = (_ffn_call(x, mod_l, ffn_w_gate[i_ffn].astype(BF16), ffn_w_up[i_ffn].astype(BF16),
                            ffn_w_down[i_ffn].astype(BF16), ln2_g[l], ln2_b[l]),)
        else:
            wr_pad = jnp.concatenate([moe_router, jnp.zeros((moe_router.shape[0], D, 128 - N_EXP), F32)], axis=2)
            xs = _moe_call(x, mod_l, wr_pad, moe_w_gate, moe_w_up, moe_w_down, ln2_g[l], ln2_b[l], l // 2)

        new_k.append(k_ctx.reshape(N_CTX_B, CTX_T, A_KV, A_DH))
        new_v.append(v_ctx.reshape(N_CTX_B, CTX_T, A_KV, A_DH))
        new_sr.append(sr_c)
        new_sg.append(sg_c)

    if len(xs) == 1:
        xs = (xs[0][:N_CTX_TOK], xs[0][N_CTX_TOK:])
    y_prompt = xs[0].reshape(N_CTX_B, CTX_T, D)
    y_sample = xs[1].reshape(N_DEC_B, DEC_T, D)
    return (y_prompt, y_sample, jnp.stack(new_k, axis=1), jnp.stack(new_v, axis=1),
            jnp.stack(new_sr, axis=1), jnp.stack(new_sg, axis=1))
```

```python
import functools

import numpy as np
import jax
import jax.numpy as jnp
from jax import lax
from jax.experimental import pallas as pl
from jax.experimental.pallas import tpu as pltpu

D = 1024
N_CTX_B, CTX_T = 32, 256
N_DEC_B, DEC_T = 2, 1024
N_CTX_TOK = N_CTX_B * CTX_T
N_TOK = N_CTX_TOK + N_DEC_B * DEC_T
DEPTH = 2
PAST = 256
GRID_W = 64
A_HEADS, A_KV, A_DH = 8, 2, 64
A_G = A_HEADS // A_KV
A_WIN, A_BLK = 128, 128
ROPE_BASE = 10000.0
B_HEADS, B_DH = 4, 64
B_W = B_HEADS * B_DH
B_GN_EPS = 64e-5
C_HEADS, C_DK, C_DV = 4, 32, 64
C_KW, C_VW = C_HEADS * C_DK, C_HEADS * C_DV
C_GATE_RANK = 16
C_GATE_NORM = 16.0
C_CHUNK = 64
C_SUB = 16
D_FF = 2816
N_EXP = 8
D_FFE = 3584
LN_EPS = 1e-5
DN_ALPHA = (2.0 * DEPTH) ** 0.25
NEG_INF = -1e30
A_QW, A_KVW = A_HEADS * A_DH, A_KV * A_DH
B_RANK, B_GATE_RANK = 64, 128
LANES = 128
IN_MAIN = 2304
COL_B = A_QW + 2 * A_KVW
COL_C = COL_B + 3 * B_W
COL_CV = COL_C + 2 * C_KW
IN_GATE = 3072
LR_W = 512
C_LR_COL = 4 * B_RANK + B_GATE_RANK
N_GROUPS = 8

F32 = jnp.float32
BF16 = jnp.bfloat16
VMEM_LIMIT = 56 * 1024 * 1024
SUBLANES = 8

NN = ((1,), (0,))
NT = ((1,), (1,))
TN = ((0,), (0,))


def _dg(a, b, dims=NN):
    return lax.dot_general(a, b, (dims, ((), ())), preferred_element_type=F32)


def _split2(x):
    hi = x.astype(BF16)
    lo = (x - hi.astype(F32)).astype(BF16)
    return hi, lo


def _split3(x):
    hi = x.astype(BF16)
    r = x - hi.astype(F32)
    mid = r.astype(BF16)
    lo = (r - mid.astype(F32)).astype(BF16)
    return hi, mid, lo


def _mm(a, b, dims=NN):
    return _dg(a.astype(BF16), b.astype(BF16), dims)


def _mm3(a, b, dims=NN):
    ah, al = _split2(a)
    bh, bl = _split2(b)
    return _dg(ah, bh, dims) + (_dg(ah, bl, dims) + _dg(al, bh, dims))


def _mm_xr(a, b_exact, passes, dims=NN):
    parts = (a.astype(BF16),) if passes == 1 else (_split2(a) if passes == 2 else _split3(a))
    out = _dg(parts[0], b_exact, dims)
    for p in parts[1:]:
        out = out + _dg(p, b_exact, dims)
    return out


def _mm_xl(a_exact, b, passes):
    parts = (b.astype(BF16),) if passes == 1 else (_split2(b) if passes == 2 else _split3(b))
    out = _dg(a_exact, parts[0])
    for p in parts[1:]:
        out = out + _dg(a_exact, p)
    return out


def _sigmoid(x):
    return 1.0 / (1.0 + jnp.exp(-x))


def _iota(shape, dim):
    return lax.broadcasted_iota(jnp.int32, shape, dim)


def _block_ones(n, blk):
    return (_iota((n, n), 0) // blk == _iota((n, n), 1) // blk).astype(BF16)


def _layer_norm(y, g, b):
    mu = jnp.mean(y, -1, keepdims=True)
    yc = y - mu
    var = jnp.mean(yc * yc, -1, keepdims=True)
    return yc * lax.rsqrt(var + LN_EPS) * g + b


def _group_of_tile(i, tm):
    n_ctx = N_CTX_TOK // tm
    per_dec = DEC_T // tm
    return jnp.where(i < n_ctx, 0, 1 + (i - n_ctx) // per_dec)


def _cparams(sem):
    return pltpu.CompilerParams(dimension_semantics=sem, vmem_limit_bytes=VMEM_LIMIT)


def _ada_kernel(c_ref, w_ref, b_ref, o_ref):
    c = c_ref[...]
    s = c * _sigmoid(c)
    o_ref[...] = _mm3(s, w_ref[...]) + b_ref[...]


def _ada_call(cvec, w_ada, b_ada):
    tn = 1536
    return pl.pallas_call(
        _ada_kernel,
        grid=(DEPTH, 6 * D // tn),
        in_specs=[
            pl.BlockSpec((N_GROUPS, D), lambda l, j: (0, 0)),
            pl.BlockSpec((None, D, tn), lambda l, j: (l, 0, j)),
            pl.BlockSpec((None, 1, tn), lambda l, j: (l, 0, j)),
        ],
        out_specs=pl.BlockSpec((None, N_GROUPS, tn), lambda l, j: (l, 0, j)),
        out_shape=jax.ShapeDtypeStruct((DEPTH, N_GROUPS, 6 * D), F32),
        compiler_params=_cparams(("arbitrary", "arbitrary")),
        name="ada",
    )(cvec, w_ada, b_ada.reshape(DEPTH, 1, 6 * D))


def _pair_specs(tm, width):
    n_ctx = N_CTX_TOK // tm
    return [pl.BlockSpec((tm, width), lambda i: (jnp.minimum(i, n_ctx - 1), 0)),
            pl.BlockSpec((tm, width), lambda i: (jnp.maximum(i - n_ctx, 0), 0))]


def _read_x(x_refs, tm):
    if len(x_refs) == 1:
        return x_refs[0][...]
    return jnp.where(pl.program_id(0) < N_CTX_TOK // tm, x_refs[0][...], x_refs[1][...])


INPROJ_TM = 256


def _inproj_kernel(n_x, *refs):
    x_refs = refs[:n_x]
    mod_ref, wm_ref, wg_ref, wl_ref, om_ref, og_ref, ol_ref, ok_ref, ov_ref = refs[n_x:]
    sh = mod_ref[0:1, :]
    sc = mod_ref[1:2, :]
    h = (_read_x(x_refs, INPROJ_TM) * (1.0 + sc) + sh).astype(BF16)
    main = _dg(h, wm_ref[...])
    om_ref[...] = main
    og_ref[...] = _sigmoid(_dg(h, wg_ref[...])).astype(BF16)
    ol_ref[...] = _dg(h, wl_ref[...])

    @pl.when(pl.program_id(0) < N_CTX_TOK // INPROJ_TM)
    def _():
        ok_ref[...] = main[:, A_QW:A_QW + A_KVW]
        ov_ref[...] = main[:, A_QW + A_KVW:A_QW + 2 * A_KVW]


def _resident(shape):
    return pl.BlockSpec(shape, lambda *_: (0,) * len(shape), pipeline_mode=pl.Buffered(1))


def _inproj_call(xs, mod_l, w_main, w_gate, w_lr):
    tm = INPROJ_TM
    n_ctx = N_CTX_TOK // tm
    x_specs = [pl.BlockSpec((tm, D), lambda i: (i, 0))] if len(xs) == 1 else _pair_specs(tm, D)
    kv_spec = pl.BlockSpec((tm, A_KV * A_DH), lambda i: (jnp.minimum(i, n_ctx - 1), 0))
    return pl.pallas_call(
        functools.partial(_inproj_kernel, len(xs)),
        grid=(N_TOK // tm,),
        in_specs=[
            *x_specs,
            pl.BlockSpec((None, 6, D), lambda i: (_group_of_tile(i, tm), 0, 0)),
            _resident((D, IN_MAIN)), _resident((D, IN_GATE)), _resident((D, LR_W)),
        ],
        out_specs=[
            pl.BlockSpec((tm, IN_MAIN), lambda i: (i, 0)),
            pl.BlockSpec((tm, IN_GATE), lambda i: (i, 0)),
            pl.BlockSpec((tm, LR_W), lambda i: (i, 0)),
            kv_spec, kv_spec,
        ],
        out_shape=[
            jax.ShapeDtypeStruct((N_TOK, IN_MAIN), F32),
            jax.ShapeDtypeStruct((N_TOK, IN_GATE), BF16),
            jax.ShapeDtypeStruct((N_TOK, LR_W), F32),
            jax.ShapeDtypeStruct((N_CTX_TOK, A_KV * A_DH), F32),
            jax.ShapeDtypeStruct((N_CTX_TOK, A_KV * A_DH), F32),
        ],
        compiler_params=_cparams(("arbitrary",)),
        name="inproj",
    )(*xs, mod_l, w_main, w_gate, w_lr)


def _sink_col(sink_ref, kvh, rows_per_head):
    n = A_G * rows_per_head
    r = _iota((n, 1), 0) // rows_per_head
    col = jnp.full((n, 1), sink_ref[kvh * A_G], F32)
    for g in range(1, A_G):
        col = jnp.where(r == g, sink_ref[kvh * A_G + g], col)
    return col


def _attn_ctx_kernel(sink_ref, q_ref, k_ref, v_ref, o_ref):
    scale = A_DH ** -0.5
    for kvh in range(A_KV):
        ks = k_ref[:, kvh * A_DH:(kvh + 1) * A_DH].astype(BF16)
        vs = v_ref[:, kvh * A_DH:(kvh + 1) * A_DH].astype(BF16)
        q4 = jnp.concatenate(
            [q_ref[:, (kvh * A_G + g) * A_DH:(kvh * A_G + g + 1) * A_DH] for g in range(A_G)], axis=0)
        s = _dg(q4.astype(BF16), ks, NT) * scale
        sink = _sink_col(sink_ref, kvh, CTX_T)
        m = jnp.maximum(jnp.max(s, -1, keepdims=True), sink)
        e = jnp.exp(s - m)
        p = e / (jnp.sum(e, -1, keepdims=True) + jnp.exp(sink - m))
        o = _dg(p.astype(BF16), vs)
        for g in range(A_G):
            h = kvh * A_G + g
            o_ref[:, h * A_DH:(h + 1) * A_DH] = o[g * CTX_T:(g + 1) * CTX_T, :]


def _attn_ctx_call(sink_l, pm):
    return pl.pallas_call(
        _attn_ctx_kernel,
        grid=(N_CTX_B,),
        in_specs=[
            pl.BlockSpec(memory_space=pltpu.SMEM),
            pl.BlockSpec((CTX_T, A_QW), lambda b: (b, 0)),
            pl.BlockSpec((CTX_T, A_KVW), lambda b: (b, A_QW // A_KVW)),
            pl.BlockSpec((CTX_T, A_KVW), lambda b: (b, A_QW // A_KVW + 1)),
        ],
        out_specs=pl.BlockSpec((CTX_T, A_QW), lambda b: (b, 0)),
        out_shape=jax.ShapeDtypeStruct((N_CTX_TOK, A_QW), F32),
        compiler_params=_cparams(("arbitrary",)),
        name="attn_ctx",
    )(sink_l, pm, pm, pm)


def _rope(x, cos, sin_signed):
    w = x.shape[-1]
    lane = _iota(x.shape, 1)
    q = A_DH // 4
    partner = jnp.where((lane % (2 * q)) < q, pltpu.roll(x, w - q, 1), pltpu.roll(x, q, 1))
    return x * cos + partner * sin_signed


def _attn_lat_kernel(sink_ref, q_ref, k_ref, v_ref, kc_ref, vc_ref, cos_ref, sin_ref, o_ref, kr_ref):
    n = pl.program_id(1)
    scale = A_DH ** -0.5

    @pl.when(n == 0)
    def _():
        kr_ref[...] = _rope(k_ref[...], cos_ref[:, 0:A_KVW], sin_ref[:, 0:A_KVW]).astype(BF16)

    q0 = pl.multiple_of(n * A_BLK, A_BLK)
    qr = _rope(q_ref[...], cos_ref[pl.ds(q0, A_BLK), :], sin_ref[pl.ds(q0, A_BLK), :])
    kstart = pl.multiple_of(jnp.clip((n - 1) * A_BLK, 0, DEC_T - 3 * A_BLK), A_BLK)
    kwin = kr_ref[pl.ds(kstart, 3 * A_BLK), :]
    vwin = v_ref[pl.ds(kstart, 3 * A_BLK), :].astype(BF16)
    kc = kc_ref[...].astype(BF16)
    vc = vc_ref[...].astype(BF16)
    rows = A_G * A_BLK
    qpos = q0 + _iota((rows, 3 * A_BLK), 0) % A_BLK
    kpos = kstart + _iota((rows, 3 * A_BLK), 1)
    valid = jnp.abs(qpos - kpos) <= A_WIN
    for kvh in range(A_KV):
        cs = slice(kvh * A_DH, (kvh + 1) * A_DH)
        q4 = jnp.concatenate(
            [qr[:, (kvh * A_G + g) * A_DH:(kvh * A_G + g + 1) * A_DH] for g in range(A_G)], axis=0).astype(BF16)
        s_loc = jnp.where(valid, _dg(q4, kwin[:, cs], NT) * scale, NEG_INF)
        s_ctx = _dg(q4, kc[:, cs], NT) * scale
        sink = _sink_col(sink_ref, kvh, A_BLK)
        m = jnp.maximum(jnp.maximum(jnp.max(s_loc, -1, keepdims=True), jnp.max(s_ctx, -1, keepdims=True)), sink)
        e_loc = jnp.exp(s_loc - m)
        e_ctx = jnp.exp(s_ctx - m)
        inv = 1.0 / (jnp.sum(e_loc, -1, keepdims=True) + jnp.sum(e_ctx, -1, keepdims=True) + jnp.exp(sink - m))
        o = _dg((e_loc * inv).astype(BF16), vwin[:, cs]) + _dg((e_ctx * inv).astype(BF16), vc[:, cs])
        for g in range(A_G):
            h = kvh * A_G + g
            o_ref[:, h * A_DH:(h + 1) * A_DH] = o[g * A_BLK:(g + 1) * A_BLK, :]


def _rope_tables():
    half = A_DH // 2
    t = np.arange(DEC_T)
    rows = (t // GRID_W).astype(np.float32)
    cols = (t % GRID_W).astype(np.float32)
    inv_freq = (ROPE_BASE ** (-np.arange(0, half, 2, dtype=np.float32) / half)).astype(np.float32)
    ang_r = rows[:, None] * inv_freq[None, :]
    ang_c = cols[:, None] * inv_freq[None, :]
    cos = np.concatenate([np.cos(ang_r), np.cos(ang_r), np.cos(ang_c), np.cos(ang_c)], -1)
    sin = np.concatenate([-np.sin(ang_r), np.sin(ang_r), -np.sin(ang_c), np.sin(ang_c)], -1)
    return (jnp.asarray(np.tile(cos, (1, A_HEADS)), F32), jnp.asarray(np.tile(sin, (1, A_HEADS)), F32))


def _attn_lat_call(sink_l, pm, kc, vc, l, cos, sin):
    nb = DEC_T // A_BLK
    row0 = N_CTX_TOK // A_BLK
    seq0 = N_CTX_TOK // DEC_T
    return pl.pallas_call(
        _attn_lat_kernel,
        grid=(N_DEC_B, nb),
        in_specs=[
            pl.BlockSpec(memory_space=pltpu.SMEM),
            pl.BlockSpec((A_BLK, A_QW), lambda b, n: (row0 + b * nb + n, 0)),
            pl.BlockSpec((DEC_T, A_KVW), lambda b, n: (seq0 + b, A_QW // A_KVW)),
            pl.BlockSpec((DEC_T, A_KVW), lambda b, n: (seq0 + b, A_QW // A_KVW + 1)),
            pl.BlockSpec((None, None, PAST, A_KVW), lambda b, n: (b, l, 0, 0)),
            pl.BlockSpec((None, None, PAST, A_KVW), lambda b, n: (b, l, 0, 0)),
            pl.BlockSpec((DEC_T, A_QW), lambda b, n: (0, 0)),
            pl.BlockSpec((DEC_T, A_QW), lambda b, n: (0, 0)),
        ],
        out_specs=pl.BlockSpec((A_BLK, A_QW), lambda b, n: (b * nb + n, 0)),
        out_shape=jax.ShapeDtypeStruct((N_DEC_B * DEC_T, A_QW), F32),
        scratch_shapes=[pltpu.VMEM((DEC_T, A_KVW), BF16)],
        compiler_params=_cparams(("arbitrary", "arbitrary")),
        name="attn_lat",
    )(sink_l, pm, pm, pm, kc, vc, cos, sin)


def _rwkv_kernel(T, NS, r_ref, k_ref, v_ref, lr_ref, s0_ref, w0_ref, wb_ref, a0_ref, ab_ref, gb_ref,
                 kk_ref, ka_ref, rk_ref, lng_ref, lnb_ref, o_ref, sfin_ref,
                 KK, W, WRP, AKK, KT, VC2, BON, Y, S):
    ones4 = _block_ones(B_W, B_DH)
    decay_c = float(np.exp(-0.5))
    RC = 256
    SUB = 16
    NP = 3

    def prep(c, carry):
        r0 = pl.multiple_of(c * RC, RC)
        rs = pl.ds(r0, RC)
        r = r_ref[rs, :]
        k = k_ref[rs, :]
        v = v_ref[rs, :]
        lr = lr_ref[rs, :]
        kkr = k * kk_ref[...]
        kk = kkr * lax.rsqrt(_mm_xr(kkr * kkr, ones4, 2) + 1e-12)
        KK[rs, :] = kk
        bonus = jnp.zeros((RC, B_W), F32)
        vc2 = jnp.zeros((RC, B_W), F32)
        for d in range(2):
            z = w0_ref[d] + _mm(jnp.tanh(lr[:, B_RANK * d:B_RANK * (d + 1)]), wb_ref[d])
            w = jnp.exp(-decay_c * _sigmoid(z))
            a = _sigmoid(a0_ref[d] + _mm(lr[:, B_RANK * (2 + d):B_RANK * (3 + d)], ab_ref[d]))
            kt = k * (1.0 + (a - 1.0) * ka_ref[...])
            akk = a * kk
            W[d, rs, :] = w
            WRP[d, rs, :] = w * r - _mm_xr(akk * r, ones4, 2) * kk
            AKK[d, rs, :] = akk
            KT[d, rs, :] = kt
            vc2 = vc2 + _mm_xr(kt * r, ones4, 2) * v
            bonus = bonus + _mm_xr(r * kt * rk_ref[...], ones4, 2) * v
        VC2[rs, :] = vc2
        BON[rs, :] = bonus
        return carry

    lax.fori_loop(0, NS * T // RC, prep, 0)

    chains = [(s, d) for s in range(NS) for d in range(2)]
    for s, d in chains:
        S[s, d] = jnp.concatenate([s0_ref[s, d, h] for h in range(B_HEADS)], axis=1)

    eye4 = _iota((B_DH, B_W), 0) == (_iota((B_DH, B_W), 1) % B_DH)

    def steps(i, carry):
        t0s = [pl.multiple_of(s * T + (i * SUB if d == 0 else T - SUB - i * SUB), SUB) for s, d in chains]

        def row(ref, g, j, d=None):
            tile = pl.ds(t0s[g] + (j // SUBLANES) * SUBLANES, SUBLANES)
            vals = ref[tile, :] if d is None else ref[d, tile, :]
            return vals[j % SUBLANES:j % SUBLANES + 1]

        ys = [[None] * SUB for _ in chains]
        for jj in range(SUB):
            lhs = []
            for g, (s, d) in enumerate(chains):
                j = jj if d == 0 else SUB - 1 - jj
                stb = S[s, d].astype(BF16)
                lhs += [stb * row(KK, g, j).astype(BF16), stb * row(WRP, g, j, d).astype(BF16),
                        jnp.where(eye4, row(v_ref, g, j), 0.0).astype(BF16)]
            res = _dg(jnp.concatenate(lhs, axis=0), ones4)
            for g, (s, d) in enumerate(chains):
                j = jj if d == 0 else SUB - 1 - jj
                sk, yp, vcol = [res[(g * NP + n) * B_DH:(g * NP + n + 1) * B_DH] for n in range(NP)]
                S[s, d] = S[s, d] * row(W, g, j, d) - sk * row(AKK, g, j, d) + vcol * row(KT, g, j, d)
                ys[g][j] = jnp.sum(jnp.where(eye4, yp, 0.0), axis=0, keepdims=True)
        for g, (s, d) in enumerate(chains):
            Y[d, pl.ds(t0s[g], SUB), :] = jnp.concatenate(ys[g], axis=0)
        return carry

    lax.fori_loop(0, T // SUB, steps, 0)

    for s, d in chains:
        st = S[s, d]
        for h in range(B_HEADS):
            sfin_ref[s, d, h] = st[:, h * B_DH:(h + 1) * B_DH]

    def post(c, carry):
        r0 = pl.multiple_of(c * RC, RC)
        rs = pl.ds(r0, RC)
        y = Y[0, rs, :] + Y[1, rs, :] + VC2[rs, :]
        mu = _mm_xr(y, ones4, 2) * (1.0 / B_DH)
        yc = y - mu
        var = _mm_xr(yc * yc, ones4, 2) * (1.0 / B_DH)
        yn = yc * lax.rsqrt(var + B_GN_EPS) * lng_ref[...] + lnb_ref[...] + BON[rs, :]
        g = _mm(_sigmoid(lr_ref[rs, 4 * B_RANK:4 * B_RANK + B_GATE_RANK]), gb_ref[...])
        o_ref[rs, :] = yn * g
        return carry

    lax.fori_loop(0, NS * T // RC, post, 0)


def _rwkv_call(T, NS, n_seq, tok0, pm, plr, s0, prm, name):
    rows = NS * T
    blk0 = tok0 // rows
    n_steps = n_seq // NS
    full = lambda shape: pl.BlockSpec(shape, lambda b: (0,) * len(shape))
    big = lambda shape, imap: (pl.BlockSpec(shape, imap, pipeline_mode=pl.Buffered(1)) if n_steps == 1
                               else pl.BlockSpec(shape, imap))
    kern = functools.partial(_rwkv_kernel, T, NS)
    return pl.pallas_call(
        kern,
        grid=(n_steps,),
        in_specs=[
            big((rows, B_W), lambda b: (blk0 + b, COL_B // B_W)),
            big((rows, B_W), lambda b: (blk0 + b, COL_B // B_W + 1)),
            big((rows, B_W), lambda b: (blk0 + b, COL_B // B_W + 2)),
            big((rows, LR_W), lambda b: (blk0 + b, 0)),
            pl.BlockSpec((NS, 2, B_HEADS, B_DH, B_DH), lambda b: (b, 0, 0, 0, 0)),
            full((2, 1, B_W)), full((2, B_RANK, B_W)), full((2, 1, B_W)), full((2, B_RANK, B_W)),
            full((B_GATE_RANK, B_W)),
            full((1, B_W)), full((1, B_W)), full((1, B_W)), full((1, B_W)), full((1, B_W)),
        ],
        out_specs=[
            big((rows, B_W), lambda b: (b, 0)),
            pl.BlockSpec((NS, 2, B_HEADS, B_DH, B_DH), lambda b: (b, 0, 0, 0, 0)),
        ],
        out_shape=[
            jax.ShapeDtypeStruct((n_seq * T, B_W), F32),
            jax.ShapeDtypeStruct((n_seq, 2, B_HEADS, B_DH, B_DH), F32),
        ],
        scratch_shapes=[
            pltpu.VMEM((rows, B_W), F32),
            pltpu.VMEM((2, rows, B_W), F32),
            pltpu.VMEM((2, rows, B_W), F32),
            pltpu.VMEM((2, rows, B_W), F32),
            pltpu.VMEM((2, rows, B_W), F32),
            pltpu.VMEM((rows, B_W), F32),
            pltpu.VMEM((rows, B_W), F32),
            pltpu.VMEM((2, rows, B_W), F32),
            pltpu.VMEM((NS, 2, B_DH, B_W), F32),
        ],
        compiler_params=_cparams(("arbitrary",)),
        name=name,
    )(pm, pm, pm, plr, s0, *prm)


def _gla_kernel(T, NS, q_ref, k_ref, v_ref, og_ref, lr_ref, s0_ref, gb_ref, bias_ref, ng_ref,
                o_ref, sfin_ref, LA, O, S):
    n_chunks = T // C_CHUNK
    nsub = C_CHUNK // C_SUB
    qscale = C_DK ** -0.5
    lr = lr_ref[...]
    for d in range(2):
        gl = _mm(lr, gb_ref[d]) + bias_ref[d]
        LA[d] = (jnp.minimum(gl, 0.0) - jnp.log(1.0 + jnp.exp(-jnp.abs(gl)))) * (1.0 / C_GATE_NORM)
    bd_state = _iota((C_KW, C_VW), 0) // C_DK == _iota((C_KW, C_VW), 1) // C_DV
    chains = [(s, d) for s in range(NS) for d in range(2)]
    for s, d in chains:
        for h in range(C_HEADS):
            pad_l = h * C_DV
            pad_r = C_VW - (h + 1) * C_DV
            blk = s0_ref[s, d, h]
            parts = ([jnp.zeros((C_DK, pad_l), F32)] if pad_l else []) + [blk] + \
                    ([jnp.zeros((C_DK, pad_r), F32)] if pad_r else [])
            S[s, d, h * C_DK:(h + 1) * C_DK, :] = jnp.concatenate(parts, axis=1)

    ti = _iota((C_CHUNK, C_CHUNK), 0)
    si = _iota((C_CHUNK, C_CHUNK), 1)
    tri = ((si <= ti).astype(BF16), (si >= ti).astype(BF16))
    trow = _iota((C_CHUNK, 1), 0)
    mask_k = _iota((C_CHUNK, C_KW), 0) // C_SUB == _iota((C_CHUNK, C_KW), 1) // C_DK
    mask_v = _iota((C_CHUNK, C_VW), 0) // C_SUB == _iota((C_CHUNK, C_VW), 1) // C_DV
    t_att = _iota((C_CHUNK, C_CHUNK), 0)
    s_att = _iota((C_CHUNK, C_CHUNK), 1) % C_SUB
    eye_k = _iota((C_KW, C_KW), 0) == _iota((C_KW, C_KW), 1)

    def body(c, carry):
        cx = []
        for s, d in chains:
            cc = c if d == 0 else n_chunks - 1 - c
            rs = pl.ds(pl.multiple_of(s * T + cc * C_CHUNK, C_CHUNK), C_CHUNK)
            b = _mm_xl(tri[d], LA[d, rs, :], 3)
            cx.append(dict(s=s, d=d, rs=rs, b=b, q=q_ref[rs, :] * qscale, k=k_ref[rs, :], v=v_ref[rs, :]))
        for x in cx:
            x["o"] = _mm(x["q"] * jnp.exp(x["b"]), S[x["s"], x["d"]])
        for j in range(nsub):
            lo, hi = j * C_SUB, (j + 1) * C_SUB
            for x in cx:
                b, q, k = x["b"], x["q"], x["k"]
                if x["d"] == 0:
                    gamma = b[hi - 1:hi, :]
                    row_ok = trow >= lo
                    att_ok = t_att >= lo + s_att
                else:
                    gamma = b[lo:lo + 1, :]
                    row_ok = trow < hi
                    att_ok = t_att <= lo + s_att
                qj = q * jnp.exp(jnp.where(row_ok, b - gamma, NEG_INF))
                kj = k[lo:hi, :] * jnp.exp(gamma - b[lo:hi, :])
                kbd = jnp.where(mask_k, jnp.concatenate([kj] * C_HEADS, axis=0), 0.0)
                x["att"] = jnp.where(att_ok, _mm(qj, kbd, NT), 0.0)
            for x in cx:
                vbd = jnp.where(mask_v, jnp.concatenate([x["v"][lo:hi, :]] * C_HEADS, axis=0), 0.0)
                x["o"] = x["o"] + _mm(x["att"], vbd)
        for x in cx:
            s, d, b = x["s"], x["d"], x["b"]
            O[d, x["rs"], :] = x["o"]
            blast = b[C_CHUNK - 1:C_CHUNK, :] if d == 0 else b[0:1, :]
            kl = x["k"] * jnp.exp(blast - b)
            upd = jnp.where(bd_state, _mm3(kl.T, x["v"]), 0.0)
            dec = jnp.where(eye_k, jnp.exp(blast), 0.0)
            S[s, d] = _mm3(dec, S[s, d]) + upd
        return carry

    lax.fori_loop(0, n_chunks, body, 0)

    for s, d in chains:
        st = S[s, d]
        for h in range(C_HEADS):
            sfin_ref[s, d, h] = st[h * C_DK:(h + 1) * C_DK, h * C_DV:(h + 1) * C_DV]

    ones4 = _block_ones(C_VW, C_DV)
    o = O[0] + O[1]
    ms = _mm_xr(o * o, ones4, 2) * (1.0 / C_DV)
    og = og_ref[...]
    o_ref[...] = o * lax.rsqrt(ms + LN_EPS) * ng_ref[...] * (og * _sigmoid(og))


def _gla_call(T, NS, n_seq, tok0, pm, plr, s0, gb_pad, bias, ng, name):
    rows = NS * T
    blk0 = tok0 // rows
    full = lambda shape: pl.BlockSpec(shape, lambda b: (0,) * len(shape))
    return pl.pallas_call(
        functools.partial(_gla_kernel, T, NS),
        grid=(n_seq // NS,),
        in_specs=[
            pl.BlockSpec((rows, C_KW), lambda b: (blk0 + b, COL_C // C_KW)),
            pl.BlockSpec((rows, C_KW), lambda b: (blk0 + b, COL_C // C_KW + 1)),
            pl.BlockSpec((rows, C_VW), lambda b: (blk0 + b, COL_CV // C_VW)),
            pl.BlockSpec((rows, C_VW), lambda b: (blk0 + b, COL_CV // C_VW + 1)),
            pl.BlockSpec((rows, LANES), lambda b: (blk0 + b, C_LR_COL // LANES)),
            pl.BlockSpec((NS, 2, C_HEADS, C_DK, C_DV), lambda b: (b, 0, 0, 0, 0)),
            full((2, LANES, C_KW)), full((2, 1, C_KW)), full((1, C_VW)),
        ],
        out_specs=[
            pl.BlockSpec((rows, C_VW), lambda b: (b, 0)),
            pl.BlockSpec((NS, 2, C_HEADS, C_DK, C_DV), lambda b: (b, 0, 0, 0, 0)),
        ],
        out_shape=[
            jax.ShapeDtypeStruct((n_seq * T, C_VW), F32),
            jax.ShapeDtypeStruct((n_seq, 2, C_HEADS, C_DK, C_DV), F32),
        ],
        scratch_shapes=[
            pltpu.VMEM((2, rows, C_KW), F32),
            pltpu.VMEM((2, rows, C_VW), F32),
            pltpu.VMEM((NS, 2, C_KW, C_VW), F32),
        ],
        compiler_params=_cparams(("arbitrary",)),
        name=name,
    )(pm, pm, pm, pm, plr, s0, gb_pad, bias, ng)


MERGE_TM = 512


def _merge_kernel(n_x, *refs):
    x_refs = refs[:n_x]
    (g_ref, oac_ref, oad_ref, obc_ref, obd_ref, occ_ref, ocd_ref, mod_ref,
     wa_ref, wb_ref, wc_ref, wo_ref, lg_ref, lb_ref, o_ref) = refs[n_x:]
    is_ctx = pl.program_id(0) < N_CTX_TOK // MERGE_TM
    oa = jnp.where(is_ctx, oac_ref[...], oad_ref[...]).astype(BF16)
    ob = jnp.where(is_ctx, obc_ref[...], obd_ref[...]).astype(BF16)
    oc = jnp.where(is_ctx, occ_ref[...], ocd_ref[...]).astype(BF16)
    merged = (g_ref[:, 0:D].astype(F32) * _dg(oa, wa_ref[...])
              + g_ref[:, D:2 * D].astype(F32) * _dg(ob, wb_ref[...])
              + g_ref[:, 2 * D:3 * D].astype(F32) * _dg(oc, wc_ref[...]))
    mix = _dg(merged.astype(BF16), wo_ref[...])
    y = DN_ALPHA * _read_x(x_refs, MERGE_TM) + mod_ref[2:3, :] * mix
    o_ref[...] = _layer_norm(y, lg_ref[...], lb_ref[...])


def _merge_call(xs, gates, oa, ob, oc, mod_l, wa, wb, wc, wo, lg, lb):
    tm = MERGE_TM
    x_specs = [pl.BlockSpec((tm, D), lambda i: (i, 0))] if len(xs) == 1 else _pair_specs(tm, D)
    return pl.pallas_call(
        functools.partial(_merge_kernel, len(xs)),
        grid=(N_TOK // tm,),
        in_specs=[
            *x_specs,
            pl.BlockSpec((tm, IN_GATE), lambda i: (i, 0)),
            *_pair_specs(tm, A_QW), *_pair_specs(tm, B_W), *_pair_specs(tm, C_VW),
            pl.BlockSpec((None, 6, D), lambda i: (_group_of_tile(i, tm), 0, 0)),
            _resident((A_QW, D)), _resident((B_W, D)), _resident((C_VW, D)), _resident((D, D)),
            _resident((1, D)), _resident((1, D)),
        ],
        out_specs=pl.BlockSpec((tm, D), lambda i: (i, 0)),
        out_shape=jax.ShapeDtypeStruct((N_TOK, D), F32),
        compiler_params=_cparams(("arbitrary",)),
        name="merge",
    )(*xs, gates, *oa, *ob, *oc, mod_l, wa, wb, wc, wo, lg.reshape(1, D), lb.reshape(1, D))


def _ffn_kernel(x_ref, mod_ref, wg_ref, wu_ref, wd_ref, lg_ref, lb_ref, o_ref):
    x = x_ref[...]
    h = (x * (1.0 + mod_ref[4:5, :]) + mod_ref[3:4, :]).astype(BF16)
    gate = _dg(h, wg_ref[...])
    up = _dg(h, wu_ref[...])
    f = _dg((gate * _sigmoid(gate) * up).astype(BF16), wd_ref[...])
    y = DN_ALPHA * x + mod_ref[5:6, :] * f
    o_ref[...] = _layer_norm(y, lg_ref[...], lb_ref[...])


def _ffn_call(x, mod_l, wg, wu, wd, lg, lb):
    tm = 512
    return pl.pallas_call(
        _ffn_kernel,
        grid=(N_TOK // tm,),
        in_specs=[
            pl.BlockSpec((tm, D), lambda i: (i, 0)),
            pl.BlockSpec((None, 6, D), lambda i: (_group_of_tile(i, tm), 0, 0)),
            _resident((D, D_FF)), _resident((D, D_FF)), _resident((D_FF, D)),
            _resident((1, D)), _resident((1, D)),
        ],
        out_specs=pl.BlockSpec((tm, D), lambda i: (i, 0)),
        out_shape=jax.ShapeDtypeStruct((N_TOK, D), F32),
        compiler_params=_cparams(("arbitrary",)),
        name="ffn",
    )(x, mod_l, wg, wu, wd, lg.reshape(1, D), lb.reshape(1, D))


MOE_TM = 1024
MOE_TR = 256
MOE_RS = 3072
MOE_TF = 512
MOE_ROWS = 2 * N_TOK + N_EXP * MOE_TR
MOE_NST = -(-MOE_ROWS // MOE_RS) + N_EXP
R_I1, R_I2, R_W1, R_W2, R_RANK1, R_RANK2 = range(6)


def _moe_route_kernel(x_ref, mod_ref, wr_ref, h_ref, info_ref, cnt_ref, carry_s):
    tm = MOE_TM

    @pl.when(pl.program_id(0) == 0)
    def _():
        carry_s[...] = jnp.zeros_like(carry_s)

    h = x_ref[...] * (1.0 + mod_ref[4:5, :]) + mod_ref[3:4, :]
    h_ref[...] = h
    logits = _mm3(h, wr_ref[...])
    lane = _iota(logits.shape, 1)
    logits = jnp.where(lane < N_EXP, logits, NEG_INF)
    v1 = jnp.max(logits, -1, keepdims=True)
    i1 = jnp.min(jnp.where(logits == v1, lane, LANES), -1, keepdims=True)
    rest = jnp.where(lane == i1, NEG_INF, logits)
    v2 = jnp.max(rest, -1, keepdims=True)
    i2 = jnp.min(jnp.where(rest == v2, lane, LANES), -1, keepdims=True)
    e2 = jnp.exp(v2 - v1)
    w1 = 1.0 / (1.0 + e2)
    w2 = e2 / (1.0 + e2)
    oh1 = lane == i1
    oh2 = lane == i2
    cnt = oh1.astype(F32) + oh2.astype(F32)
    earlier = (_iota((tm, tm), 1) < _iota((tm, tm), 0)).astype(BF16)
    before = _dg(earlier, cnt.astype(BF16)) + carry_s[...]
    rank1 = jnp.sum(jnp.where(oh1, before, 0.0), -1, keepdims=True)
    rank2 = jnp.sum(jnp.where(oh2, before, 0.0), -1, keepdims=True)
    info = jnp.zeros(logits.shape, F32)
    for ln, val in ((R_I1, i1.astype(F32)), (R_I2, i2.astype(F32)), (R_W1, w1), (R_W2, w2),
                    (R_RANK1, rank1), (R_RANK2, rank2)):
        info = jnp.where(lane == ln, val, info)
    info_ref[...] = info
    carry_s[...] += jnp.sum(cnt, axis=0, keepdims=True)
    cnt_ref[...] = carry_s[...]


def _moe_route_call(x, mod_l, wr_pad, i_moe):
    tm = MOE_TM
    return pl.pallas_call(
        _moe_route_kernel,
        grid=(N_TOK // tm,),
        in_specs=[
            pl.BlockSpec((tm, D), lambda i: (i, 0)),
            pl.BlockSpec((None, 6, D), lambda i: (_group_of_tile(i, tm), 0, 0)),
            pl.BlockSpec((None, D, LANES), lambda i: (i_moe, 0, 0)),
        ],
        out_specs=[
            pl.BlockSpec((tm, D), lambda i: (i, 0)),
            pl.BlockSpec((tm, LANES), lambda i: (i, 0)),
            pl.BlockSpec((1, LANES), lambda i: (0, 0)),
        ],
        out_shape=[
            jax.ShapeDtypeStruct((N_TOK, D), F32),
            jax.ShapeDtypeStruct((N_TOK, LANES), F32),
            jax.ShapeDtypeStruct((1, LANES), F32),
        ],
        scratch_shapes=[pltpu.VMEM((1, LANES), F32)],
        compiler_params=_cparams(("arbitrary",)),
        name="moe_route",
    )(x, mod_l, wr_pad)


def _moe_plan(info, cnt):
    i32 = jnp.int32
    i1 = info[:, R_I1].astype(i32)
    i2 = info[:, R_I2].astype(i32)
    counts = cnt[0, :N_EXP].astype(i32)
    padded = (counts + MOE_TR - 1) // MOE_TR * MOE_TR
    seg_start = jnp.cumsum(padded) - padded
    pos1 = seg_start[i1] + info[:, R_RANK1].astype(i32)
    pos2 = seg_start[i2] + info[:, R_RANK2].astype(i32)
    dst = jnp.zeros((MOE_ROWS,), i32).at[jnp.concatenate([pos1, pos2])].set(jnp.arange(2 * N_TOK, dtype=i32))
    n_pass = (padded + MOE_RS - 1) // MOE_RS
    pass_end = jnp.cumsum(n_pass)
    total = pass_end[-1]
    sidx = jnp.arange(MOE_NST, dtype=i32)
    used = sidx < total
    e_of = jnp.minimum(jnp.searchsorted(pass_end, jnp.minimum(sidx, total - 1), side="right"), N_EXP - 1).astype(i32)
    k = jnp.minimum(sidx, total - 1) - (pass_end - n_pass)[e_of]
    row0 = seg_start[e_of] + k * MOE_RS
    nrows = jnp.where(used, jnp.clip(padded[e_of] - k * MOE_RS, 0, MOE_RS), 0)
    nvalid = jnp.where(used, jnp.clip(counts[e_of] - k * MOE_RS, 0, MOE_RS), 0)
    return dst, e_of, row0.astype(i32), nrows.astype(i32), nvalid.astype(i32)


def _moe_expert_kernel(dst_ref, exp_ref, row0_ref, nrows_ref, nvalid_ref,
                       h_hbm, wg_ref, wu_ref, wd_ref, yo_hbm, xs, xb, acc, wgb, wub, wdb, gsem, ssem):
    s = pl.program_id(0)
    f = pl.program_id(1)
    nf = pl.num_programs(1)
    nrows = pl.multiple_of(nrows_ref[s], MOE_TR)
    nvalid = nvalid_ref[s]
    row0 = row0_ref[s]
    n_chunks = nrows // MOE_TR

    def row_copy(src, dst, sem):
        return pltpu.make_async_copy(src, dst, sem)

    def hbm_row(ref, i):
        return ref.at[pl.ds(i, 1), :]

    def wait_rows(buf, n_groups, sem):
        pltpu.make_async_copy(buf.at[pl.ds(0, n_groups)], buf.at[pl.ds(0, n_groups)], sem).wait()

    @pl.when((f == 0) & (nrows > 0))
    def _gather():
        def issue(g, carry):
            for u in range(SUBLANES):
                d = dst_ref[row0 + g * SUBLANES + u]
                tok = jnp.where(d >= N_TOK, d - N_TOK, d)
                row_copy(hbm_row(h_hbm, tok), xs.at[g, pl.ds(u, 1), :], gsem).start()
            return carry

        ng = nrows // SUBLANES
        lax.fori_loop(0, ng, issue, 0)
        wait_rows(xs, ng, gsem)

        def cvt(c, carry):
            g0 = pl.multiple_of(c * (MOE_TR // SUBLANES), MOE_TR // SUBLANES)
            rs = pl.ds(pl.multiple_of(c * MOE_TR, MOE_TR), MOE_TR)
            xb[rs, :] = xs[pl.ds(g0, MOE_TR // SUBLANES)].reshape(MOE_TR, D).astype(BF16)
            return carry

        lax.fori_loop(0, n_chunks, cvt, 0)

    @pl.when(nrows > 0)
    def _compute():
        wgb[...] = wg_ref[...].astype(BF16)
        wub[...] = wu_ref[...].astype(BF16)
        wdb[...] = wd_ref[...].astype(BF16)

        def chunk(start, n):
            x = xb[pl.ds(pl.multiple_of(start, MOE_TR), n), :]
            gate = _dg(x, wgb[...])
            up = _dg(x, wub[...])
            y = _dg((gate * _sigmoid(gate) * up).astype(BF16), wdb[...]).reshape(n // SUBLANES, SUBLANES, D)
            gs = pl.ds(pl.multiple_of(start // SUBLANES, MOE_TR // SUBLANES), n // SUBLANES)

            @pl.when(f == 0)
            def _():
                acc[gs] = y

            @pl.when(f > 0)
            def _():
                acc[gs] += y

        def chunk_quad(c4, carry):
            chunk(c4 * (4 * MOE_TR), 4 * MOE_TR)
            return carry

        lax.fori_loop(0, n_chunks // 4, chunk_quad, 0)
        done = n_chunks // 4 * 4

        @pl.when(n_chunks % 4 >= 2)
        def _():
            chunk(done * MOE_TR, 2 * MOE_TR)

        @pl.when(n_chunks % 2 == 1)
        def _():
            chunk((n_chunks - 1) * MOE_TR, MOE_TR)

    @pl.when((f == nf - 1) & (nvalid > 0))
    def _scatter():
        n8 = nvalid // SUBLANES

        def issue(g, carry):
            for u in range(SUBLANES):
                d = dst_ref[row0 + g * SUBLANES + u]
                row_copy(acc.at[g, pl.ds(u, 1), :], hbm_row(yo_hbm, d), ssem).start()
            return carry

        def issue_tail(r, carry):
            d = dst_ref[row0 + r]
            row_copy(acc.at[n8, pl.ds(r - n8 * SUBLANES, 1), :], hbm_row(yo_hbm, d), ssem).start()
            return carry

        lax.fori_loop(0, n8, issue, 0)
        lax.fori_loop(n8 * SUBLANES, nvalid, issue_tail, 0)

        @pl.when(n8 > 0)
        def _():
            wait_rows(acc, n8, ssem)

        def wait_one(r, carry):
            row_copy(acc.at[0, pl.ds(0, 1), :], hbm_row(yo_hbm, 0), ssem).wait()
            return carry

        lax.fori_loop(n8 * SUBLANES, nvalid, wait_one, 0)


def _moe_expert_call(h, plan, wg, wu, wd, i_moe):
    nf = D_FFE // MOE_TF

    def wspec(shape, fdim):
        def imap(s, f, dst, exp, row0, nrows, nvalid):
            fe = jnp.where(nrows[s] > 0, f, nf - 1)
            return (i_moe, exp[s], 0, fe) if fdim == 3 else (i_moe, exp[s], fe, 0)
        return pl.BlockSpec(shape, imap)

    grid_spec = pltpu.PrefetchScalarGridSpec(
        num_scalar_prefetch=5,
        grid=(MOE_NST, nf),
        in_specs=[
            pl.BlockSpec(memory_space=pl.ANY),
            wspec((None, None, D, MOE_TF), 3),
            wspec((None, None, D, MOE_TF), 3),
            wspec((None, None, MOE_TF, D), 2),
        ],
        out_specs=pl.BlockSpec(memory_space=pl.ANY),
        scratch_shapes=[
            pltpu.VMEM((MOE_RS // SUBLANES, SUBLANES, D), F32),
            pltpu.VMEM((MOE_RS, D), BF16),
            pltpu.VMEM((MOE_RS // SUBLANES, SUBLANES, D), F32),
            pltpu.VMEM((D, MOE_TF), BF16), pltpu.VMEM((D, MOE_TF), BF16), pltpu.VMEM((MOE_TF, D), BF16),
            pltpu.SemaphoreType.DMA(()), pltpu.SemaphoreType.DMA(()),
        ],
    )
    return pl.pallas_call(
        _moe_expert_kernel,
        grid_spec=grid_spec,
        out_shape=jax.ShapeDtypeStruct((2 * N_TOK, D), F32),
        compiler_params=pltpu.CompilerParams(dimension_semantics=("arbitrary", "arbitrary"),
                                             vmem_limit_bytes=VMEM_LIMIT, disable_bounds_checks=True),
        name="moe_experts",
    )(*plan, h, wg, wu, wd)


def _moe_combine_kernel(x_ref, y1_ref, y2_ref, info_ref, mod_ref, lg_ref, lb_ref, oc_ref, od_ref):
    f = info_ref[:, R_W1:R_W1 + 1] * y1_ref[...] + info_ref[:, R_W2:R_W2 + 1] * y2_ref[...]
    y = DN_ALPHA * x_ref[...] + mod_ref[5:6, :] * f
    out = _layer_norm(y, lg_ref[...], lb_ref[...])
    is_ctx = pl.program_id(0) < N_CTX_TOK // MOE_TM

    @pl.when(is_ctx)
    def _():
        oc_ref[...] = out

    @pl.when(jnp.logical_not(is_ctx))
    def _():
        od_ref[...] = out


def _moe_combine_call(x, yo, info, mod_l, lg, lb):
    tm = MOE_TM
    nt = N_TOK // tm
    return pl.pallas_call(
        _moe_combine_kernel,
        grid=(nt,),
        in_specs=[
            pl.BlockSpec((tm, D), lambda i: (i, 0)),
            pl.BlockSpec((tm, D), lambda i: (i, 0)),
            pl.BlockSpec((tm, D), lambda i: (nt + i, 0)),
            pl.BlockSpec((tm, LANES), lambda i: (i, 0)),
            pl.BlockSpec((None, 6, D), lambda i: (_group_of_tile(i, tm), 0, 0)),
            _resident((1, D)), _resident((1, D)),
        ],
        out_specs=_pair_specs(tm, D),
        out_shape=[jax.ShapeDtypeStruct((N_CTX_TOK, D), F32), jax.ShapeDtypeStruct((N_DEC_B * DEC_T, D), F32)],
        compiler_params=_cparams(("arbitrary",)),
        name="moe_combine",
    )(x, yo, yo, info, mod_l, lg.reshape(1, D), lb.reshape(1, D))


def _moe_call(x, mod_l, wr_pad, wg, wu, wd, lg, lb, i_moe):
    h, info, cnt = _moe_route_call(x, mod_l, wr_pad, i_moe)
    yo = _moe_expert_call(h, _moe_plan(info, cnt), wg, wu, wd, i_moe)
    return _moe_combine_call(x, yo, info, mod_l, lg, lb)


def kernel(x_prompt, x_sample, cache_attn_k, cache_attn_v, state_rwkv, state_gla, c, c_ctx, w_ada, b_ada, w_in,
           attn_sink, rwkv_w0, rwkv_w_a, rwkv_w_b, rwkv_a0, rwkv_a_a, rwkv_a_b, rwkv_g_a, rwkv_g_b, rwkv_k_k,
           rwkv_k_a, rwkv_r_k, rwkv_ln_g, rwkv_ln_b, gla_gate_a, gla_gate_b, gla_gate_bias, gla_norm_g, w_up_a,
           w_up_b, w_up_c, w_out, ln1_g, ln1_b, ln2_g, ln2_b, ffn_w_gate, ffn_w_up, ffn_w_down, moe_router,
           moe_w_gate, moe_w_up, moe_w_down):
    cvec = jnp.concatenate([c_ctx[None, :], c, jnp.zeros((N_GROUPS - 1 - N_DEC_B, D), F32)], axis=0)
    mods = _ada_call(cvec, w_ada, b_ada).reshape(DEPTH, N_GROUPS, 6, D)
    xs = (x_prompt.reshape(N_CTX_TOK, D), x_sample.reshape(N_DEC_B * DEC_T, D))
    cos, sin = _rope_tables()
    kc_all = cache_attn_k.reshape(N_DEC_B, DEPTH, PAST, A_KV * A_DH)
    vc_all = cache_attn_v.reshape(N_DEC_B, DEPTH, PAST, A_KV * A_DH)
    zeros_r = jnp.zeros((N_CTX_B, 2, B_HEADS, B_DH, B_DH), F32)
    zeros_g = jnp.zeros((N_CTX_B, 2, C_HEADS, C_DK, C_DV), F32)

    new_k, new_v, new_sr, new_sg = [], [], [], []
    for l in range(DEPTH):
        mod_l = mods[l]
        w_lr = jnp.concatenate(
            [rwkv_w_a[l, 0], rwkv_w_a[l, 1], rwkv_a_a[l, 0], rwkv_a_a[l, 1], rwkv_g_a[l],
             gla_gate_a[l, 0], gla_gate_a[l, 1], jnp.zeros((D, LR_W - C_LR_COL - 2 * C_GATE_RANK), F32)], axis=1).astype(BF16)
        pm, gates, plr, k_ctx, v_ctx = _inproj_call(xs, mod_l, w_in[l, :, :IN_MAIN].astype(BF16),
                                                    w_in[l, :, IN_MAIN:].astype(BF16), w_lr)

        sink_l = attn_sink[l]
        oa = (_attn_ctx_call(sink_l, pm), _attn_lat_call(sink_l, pm, kc_all, vc_all, l, cos, sin))

        rprm = (rwkv_w0[l].reshape(2, 1, B_W), rwkv_w_b[l], rwkv_a0[l].reshape(2, 1, B_W), rwkv_a_b[l],
                rwkv_g_b[l], rwkv_k_k[l].reshape(1, B_W), rwkv_k_a[l].reshape(1, B_W),
                rwkv_r_k[l].reshape(1, B_W), rwkv_ln_g[l].reshape(1, B_W), rwkv_ln_b[l].reshape(1, B_W))
        ob_c, sr_c = _rwkv_call(CTX_T, 4, N_CTX_B, 0, pm, plr, zeros_r, rprm, "rwkv_ctx")
        ob_d, _ = _rwkv_call(DEC_T, 2, N_DEC_B, N_CTX_TOK, pm, plr, state_rwkv[:, l], rprm, "rwkv_lat")

        gb_pad = jnp.zeros((2, LANES, C_KW), F32)
        gb_pad = gb_pad.at[0, 0:C_GATE_RANK].set(gla_gate_b[l, 0]).at[1, C_GATE_RANK:2 * C_GATE_RANK].set(
            gla_gate_b[l, 1])
        gbias = gla_gate_bias[l].reshape(2, 1, C_KW)
        ng = jnp.tile(gla_norm_g[l], C_HEADS).reshape(1, C_VW)
        oc_c, sg_c = _gla_call(CTX_T, 4, N_CTX_B, 0, pm, plr, zeros_g, gb_pad, gbias, ng, "gla_ctx")
        oc_d, _ = _gla_call(DEC_T, 2, N_DEC_B, N_CTX_TOK, pm, plr, state_gla[:, l], gb_pad, gbias, ng, "gla_lat")

        x = _merge_call(xs, gates, oa, (ob_c, ob_d), (oc_c, oc_d), mod_l, w_up_a[l].astype(BF16),
                        w_up_b[l].astype(BF16), w_up_c[l].astype(BF16), w_out[l].astype(BF16), ln1_g[l], ln1_b[l])
        if l % 2 == 0:
            i_ffn = l // 2
            xs = (_ffn_call(x, mod_l, ffn_w_gate[i_ffn].astype(BF16), ffn_w_up[i_ffn].astype(BF16),
                            ffn_w_down[i_ffn].astype(BF16), ln2_g[l], ln2_b[l]),)
        else:
            wr_pad = jnp.concatenate([moe_router, jnp.zeros((moe_router.shape[0], D, LANES - N_EXP), F32)], axis=2)
            xs = _moe_call(x, mod_l, wr_pad, moe_w_gate, moe_w_up, moe_w_down, ln2_g[l], ln2_b[l], l // 2)

        new_k.append(k_ctx.reshape(N_CTX_B, CTX_T, A_KV, A_DH))
        new_v.append(v_ctx.reshape(N_CTX_B, CTX_T, A_KV, A_DH))
        new_sr.append(sr_c)
        new_sg.append(sg_c)

    if len(xs) == 1:
        xs = (xs[0][:N_CTX_TOK], xs[0][N_CTX_TOK:])
    y_prompt = xs[0].reshape(N_CTX_B, CTX_T, D)
    y_sample = xs[1].reshape(N_DEC_B, DEC_T, D)
    return (y_prompt, y_sample, jnp.stack(new_k, axis=1), jnp.stack(new_v, axis=1),
            jnp.stack(new_sr, axis=1), jnp.stack(new_sg, axis=1))
```

```python
import functools

import numpy as np
import jax
import jax.numpy as jnp
from jax import lax
from jax.experimental import pallas as pl
from jax.experimental.pallas import tpu as pltpu

D = 1024
N_CTX_B, CTX_T = 32, 256
N_DEC_B, DEC_T = 2, 1024
N_CTX_TOK = N_CTX_B * CTX_T
N_TOK = N_CTX_TOK + N_DEC_B * DEC_T
DEPTH = 2
PAST = 256
GRID_W = 64
A_HEADS, A_KV, A_DH = 8, 2, 64
A_G = A_HEADS // A_KV
A_WIN, A_BLK = 128, 128
ROPE_BASE = 10000.0
B_HEADS, B_DH = 4, 64
B_W = B_HEADS * B_DH
B_GN_EPS = 64e-5
C_HEADS, C_DK, C_DV = 4, 32, 64
C_KW, C_VW = C_HEADS * C_DK, C_HEADS * C_DV
C_GATE_RANK = 16
C_GATE_NORM = 16.0
C_CHUNK = 64
C_SUB = 16
D_FF = 2816
N_EXP = 8
D_FFE = 3584
LN_EPS = 1e-5
DN_ALPHA = (2.0 * DEPTH) ** 0.25
NEG_INF = -1e30
A_QW, A_KVW = A_HEADS * A_DH, A_KV * A_DH
B_RANK, B_GATE_RANK = 64, 128
LANES = 128
IN_MAIN = 2304
COL_B = A_QW + 2 * A_KVW
COL_C = COL_B + 3 * B_W
COL_CV = COL_C + 2 * C_KW
IN_GATE = 3072
LR_W = 512
C_LR_COL = 4 * B_RANK + B_GATE_RANK
N_GROUPS = 8

F32 = jnp.float32
BF16 = jnp.bfloat16
VMEM_LIMIT = 56 * 1024 * 1024
SUBLANES = 8

NN = ((1,), (0,))
NT = ((1,), (1,))
TN = ((0,), (0,))


def _dg(a, b, dims=NN):
    return lax.dot_general(a, b, (dims, ((), ())), preferred_element_type=F32)


def _split2(x):
    hi = x.astype(BF16)
    lo = (x - hi.astype(F32)).astype(BF16)
    return hi, lo


def _split3(x):
    hi = x.astype(BF16)
    r = x - hi.astype(F32)
    mid = r.astype(BF16)
    lo = (r - mid.astype(F32)).astype(BF16)
    return hi, mid, lo


def _mm(a, b, dims=NN):
    return _dg(a.astype(BF16), b.astype(BF16), dims)


def _mm3(a, b, dims=NN):
    ah, al = _split2(a)
    bh, bl = _split2(b)
    return _dg(ah, bh, dims) + (_dg(ah, bl, dims) + _dg(al, bh, dims))


def _mm_xr(a, b_exact, passes, dims=NN):
    parts = (a.astype(BF16),) if passes == 1 else (_split2(a) if passes == 2 else _split3(a))
    out = _dg(parts[0], b_exact, dims)
    for p in parts[1:]:
        out = out + _dg(p, b_exact, dims)
    return out


def _mm_xl(a_exact, b, passes):
    parts = (b.astype(BF16),) if passes == 1 else (_split2(b) if passes == 2 else _split3(b))
    out = _dg(a_exact, parts[0])
    for p in parts[1:]:
        out = out + _dg(a_exact, p)
    return out


def _sigmoid(x):
    return 1.0 / (1.0 + jnp.exp(-x))


def _iota(shape, dim):
    return lax.broadcasted_iota(jnp.int32, shape, dim)


def _block_ones(n, blk):
    return (_iota((n, n), 0) // blk == _iota((n, n), 1) // blk).astype(BF16)


def _layer_norm(y, g, b):
    mu = jnp.mean(y, -1, keepdims=True)
    yc = y - mu
    var = jnp.mean(yc * yc, -1, keepdims=True)
    return yc * lax.rsqrt(var + LN_EPS) * g + b


def _group_of_tile(i, tm):
    n_ctx = N_CTX_TOK // tm
    per_dec = DEC_T // tm
    return jnp.where(i < n_ctx, 0, 1 + (i - n_ctx) // per_dec)


def _cparams(sem):
    return pltpu.CompilerParams(dimension_semantics=sem, vmem_limit_bytes=VMEM_LIMIT)


def _ada_kernel(c_ref, w_ref, b_ref, o_ref):
    c = c_ref[...]
    s = c * _sigmoid(c)
    o_ref[...] = _mm3(s, w_ref[...]) + b_ref[...]


def _ada_call(cvec, w_ada, b_ada):
    tn = 1536
    return pl.pallas_call(
        _ada_kernel,
        grid=(DEPTH, 6 * D // tn),
        in_specs=[
            pl.BlockSpec((N_GROUPS, D), lambda l, j: (0, 0)),
            pl.BlockSpec((None, D, tn), lambda l, j: (l, 0, j)),
            pl.BlockSpec((None, 1, tn), lambda l, j: (l, 0, j)),
        ],
        out_specs=pl.BlockSpec((None, N_GROUPS, tn), lambda l, j: (l, 0, j)),
        out_shape=jax.ShapeDtypeStruct((DEPTH, N_GROUPS, 6 * D), F32),
        compiler_params=_cparams(("arbitrary", "arbitrary")),
        name="ada",
    )(cvec, w_ada, b_ada.reshape(DEPTH, 1, 6 * D))


def _pair_specs(tm, width):
    n_ctx = N_CTX_TOK // tm
    return [pl.BlockSpec((tm, width), lambda i: (jnp.minimum(i, n_ctx - 1), 0)),
            pl.BlockSpec((tm, width), lambda i: (jnp.maximum(i - n_ctx, 0), 0))]


def _read_x(x_refs, tm):
    if len(x_refs) == 1:
        return x_refs[0][...]
    return jnp.where(pl.program_id(0) < N_CTX_TOK // tm, x_refs[0][...], x_refs[1][...])


INPROJ_TM = 256


def _inproj_kernel(n_x, *refs):
    x_refs = refs[:n_x]
    mod_ref, wm_ref, wg_ref, wl_ref, om_ref, og_ref, ol_ref, ok_ref, ov_ref = refs[n_x:]
    sh = mod_ref[0:1, :]
    sc = mod_ref[1:2, :]
    h = (_read_x(x_refs, INPROJ_TM) * (1.0 + sc) + sh).astype(BF16)
    main = _dg(h, wm_ref[...])
    om_ref[...] = main
    og_ref[...] = _sigmoid(_dg(h, wg_ref[...])).astype(BF16)
    ol_ref[...] = _dg(h, wl_ref[...])

    @pl.when(pl.program_id(0) < N_CTX_TOK // INPROJ_TM)
    def _():
        ok_ref[...] = main[:, A_QW:A_QW + A_KVW]
        ov_ref[...] = main[:, A_QW + A_KVW:A_QW + 2 * A_KVW]


def _resident(shape):
    return pl.BlockSpec(shape, lambda *_: (0,) * len(shape), pipeline_mode=pl.Buffered(1))


def _inproj_call(xs, mod_l, w_main, w_gate, w_lr):
    tm = INPROJ_TM
    n_ctx = N_CTX_TOK // tm
    x_specs = [pl.BlockSpec((tm, D), lambda i: (i, 0))] if len(xs) == 1 else _pair_specs(tm, D)
    kv_spec = pl.BlockSpec((tm, A_KV * A_DH), lambda i: (jnp.minimum(i, n_ctx - 1), 0))
    return pl.pallas_call(
        functools.partial(_inproj_kernel, len(xs)),
        grid=(N_TOK // tm,),
        in_specs=[
            *x_specs,
            pl.BlockSpec((None, 6, D), lambda i: (_group_of_tile(i, tm), 0, 0)),
            _resident((D, IN_MAIN)), _resident((D, IN_GATE)), _resident((D, LR_W)),
        ],
        out_specs=[
            pl.BlockSpec((tm, IN_MAIN), lambda i: (i, 0)),
            pl.BlockSpec((tm, IN_GATE), lambda i: (i, 0)),
            pl.BlockSpec((tm, LR_W), lambda i: (i, 0)),
            kv_spec, kv_spec,
        ],
        out_shape=[
            jax.ShapeDtypeStruct((N_TOK, IN_MAIN), F32),
            jax.ShapeDtypeStruct((N_TOK, IN_GATE), BF16),
            jax.ShapeDtypeStruct((N_TOK, LR_W), F32),
            jax.ShapeDtypeStruct((N_CTX_TOK, A_KV * A_DH), F32),
            jax.ShapeDtypeStruct((N_CTX_TOK, A_KV * A_DH), F32),
        ],
        compiler_params=_cparams(("arbitrary",)),
        name="inproj",
    )(*xs, mod_l, w_main, w_gate, w_lr)


def _sink_col(sink_ref, kvh, rows_per_head):
    n = A_G * rows_per_head
    r = _iota((n, 1), 0) // rows_per_head
    col = jnp.full((n, 1), sink_ref[kvh * A_G], F32)
    for g in range(1, A_G):
        col = jnp.where(r == g, sink_ref[kvh * A_G + g], col)
    return col


def _attn_ctx_kernel(sink_ref, q_ref, k_ref, v_ref, o_ref):
    scale = A_DH ** -0.5
    for kvh in range(A_KV):
        ks = k_ref[:, kvh * A_DH:(kvh + 1) * A_DH].astype(BF16)
        vs = v_ref[:, kvh * A_DH:(kvh + 1) * A_DH].astype(BF16)
        q4 = jnp.concatenate(
            [q_ref[:, (kvh * A_G + g) * A_DH:(kvh * A_G + g + 1) * A_DH] for g in range(A_G)], axis=0)
        s = _dg(q4.astype(BF16), ks, NT) * scale
        sink = _sink_col(sink_ref, kvh, CTX_T)
        m = jnp.maximum(jnp.max(s, -1, keepdims=True), sink)
        e = jnp.exp(s - m)
        p = e / (jnp.sum(e, -1, keepdims=True) + jnp.exp(sink - m))
        o = _dg(p.astype(BF16), vs)
        for g in range(A_G):
            h = kvh * A_G + g
            o_ref[:, h * A_DH:(h + 1) * A_DH] = o[g * CTX_T:(g + 1) * CTX_T, :]


def _attn_ctx_call(sink_l, pm):
    return pl.pallas_call(
        _attn_ctx_kernel,
        grid=(N_CTX_B,),
        in_specs=[
            pl.BlockSpec(memory_space=pltpu.SMEM),
            pl.BlockSpec((CTX_T, A_QW), lambda b: (b, 0)),
            pl.BlockSpec((CTX_T, A_KVW), lambda b: (b, A_QW // A_KVW)),
            pl.BlockSpec((CTX_T, A_KVW), lambda b: (b, A_QW // A_KVW + 1)),
        ],
        out_specs=pl.BlockSpec((CTX_T, A_QW), lambda b: (b, 0)),
        out_shape=jax.ShapeDtypeStruct((N_CTX_TOK, A_QW), F32),
        compiler_params=_cparams(("arbitrary",)),
        name="attn_ctx",
    )(sink_l, pm, pm, pm)


def _rope(x, cos, sin_signed):
    w = x.shape[-1]
    lane = _iota(x.shape, 1)
    q = A_DH // 4
    partner = jnp.where((lane % (2 * q)) < q, pltpu.roll(x, w - q, 1), pltpu.roll(x, q, 1))
    return x * cos + partner * sin_signed


def _attn_lat_kernel(sink_ref, q_ref, k_ref, v_ref, kc_ref, vc_ref, cos_ref, sin_ref, o_ref, kr_ref):
    n = pl.program_id(1)
    scale = A_DH ** -0.5

    @pl.when(n == 0)
    def _():
        kr_ref[...] = _rope(k_ref[...], cos_ref[:, 0:A_KVW], sin_ref[:, 0:A_KVW]).astype(BF16)

    q0 = pl.multiple_of(n * A_BLK, A_BLK)
    qr = _rope(q_ref[...], cos_ref[pl.ds(q0, A_BLK), :], sin_ref[pl.ds(q0, A_BLK), :])
    kstart = pl.multiple_of(jnp.clip((n - 1) * A_BLK, 0, DEC_T - 3 * A_BLK), A_BLK)
    kwin = kr_ref[pl.ds(kstart, 3 * A_BLK), :]
    vwin = v_ref[pl.ds(kstart, 3 * A_BLK), :].astype(BF16)
    kc = kc_ref[...].astype(BF16)
    vc = vc_ref[...].astype(BF16)
    rows = A_G * A_BLK
    qpos = q0 + _iota((rows, 3 * A_BLK), 0) % A_BLK
    kpos = kstart + _iota((rows, 3 * A_BLK), 1)
    valid = jnp.abs(qpos - kpos) <= A_WIN
    for kvh in range(A_KV):
        cs = slice(kvh * A_DH, (kvh + 1) * A_DH)
        q4 = jnp.concatenate(
            [qr[:, (kvh * A_G + g) * A_DH:(kvh * A_G + g + 1) * A_DH] for g in range(A_G)], axis=0).astype(BF16)
        s_loc = jnp.where(valid, _dg(q4, kwin[:, cs], NT) * scale, NEG_INF)
        s_ctx = _dg(q4, kc[:, cs], NT) * scale
        sink = _sink_col(sink_ref, kvh, A_BLK)
        m = jnp.maximum(jnp.maximum(jnp.max(s_loc, -1, keepdims=True), jnp.max(s_ctx, -1, keepdims=True)), sink)
        e_loc = jnp.exp(s_loc - m)
        e_ctx = jnp.exp(s_ctx - m)
        inv = 1.0 / (jnp.sum(e_loc, -1, keepdims=True) + jnp.sum(e_ctx, -1, keepdims=True) + jnp.exp(sink - m))
        o = _dg((e_loc * inv).astype(BF16), vwin[:, cs]) + _dg((e_ctx * inv).astype(BF16), vc[:, cs])
        for g in range(A_G):
            h = kvh * A_G + g
            o_ref[:, h * A_DH:(h + 1) * A_DH] = o[g * A_BLK:(g + 1) * A_BLK, :]


def _rope_tables():
    half = A_DH // 2
    t = np.arange(DEC_T)
    rows = (t // GRID_W).astype(np.float32)
    cols = (t % GRID_W).astype(np.float32)
    inv_freq = (ROPE_BASE ** (-np.arange(0, half, 2, dtype=np.float32) / half)).astype(np.float32)
    ang_r = rows[:, None] * inv_freq[None, :]
    ang_c = cols[:, None] * inv_freq[None, :]
    cos = np.concatenate([np.cos(ang_r), np.cos(ang_r), np.cos(ang_c), np.cos(ang_c)], -1)
    sin = np.concatenate([-np.sin(ang_r), np.sin(ang_r), -np.sin(ang_c), np.sin(ang_c)], -1)
    return (jnp.asarray(np.tile(cos, (1, A_HEADS)), F32), jnp.asarray(np.tile(sin, (1, A_HEADS)), F32))


def _attn_lat_call(sink_l, pm, kc, vc, l, cos, sin):
    nb = DEC_T // A_BLK
    row0 = N_CTX_TOK // A_BLK
    seq0 = N_CTX_TOK // DEC_T
    return pl.pallas_call(
        _attn_lat_kernel,
        grid=(N_DEC_B, nb),
        in_specs=[
            pl.BlockSpec(memory_space=pltpu.SMEM),
            pl.BlockSpec((A_BLK, A_QW), lambda b, n: (row0 + b * nb + n, 0)),
            pl.BlockSpec((DEC_T, A_KVW), lambda b, n: (seq0 + b, A_QW // A_KVW)),
            pl.BlockSpec((DEC_T, A_KVW), lambda b, n: (seq0 + b, A_QW // A_KVW + 1)),
            pl.BlockSpec((None, None, PAST, A_KVW), lambda b, n: (b, l, 0, 0)),
            pl.BlockSpec((None, None, PAST, A_KVW), lambda b, n: (b, l, 0, 0)),
            pl.BlockSpec((DEC_T, A_QW), lambda b, n: (0, 0)),
            pl.BlockSpec((DEC_T, A_QW), lambda b, n: (0, 0)),
        ],
        out_specs=pl.BlockSpec((A_BLK, A_QW), lambda b, n: (b * nb + n, 0)),
        out_shape=jax.ShapeDtypeStruct((N_DEC_B * DEC_T, A_QW), F32),
        scratch_shapes=[pltpu.VMEM((DEC_T, A_KVW), BF16)],
        compiler_params=_cparams(("arbitrary", "arbitrary")),
        name="attn_lat",
    )(sink_l, pm, pm, pm, kc, vc, cos, sin)


def _rwkv_kernel(T, NS, r_ref, k_ref, v_ref, lr_ref, s0_ref, w0_ref, wb_ref, a0_ref, ab_ref, gb_ref,
                 kk_ref, ka_ref, rk_ref, lng_ref, lnb_ref, o_ref, sfin_ref,
                 KK, W, WRP, AKK, KT, VC2, BON, Y, S):
    ones4 = _block_ones(B_W, B_DH)
    decay_c = float(np.exp(-0.5))
    RC = 256
    SUB = 32
    NP = 3

    def prep(c, carry):
        r0 = pl.multiple_of(c * RC, RC)
        rs = pl.ds(r0, RC)
        r = r_ref[rs, :]
        k = k_ref[rs, :]
        v = v_ref[rs, :]
        lr = lr_ref[rs, :]
        kkr = k * kk_ref[...]
        kk = kkr * lax.rsqrt(_mm_xr(kkr * kkr, ones4, 2) + 1e-12)
        KK[rs, :] = kk
        bonus = jnp.zeros((RC, B_W), F32)
        vc2 = jnp.zeros((RC, B_W), F32)
        for d in range(2):
            z = w0_ref[d] + _mm(jnp.tanh(lr[:, B_RANK * d:B_RANK * (d + 1)]), wb_ref[d])
            w = jnp.exp(-decay_c * _sigmoid(z))
            a = _sigmoid(a0_ref[d] + _mm(lr[:, B_RANK * (2 + d):B_RANK * (3 + d)], ab_ref[d]))
            kt = k * (1.0 + (a - 1.0) * ka_ref[...])
            akk = a * kk
            W[d, rs, :] = w
            WRP[d, rs, :] = w * r - _mm_xr(akk * r, ones4, 2) * kk
            AKK[d, rs, :] = akk
            KT[d, rs, :] = kt
            vc2 = vc2 + _mm_xr(kt * r, ones4, 2) * v
            bonus = bonus + _mm_xr(r * kt * rk_ref[...], ones4, 2) * v
        VC2[rs, :] = vc2
        BON[rs, :] = bonus
        return carry

    lax.fori_loop(0, NS * T // RC, prep, 0)

    chains = [(s, d) for s in range(NS) for d in range(2)]
    for s, d in chains:
        S[s, d] = jnp.concatenate([s0_ref[s, d, h] for h in range(B_HEADS)], axis=1)

    eye4 = _iota((B_DH, B_W), 0) == (_iota((B_DH, B_W), 1) % B_DH)

    def steps(i, carry):
        t0s = [pl.multiple_of(s * T + (i * SUB if d == 0 else T - SUB - i * SUB), SUB) for s, d in chains]

        def row(ref, g, j, d=None):
            tile = pl.ds(t0s[g] + (j // SUBLANES) * SUBLANES, SUBLANES)
            vals = ref[tile, :] if d is None else ref[d, tile, :]
            return vals[j % SUBLANES:j % SUBLANES + 1]

        ys = [[None] * SUB for _ in chains]
        for jj in range(SUB):
            lhs = []
            for g, (s, d) in enumerate(chains):
                j = jj if d == 0 else SUB - 1 - jj
                stb = S[s, d].astype(BF16)
                lhs += [stb * row(KK, g, j).astype(BF16), stb * row(WRP, g, j, d).astype(BF16),
                        jnp.where(eye4, row(v_ref, g, j), 0.0).astype(BF16)]
            res = _dg(jnp.concatenate(lhs, axis=0), ones4)
            for g, (s, d) in enumerate(chains):
                j = jj if d == 0 else SUB - 1 - jj
                sk, yp, vcol = [res[(g * NP + n) * B_DH:(g * NP + n + 1) * B_DH] for n in range(NP)]
                S[s, d] = S[s, d] * row(W, g, j, d) - sk * row(AKK, g, j, d) + vcol * row(KT, g, j, d)
                ys[g][j] = jnp.sum(jnp.where(eye4, yp, 0.0), axis=0, keepdims=True)
        for g, (s, d) in enumerate(chains):
            Y[d, pl.ds(t0s[g], SUB), :] = jnp.concatenate(ys[g], axis=0)
        return carry

    lax.fori_loop(0, T // SUB, steps, 0)

    for s, d in chains:
        st = S[s, d]
        for h in range(B_HEADS):
            sfin_ref[s, d, h] = st[:, h * B_DH:(h + 1) * B_DH]

    def post(c, carry):
        r0 = pl.multiple_of(c * RC, RC)
        rs = pl.ds(r0, RC)
        y = Y[0, rs, :] + Y[1, rs, :] + VC2[rs, :]
        mu = _mm_xr(y, ones4, 2) * (1.0 / B_DH)
        yc = y - mu
        var = _mm_xr(yc * yc, ones4, 2) * (1.0 / B_DH)
        yn = yc * lax.rsqrt(var + B_GN_EPS) * lng_ref[...] + lnb_ref[...] + BON[rs, :]
        g = _mm(_sigmoid(lr_ref[rs, 4 * B_RANK:4 * B_RANK + B_GATE_RANK]), gb_ref[...])
        o_ref[rs, :] = yn * g
        return carry

    lax.fori_loop(0, NS * T // RC, post, 0)


def _rwkv_call(T, NS, n_seq, tok0, pm, plr, s0, prm, name):
    rows = NS * T
    blk0 = tok0 // rows
    n_steps = n_seq // NS
    full = lambda shape: pl.BlockSpec(shape, lambda b: (0,) * len(shape))
    big = lambda shape, imap: (pl.BlockSpec(shape, imap, pipeline_mode=pl.Buffered(1)) if n_steps == 1
                               else pl.BlockSpec(shape, imap))
    kern = functools.partial(_rwkv_kernel, T, NS)
    return pl.pallas_call(
        kern,
        grid=(n_steps,),
        in_specs=[
            big((rows, B_W), lambda b: (blk0 + b, COL_B // B_W)),
            big((rows, B_W), lambda b: (blk0 + b, COL_B // B_W + 1)),
            big((rows, B_W), lambda b: (blk0 + b, COL_B // B_W + 2)),
            big((rows, LR_W), lambda b: (blk0 + b, 0)),
            pl.BlockSpec((NS, 2, B_HEADS, B_DH, B_DH), lambda b: (b, 0, 0, 0, 0)),
            full((2, 1, B_W)), full((2, B_RANK, B_W)), full((2, 1, B_W)), full((2, B_RANK, B_W)),
            full((B_GATE_RANK, B_W)),
            full((1, B_W)), full((1, B_W)), full((1, B_W)), full((1, B_W)), full((1, B_W)),
        ],
        out_specs=[
            big((rows, B_W), lambda b: (b, 0)),
            pl.BlockSpec((NS, 2, B_HEADS, B_DH, B_DH), lambda b: (b, 0, 0, 0, 0)),
        ],
        out_shape=[
            jax.ShapeDtypeStruct((n_seq * T, B_W), F32),
            jax.ShapeDtypeStruct((n_seq, 2, B_HEADS, B_DH, B_DH), F32),
        ],
        scratch_shapes=[
            pltpu.VMEM((rows, B_W), F32),
            pltpu.VMEM((2, rows, B_W), F32),
            pltpu.VMEM((2, rows, B_W), F32),
            pltpu.VMEM((2, rows, B_W), F32),
            pltpu.VMEM((2, rows, B_W), F32),
            pltpu.VMEM((rows, B_W), F32),
            pltpu.VMEM((rows, B_W), F32),
            pltpu.VMEM((2, rows, B_W), F32),
            pltpu.VMEM((NS, 2, B_DH, B_W), F32),
        ],
        compiler_params=_cparams(("arbitrary",)),
        name=name,
    )(pm, pm, pm, plr, s0, *prm)


def _gla_kernel(T, NS, q_ref, k_ref, v_ref, og_ref, lr_ref, s0_ref, gb_ref, bias_ref, ng_ref,
                o_ref, sfin_ref, LA, O, S):
    n_chunks = T // C_CHUNK
    nsub = C_CHUNK // C_SUB
    qscale = C_DK ** -0.5
    lr = lr_ref[...]
    for d in range(2):
        gl = _mm(lr, gb_ref[d]) + bias_ref[d]
        LA[d] = (jnp.minimum(gl, 0.0) - jnp.log(1.0 + jnp.exp(-jnp.abs(gl)))) * (1.0 / C_GATE_NORM)
    bd_state = _iota((C_KW, C_VW), 0) // C_DK == _iota((C_KW, C_VW), 1) // C_DV
    chains = [(s, d) for s in range(NS) for d in range(2)]
    for s, d in chains:
        for h in range(C_HEADS):
            pad_l = h * C_DV
            pad_r = C_VW - (h + 1) * C_DV
            blk = s0_ref[s, d, h]
            parts = ([jnp.zeros((C_DK, pad_l), F32)] if pad_l else []) + [blk] + \
                    ([jnp.zeros((C_DK, pad_r), F32)] if pad_r else [])
            S[s, d, h * C_DK:(h + 1) * C_DK, :] = jnp.concatenate(parts, axis=1)

    ti = _iota((C_CHUNK, C_CHUNK), 0)
    si = _iota((C_CHUNK, C_CHUNK), 1)
    tri = ((si <= ti).astype(BF16), (si >= ti).astype(BF16))
    trow = _iota((C_CHUNK, 1), 0)
    mask_k = _iota((C_CHUNK, C_KW), 0) // C_SUB == _iota((C_CHUNK, C_KW), 1) // C_DK
    mask_v = _iota((C_CHUNK, C_VW), 0) // C_SUB == _iota((C_CHUNK, C_VW), 1) // C_DV
    t_att = _iota((C_CHUNK, C_CHUNK), 0)
    s_att = _iota((C_CHUNK, C_CHUNK), 1) % C_SUB
    eye_k = _iota((C_KW, C_KW), 0) == _iota((C_KW, C_KW), 1)

    def body(c, carry):
        cx = []
        for s, d in chains:
            cc = c if d == 0 else n_chunks - 1 - c
            rs = pl.ds(pl.multiple_of(s * T + cc * C_CHUNK, C_CHUNK), C_CHUNK)
            b = _mm_xl(tri[d], LA[d, rs, :], 3)
            cx.append(dict(s=s, d=d, rs=rs, b=b, q=q_ref[rs, :] * qscale, k=k_ref[rs, :], v=v_ref[rs, :]))
        for x in cx:
            x["o"] = _mm(x["q"] * jnp.exp(x["b"]), S[x["s"], x["d"]])
        for j in range(nsub):
            lo, hi = j * C_SUB, (j + 1) * C_SUB
            for x in cx:
                b, q, k = x["b"], x["q"], x["k"]
                if x["d"] == 0:
                    gamma = b[hi - 1:hi, :]
                    row_ok = trow >= lo
                    att_ok = t_att >= lo + s_att
                else:
                    gamma = b[lo:lo + 1, :]
                    row_ok = trow < hi
                    att_ok = t_att <= lo + s_att
                qj = q * jnp.exp(jnp.where(row_ok, b - gamma, NEG_INF))
                kj = k[lo:hi, :] * jnp.exp(gamma - b[lo:hi, :])
                kbd = jnp.where(mask_k, jnp.concatenate([kj] * C_HEADS, axis=0), 0.0)
                x["att"] = jnp.where(att_ok, _mm(qj, kbd, NT), 0.0)
            for x in cx:
                vbd = jnp.where(mask_v, jnp.concatenate([x["v"][lo:hi, :]] * C_HEADS, axis=0), 0.0)
                x["o"] = x["o"] + _mm(x["att"], vbd)
        for x in cx:
            s, d, b = x["s"], x["d"], x["b"]
            O[d, x["rs"], :] = x["o"]
            blast = b[C_CHUNK - 1:C_CHUNK, :] if d == 0 else b[0:1, :]
            kl = x["k"] * jnp.exp(blast - b)
            upd = jnp.where(bd_state, _mm3(kl.T, x["v"]), 0.0)
            dec = jnp.where(eye_k, jnp.exp(blast), 0.0)
            S[s, d] = _mm3(dec, S[s, d]) + upd
        return carry

    lax.fori_loop(0, n_chunks, body, 0)

    for s, d in chains:
        st = S[s, d]
        for h in range(C_HEADS):
            sfin_ref[s, d, h] = st[h * C_DK:(h + 1) * C_DK, h * C_DV:(h + 1) * C_DV]

    ones4 = _block_ones(C_VW, C_DV)
    o = O[0] + O[1]
    ms = _mm_xr(o * o, ones4, 2) * (1.0 / C_DV)
    og = og_ref[...]
    o_ref[...] = o * lax.rsqrt(ms + LN_EPS) * ng_ref[...] * (og * _sigmoid(og))


def _gla_call(T, NS, n_seq, tok0, pm, plr, s0, gb_pad, bias, ng, name):
    rows = NS * T
    blk0 = tok0 // rows
    full = lambda shape: pl.BlockSpec(shape, lambda b: (0,) * len(shape))
    return pl.pallas_call(
        functools.partial(_gla_kernel, T, NS),
        grid=(n_seq // NS,),
        in_specs=[
            pl.BlockSpec((rows, C_KW), lambda b: (blk0 + b, COL_C // C_KW)),
            pl.BlockSpec((rows, C_KW), lambda b: (blk0 + b, COL_C // C_KW + 1)),
            pl.BlockSpec((rows, C_VW), lambda b: (blk0 + b, COL_CV // C_VW)),
            pl.BlockSpec((rows, C_VW), lambda b: (blk0 + b, COL_CV // C_VW + 1)),
            pl.BlockSpec((rows, LANES), lambda b: (blk0 + b, C_LR_COL // LANES)),
            pl.BlockSpec((NS, 2, C_HEADS, C_DK, C_DV), lambda b: (b, 0, 0, 0, 0)),
            full((2, LANES, C_KW)), full((2, 1, C_KW)), full((1, C_VW)),
        ],
        out_specs=[
            pl.BlockSpec((rows, C_VW), lambda b: (b, 0)),
            pl.BlockSpec((NS, 2, C_HEADS, C_DK, C_DV), lambda b: (b, 0, 0, 0, 0)),
        ],
        out_shape=[
            jax.ShapeDtypeStruct((n_seq * T, C_VW), F32),
            jax.ShapeDtypeStruct((n_seq, 2, C_HEADS, C_DK, C_DV), F32),
        ],
        scratch_shapes=[
            pltpu.VMEM((2, rows, C_KW), F32),
            pltpu.VMEM((2, rows, C_VW), F32),
            pltpu.VMEM((NS, 2, C_KW, C_VW), F32),
        ],
        compiler_params=_cparams(("arbitrary",)),
        name=name,
    )(pm, pm, pm, pm, plr, s0, gb_pad, bias, ng)


MERGE_TM = 512


def _merge_kernel(n_x, *refs):
    x_refs = refs[:n_x]
    (g_ref, oac_ref, oad_ref, obc_ref, obd_ref, occ_ref, ocd_ref, mod_ref,
     wa_ref, wb_ref, wc_ref, wo_ref, lg_ref, lb_ref, o_ref) = refs[n_x:]
    is_ctx = pl.program_id(0) < N_CTX_TOK // MERGE_TM
    oa = jnp.where(is_ctx, oac_ref[...], oad_ref[...]).astype(BF16)
    ob = jnp.where(is_ctx, obc_ref[...], obd_ref[...]).astype(BF16)
    oc = jnp.where(is_ctx, occ_ref[...], ocd_ref[...]).astype(BF16)
    merged = (g_ref[:, 0:D].astype(F32) * _dg(oa, wa_ref[...])
              + g_ref[:, D:2 * D].astype(F32) * _dg(ob, wb_ref[...])
              + g_ref[:, 2 * D:3 * D].astype(F32) * _dg(oc, wc_ref[...]))
    mix = _dg(merged.astype(BF16), wo_ref[...])
    y = DN_ALPHA * _read_x(x_refs, MERGE_TM) + mod_ref[2:3, :] * mix
    o_ref[...] = _layer_norm(y, lg_ref[...], lb_ref[...])


def _merge_call(xs, gates, oa, ob, oc, mod_l, wa, wb, wc, wo, lg, lb):
    tm = MERGE_TM
    x_specs = [pl.BlockSpec((tm, D), lambda i: (i, 0))] if len(xs) == 1 else _pair_specs(tm, D)
    return pl.pallas_call(
        functools.partial(_merge_kernel, len(xs)),
        grid=(N_TOK // tm,),
        in_specs=[
            *x_specs,
            pl.BlockSpec((tm, IN_GATE), lambda i: (i, 0)),
            *_pair_specs(tm, A_QW), *_pair_specs(tm, B_W), *_pair_specs(tm, C_VW),
            pl.BlockSpec((None, 6, D), lambda i: (_group_of_tile(i, tm), 0, 0)),
            _resident((A_QW, D)), _resident((B_W, D)), _resident((C_VW, D)), _resident((D, D)),
            _resident((1, D)), _resident((1, D)),
        ],
        out_specs=pl.BlockSpec((tm, D), lambda i: (i, 0)),
        out_shape=jax.ShapeDtypeStruct((N_TOK, D), F32),
        compiler_params=_cparams(("arbitrary",)),
        name="merge",
    )(*xs, gates, *oa, *ob, *oc, mod_l, wa, wb, wc, wo, lg.reshape(1, D), lb.reshape(1, D))


def _ffn_kernel(x_ref, mod_ref, wg_ref, wu_ref, wd_ref, lg_ref, lb_ref, o_ref):
    x = x_ref[...]
    h = (x * (1.0 + mod_ref[4:5, :]) + mod_ref[3:4, :]).astype(BF16)
    gate = _dg(h, wg_ref[...])
    up = _dg(h, wu_ref[...])
    f = _dg((gate * _sigmoid(gate) * up).astype(BF16), wd_ref[...])
    y = DN_ALPHA * x + mod_ref[5:6, :] * f
    o_ref[...] = _layer_norm(y, lg_ref[...], lb_ref[...])


def _ffn_call(x, mod_l, wg, wu, wd, lg, lb):
    tm = 512
    return pl.pallas_call(
        _ffn_kernel,
        grid=(N_TOK // tm,),
        in_specs=[
            pl.BlockSpec((tm, D), lambda i: (i, 0)),
            pl.BlockSpec((None, 6, D), lambda i: (_group_of_tile(i, tm), 0, 0)),
            _resident((D, D_FF)), _resident((D, D_FF)), _resident((D_FF, D)),
            _resident((1, D)), _resident((1, D)),
        ],
        out_specs=pl.BlockSpec((tm, D), lambda i: (i, 0)),
        out_shape=jax.ShapeDtypeStruct((N_TOK, D), F32),
        compiler_params=_cparams(("arbitrary",)),
        name="ffn",
    )(x, mod_l, wg, wu, wd, lg.reshape(1, D), lb.reshape(1, D))


MOE_TM = 1024
MOE_TR = 256
MOE_RS = 3072
MOE_TF = 512
MOE_ROWS = 2 * N_TOK + N_EXP * MOE_TR
MOE_NST = -(-MOE_ROWS // MOE_RS) + N_EXP
R_I1, R_I2, R_W1, R_W2, R_RANK1, R_RANK2 = range(6)


def _moe_route_kernel(x_ref, mod_ref, wr_ref, h_ref, info_ref, cnt_ref, carry_s):
    tm = MOE_TM

    @pl.when(pl.program_id(0) == 0)
    def _():
        carry_s[...] = jnp.zeros_like(carry_s)

    h = x_ref[...] * (1.0 + mod_ref[4:5, :]) + mod_ref[3:4, :]
    h_ref[...] = h
    logits = _mm3(h, wr_ref[...])
    lane = _iota(logits.shape, 1)
    logits = jnp.where(lane < N_EXP, logits, NEG_INF)
    v1 = jnp.max(logits, -1, keepdims=True)
    i1 = jnp.min(jnp.where(logits == v1, lane, LANES), -1, keepdims=True)
    rest = jnp.where(lane == i1, NEG_INF, logits)
    v2 = jnp.max(rest, -1, keepdims=True)
    i2 = jnp.min(jnp.where(rest == v2, lane, LANES), -1, keepdims=True)
    e2 = jnp.exp(v2 - v1)
    w1 = 1.0 / (1.0 + e2)
    w2 = e2 / (1.0 + e2)
    oh1 = lane == i1
    oh2 = lane == i2
    cnt = oh1.astype(F32) + oh2.astype(F32)
    earlier = (_iota((tm, tm), 1) < _iota((tm, tm), 0)).astype(BF16)
    before = _dg(earlier, cnt.astype(BF16)) + carry_s[...]
    rank1 = jnp.sum(jnp.where(oh1, before, 0.0), -1, keepdims=True)
    rank2 = jnp.sum(jnp.where(oh2, before, 0.0), -1, keepdims=True)
    info = jnp.zeros(logits.shape, F32)
    for ln, val in ((R_I1, i1.astype(F32)), (R_I2, i2.astype(F32)), (R_W1, w1), (R_W2, w2),
                    (R_RANK1, rank1), (R_RANK2, rank2)):
        info = jnp.where(lane == ln, val, info)
    info_ref[...] = info
    carry_s[...] += jnp.sum(cnt, axis=0, keepdims=True)
    cnt_ref[...] = carry_s[...]


def _moe_route_call(x, mod_l, wr_pad, i_moe):
    tm = MOE_TM
    return pl.pallas_call(
        _moe_route_kernel,
        grid=(N_TOK // tm,),
        in_specs=[
            pl.BlockSpec((tm, D), lambda i: (i, 0)),
            pl.BlockSpec((None, 6, D), lambda i: (_group_of_tile(i, tm), 0, 0)),
            pl.BlockSpec((None, D, LANES), lambda i: (i_moe, 0, 0)),
        ],
        out_specs=[
            pl.BlockSpec((tm, D), lambda i: (i, 0)),
            pl.BlockSpec((tm, LANES), lambda i: (i, 0)),
            pl.BlockSpec((1, LANES), lambda i: (0, 0)),
        ],
        out_shape=[
            jax.ShapeDtypeStruct((N_TOK, D), F32),
            jax.ShapeDtypeStruct((N_TOK, LANES), F32),
            jax.ShapeDtypeStruct((1, LANES), F32),
        ],
        scratch_shapes=[pltpu.VMEM((1, LANES), F32)],
        compiler_params=_cparams(("arbitrary",)),
        name="moe_route",
    )(x, mod_l, wr_pad)


def _moe_plan(info, cnt):
    i32 = jnp.int32
    i1 = info[:, R_I1].astype(i32)
    i2 = info[:, R_I2].astype(i32)
    counts = cnt[0, :N_EXP].astype(i32)
    padded = (counts + MOE_TR - 1) // MOE_TR * MOE_TR
    seg_start = jnp.cumsum(padded) - padded
    pos1 = seg_start[i1] + info[:, R_RANK1].astype(i32)
    pos2 = seg_start[i2] + info[:, R_RANK2].astype(i32)
    dst = jnp.zeros((MOE_ROWS,), i32).at[jnp.concatenate([pos1, pos2])].set(jnp.arange(2 * N_TOK, dtype=i32))
    n_pass = (padded + MOE_RS - 1) // MOE_RS
    pass_end = jnp.cumsum(n_pass)
    total = pass_end[-1]
    sidx = jnp.arange(MOE_NST, dtype=i32)
    used = sidx < total
    e_of = jnp.minimum(jnp.searchsorted(pass_end, jnp.minimum(sidx, total - 1), side="right"), N_EXP - 1).astype(i32)
    k = jnp.minimum(sidx, total - 1) - (pass_end - n_pass)[e_of]
    row0 = seg_start[e_of] + k * MOE_RS
    nrows = jnp.where(used, jnp.clip(padded[e_of] - k * MOE_RS, 0, MOE_RS), 0)
    nvalid = jnp.where(used, jnp.clip(counts[e_of] - k * MOE_RS, 0, MOE_RS), 0)
    return dst, e_of, row0.astype(i32), nrows.astype(i32), nvalid.astype(i32)


def _moe_expert_kernel(dst_ref, exp_ref, row0_ref, nrows_ref, nvalid_ref,
                       h_hbm, wg_ref, wu_ref, wd_ref, yo_hbm, xs, xb, acc, wgb, wub, wdb, gsem, ssem):
    s = pl.program_id(0)
    f = pl.program_id(1)
    nf = pl.num_programs(1)
    nrows = pl.multiple_of(nrows_ref[s], MOE_TR)
    nvalid = nvalid_ref[s]
    row0 = row0_ref[s]
    n_chunks = nrows // MOE_TR

    def row_copy(src, dst, sem):
        return pltpu.make_async_copy(src, dst, sem)

    def hbm_row(ref, i):
        return ref.at[pl.ds(i, 1), :]

    def wait_rows(buf, n_groups, sem):
        pltpu.make_async_copy(buf.at[pl.ds(0, n_groups)], buf.at[pl.ds(0, n_groups)], sem).wait()

    @pl.when((f == 0) & (nrows > 0))
    def _gather():
        def issue(g, carry):
            for u in range(SUBLANES):
                d = dst_ref[row0 + g * SUBLANES + u]
                tok = jnp.where(d >= N_TOK, d - N_TOK, d)
                row_copy(hbm_row(h_hbm, tok), xs.at[g, pl.ds(u, 1), :], gsem).start()
            return carry

        ng = nrows // SUBLANES
        lax.fori_loop(0, ng, issue, 0)
        wait_rows(xs, ng, gsem)

        def cvt(c, carry):
            g0 = pl.multiple_of(c * (MOE_TR // SUBLANES), MOE_TR // SUBLANES)
            rs = pl.ds(pl.multiple_of(c * MOE_TR, MOE_TR), MOE_TR)
            xb[rs, :] = xs[pl.ds(g0, MOE_TR // SUBLANES)].reshape(MOE_TR, D).astype(BF16)
            return carry

        lax.fori_loop(0, n_chunks, cvt, 0)

    @pl.when(nrows > 0)
    def _compute():
        wgb[...] = wg_ref[...].astype(BF16)
        wub[...] = wu_ref[...].astype(BF16)
        wdb[...] = wd_ref[...].astype(BF16)

        def chunk(start, n):
            x = xb[pl.ds(pl.multiple_of(start, MOE_TR), n), :]
            gate = _dg(x, wgb[...])
            up = _dg(x, wub[...])
            y = _dg((gate * _sigmoid(gate) * up).astype(BF16), wdb[...]).reshape(n // SUBLANES, SUBLANES, D)
            gs = pl.ds(pl.multiple_of(start // SUBLANES, MOE_TR // SUBLANES), n // SUBLANES)

            @pl.when(f == 0)
            def _():
                acc[gs] = y

            @pl.when(f > 0)
            def _():
                acc[gs] += y

        def chunk_quad(c4, carry):
            chunk(c4 * (4 * MOE_TR), 4 * MOE_TR)
            return carry

        lax.fori_loop(0, n_chunks // 4, chunk_quad, 0)
        done = n_chunks // 4 * 4

        @pl.when(n_chunks % 4 >= 2)
        def _():
            chunk(done * MOE_TR, 2 * MOE_TR)

        @pl.when(n_chunks % 2 == 1)
        def _():
            chunk((n_chunks - 1) * MOE_TR, MOE_TR)

    @pl.when((f == nf - 1) & (nvalid > 0))
    def _scatter():
        n8 = nvalid // SUBLANES

        def issue(g, carry):
            for u in range(SUBLANES):
                d = dst_ref[row0 + g * SUBLANES + u]
                row_copy(acc.at[g, pl.ds(u, 1), :], hbm_row(yo_hbm, d), ssem).start()
            return carry

        def issue_tail(r, carry):
            d = dst_ref[row0 + r]
            row_copy(acc.at[n8, pl.ds(r - n8 * SUBLANES, 1), :], hbm_row(yo_hbm, d), ssem).start()
            return carry

        lax.fori_loop(0, n8, issue, 0)
        lax.fori_loop(n8 * SUBLANES, nvalid, issue_tail, 0)

        @pl.when(n8 > 0)
        def _():
            wait_rows(acc, n8, ssem)

        def wait_one(r, carry):
            row_copy(acc.at[0, pl.ds(0, 1), :], hbm_row(yo_hbm, 0), ssem).wait()
            return carry

        lax.fori_loop(n8 * SUBLANES, nvalid, wait_one, 0)


def _moe_expert_call(h, plan, wg, wu, wd, i_moe):
    nf = D_FFE // MOE_TF

    def wspec(shape, fdim):
        def imap(s, f, dst, exp, row0, nrows, nvalid):
            fe = jnp.where(nrows[s] > 0, f, nf - 1)
            return (i_moe, exp[s], 0, fe) if fdim == 3 else (i_moe, exp[s], fe, 0)
        return pl.BlockSpec(shape, imap)

    grid_spec = pltpu.PrefetchScalarGridSpec(
        num_scalar_prefetch=5,
        grid=(MOE_NST, nf),
        in_specs=[
            pl.BlockSpec(memory_space=pl.ANY),
            wspec((None, None, D, MOE_TF), 3),
            wspec((None, None, D, MOE_TF), 3),
            wspec((None, None, MOE_TF, D), 2),
        ],
        out_specs=pl.BlockSpec(memory_space=pl.ANY),
        scratch_shapes=[
            pltpu.VMEM((MOE_RS // SUBLANES, SUBLANES, D), F32),
            pltpu.VMEM((MOE_RS, D), BF16),
            pltpu.VMEM((MOE_RS // SUBLANES, SUBLANES, D), F32),
            pltpu.VMEM((D, MOE_TF), BF16), pltpu.VMEM((D, MOE_TF), BF16), pltpu.VMEM((MOE_TF, D), BF16),
            pltpu.SemaphoreType.DMA(()), pltpu.SemaphoreType.DMA(()),
        ],
    )
    return pl.pallas_call(
        _moe_expert_kernel,
        grid_spec=grid_spec,
        out_shape=jax.ShapeDtypeStruct((2 * N_TOK, D), F32),
        compiler_params=pltpu.CompilerParams(dimension_semantics=("arbitrary", "arbitrary"),
                                             vmem_limit_bytes=VMEM_LIMIT, disable_bounds_checks=True),
        name="moe_experts",
    )(*plan, h, wg, wu, wd)


def _moe_combine_kernel(x_ref, y1_ref, y2_ref, info_ref, mod_ref, lg_ref, lb_ref, oc_ref, od_ref):
    f = info_ref[:, R_W1:R_W1 + 1] * y1_ref[...] + info_ref[:, R_W2:R_W2 + 1] * y2_ref[...]
    y = DN_ALPHA * x_ref[...] + mod_ref[5:6, :] * f
    out = _layer_norm(y, lg_ref[...], lb_ref[...])
    is_ctx = pl.program_id(0) < N_CTX_TOK // MOE_TM

    @pl.when(is_ctx)
    def _():
        oc_ref[...] = out

    @pl.when(jnp.logical_not(is_ctx))
    def _():
        od_ref[...] = out


def _moe_combine_call(x, yo, info, mod_l, lg, lb):
    tm = MOE_TM
    nt = N_TOK // tm
    return pl.pallas_call(
        _moe_combine_kernel,
        grid=(nt,),
        in_specs=[
            pl.BlockSpec((tm, D), lambda i: (i, 0)),
            pl.BlockSpec((tm, D), lambda i: (i, 0)),
            pl.BlockSpec((tm, D), lambda i: (nt + i, 0)),
            pl.BlockSpec((tm, LANES), lambda i: (i, 0)),
            pl.BlockSpec((None, 6, D), lambda i: (_group_of_tile(i, tm), 0, 0)),
            _resident((1, D)), _resident((1, D)),
        ],
        out_specs=_pair_specs(tm, D),
        out_shape=[jax.ShapeDtypeStruct((N_CTX_TOK, D), F32), jax.ShapeDtypeStruct((N_DEC_B * DEC_T, D), F32)],
        compiler_params=_cparams(("arbitrary",)),
        name="moe_combine",
    )(x, yo, yo, info, mod_l, lg.reshape(1, D), lb.reshape(1, D))


def _moe_call(x, mod_l, wr_pad, wg, wu, wd, lg, lb, i_moe):
    h, info, cnt = _moe_route_call(x, mod_l, wr_pad, i_moe)
    yo = _moe_expert_call(h, _moe_plan(info, cnt), wg, wu, wd, i_moe)
    return _moe_combine_call(x, yo, info, mod_l, lg, lb)


def kernel(x_prompt, x_sample, cache_attn_k, cache_attn_v, state_rwkv, state_gla, c, c_ctx, w_ada, b_ada, w_in,
           attn_sink, rwkv_w0, rwkv_w_a, rwkv_w_b, rwkv_a0, rwkv_a_a, rwkv_a_b, rwkv_g_a, rwkv_g_b, rwkv_k_k,
           rwkv_k_a, rwkv_r_k, rwkv_ln_g, rwkv_ln_b, gla_gate_a, gla_gate_b, gla_gate_bias, gla_norm_g, w_up_a,
           w_up_b, w_up_c, w_out, ln1_g, ln1_b, ln2_g, ln2_b, ffn_w_gate, ffn_w_up, ffn_w_down, moe_router,
           moe_w_gate, moe_w_up, moe_w_down):
    cvec = jnp.concatenate([c_ctx[None, :], c, jnp.zeros((N_GROUPS - 1 - N_DEC_B, D), F32)], axis=0)
    mods = _ada_call(cvec, w_ada, b_ada).reshape(DEPTH, N_GROUPS, 6, D)
    xs = (x_prompt.reshape(N_CTX_TOK, D), x_sample.reshape(N_DEC_B * DEC_T, D))
    cos, sin = _rope_tables()
    kc_all = cache_attn_k.reshape(N_DEC_B, DEPTH, PAST, A_KV * A_DH)
    vc_all = cache_attn_v.reshape(N_DEC_B, DEPTH, PAST, A_KV * A_DH)
    zeros_r = jnp.zeros((N_CTX_B, 2, B_HEADS, B_DH, B_DH), F32)
    zeros_g = jnp.zeros((N_CTX_B, 2, C_HEADS, C_DK, C_DV), F32)

    new_k, new_v, new_sr, new_sg = [], [], [], []
    for l in range(DEPTH):
        mod_l = mods[l]
        w_lr = jnp.concatenate(
            [rwkv_w_a[l, 0], rwkv_w_a[l, 1], rwkv_a_a[l, 0], rwkv_a_a[l, 1], rwkv_g_a[l],
             gla_gate_a[l, 0], gla_gate_a[l, 1], jnp.zeros((D, LR_W - C_LR_COL - 2 * C_GATE_RANK), F32)], axis=1).astype(BF16)
        pm, gates, plr, k_ctx, v_ctx = _inproj_call(xs, mod_l, w_in[l, :, :IN_MAIN].astype(BF16),
                                                    w_in[l, :, IN_MAIN:].astype(BF16), w_lr)

        sink_l = attn_sink[l]
        oa = (_attn_ctx_call(sink_l, pm), _attn_lat_call(sink_l, pm, kc_all, vc_all, l, cos, sin))

        rprm = (rwkv_w0[l].reshape(2, 1, B_W), rwkv_w_b[l], rwkv_a0[l].reshape(2, 1, B_W), rwkv_a_b[l],
                rwkv_g_b[l], rwkv_k_k[l].reshape(1, B_W), rwkv_k_a[l].reshape(1, B_W),
                rwkv_r_k[l].reshape(1, B_W), rwkv_ln_g[l].reshape(1, B_W), rwkv_ln_b[l].reshape(1, B_W))
        ob_c, sr_c = _rwkv_call(CTX_T, 4, N_CTX_B, 0, pm, plr, zeros_r, rprm, "rwkv_ctx")
        ob_d, _ = _rwkv_call(DEC_T, 2, N_DEC_B, N_CTX_TOK, pm, plr, state_rwkv[:, l], rprm, "rwkv_lat")

        gb_pad = jnp.zeros((2, LANES, C_KW), F32)
        gb_pad = gb_pad.at[0, 0:C_GATE_RANK].set(gla_gate_b[l, 0]).at[1, C_GATE_RANK:2 * C_GATE_RANK].set(
            gla_gate_b[l, 1])
        gbias = gla_gate_bias[l].reshape(2, 1, C_KW)
        ng = jnp.tile(gla_norm_g[l], C_HEADS).reshape(1, C_VW)
        oc_c, sg_c = _gla_call(CTX_T, 4, N_CTX_B, 0, pm, plr, zeros_g, gb_pad, gbias, ng, "gla_ctx")
        oc_d, _ = _gla_call(DEC_T, 2, N_DEC_B, N_CTX_TOK, pm, plr, state_gla[:, l], gb_pad, gbias, ng, "gla_lat")

        x = _merge_call(xs, gates, oa, (ob_c, ob_d), (oc_c, oc_d), mod_l, w_up_a[l].astype(BF16),
                        w_up_b[l].astype(BF16), w_up_c[l].astype(BF16), w_out[l].astype(BF16), ln1_g[l], ln1_b[l])
        if l % 2 == 0:
            i_ffn = l // 2
            xs = (_ffn_call(x, mod_l, ffn_w_gate[i_ffn].astype(BF16), ffn_w_up[i_ffn].astype(BF16),
                            ffn_w_down[i_ffn].astype(BF16), ln2_g[l], ln2_b[l]),)
        else:
            wr_pad = jnp.concatenate([moe_router, jnp.zeros((moe_router.shape[0], D, LANES - N_EXP), F32)], axis=2)
            xs = _moe_call(x, mod_l, wr_pad, moe_w_gate, moe_w_up, moe_w_down, ln2_g[l], ln2_b[l], l // 2)

        new_k.append(k_ctx.reshape(N_CTX_B, CTX_T, A_KV, A_DH))
        new_v.append(v_ctx.reshape(N_CTX_B, CTX_T, A_KV, A_DH))
        new_sr.append(sr_c)
        new_sg.append(sg_c)

    if len(xs) == 1:
        xs = (xs[0][:N_CTX_TOK], xs[0][N_CTX_TOK:])
    y_prompt = xs[0].reshape(N_CTX_B, CTX_T, D)
    y_sample = xs[1].reshape(N_DEC_B, DEC_T, D)
    return (y_prompt, y_sample, jnp.stack(new_k, axis=1), jnp.stack(new_v, axis=1),
            jnp.stack(new_sr, axis=1), jnp.stack(new_sg, axis=1))
```

```python
import functools

import numpy as np
import jax
import jax.numpy as jnp
from jax import lax
from jax.experimental import pallas as pl
from jax.experimental.pallas import tpu as pltpu

D = 1024
N_CTX_B, CTX_T = 32, 256
N_DEC_B, DEC_T = 2, 1024
N_CTX_TOK = N_CTX_B * CTX_T
N_TOK = N_CTX_TOK + N_DEC_B * DEC_T
DEPTH = 2
PAST = 256
GRID_W = 64
A_HEADS, A_KV, A_DH = 8, 2, 64
A_G = A_HEADS // A_KV
A_WIN, A_BLK = 128, 128
ROPE_BASE = 10000.0
B_HEADS, B_DH = 4, 64
B_W = B_HEADS * B_DH
B_GN_EPS = 64e-5
C_HEADS, C_DK, C_DV = 4, 32, 64
C_KW, C_VW = C_HEADS * C_DK, C_HEADS * C_DV
C_GATE_RANK = 16
C_GATE_NORM = 16.0
C_CHUNK = 64
C_SUB = 16
D_FF = 2816
N_EXP = 8
D_FFE = 3584
LN_EPS = 1e-5
DN_ALPHA = (2.0 * DEPTH) ** 0.25
NEG_INF = -1e30
A_QW, A_KVW = A_HEADS * A_DH, A_KV * A_DH
B_RANK, B_GATE_RANK = 64, 128
LANES = 128
IN_MAIN = 2304
COL_B = A_QW + 2 * A_KVW
COL_C = COL_B + 3 * B_W
COL_CV = COL_C + 2 * C_KW
IN_GATE = 3072
LR_W = 512
C_LR_COL = 4 * B_RANK + B_GATE_RANK
N_GROUPS = 8

F32 = jnp.float32
BF16 = jnp.bfloat16
VMEM_LIMIT = 56 * 1024 * 1024
SUBLANES = 8

NN = ((1,), (0,))
NT = ((1,), (1,))
TN = ((0,), (0,))


def _dg(a, b, dims=NN):
    return lax.dot_general(a, b, (dims, ((), ())), preferred_element_type=F32)


def _split2(x):
    hi = x.astype(BF16)
    lo = (x - hi.astype(F32)).astype(BF16)
    return hi, lo


def _split3(x):
    hi = x.astype(BF16)
    r = x - hi.astype(F32)
    mid = r.astype(BF16)
    lo = (r - mid.astype(F32)).astype(BF16)
    return hi, mid, lo


def _mm(a, b, dims=NN):
    return _dg(a.astype(BF16), b.astype(BF16), dims)


def _mm3(a, b, dims=NN):
    ah, al = _split2(a)
    bh, bl = _split2(b)
    return _dg(ah, bh, dims) + (_dg(ah, bl, dims) + _dg(al, bh, dims))


def _mm_xr(a, b_exact, passes, dims=NN):
    parts = (a.astype(BF16),) if passes == 1 else (_split2(a) if passes == 2 else _split3(a))
    out = _dg(parts[0], b_exact, dims)
    for p in parts[1:]:
        out = out + _dg(p, b_exact, dims)
    return out


def _mm_xl(a_exact, b, passes):
    parts = (b.astype(BF16),) if passes == 1 else (_split2(b) if passes == 2 else _split3(b))
    out = _dg(a_exact, parts[0])
    for p in parts[1:]:
        out = out + _dg(a_exact, p)
    return out


def _sigmoid(x):
    return 1.0 / (1.0 + jnp.exp(-x))


def _iota(shape, dim):
    return lax.broadcasted_iota(jnp.int32, shape, dim)


def _block_ones(n, blk):
    return (_iota((n, n), 0) // blk == _iota((n, n), 1) // blk).astype(BF16)


def _layer_norm(y, g, b):
    mu = jnp.mean(y, -1, keepdims=True)
    yc = y - mu
    var = jnp.mean(yc * yc, -1, keepdims=True)
    return yc * lax.rsqrt(var + LN_EPS) * g + b


def _group_of_tile(i, tm):
    n_ctx = N_CTX_TOK // tm
    per_dec = DEC_T // tm
    return jnp.where(i < n_ctx, 0, 1 + (i - n_ctx) // per_dec)


def _cparams(sem):
    return pltpu.CompilerParams(dimension_semantics=sem, vmem_limit_bytes=VMEM_LIMIT)


def _ada_kernel(c_ref, w_ref, b_ref, o_ref):
    c = c_ref[...]
    s = c * _sigmoid(c)
    o_ref[...] = _mm3(s, w_ref[...]) + b_ref[...]


def _ada_call(cvec, w_ada, b_ada):
    tn = 1536
    return pl.pallas_call(
        _ada_kernel,
        grid=(DEPTH, 6 * D // tn),
        in_specs=[
            pl.BlockSpec((N_GROUPS, D), lambda l, j: (0, 0)),
            pl.BlockSpec((None, D, tn), lambda l, j: (l, 0, j)),
            pl.BlockSpec((None, 1, tn), lambda l, j: (l, 0, j)),
        ],
        out_specs=pl.BlockSpec((None, N_GROUPS, tn), lambda l, j: (l, 0, j)),
        out_shape=jax.ShapeDtypeStruct((DEPTH, N_GROUPS, 6 * D), F32),
        compiler_params=_cparams(("arbitrary", "arbitrary")),
        name="ada",
    )(cvec, w_ada, b_ada.reshape(DEPTH, 1, 6 * D))


def _pair_specs(tm, width):
    n_ctx = N_CTX_TOK // tm
    return [pl.BlockSpec((tm, width), lambda i: (jnp.minimum(i, n_ctx - 1), 0)),
            pl.BlockSpec((tm, width), lambda i: (jnp.maximum(i - n_ctx, 0), 0))]


def _read_x(x_refs, tm):
    if len(x_refs) == 1:
        return x_refs[0][...]
    return jnp.where(pl.program_id(0) < N_CTX_TOK // tm, x_refs[0][...], x_refs[1][...])


INPROJ_TM = 256


W_IN_TILE = 768
N_MAIN_TILES, N_GATE_TILES = IN_MAIN // W_IN_TILE, IN_GATE // W_IN_TILE


def _inproj_kernel(n_x, *refs):
    n_w = N_MAIN_TILES + N_GATE_TILES
    x_refs = refs[:n_x]
    mod_ref = refs[n_x]
    w_refs = refs[n_x + 1:n_x + 1 + n_w]
    wl_ref, om_ref, og_ref, ol_ref, ok_ref, ov_ref, wm_s, wg_s = refs[n_x + 1 + n_w:]

    @pl.when(pl.program_id(0) == 0)
    def _():
        for j, w_ref in enumerate(w_refs):
            dst, jj = (wm_s, j) if j < N_MAIN_TILES else (wg_s, j - N_MAIN_TILES)
            dst[:, jj * W_IN_TILE:(jj + 1) * W_IN_TILE] = w_ref[...].astype(BF16)

    sh = mod_ref[0:1, :]
    sc = mod_ref[1:2, :]
    h = (_read_x(x_refs, INPROJ_TM) * (1.0 + sc) + sh).astype(BF16)
    main = _dg(h, wm_s[...])
    om_ref[...] = main
    og_ref[...] = _sigmoid(_dg(h, wg_s[...])).astype(BF16)
    ol_ref[...] = _dg(h, wl_ref[...])

    @pl.when(pl.program_id(0) < N_CTX_TOK // INPROJ_TM)
    def _():
        ok_ref[...] = main[:, A_QW:A_QW + A_KVW]
        ov_ref[...] = main[:, A_QW + A_KVW:A_QW + 2 * A_KVW]


def _resident(shape):
    return pl.BlockSpec(shape, lambda *_: (0,) * len(shape), pipeline_mode=pl.Buffered(1))


def _layer_resident(shape, l):
    return pl.BlockSpec((None, *shape), lambda *_: (l,) + (0,) * len(shape), pipeline_mode=pl.Buffered(1))


def _inproj_call(xs, mod_l, w_in, w_lr, l):
    tm = INPROJ_TM
    n_ctx = N_CTX_TOK // tm
    x_specs = [pl.BlockSpec((tm, D), lambda i: (i, 0))] if len(xs) == 1 else _pair_specs(tm, D)
    kv_spec = pl.BlockSpec((tm, A_KVW), lambda i: (jnp.minimum(i, n_ctx - 1), 0))
    n_w = N_MAIN_TILES + N_GATE_TILES
    w_specs = [pl.BlockSpec((None, D, W_IN_TILE), lambda i, j=j: (l, 0, j), pipeline_mode=pl.Buffered(1))
               for j in range(n_w)]
    return pl.pallas_call(
        functools.partial(_inproj_kernel, len(xs)),
        grid=(N_TOK // tm,),
        in_specs=[
            *x_specs,
            pl.BlockSpec((None, 6, D), lambda i: (_group_of_tile(i, tm), 0, 0)),
            *w_specs,
            _resident((D, LR_W)),
        ],
        out_specs=[
            pl.BlockSpec((tm, IN_MAIN), lambda i: (i, 0)),
            pl.BlockSpec((tm, IN_GATE), lambda i: (i, 0)),
            pl.BlockSpec((tm, LR_W), lambda i: (i, 0)),
            kv_spec, kv_spec,
        ],
        out_shape=[
            jax.ShapeDtypeStruct((N_TOK, IN_MAIN), F32),
            jax.ShapeDtypeStruct((N_TOK, IN_GATE), BF16),
            jax.ShapeDtypeStruct((N_TOK, LR_W), F32),
            jax.ShapeDtypeStruct((N_CTX_TOK, A_KVW), F32),
            jax.ShapeDtypeStruct((N_CTX_TOK, A_KVW), F32),
        ],
        scratch_shapes=[pltpu.VMEM((D, IN_MAIN), BF16), pltpu.VMEM((D, IN_GATE), BF16)],
        compiler_params=_cparams(("arbitrary",)),
        name="inproj",
    )(*xs, mod_l, *([w_in] * n_w), w_lr)


def _sink_col(sink_ref, kvh, rows_per_head):
    n = A_G * rows_per_head
    r = _iota((n, 1), 0) // rows_per_head
    col = jnp.full((n, 1), sink_ref[kvh * A_G], F32)
    for g in range(1, A_G):
        col = jnp.where(r == g, sink_ref[kvh * A_G + g], col)
    return col


def _attn_ctx_kernel(sink_ref, q_ref, k_ref, v_ref, o_ref):
    scale = A_DH ** -0.5
    for kvh in range(A_KV):
        ks = k_ref[:, kvh * A_DH:(kvh + 1) * A_DH].astype(BF16)
        vs = v_ref[:, kvh * A_DH:(kvh + 1) * A_DH].astype(BF16)
        q4 = jnp.concatenate(
            [q_ref[:, (kvh * A_G + g) * A_DH:(kvh * A_G + g + 1) * A_DH] for g in range(A_G)], axis=0)
        s = _dg(q4.astype(BF16), ks, NT) * scale
        sink = _sink_col(sink_ref, kvh, CTX_T)
        m = jnp.maximum(jnp.max(s, -1, keepdims=True), sink)
        e = jnp.exp(s - m)
        p = e / (jnp.sum(e, -1, keepdims=True) + jnp.exp(sink - m))
        o = _dg(p.astype(BF16), vs)
        for g in range(A_G):
            h = kvh * A_G + g
            o_ref[:, h * A_DH:(h + 1) * A_DH] = o[g * CTX_T:(g + 1) * CTX_T, :]


def _attn_ctx_call(sink_l, pm):
    return pl.pallas_call(
        _attn_ctx_kernel,
        grid=(N_CTX_B,),
        in_specs=[
            pl.BlockSpec(memory_space=pltpu.SMEM),
            pl.BlockSpec((CTX_T, A_QW), lambda b: (b, 0)),
            pl.BlockSpec((CTX_T, A_KVW), lambda b: (b, A_QW // A_KVW)),
            pl.BlockSpec((CTX_T, A_KVW), lambda b: (b, A_QW // A_KVW + 1)),
        ],
        out_specs=pl.BlockSpec((CTX_T, A_QW), lambda b: (b, 0)),
        out_shape=jax.ShapeDtypeStruct((N_CTX_TOK, A_QW), F32),
        compiler_params=_cparams(("arbitrary",)),
        name="attn_ctx",
    )(sink_l, pm, pm, pm)


def _rope(x, cos, sin_signed):
    w = x.shape[-1]
    lane = _iota(x.shape, 1)
    q = A_DH // 4
    partner = jnp.where((lane % (2 * q)) < q, pltpu.roll(x, w - q, 1), pltpu.roll(x, q, 1))
    return x * cos + partner * sin_signed


def _attn_lat_kernel(sink_ref, q_ref, k_ref, v_ref, kc_ref, vc_ref, cos_ref, sin_ref, o_ref, kr_ref):
    n = pl.program_id(1)
    scale = A_DH ** -0.5

    @pl.when(n == 0)
    def _():
        kr_ref[...] = _rope(k_ref[...], cos_ref[:, 0:A_KVW], sin_ref[:, 0:A_KVW]).astype(BF16)

    q0 = pl.multiple_of(n * A_BLK, A_BLK)
    qr = _rope(q_ref[...], cos_ref[pl.ds(q0, A_BLK), :], sin_ref[pl.ds(q0, A_BLK), :])
    kstart = pl.multiple_of(jnp.clip((n - 1) * A_BLK, 0, DEC_T - 3 * A_BLK), A_BLK)
    kwin = kr_ref[pl.ds(kstart, 3 * A_BLK), :]
    vwin = v_ref[pl.ds(kstart, 3 * A_BLK), :].astype(BF16)
    kc = kc_ref[...].astype(BF16)
    vc = vc_ref[...].astype(BF16)
    rows = A_G * A_BLK
    qpos = q0 + _iota((rows, 3 * A_BLK), 0) % A_BLK
    kpos = kstart + _iota((rows, 3 * A_BLK), 1)
    valid = jnp.abs(qpos - kpos) <= A_WIN
    for kvh in range(A_KV):
        cs = slice(kvh * A_DH, (kvh + 1) * A_DH)
        q4 = jnp.concatenate(
            [qr[:, (kvh * A_G + g) * A_DH:(kvh * A_G + g + 1) * A_DH] for g in range(A_G)], axis=0).astype(BF16)
        s_loc = jnp.where(valid, _dg(q4, kwin[:, cs], NT) * scale, NEG_INF)
        s_ctx = _dg(q4, kc[:, cs], NT) * scale
        sink = _sink_col(sink_ref, kvh, A_BLK)
        m = jnp.maximum(jnp.maximum(jnp.max(s_loc, -1, keepdims=True), jnp.max(s_ctx, -1, keepdims=True)), sink)
        e_loc = jnp.exp(s_loc - m)
        e_ctx = jnp.exp(s_ctx - m)
        inv = 1.0 / (jnp.sum(e_loc, -1, keepdims=True) + jnp.sum(e_ctx, -1, keepdims=True) + jnp.exp(sink - m))
        o = _dg((e_loc * inv).astype(BF16), vwin[:, cs]) + _dg((e_ctx * inv).astype(BF16), vc[:, cs])
        for g in range(A_G):
            h = kvh * A_G + g
            o_ref[:, h * A_DH:(h + 1) * A_DH] = o[g * A_BLK:(g + 1) * A_BLK, :]


def _rope_tables():
    half = A_DH // 2
    t = np.arange(DEC_T)
    rows = (t // GRID_W).astype(np.float32)
    cols = (t % GRID_W).astype(np.float32)
    inv_freq = (ROPE_BASE ** (-np.arange(0, half, 2, dtype=np.float32) / half)).astype(np.float32)
    ang_r = rows[:, None] * inv_freq[None, :]
    ang_c = cols[:, None] * inv_freq[None, :]
    cos = np.concatenate([np.cos(ang_r), np.cos(ang_r), np.cos(ang_c), np.cos(ang_c)], -1)
    sin = np.concatenate([-np.sin(ang_r), np.sin(ang_r), -np.sin(ang_c), np.sin(ang_c)], -1)
    return (jnp.asarray(np.tile(cos, (1, A_HEADS)), F32), jnp.asarray(np.tile(sin, (1, A_HEADS)), F32))


def _attn_lat_call(sink_l, pm, kc, vc, l, cos, sin):
    nb = DEC_T // A_BLK
    row0 = N_CTX_TOK // A_BLK
    seq0 = N_CTX_TOK // DEC_T
    return pl.pallas_call(
        _attn_lat_kernel,
        grid=(N_DEC_B, nb),
        in_specs=[
            pl.BlockSpec(memory_space=pltpu.SMEM),
            pl.BlockSpec((A_BLK, A_QW), lambda b, n: (row0 + b * nb + n, 0)),
            pl.BlockSpec((DEC_T, A_KVW), lambda b, n: (seq0 + b, A_QW // A_KVW)),
            pl.BlockSpec((DEC_T, A_KVW), lambda b, n: (seq0 + b, A_QW // A_KVW + 1)),
            pl.BlockSpec((None, None, PAST, A_KVW), lambda b, n: (b, l, 0, 0)),
            pl.BlockSpec((None, None, PAST, A_KVW), lambda b, n: (b, l, 0, 0)),
            pl.BlockSpec((DEC_T, A_QW), lambda b, n: (0, 0)),
            pl.BlockSpec((DEC_T, A_QW), lambda b, n: (0, 0)),
        ],
        out_specs=pl.BlockSpec((A_BLK, A_QW), lambda b, n: (b * nb + n, 0)),
        out_shape=jax.ShapeDtypeStruct((N_DEC_B * DEC_T, A_QW), F32),
        scratch_shapes=[pltpu.VMEM((DEC_T, A_KVW), BF16)],
        compiler_params=_cparams(("arbitrary", "arbitrary")),
        name="attn_lat",
    )(sink_l, pm, pm, pm, kc, vc, cos, sin)


def _rwkv_kernel(T, NS, r_ref, k_ref, v_ref, lr_ref, s0_ref, w0_ref, wb_ref, a0_ref, ab_ref, gb_ref,
                 kk_ref, ka_ref, rk_ref, lng_ref, lnb_ref, o_ref, sfin_ref,
                 KK, W, WRP, AKK, KT, VC2, BON, Y, S):
    ones4 = _block_ones(B_W, B_DH)
    decay_c = float(np.exp(-0.5))
    RC = 256
    SUB = 32
    NP = 3

    def prep(c, carry):
        r0 = pl.multiple_of(c * RC, RC)
        rs = pl.ds(r0, RC)
        r = r_ref[rs, :]
        k = k_ref[rs, :]
        v = v_ref[rs, :]
        lr = lr_ref[rs, :]
        kkr = k * kk_ref[...]
        kk = kkr * lax.rsqrt(_mm_xr(kkr * kkr, ones4, 2) + 1e-12)
        KK[rs, :] = kk
        bonus = jnp.zeros((RC, B_W), F32)
        vc2 = jnp.zeros((RC, B_W), F32)
        for d in range(2):
            z = w0_ref[d] + _mm(jnp.tanh(lr[:, B_RANK * d:B_RANK * (d + 1)]), wb_ref[d])
            w = jnp.exp(-decay_c * _sigmoid(z))
            a = _sigmoid(a0_ref[d] + _mm(lr[:, B_RANK * (2 + d):B_RANK * (3 + d)], ab_ref[d]))
            kt = k * (1.0 + (a - 1.0) * ka_ref[...])
            akk = a * kk
            W[d, rs, :] = w
            WRP[d, rs, :] = w * r - _mm_xr(akk * r, ones4, 2) * kk
            AKK[d, rs, :] = akk
            KT[d, rs, :] = kt
            vc2 = vc2 + _mm_xr(kt * r, ones4, 2) * v
            bonus = bonus + _mm_xr(r * kt * rk_ref[...], ones4, 2) * v
        VC2[rs, :] = vc2
        BON[rs, :] = bonus
        return carry

    lax.fori_loop(0, NS * T // RC, prep, 0)

    chains = [(s, d) for s in range(NS) for d in range(2)]
    for s, d in chains:
        S[s, d] = jnp.concatenate([s0_ref[s, d, h] for h in range(B_HEADS)], axis=1)

    eye4 = _iota((B_DH, B_W), 0) == (_iota((B_DH, B_W), 1) % B_DH)

    def steps(i, carry):
        t0s = [pl.multiple_of(s * T + (i * SUB if d == 0 else T - SUB - i * SUB), SUB) for s, d in chains]

        def row(ref, g, j, d=None):
            tile = pl.ds(t0s[g] + (j // SUBLANES) * SUBLANES, SUBLANES)
            vals = ref[tile, :] if d is None else ref[d, tile, :]
            return vals[j % SUBLANES:j % SUBLANES + 1]

        ys = [[None] * SUB for _ in chains]
        for jj in range(SUB):
            lhs = []
            for g, (s, d) in enumerate(chains):
                j = jj if d == 0 else SUB - 1 - jj
                stb = S[s, d].astype(BF16)
                lhs += [stb * row(KK, g, j).astype(BF16), stb * row(WRP, g, j, d).astype(BF16),
                        jnp.where(eye4, row(v_ref, g, j), 0.0).astype(BF16)]
            res = _dg(jnp.concatenate(lhs, axis=0), ones4)
            for g, (s, d) in enumerate(chains):
                j = jj if d == 0 else SUB - 1 - jj
                sk, yp, vcol = [res[(g * NP + n) * B_DH:(g * NP + n + 1) * B_DH] for n in range(NP)]
                S[s, d] = S[s, d] * row(W, g, j, d) - sk * row(AKK, g, j, d) + vcol * row(KT, g, j, d)
                ys[g][j] = jnp.sum(jnp.where(eye4, yp, 0.0), axis=0, keepdims=True)
        for g, (s, d) in enumerate(chains):
            Y[d, pl.ds(t0s[g], SUB), :] = jnp.concatenate(ys[g], axis=0)
        return carry

    lax.fori_loop(0, T // SUB, steps, 0)

    for s, d in chains:
        st = S[s, d]
        for h in range(B_HEADS):
            sfin_ref[s, d, h] = st[:, h * B_DH:(h + 1) * B_DH]

    def post(c, carry):
        r0 = pl.multiple_of(c * RC, RC)
        rs = pl.ds(r0, RC)
        y = Y[0, rs, :] + Y[1, rs, :] + VC2[rs, :]
        mu = _mm_xr(y, ones4, 2) * (1.0 / B_DH)
        yc = y - mu
        var = _mm_xr(yc * yc, ones4, 2) * (1.0 / B_DH)
        yn = yc * lax.rsqrt(var + B_GN_EPS) * lng_ref[...] + lnb_ref[...] + BON[rs, :]
        g = _mm(_sigmoid(lr_ref[rs, 4 * B_RANK:4 * B_RANK + B_GATE_RANK]), gb_ref[...])
        o_ref[rs, :] = yn * g
        return carry

    lax.fori_loop(0, NS * T // RC, post, 0)


def _rwkv_call(T, NS, n_seq, tok0, pm, plr, s0, prm, name):
    rows = NS * T
    blk0 = tok0 // rows
    n_steps = n_seq // NS
    full = lambda shape: pl.BlockSpec(shape, lambda b: (0,) * len(shape))
    big = lambda shape, imap: (pl.BlockSpec(shape, imap, pipeline_mode=pl.Buffered(1)) if n_steps == 1
                               else pl.BlockSpec(shape, imap))
    kern = functools.partial(_rwkv_kernel, T, NS)
    return pl.pallas_call(
        kern,
        grid=(n_steps,),
        in_specs=[
            big((rows, B_W), lambda b: (blk0 + b, COL_B // B_W)),
            big((rows, B_W), lambda b: (blk0 + b, COL_B // B_W + 1)),
            big((rows, B_W), lambda b: (blk0 + b, COL_B // B_W + 2)),
            big((rows, LR_W), lambda b: (blk0 + b, 0)),
            pl.BlockSpec((NS, 2, B_HEADS, B_DH, B_DH), lambda b: (b, 0, 0, 0, 0)),
            full((2, 1, B_W)), full((2, B_RANK, B_W)), full((2, 1, B_W)), full((2, B_RANK, B_W)),
            full((B_GATE_RANK, B_W)),
            full((1, B_W)), full((1, B_W)), full((1, B_W)), full((1, B_W)), full((1, B_W)),
        ],
        out_specs=[
            big((rows, B_W), lambda b: (b, 0)),
            pl.BlockSpec((NS, 2, B_HEADS, B_DH, B_DH), lambda b: (b, 0, 0, 0, 0)),
        ],
        out_shape=[
            jax.ShapeDtypeStruct((n_seq * T, B_W), F32),
            jax.ShapeDtypeStruct((n_seq, 2, B_HEADS, B_DH, B_DH), F32),
        ],
        scratch_shapes=[
            pltpu.VMEM((rows, B_W), F32),
            pltpu.VMEM((2, rows, B_W), F32),
            pltpu.VMEM((2, rows, B_W), F32),
            pltpu.VMEM((2, rows, B_W), F32),
            pltpu.VMEM((2, rows, B_W), F32),
            pltpu.VMEM((rows, B_W), F32),
            pltpu.VMEM((rows, B_W), F32),
            pltpu.VMEM((2, rows, B_W), F32),
            pltpu.VMEM((NS, 2, B_DH, B_W), F32),
        ],
        compiler_params=_cparams(("arbitrary",)),
        name=name,
    )(pm, pm, pm, plr, s0, *prm)


def _gla_kernel(T, NS, q_ref, k_ref, v_ref, og_ref, lr_ref, s0_ref, gb_ref, bias_ref, ng_ref,
                o_ref, sfin_ref, LA, O, S):
    n_chunks = T // C_CHUNK
    nsub = C_CHUNK // C_SUB
    qscale = C_DK ** -0.5
    lr = lr_ref[...]
    for d in range(2):
        gl = _mm(lr, gb_ref[d]) + bias_ref[d]
        LA[d] = (jnp.minimum(gl, 0.0) - jnp.log(1.0 + jnp.exp(-jnp.abs(gl)))) * (1.0 / C_GATE_NORM)
    bd_state = _iota((C_KW, C_VW), 0) // C_DK == _iota((C_KW, C_VW), 1) // C_DV
    chains = [(s, d) for s in range(NS) for d in range(2)]
    for s, d in chains:
        for h in range(C_HEADS):
            pad_l = h * C_DV
            pad_r = C_VW - (h + 1) * C_DV
            blk = s0_ref[s, d, h]
            parts = ([jnp.zeros((C_DK, pad_l), F32)] if pad_l else []) + [blk] + \
                    ([jnp.zeros((C_DK, pad_r), F32)] if pad_r else [])
            S[s, d, h * C_DK:(h + 1) * C_DK, :] = jnp.concatenate(parts, axis=1)

    ti = _iota((C_CHUNK, C_CHUNK), 0)
    si = _iota((C_CHUNK, C_CHUNK), 1)
    tri = ((si <= ti).astype(BF16), (si >= ti).astype(BF16))
    trow = _iota((C_CHUNK, 1), 0)
    mask_k = _iota((C_CHUNK, C_KW), 0) // C_SUB == _iota((C_CHUNK, C_KW), 1) // C_DK
    mask_v = _iota((C_CHUNK, C_VW), 0) // C_SUB == _iota((C_CHUNK, C_VW), 1) // C_DV
    t_att = _iota((C_CHUNK, C_CHUNK), 0)
    s_att = _iota((C_CHUNK, C_CHUNK), 1) % C_SUB
    eye_k = _iota((C_KW, C_KW), 0) == _iota((C_KW, C_KW), 1)

    def body(c, carry):
        cx = []
        for s, d in chains:
            cc = c if d == 0 else n_chunks - 1 - c
            rs = pl.ds(pl.multiple_of(s * T + cc * C_CHUNK, C_CHUNK), C_CHUNK)
            b = _mm_xl(tri[d], LA[d, rs, :], 3)
            cx.append(dict(s=s, d=d, rs=rs, b=b, q=q_ref[rs, :] * qscale, k=k_ref[rs, :], v=v_ref[rs, :]))
        for x in cx:
            x["o"] = _mm(x["q"] * jnp.exp(x["b"]), S[x["s"], x["d"]])
        for j in range(nsub):
            lo, hi = j * C_SUB, (j + 1) * C_SUB
            for x in cx:
                b, q, k = x["b"], x["q"], x["k"]
                if x["d"] == 0:
                    gamma = b[hi - 1:hi, :]
                    row_ok = trow >= lo
                    att_ok = t_att >= lo + s_att
                else:
                    gamma = b[lo:lo + 1, :]
                    row_ok = trow < hi
                    att_ok = t_att <= lo + s_att
                qj = q * jnp.exp(jnp.where(row_ok, b - gamma, NEG_INF))
                kj = k[lo:hi, :] * jnp.exp(gamma - b[lo:hi, :])
                kbd = jnp.where(mask_k, jnp.concatenate([kj] * C_HEADS, axis=0), 0.0)
                x["att"] = jnp.where(att_ok, _mm(qj, kbd, NT), 0.0)
            for x in cx:
                vbd = jnp.where(mask_v, jnp.concatenate([x["v"][lo:hi, :]] * C_HEADS, axis=0), 0.0)
                x["o"] = x["o"] + _mm(x["att"], vbd)
        for x in cx:
            s, d, b = x["s"], x["d"], x["b"]
            O[d, x["rs"], :] = x["o"]
            blast = b[C_CHUNK - 1:C_CHUNK, :] if d == 0 else b[0:1, :]
            kl = x["k"] * jnp.exp(blast - b)
            upd = jnp.where(bd_state, _mm3(kl.T, x["v"]), 0.0)
            dec = jnp.where(eye_k, jnp.exp(blast), 0.0)
            S[s, d] = _mm3(dec, S[s, d]) + upd
        return carry

    lax.fori_loop(0, n_chunks, body, 0)

    for s, d in chains:
        st = S[s, d]
        for h in range(C_HEADS):
            sfin_ref[s, d, h] = st[h * C_DK:(h + 1) * C_DK, h * C_DV:(h + 1) * C_DV]

    ones4 = _block_ones(C_VW, C_DV)
    o = O[0] + O[1]
    ms = _mm_xr(o * o, ones4, 2) * (1.0 / C_DV)
    og = og_ref[...]
    o_ref[...] = o * lax.rsqrt(ms + LN_EPS) * ng_ref[...] * (og * _sigmoid(og))


def _gla_call(T, NS, n_seq, tok0, pm, plr, s0, gb_pad, bias, ng, name):
    rows = NS * T
    blk0 = tok0 // rows
    full = lambda shape: pl.BlockSpec(shape, lambda b: (0,) * len(shape))
    return pl.pallas_call(
        functools.partial(_gla_kernel, T, NS),
        grid=(n_seq // NS,),
        in_specs=[
            pl.BlockSpec((rows, C_KW), lambda b: (blk0 + b, COL_C // C_KW)),
            pl.BlockSpec((rows, C_KW), lambda b: (blk0 + b, COL_C // C_KW + 1)),
            pl.BlockSpec((rows, C_VW), lambda b: (blk0 + b, COL_CV // C_VW)),
            pl.BlockSpec((rows, C_VW), lambda b: (blk0 + b, COL_CV // C_VW + 1)),
            pl.BlockSpec((rows, LANES), lambda b: (blk0 + b, C_LR_COL // LANES)),
            pl.BlockSpec((NS, 2, C_HEADS, C_DK, C_DV), lambda b: (b, 0, 0, 0, 0)),
            full((2, LANES, C_KW)), full((2, 1, C_KW)), full((1, C_VW)),
        ],
        out_specs=[
            pl.BlockSpec((rows, C_VW), lambda b: (b, 0)),
            pl.BlockSpec((NS, 2, C_HEADS, C_DK, C_DV), lambda b: (b, 0, 0, 0, 0)),
        ],
        out_shape=[
            jax.ShapeDtypeStruct((n_seq * T, C_VW), F32),
            jax.ShapeDtypeStruct((n_seq, 2, C_HEADS, C_DK, C_DV), F32),
        ],
        scratch_shapes=[
            pltpu.VMEM((2, rows, C_KW), F32),
            pltpu.VMEM((2, rows, C_VW), F32),
            pltpu.VMEM((NS, 2, C_KW, C_VW), F32),
        ],
        compiler_params=_cparams(("arbitrary",)),
        name=name,
    )(pm, pm, pm, pm, plr, s0, gb_pad, bias, ng)


MERGE_TM = 512


def _merge_kernel(n_x, *refs):
    x_refs = refs[:n_x]
    (g_ref, oac_ref, oad_ref, obc_ref, obd_ref, occ_ref, ocd_ref, mod_ref,
     wa_ref, wb_ref, wc_ref, wo_ref, lg_ref, lb_ref, o_ref, wa_s, wb_s, wc_s, wo_s) = refs[n_x:]

    @pl.when(pl.program_id(0) == 0)
    def _():
        wa_s[...] = wa_ref[...].astype(BF16)
        wb_s[...] = wb_ref[...].astype(BF16)
        wc_s[...] = wc_ref[...].astype(BF16)
        wo_s[...] = wo_ref[...].astype(BF16)

    is_ctx = pl.program_id(0) < N_CTX_TOK // MERGE_TM
    oa = jnp.where(is_ctx, oac_ref[...], oad_ref[...]).astype(BF16)
    ob = jnp.where(is_ctx, obc_ref[...], obd_ref[...]).astype(BF16)
    oc = jnp.where(is_ctx, occ_ref[...], ocd_ref[...]).astype(BF16)
    merged = (g_ref[:, 0:D].astype(F32) * _dg(oa, wa_s[...])
              + g_ref[:, D:2 * D].astype(F32) * _dg(ob, wb_s[...])
              + g_ref[:, 2 * D:3 * D].astype(F32) * _dg(oc, wc_s[...]))
    mix = _dg(merged.astype(BF16), wo_s[...])
    y = DN_ALPHA * _read_x(x_refs, MERGE_TM) + mod_ref[2:3, :] * mix
    o_ref[...] = _layer_norm(y, lg_ref[...], lb_ref[...])


def _merge_call(xs, gates, oa, ob, oc, mod_l, wa, wb, wc, wo, lg, lb, l):
    tm = MERGE_TM
    x_specs = [pl.BlockSpec((tm, D), lambda i: (i, 0))] if len(xs) == 1 else _pair_specs(tm, D)
    return pl.pallas_call(
        functools.partial(_merge_kernel, len(xs)),
        grid=(N_TOK // tm,),
        in_specs=[
            *x_specs,
            pl.BlockSpec((tm, IN_GATE), lambda i: (i, 0)),
            *_pair_specs(tm, A_QW), *_pair_specs(tm, B_W), *_pair_specs(tm, C_VW),
            pl.BlockSpec((None, 6, D), lambda i: (_group_of_tile(i, tm), 0, 0)),
            _layer_resident((A_QW, D), l), _layer_resident((B_W, D), l), _layer_resident((C_VW, D), l),
            _layer_resident((D, D), l), _layer_resident((1, D), l), _layer_resident((1, D), l),
        ],
        out_specs=pl.BlockSpec((tm, D), lambda i: (i, 0)),
        out_shape=jax.ShapeDtypeStruct((N_TOK, D), F32),
        scratch_shapes=[pltpu.VMEM((A_QW, D), BF16), pltpu.VMEM((B_W, D), BF16),
                        pltpu.VMEM((C_VW, D), BF16), pltpu.VMEM((D, D), BF16)],
        compiler_params=_cparams(("arbitrary",)),
        name="merge",
    )(*xs, gates, *oa, *ob, *oc, mod_l, wa, wb, wc, wo, lg.reshape(DEPTH, 1, D), lb.reshape(DEPTH, 1, D))


def _ffn_kernel(x_ref, mod_ref, wg_ref, wu_ref, wd_ref, lg_ref, lb_ref, o_ref):
    x = x_ref[...]
    h = (x * (1.0 + mod_ref[4:5, :]) + mod_ref[3:4, :]).astype(BF16)
    gate = _dg(h, wg_ref[...])
    up = _dg(h, wu_ref[...])
    f = _dg((gate * _sigmoid(gate) * up).astype(BF16), wd_ref[...])
    y = DN_ALPHA * x + mod_ref[5:6, :] * f
    o_ref[...] = _layer_norm(y, lg_ref[...], lb_ref[...])


def _ffn_call(x, mod_l, wg, wu, wd, lg, lb):
    tm = 512
    return pl.pallas_call(
        _ffn_kernel,
        grid=(N_TOK // tm,),
        in_specs=[
            pl.BlockSpec((tm, D), lambda i: (i, 0)),
            pl.BlockSpec((None, 6, D), lambda i: (_group_of_tile(i, tm), 0, 0)),
            _resident((D, D_FF)), _resident((D, D_FF)), _resident((D_FF, D)),
            _resident((1, D)), _resident((1, D)),
        ],
        out_specs=pl.BlockSpec((tm, D), lambda i: (i, 0)),
        out_shape=jax.ShapeDtypeStruct((N_TOK, D), F32),
        compiler_params=_cparams(("arbitrary",)),
        name="ffn",
    )(x, mod_l, wg, wu, wd, lg.reshape(1, D), lb.reshape(1, D))


MOE_TM = 1024
MOE_TR = 256
MOE_RS = 3072
MOE_TF = 512
MOE_ROWS = 2 * N_TOK + N_EXP * MOE_TR
MOE_NST = -(-MOE_ROWS // MOE_RS) + N_EXP
R_I1, R_I2, R_W1, R_W2, R_RANK1, R_RANK2 = range(6)


def _moe_route_kernel(x_ref, mod_ref, wr_ref, h_ref, info_ref, cnt_ref, carry_s):
    tm = MOE_TM

    @pl.when(pl.program_id(0) == 0)
    def _():
        carry_s[...] = jnp.zeros_like(carry_s)

    h = x_ref[...] * (1.0 + mod_ref[4:5, :]) + mod_ref[3:4, :]
    h_ref[...] = h
    logits = _mm3(h, wr_ref[...])
    lane = _iota(logits.shape, 1)
    logits = jnp.where(lane < N_EXP, logits, NEG_INF)
    v1 = jnp.max(logits, -1, keepdims=True)
    i1 = jnp.min(jnp.where(logits == v1, lane, LANES), -1, keepdims=True)
    rest = jnp.where(lane == i1, NEG_INF, logits)
    v2 = jnp.max(rest, -1, keepdims=True)
    i2 = jnp.min(jnp.where(rest == v2, lane, LANES), -1, keepdims=True)
    e2 = jnp.exp(v2 - v1)
    w1 = 1.0 / (1.0 + e2)
    w2 = e2 / (1.0 + e2)
    oh1 = lane == i1
    oh2 = lane == i2
    cnt = oh1.astype(F32) + oh2.astype(F32)
    earlier = (_iota((tm, tm), 1) < _iota((tm, tm), 0)).astype(BF16)
    before = _dg(earlier, cnt.astype(BF16)) + carry_s[...]
    rank1 = jnp.sum(jnp.where(oh1, before, 0.0), -1, keepdims=True)
    rank2 = jnp.sum(jnp.where(oh2, before, 0.0), -1, keepdims=True)
    info = jnp.zeros(logits.shape, F32)
    for ln, val in ((R_I1, i1.astype(F32)), (R_I2, i2.astype(F32)), (R_W1, w1), (R_W2, w2),
                    (R_RANK1, rank1), (R_RANK2, rank2)):
        info = jnp.where(lane == ln, val, info)
    info_ref[...] = info
    carry_s[...] += jnp.sum(cnt, axis=0, keepdims=True)
    cnt_ref[...] = carry_s[...]


def _moe_route_call(x, mod_l, wr_pad, i_moe):
    tm = MOE_TM
    return pl.pallas_call(
        _moe_route_kernel,
        grid=(N_TOK // tm,),
        in_specs=[
            pl.BlockSpec((tm, D), lambda i: (i, 0)),
            pl.BlockSpec((None, 6, D), lambda i: (_group_of_tile(i, tm), 0, 0)),
            pl.BlockSpec((None, D, LANES), lambda i: (i_moe, 0, 0)),
        ],
        out_specs=[
            pl.BlockSpec((tm, D), lambda i: (i, 0)),
            pl.BlockSpec((tm, LANES), lambda i: (i, 0)),
            pl.BlockSpec((1, LANES), lambda i: (0, 0)),
        ],
        out_shape=[
            jax.ShapeDtypeStruct((N_TOK, D), F32),
            jax.ShapeDtypeStruct((N_TOK, LANES), F32),
            jax.ShapeDtypeStruct((1, LANES), F32),
        ],
        scratch_shapes=[pltpu.VMEM((1, LANES), F32)],
        compiler_params=_cparams(("arbitrary",)),
        name="moe_route",
    )(x, mod_l, wr_pad)


def _moe_plan(info, cnt):
    i32 = jnp.int32
    i1 = info[:, R_I1].astype(i32)
    i2 = info[:, R_I2].astype(i32)
    counts = cnt[0, :N_EXP].astype(i32)
    padded = (counts + MOE_TR - 1) // MOE_TR * MOE_TR
    seg_start = jnp.cumsum(padded) - padded
    pos1 = seg_start[i1] + info[:, R_RANK1].astype(i32)
    pos2 = seg_start[i2] + info[:, R_RANK2].astype(i32)
    dst = jnp.zeros((MOE_ROWS,), i32).at[jnp.concatenate([pos1, pos2])].set(jnp.arange(2 * N_TOK, dtype=i32))
    n_pass = (padded + MOE_RS - 1) // MOE_RS
    pass_end = jnp.cumsum(n_pass)
    total = pass_end[-1]
    sidx = jnp.arange(MOE_NST, dtype=i32)
    used = sidx < total
    e_of = jnp.minimum(jnp.searchsorted(pass_end, jnp.minimum(sidx, total - 1), side="right"), N_EXP - 1).astype(i32)
    k = jnp.minimum(sidx, total - 1) - (pass_end - n_pass)[e_of]
    row0 = seg_start[e_of] + k * MOE_RS
    nrows = jnp.where(used, jnp.clip(padded[e_of] - k * MOE_RS, 0, MOE_RS), 0)
    nvalid = jnp.where(used, jnp.clip(counts[e_of] - k * MOE_RS, 0, MOE_RS), 0)
    return dst, e_of, row0.astype(i32), nrows.astype(i32), nvalid.astype(i32)


def _moe_expert_kernel(dst_ref, exp_ref, row0_ref, nrows_ref, nvalid_ref,
                       h_hbm, wg_ref, wu_ref, wd_ref, yo_hbm, xs, xb, acc, wgb, wub, wdb, gsem, ssem):
    s = pl.program_id(0)
    f = pl.program_id(1)
    nf = pl.num_programs(1)
    nrows = pl.multiple_of(nrows_ref[s], MOE_TR)
    nvalid = nvalid_ref[s]
    row0 = row0_ref[s]
    n_chunks = nrows // MOE_TR

    def row_copy(src, dst, sem):
        return pltpu.make_async_copy(src, dst, sem)

    def hbm_row(ref, i):
        return ref.at[pl.ds(i, 1), :]

    def wait_rows(buf, n_groups, sem):
        pltpu.make_async_copy(buf.at[pl.ds(0, n_groups)], buf.at[pl.ds(0, n_groups)], sem).wait()

    @pl.when((f == 0) & (nrows > 0))
    def _gather():
        def issue(g, carry):
            for u in range(SUBLANES):
                d = dst_ref[row0 + g * SUBLANES + u]
                tok = jnp.where(d >= N_TOK, d - N_TOK, d)
                row_copy(hbm_row(h_hbm, tok), xs.at[g, pl.ds(u, 1), :], gsem).start()
            return carry

        ng = nrows // SUBLANES
        lax.fori_loop(0, ng, issue, 0)
        wait_rows(xs, ng, gsem)

        def cvt(c, carry):
            g0 = pl.multiple_of(c * (MOE_TR // SUBLANES), MOE_TR // SUBLANES)
            rs = pl.ds(pl.multiple_of(c * MOE_TR, MOE_TR), MOE_TR)
            xb[rs, :] = xs[pl.ds(g0, MOE_TR // SUBLANES)].reshape(MOE_TR, D).astype(BF16)
            return carry

        lax.fori_loop(0, n_chunks, cvt, 0)

    @pl.when(nrows > 0)
    def _compute():
        wgb[...] = wg_ref[...].astype(BF16)
        wub[...] = wu_ref[...].astype(BF16)
        wdb[...] = wd_ref[...].astype(BF16)

        def chunk(start, n):
            x = xb[pl.ds(pl.multiple_of(start, MOE_TR), n), :]
            gate = _dg(x, wgb[...])
            up = _dg(x, wub[...])
            y = _dg((gate * _sigmoid(gate) * up).astype(BF16), wdb[...]).reshape(n // SUBLANES, SUBLANES, D)
            gs = pl.ds(pl.multiple_of(start // SUBLANES, MOE_TR // SUBLANES), n // SUBLANES)

            @pl.when(f == 0)
            def _():
                acc[gs] = y

            @pl.when(f > 0)
            def _():
                acc[gs] += y

        def chunk_quad(c4, carry):
            chunk(c4 * (4 * MOE_TR), 4 * MOE_TR)
            return carry

        lax.fori_loop(0, n_chunks // 4, chunk_quad, 0)
        done = n_chunks // 4 * 4

        @pl.when(n_chunks % 4 >= 2)
        def _():
            chunk(done * MOE_TR, 2 * MOE_TR)

        @pl.when(n_chunks % 2 == 1)
        def _():
            chunk((n_chunks - 1) * MOE_TR, MOE_TR)

    @pl.when((f == nf - 1) & (nvalid > 0))
    def _scatter():
        n8 = nvalid // SUBLANES

        def issue(g, carry):
            for u in range(SUBLANES):
                d = dst_ref[row0 + g * SUBLANES + u]
                row_copy(acc.at[g, pl.ds(u, 1), :], hbm_row(yo_hbm, d), ssem).start()
            return carry

        def issue_tail(r, carry):
            d = dst_ref[row0 + r]
            row_copy(acc.at[n8, pl.ds(r - n8 * SUBLANES, 1), :], hbm_row(yo_hbm, d), ssem).start()
            return carry

        lax.fori_loop(0, n8, issue, 0)
        lax.fori_loop(n8 * SUBLANES, nvalid, issue_tail, 0)

        @pl.when(n8 > 0)
        def _():
            wait_rows(acc, n8, ssem)

        def wait_one(r, carry):
            row_copy(acc.at[0, pl.ds(0, 1), :], hbm_row(yo_hbm, 0), ssem).wait()
            return carry

        lax.fori_loop(n8 * SUBLANES, nvalid, wait_one, 0)


def _moe_expert_call(h, plan, wg, wu, wd, i_moe):
    nf = D_FFE // MOE_TF

    def wspec(shape, fdim):
        def imap(s, f, dst, exp, row0, nrows, nvalid):
            fe = jnp.where(nrows[s] > 0, f, nf - 1)
            return (i_moe, exp[s], 0, fe) if fdim == 3 else (i_moe, exp[s], fe, 0)
        return pl.BlockSpec(shape, imap)

    grid_spec = pltpu.PrefetchScalarGridSpec(
        num_scalar_prefetch=5,
        grid=(MOE_NST, nf),
        in_specs=[
            pl.BlockSpec(memory_space=pl.ANY),
            wspec((None, None, D, MOE_TF), 3),
            wspec((None, None, D, MOE_TF), 3),
            wspec((None, None, MOE_TF, D), 2),
        ],
        out_specs=pl.BlockSpec(memory_space=pl.ANY),
        scratch_shapes=[
            pltpu.VMEM((MOE_RS // SUBLANES, SUBLANES, D), F32),
            pltpu.VMEM((MOE_RS, D), BF16),
            pltpu.VMEM((MOE_RS // SUBLANES, SUBLANES, D), F32),
            pltpu.VMEM((D, MOE_TF), BF16), pltpu.VMEM((D, MOE_TF), BF16), pltpu.VMEM((MOE_TF, D), BF16),
            pltpu.SemaphoreType.DMA(()), pltpu.SemaphoreType.DMA(()),
        ],
    )
    return pl.pallas_call(
        _moe_expert_kernel,
        grid_spec=grid_spec,
        out_shape=jax.ShapeDtypeStruct((2 * N_TOK, D), F32),
        compiler_params=pltpu.CompilerParams(dimension_semantics=("arbitrary", "arbitrary"),
                                             vmem_limit_bytes=VMEM_LIMIT, disable_bounds_checks=True),
        name="moe_experts",
    )(*plan, h, wg, wu, wd)


def _moe_combine_kernel(x_ref, y1_ref, y2_ref, info_ref, mod_ref, lg_ref, lb_ref, oc_ref, od_ref):
    f = info_ref[:, R_W1:R_W1 + 1] * y1_ref[...] + info_ref[:, R_W2:R_W2 + 1] * y2_ref[...]
    y = DN_ALPHA * x_ref[...] + mod_ref[5:6, :] * f
    out = _layer_norm(y, lg_ref[...], lb_ref[...])
    is_ctx = pl.program_id(0) < N_CTX_TOK // MOE_TM

    @pl.when(is_ctx)
    def _():
        oc_ref[...] = out

    @pl.when(jnp.logical_not(is_ctx))
    def _():
        od_ref[...] = out


def _moe_combine_call(x, yo, info, mod_l, lg, lb):
    tm = MOE_TM
    nt = N_TOK // tm
    return pl.pallas_call(
        _moe_combine_kernel,
        grid=(nt,),
        in_specs=[
            pl.BlockSpec((tm, D), lambda i: (i, 0)),
            pl.BlockSpec((tm, D), lambda i: (i, 0)),
            pl.BlockSpec((tm, D), lambda i: (nt + i, 0)),
            pl.BlockSpec((tm, LANES), lambda i: (i, 0)),
            pl.BlockSpec((None, 6, D), lambda i: (_group_of_tile(i, tm), 0, 0)),
            _resident((1, D)), _resident((1, D)),
        ],
        out_specs=_pair_specs(tm, D),
        out_shape=[jax.ShapeDtypeStruct((N_CTX_TOK, D), F32), jax.ShapeDtypeStruct((N_DEC_B * DEC_T, D), F32)],
        compiler_params=_cparams(("arbitrary",)),
        name="moe_combine",
    )(x, yo, yo, info, mod_l, lg.reshape(1, D), lb.reshape(1, D))


def _moe_call(x, mod_l, wr_pad, wg, wu, wd, lg, lb, i_moe):
    h, info, cnt = _moe_route_call(x, mod_l, wr_pad, i_moe)
    yo = _moe_expert_call(h, _moe_plan(info, cnt), wg, wu, wd, i_moe)
    return _moe_combine_call(x, yo, info, mod_l, lg, lb)


def kernel(x_prompt, x_sample, cache_attn_k, cache_attn_v, state_rwkv, state_gla, c, c_ctx, w_ada, b_ada, w_in,
           attn_sink, rwkv_w0, rwkv_w_a, rwkv_w_b, rwkv_a0, rwkv_a_a, rwkv_a_b, rwkv_g_a, rwkv_g_b, rwkv_k_k,
           rwkv_k_a, rwkv_r_k, rwkv_ln_g, rwkv_ln_b, gla_gate_a, gla_gate_b, gla_gate_bias, gla_norm_g, w_up_a,
           w_up_b, w_up_c, w_out, ln1_g, ln1_b, ln2_g, ln2_b, ffn_w_gate, ffn_w_up, ffn_w_down, moe_router,
           moe_w_gate, moe_w_up, moe_w_down):
    cvec = jnp.concatenate([c_ctx[None, :], c, jnp.zeros((N_GROUPS - 1 - N_DEC_B, D), F32)], axis=0)
    mods = _ada_call(cvec, w_ada, b_ada).reshape(DEPTH, N_GROUPS, 6, D)
    xs = (x_prompt.reshape(N_CTX_TOK, D), x_sample.reshape(N_DEC_B * DEC_T, D))
    cos, sin = _rope_tables()
    kc_all = cache_attn_k.reshape(N_DEC_B, DEPTH, PAST, A_KV * A_DH)
    vc_all = cache_attn_v.reshape(N_DEC_B, DEPTH, PAST, A_KV * A_DH)
    zeros_r = jnp.zeros((N_CTX_B, 2, B_HEADS, B_DH, B_DH), F32)
    zeros_g = jnp.zeros((N_CTX_B, 2, C_HEADS, C_DK, C_DV), F32)

    new_k, new_v, new_sr, new_sg = [], [], [], []
    for l in range(DEPTH):
        mod_l = mods[l]
        w_lr = jnp.concatenate(
            [rwkv_w_a[l, 0], rwkv_w_a[l, 1], rwkv_a_a[l, 0], rwkv_a_a[l, 1], rwkv_g_a[l],
             gla_gate_a[l, 0], gla_gate_a[l, 1], jnp.zeros((D, LR_W - C_LR_COL - 2 * C_GATE_RANK), F32)], axis=1).astype(BF16)
        pm, gates, plr, k_ctx, v_ctx = _inproj_call(xs, mod_l, w_in, w_lr, l)

        sink_l = attn_sink[l]
        oa = (_attn_ctx_call(sink_l, pm), _attn_lat_call(sink_l, pm, kc_all, vc_all, l, cos, sin))

        rprm = (rwkv_w0[l].reshape(2, 1, B_W), rwkv_w_b[l], rwkv_a0[l].reshape(2, 1, B_W), rwkv_a_b[l],
                rwkv_g_b[l], rwkv_k_k[l].reshape(1, B_W), rwkv_k_a[l].reshape(1, B_W),
                rwkv_r_k[l].reshape(1, B_W), rwkv_ln_g[l].reshape(1, B_W), rwkv_ln_b[l].reshape(1, B_W))
        ob_c, sr_c = _rwkv_call(CTX_T, 4, N_CTX_B, 0, pm, plr, zeros_r, rprm, "rwkv_ctx")
        ob_d, _ = _rwkv_call(DEC_T, 2, N_DEC_B, N_CTX_TOK, pm, plr, state_rwkv[:, l], rprm, "rwkv_lat")

        gb_pad = jnp.zeros((2, LANES, C_KW), F32)
        gb_pad = gb_pad.at[0, 0:C_GATE_RANK].set(gla_gate_b[l, 0]).at[1, C_GATE_RANK:2 * C_GATE_RANK].set(
            gla_gate_b[l, 1])
        gbias = gla_gate_bias[l].reshape(2, 1, C_KW)
        ng = jnp.tile(gla_norm_g[l], C_HEADS).reshape(1, C_VW)
        oc_c, sg_c = _gla_call(CTX_T, 4, N_CTX_B, 0, pm, plr, zeros_g, gb_pad, gbias, ng, "gla_ctx")
        oc_d, _ = _gla_call(DEC_T, 2, N_DEC_B, N_CTX_TOK, pm, plr, state_gla[:, l], gb_pad, gbias, ng, "gla_lat")

        x = _merge_call(xs, gates, oa, (ob_c, ob_d), (oc_c, oc_d), mod_l, w_up_a, w_up_b, w_up_c, w_out,
                        ln1_g, ln1_b, l)
        if l % 2 == 0:
            i_ffn = l // 2
            xs = (_ffn_call(x, mod_l, ffn_w_gate[i_ffn].astype(BF16), ffn_w_up[i_ffn].astype(BF16),
                            ffn_w_down[i_ffn].astype(BF16), ln2_g[l], ln2_b[l]),)
        else:
            wr_pad = jnp.concatenate([moe_router, jnp.zeros((moe_router.shape[0], D, LANES - N_EXP), F32)], axis=2)
            xs = _moe_call(x, mod_l, wr_pad, moe_w_gate, moe_w_up, moe_w_down, ln2_g[l], ln2_b[l], l // 2)

        new_k.append(k_ctx.reshape(N_CTX_B, CTX_T, A_KV, A_DH))
        new_v.append(v_ctx.reshape(N_CTX_B, CTX_T, A_KV, A_DH))
        new_sr.append(sr_c)
        new_sg.append(sg_c)

    if len(xs) == 1:
        xs = (xs[0][:N_CTX_TOK], xs[0][N_CTX_TOK:])
    y_prompt = xs[0].reshape(N_CTX_B, CTX_T, D)
    y_sample = xs[1].reshape(N_DEC_B, DEC_T, D)
    return (y_prompt, y_sample, jnp.stack(new_k, axis=1), jnp.stack(new_v, axis=1),
            jnp.stack(new_sr, axis=1), jnp.stack(new_sg, axis=1))
```

```python
import functools

import numpy as np
import jax
import jax.numpy as jnp
from jax import lax
from jax.experimental import pallas as pl
from jax.experimental.pallas import tpu as pltpu

D = 1024
N_CTX_B, CTX_T = 32, 256
N_DEC_B, DEC_T = 2, 1024
N_CTX_TOK = N_CTX_B * CTX_T
N_TOK = N_CTX_TOK + N_DEC_B * DEC_T
DEPTH = 2
PAST = 256
GRID_W = 64
A_HEADS, A_KV, A_DH = 8, 2, 64
A_G = A_HEADS // A_KV
A_WIN, A_BLK = 128, 128
ROPE_BASE = 10000.0
B_HEADS, B_DH = 4, 64
B_W = B_HEADS * B_DH
B_GN_EPS = 64e-5
C_HEADS, C_DK, C_DV = 4, 32, 64
C_KW, C_VW = C_HEADS * C_DK, C_HEADS * C_DV
C_GATE_RANK = 16
C_GATE_NORM = 16.0
C_CHUNK = 64
C_SUB = 16
GLA_UNROLL = 2
D_FF = 2816
N_EXP = 8
D_FFE = 3584
LN_EPS = 1e-5
DN_ALPHA = (2.0 * DEPTH) ** 0.25
NEG_INF = -1e30
A_QW, A_KVW = A_HEADS * A_DH, A_KV * A_DH
B_RANK, B_GATE_RANK = 64, 128
LANES = 128
IN_MAIN = 2304
COL_B = A_QW + 2 * A_KVW
COL_C = COL_B + 3 * B_W
COL_CV = COL_C + 2 * C_KW
IN_GATE = 3072
LR_W = 512
C_LR_COL = 4 * B_RANK + B_GATE_RANK
N_GROUPS = 8

F32 = jnp.float32
BF16 = jnp.bfloat16
VMEM_LIMIT = 56 * 1024 * 1024
SUBLANES = 8

NN = ((1,), (0,))
NT = ((1,), (1,))
TN = ((0,), (0,))


def _dg(a, b, dims=NN):
    return lax.dot_general(a, b, (dims, ((), ())), preferred_element_type=F32)


def _split2(x):
    hi = x.astype(BF16)
    lo = (x - hi.astype(F32)).astype(BF16)
    return hi, lo


def _split3(x):
    hi = x.astype(BF16)
    r = x - hi.astype(F32)
    mid = r.astype(BF16)
    lo = (r - mid.astype(F32)).astype(BF16)
    return hi, mid, lo


def _mm(a, b, dims=NN):
    return _dg(a.astype(BF16), b.astype(BF16), dims)


def _mm3(a, b, dims=NN):
    ah, al = _split2(a)
    bh, bl = _split2(b)
    return _dg(ah, bh, dims) + (_dg(ah, bl, dims) + _dg(al, bh, dims))


def _mm_xr(a, b_exact, passes, dims=NN):
    parts = (a.astype(BF16),) if passes == 1 else (_split2(a) if passes == 2 else _split3(a))
    out = _dg(parts[0], b_exact, dims)
    for p in parts[1:]:
        out = out + _dg(p, b_exact, dims)
    return out


def _mm_xl(a_exact, b, passes):
    parts = (b.astype(BF16),) if passes == 1 else (_split2(b) if passes == 2 else _split3(b))
    out = _dg(a_exact, parts[0])
    for p in parts[1:]:
        out = out + _dg(a_exact, p)
    return out


def _sigmoid(x):
    return 1.0 / (1.0 + jnp.exp(-x))


def _iota(shape, dim):
    return lax.broadcasted_iota(jnp.int32, shape, dim)


def _block_ones(n, blk):
    return (_iota((n, n), 0) // blk == _iota((n, n), 1) // blk).astype(BF16)


def _layer_norm(y, g, b):
    mu = jnp.mean(y, -1, keepdims=True)
    yc = y - mu
    var = jnp.mean(yc * yc, -1, keepdims=True)
    return yc * lax.rsqrt(var + LN_EPS) * g + b


def _group_of_tile(i, tm):
    n_ctx = N_CTX_TOK // tm
    per_dec = DEC_T // tm
    return jnp.where(i < n_ctx, 0, 1 + (i - n_ctx) // per_dec)


def _cparams(sem):
    return pltpu.CompilerParams(dimension_semantics=sem, vmem_limit_bytes=VMEM_LIMIT)


def _ada_kernel(c_ref, w_ref, b_ref, o_ref):
    c = c_ref[...]
    s = c * _sigmoid(c)
    o_ref[...] = _mm3(s, w_ref[...]) + b_ref[...]


def _ada_call(cvec, w_ada, b_ada):
    tn = 1536
    return pl.pallas_call(
        _ada_kernel,
        grid=(DEPTH, 6 * D // tn),
        in_specs=[
            pl.BlockSpec((N_GROUPS, D), lambda l, j: (0, 0)),
            pl.BlockSpec((None, D, tn), lambda l, j: (l, 0, j)),
            pl.BlockSpec((None, 1, tn), lambda l, j: (l, 0, j)),
        ],
        out_specs=pl.BlockSpec((None, N_GROUPS, tn), lambda l, j: (l, 0, j)),
        out_shape=jax.ShapeDtypeStruct((DEPTH, N_GROUPS, 6 * D), F32),
        compiler_params=_cparams(("arbitrary", "arbitrary")),
        name="ada",
    )(cvec, w_ada, b_ada.reshape(DEPTH, 1, 6 * D))


def _pair_specs(tm, width):
    n_ctx = N_CTX_TOK // tm
    return [pl.BlockSpec((tm, width), lambda i: (jnp.minimum(i, n_ctx - 1), 0)),
            pl.BlockSpec((tm, width), lambda i: (jnp.maximum(i - n_ctx, 0), 0))]


def _read_x(x_refs, tm):
    if len(x_refs) == 1:
        return x_refs[0][...]
    return jnp.where(pl.program_id(0) < N_CTX_TOK // tm, x_refs[0][...], x_refs[1][...])


INPROJ_TM = 256


W_IN_TILE = 768
N_MAIN_TILES, N_GATE_TILES = IN_MAIN // W_IN_TILE, IN_GATE // W_IN_TILE


def _inproj_kernel(n_x, *refs):
    n_w = N_MAIN_TILES + N_GATE_TILES
    x_refs = refs[:n_x]
    mod_ref = refs[n_x]
    w_refs = refs[n_x + 1:n_x + 1 + n_w]
    wl_ref, om_ref, og_ref, ol_ref, ok_ref, ov_ref, wm_s, wg_s = refs[n_x + 1 + n_w:]

    @pl.when(pl.program_id(0) == 0)
    def _():
        for j, w_ref in enumerate(w_refs):
            dst, jj = (wm_s, j) if j < N_MAIN_TILES else (wg_s, j - N_MAIN_TILES)
            dst[:, jj * W_IN_TILE:(jj + 1) * W_IN_TILE] = w_ref[...].astype(BF16)

    sh = mod_ref[0:1, :]
    sc = mod_ref[1:2, :]
    h = (_read_x(x_refs, INPROJ_TM) * (1.0 + sc) + sh).astype(BF16)
    main = _dg(h, wm_s[...])
    om_ref[...] = main
    og_ref[...] = _sigmoid(_dg(h, wg_s[...])).astype(BF16)
    ol_ref[...] = _dg(h, wl_ref[...])

    @pl.when(pl.program_id(0) < N_CTX_TOK // INPROJ_TM)
    def _():
        ok_ref[...] = main[:, A_QW:A_QW + A_KVW]
        ov_ref[...] = main[:, A_QW + A_KVW:A_QW + 2 * A_KVW]


def _resident(shape):
    return pl.BlockSpec(shape, lambda *_: (0,) * len(shape), pipeline_mode=pl.Buffered(1))


def _layer_resident(shape, l):
    return pl.BlockSpec((None, *shape), lambda *_: (l,) + (0,) * len(shape), pipeline_mode=pl.Buffered(1))


def _inproj_call(xs, mod_l, w_in, w_lr, l):
    tm = INPROJ_TM
    n_ctx = N_CTX_TOK // tm
    x_specs = [pl.BlockSpec((tm, D), lambda i: (i, 0))] if len(xs) == 1 else _pair_specs(tm, D)
    kv_spec = pl.BlockSpec((tm, A_KVW), lambda i: (jnp.minimum(i, n_ctx - 1), 0))
    n_w = N_MAIN_TILES + N_GATE_TILES
    w_specs = [pl.BlockSpec((None, D, W_IN_TILE), lambda i, j=j: (l, 0, j), pipeline_mode=pl.Buffered(1))
               for j in range(n_w)]
    return pl.pallas_call(
        functools.partial(_inproj_kernel, len(xs)),
        grid=(N_TOK // tm,),
        in_specs=[
            *x_specs,
            pl.BlockSpec((None, 6, D), lambda i: (_group_of_tile(i, tm), 0, 0)),
            *w_specs,
            _resident((D, LR_W)),
        ],
        out_specs=[
            pl.BlockSpec((tm, IN_MAIN), lambda i: (i, 0)),
            pl.BlockSpec((tm, IN_GATE), lambda i: (i, 0)),
            pl.BlockSpec((tm, LR_W), lambda i: (i, 0)),
            kv_spec, kv_spec,
        ],
        out_shape=[
            jax.ShapeDtypeStruct((N_TOK, IN_MAIN), F32),
            jax.ShapeDtypeStruct((N_TOK, IN_GATE), BF16),
            jax.ShapeDtypeStruct((N_TOK, LR_W), F32),
            jax.ShapeDtypeStruct((N_CTX_TOK, A_KVW), F32),
            jax.ShapeDtypeStruct((N_CTX_TOK, A_KVW), F32),
        ],
        scratch_shapes=[pltpu.VMEM((D, IN_MAIN), BF16), pltpu.VMEM((D, IN_GATE), BF16)],
        compiler_params=_cparams(("arbitrary",)),
        name="inproj",
    )(*xs, mod_l, *([w_in] * n_w), w_lr)


def _sink_col(sink_ref, kvh, rows_per_head):
    n = A_G * rows_per_head
    r = _iota((n, 1), 0) // rows_per_head
    col = jnp.full((n, 1), sink_ref[kvh * A_G], F32)
    for g in range(1, A_G):
        col = jnp.where(r == g, sink_ref[kvh * A_G + g], col)
    return col


def _attn_ctx_kernel(sink_ref, q_ref, k_ref, v_ref, o_ref):
    scale = A_DH ** -0.5
    for kvh in range(A_KV):
        ks = k_ref[:, kvh * A_DH:(kvh + 1) * A_DH].astype(BF16)
        vs = v_ref[:, kvh * A_DH:(kvh + 1) * A_DH].astype(BF16)
        q4 = jnp.concatenate(
            [q_ref[:, (kvh * A_G + g) * A_DH:(kvh * A_G + g + 1) * A_DH] for g in range(A_G)], axis=0)
        s = _dg(q4.astype(BF16), ks, NT) * scale
        sink = _sink_col(sink_ref, kvh, CTX_T)
        m = jnp.maximum(jnp.max(s, -1, keepdims=True), sink)
        e = jnp.exp(s - m)
        p = e / (jnp.sum(e, -1, keepdims=True) + jnp.exp(sink - m))
        o = _dg(p.astype(BF16), vs)
        for g in range(A_G):
            h = kvh * A_G + g
            o_ref[:, h * A_DH:(h + 1) * A_DH] = o[g * CTX_T:(g + 1) * CTX_T, :]


def _attn_ctx_call(sink_l, pm):
    return pl.pallas_call(
        _attn_ctx_kernel,
        grid=(N_CTX_B,),
        in_specs=[
            pl.BlockSpec(memory_space=pltpu.SMEM),
            pl.BlockSpec((CTX_T, A_QW), lambda b: (b, 0)),
            pl.BlockSpec((CTX_T, A_KVW), lambda b: (b, A_QW // A_KVW)),
            pl.BlockSpec((CTX_T, A_KVW), lambda b: (b, A_QW // A_KVW + 1)),
        ],
        out_specs=pl.BlockSpec((CTX_T, A_QW), lambda b: (b, 0)),
        out_shape=jax.ShapeDtypeStruct((N_CTX_TOK, A_QW), F32),
        compiler_params=_cparams(("arbitrary",)),
        name="attn_ctx",
    )(sink_l, pm, pm, pm)


def _rope(x, cos, sin_signed):
    w = x.shape[-1]
    lane = _iota(x.shape, 1)
    q = A_DH // 4
    partner = jnp.where((lane % (2 * q)) < q, pltpu.roll(x, w - q, 1), pltpu.roll(x, q, 1))
    return x * cos + partner * sin_signed


def _attn_lat_kernel(sink_ref, q_ref, k_ref, v_ref, kc_ref, vc_ref, cos_ref, sin_ref, o_ref, kr_ref):
    n = pl.program_id(1)
    scale = A_DH ** -0.5

    @pl.when(n == 0)
    def _():
        kr_ref[...] = _rope(k_ref[...], cos_ref[:, 0:A_KVW], sin_ref[:, 0:A_KVW]).astype(BF16)

    q0 = pl.multiple_of(n * A_BLK, A_BLK)
    qr = _rope(q_ref[...], cos_ref[pl.ds(q0, A_BLK), :], sin_ref[pl.ds(q0, A_BLK), :])
    kstart = pl.multiple_of(jnp.clip((n - 1) * A_BLK, 0, DEC_T - 3 * A_BLK), A_BLK)
    kwin = kr_ref[pl.ds(kstart, 3 * A_BLK), :]
    vwin = v_ref[pl.ds(kstart, 3 * A_BLK), :].astype(BF16)
    kc = kc_ref[...].astype(BF16)
    vc = vc_ref[...].astype(BF16)
    rows = A_G * A_BLK
    qpos = q0 + _iota((rows, 3 * A_BLK), 0) % A_BLK
    kpos = kstart + _iota((rows, 3 * A_BLK), 1)
    valid = jnp.abs(qpos - kpos) <= A_WIN
    for kvh in range(A_KV):
        cs = slice(kvh * A_DH, (kvh + 1) * A_DH)
        q4 = jnp.concatenate(
            [qr[:, (kvh * A_G + g) * A_DH:(kvh * A_G + g + 1) * A_DH] for g in range(A_G)], axis=0).astype(BF16)
        s_loc = jnp.where(valid, _dg(q4, kwin[:, cs], NT) * scale, NEG_INF)
        s_ctx = _dg(q4, kc[:, cs], NT) * scale
        sink = _sink_col(sink_ref, kvh, A_BLK)
        m = jnp.maximum(jnp.maximum(jnp.max(s_loc, -1, keepdims=True), jnp.max(s_ctx, -1, keepdims=True)), sink)
        e_loc = jnp.exp(s_loc - m)
        e_ctx = jnp.exp(s_ctx - m)
        inv = 1.0 / (jnp.sum(e_loc, -1, keepdims=True) + jnp.sum(e_ctx, -1, keepdims=True) + jnp.exp(sink - m))
        o = _dg((e_loc * inv).astype(BF16), vwin[:, cs]) + _dg((e_ctx * inv).astype(BF16), vc[:, cs])
        for g in range(A_G):
            h = kvh * A_G + g
            o_ref[:, h * A_DH:(h + 1) * A_DH] = o[g * A_BLK:(g + 1) * A_BLK, :]


def _rope_tables():
    half = A_DH // 2
    t = np.arange(DEC_T)
    rows = (t // GRID_W).astype(np.float32)
    cols = (t % GRID_W).astype(np.float32)
    inv_freq = (ROPE_BASE ** (-np.arange(0, half, 2, dtype=np.float32) / half)).astype(np.float32)
    ang_r = rows[:, None] * inv_freq[None, :]
    ang_c = cols[:, None] * inv_freq[None, :]
    cos = np.concatenate([np.cos(ang_r), np.cos(ang_r), np.cos(ang_c), np.cos(ang_c)], -1)
    sin = np.concatenate([-np.sin(ang_r), np.sin(ang_r), -np.sin(ang_c), np.sin(ang_c)], -1)
    return (jnp.asarray(np.tile(cos, (1, A_HEADS)), F32), jnp.asarray(np.tile(sin, (1, A_HEADS)), F32))


def _attn_lat_call(sink_l, pm, kc, vc, l, cos, sin):
    nb = DEC_T // A_BLK
    row0 = N_CTX_TOK // A_BLK
    seq0 = N_CTX_TOK // DEC_T
    return pl.pallas_call(
        _attn_lat_kernel,
        grid=(N_DEC_B, nb),
        in_specs=[
            pl.BlockSpec(memory_space=pltpu.SMEM),
            pl.BlockSpec((A_BLK, A_QW), lambda b, n: (row0 + b * nb + n, 0)),
            pl.BlockSpec((DEC_T, A_KVW), lambda b, n: (seq0 + b, A_QW // A_KVW)),
            pl.BlockSpec((DEC_T, A_KVW), lambda b, n: (seq0 + b, A_QW // A_KVW + 1)),
            pl.BlockSpec((None, None, PAST, A_KVW), lambda b, n: (b, l, 0, 0)),
            pl.BlockSpec((None, None, PAST, A_KVW), lambda b, n: (b, l, 0, 0)),
            pl.BlockSpec((DEC_T, A_QW), lambda b, n: (0, 0)),
            pl.BlockSpec((DEC_T, A_QW), lambda b, n: (0, 0)),
        ],
        out_specs=pl.BlockSpec((A_BLK, A_QW), lambda b, n: (b * nb + n, 0)),
        out_shape=jax.ShapeDtypeStruct((N_DEC_B * DEC_T, A_QW), F32),
        scratch_shapes=[pltpu.VMEM((DEC_T, A_KVW), BF16)],
        compiler_params=_cparams(("arbitrary", "arbitrary")),
        name="attn_lat",
    )(sink_l, pm, pm, pm, kc, vc, cos, sin)


def _rwkv_kernel(T, NS, r_ref, k_ref, v_ref, lr_ref, s0_ref, w0_ref, wb_ref, a0_ref, ab_ref, gb_ref,
                 kk_ref, ka_ref, rk_ref, lng_ref, lnb_ref, o_ref, sfin_ref,
                 KK, W, WRP, AKK, KT, VC2, BON, Y, S):
    ones4 = _block_ones(B_W, B_DH)
    decay_c = float(np.exp(-0.5))
    RC = 256
    SUB = 32
    NP = 3

    def prep(c, carry):
        r0 = pl.multiple_of(c * RC, RC)
        rs = pl.ds(r0, RC)
        r = r_ref[rs, :]
        k = k_ref[rs, :]
        v = v_ref[rs, :]
        lr = lr_ref[rs, :]
        kkr = k * kk_ref[...]
        kk = kkr * lax.rsqrt(_mm_xr(kkr * kkr, ones4, 2) + 1e-12)
        KK[rs, :] = kk
        bonus = jnp.zeros((RC, B_W), F32)
        vc2 = jnp.zeros((RC, B_W), F32)
        for d in range(2):
            z = w0_ref[d] + _mm(jnp.tanh(lr[:, B_RANK * d:B_RANK * (d + 1)]), wb_ref[d])
            w = jnp.exp(-decay_c * _sigmoid(z))
            a = _sigmoid(a0_ref[d] + _mm(lr[:, B_RANK * (2 + d):B_RANK * (3 + d)], ab_ref[d]))
            kt = k * (1.0 + (a - 1.0) * ka_ref[...])
            akk = a * kk
            W[d, rs, :] = w
            WRP[d, rs, :] = w * r - _mm_xr(akk * r, ones4, 2) * kk
            AKK[d, rs, :] = akk
            KT[d, rs, :] = kt
            vc2 = vc2 + _mm_xr(kt * r, ones4, 2) * v
            bonus = bonus + _mm_xr(r * kt * rk_ref[...], ones4, 2) * v
        VC2[rs, :] = vc2
        BON[rs, :] = bonus
        return carry

    lax.fori_loop(0, NS * T // RC, prep, 0)

    chains = [(s, d) for s in range(NS) for d in range(2)]
    for s, d in chains:
        S[s, d] = jnp.concatenate([s0_ref[s, d, h] for h in range(B_HEADS)], axis=1)

    eye4 = _iota((B_DH, B_W), 0) == (_iota((B_DH, B_W), 1) % B_DH)

    def steps(i, carry):
        t0s = [pl.multiple_of(s * T + (i * SUB if d == 0 else T - SUB - i * SUB), SUB) for s, d in chains]

        def row(ref, g, j, d=None):
            tile = pl.ds(t0s[g] + (j // SUBLANES) * SUBLANES, SUBLANES)
            vals = ref[tile, :] if d is None else ref[d, tile, :]
            return vals[j % SUBLANES:j % SUBLANES + 1]

        ys = [[None] * SUB for _ in chains]
        for jj in range(SUB):
            lhs = []
            for g, (s, d) in enumerate(chains):
                j = jj if d == 0 else SUB - 1 - jj
                stb = S[s, d].astype(BF16)
                lhs += [stb * row(KK, g, j).astype(BF16), stb * row(WRP, g, j, d).astype(BF16),
                        jnp.where(eye4, row(v_ref, g, j), 0.0).astype(BF16)]
            res = _dg(jnp.concatenate(lhs, axis=0), ones4)
            for g, (s, d) in enumerate(chains):
                j = jj if d == 0 else SUB - 1 - jj
                sk, yp, vcol = [res[(g * NP + n) * B_DH:(g * NP + n + 1) * B_DH] for n in range(NP)]
                S[s, d] = S[s, d] * row(W, g, j, d) - sk * row(AKK, g, j, d) + vcol * row(KT, g, j, d)
                ys[g][j] = jnp.sum(jnp.where(eye4, yp, 0.0), axis=0, keepdims=True)
        for g, (s, d) in enumerate(chains):
            Y[d, pl.ds(t0s[g], SUB), :] = jnp.concatenate(ys[g], axis=0)
        return carry

    lax.fori_loop(0, T // SUB, steps, 0)

    for s, d in chains:
        st = S[s, d]
        for h in range(B_HEADS):
            sfin_ref[s, d, h] = st[:, h * B_DH:(h + 1) * B_DH]

    def post(c, carry):
        r0 = pl.multiple_of(c * RC, RC)
        rs = pl.ds(r0, RC)
        y = Y[0, rs, :] + Y[1, rs, :] + VC2[rs, :]
        mu = _mm_xr(y, ones4, 2) * (1.0 / B_DH)
        yc = y - mu
        var = _mm_xr(yc * yc, ones4, 2) * (1.0 / B_DH)
        yn = yc * lax.rsqrt(var + B_GN_EPS) * lng_ref[...] + lnb_ref[...] + BON[rs, :]
        g = _mm(_sigmoid(lr_ref[rs, 4 * B_RANK:4 * B_RANK + B_GATE_RANK]), gb_ref[...])
        o_ref[rs, :] = yn * g
        return carry

    lax.fori_loop(0, NS * T // RC, post, 0)


def _rwkv_call(T, NS, n_seq, tok0, pm, plr, s0, prm, name):
    rows = NS * T
    blk0 = tok0 // rows
    n_steps = n_seq // NS
    full = lambda shape: pl.BlockSpec(shape, lambda b: (0,) * len(shape))
    big = lambda shape, imap: (pl.BlockSpec(shape, imap, pipeline_mode=pl.Buffered(1)) if n_steps == 1
                               else pl.BlockSpec(shape, imap))
    kern = functools.partial(_rwkv_kernel, T, NS)
    return pl.pallas_call(
        kern,
        grid=(n_steps,),
        in_specs=[
            big((rows, B_W), lambda b: (blk0 + b, COL_B // B_W)),
            big((rows, B_W), lambda b: (blk0 + b, COL_B // B_W + 1)),
            big((rows, B_W), lambda b: (blk0 + b, COL_B // B_W + 2)),
            big((rows, LR_W), lambda b: (blk0 + b, 0)),
            pl.BlockSpec((NS, 2, B_HEADS, B_DH, B_DH), lambda b: (b, 0, 0, 0, 0)),
            full((2, 1, B_W)), full((2, B_RANK, B_W)), full((2, 1, B_W)), full((2, B_RANK, B_W)),
            full((B_GATE_RANK, B_W)),
            full((1, B_W)), full((1, B_W)), full((1, B_W)), full((1, B_W)), full((1, B_W)),
        ],
        out_specs=[
            big((rows, B_W), lambda b: (b, 0)),
            pl.BlockSpec((NS, 2, B_HEADS, B_DH, B_DH), lambda b: (b, 0, 0, 0, 0)),
        ],
        out_shape=[
            jax.ShapeDtypeStruct((n_seq * T, B_W), F32),
            jax.ShapeDtypeStruct((n_seq, 2, B_HEADS, B_DH, B_DH), F32),
        ],
        scratch_shapes=[
            pltpu.VMEM((rows, B_W), F32),
            pltpu.VMEM((2, rows, B_W), F32),
            pltpu.VMEM((2, rows, B_W), F32),
            pltpu.VMEM((2, rows, B_W), F32),
            pltpu.VMEM((2, rows, B_W), F32),
            pltpu.VMEM((rows, B_W), F32),
            pltpu.VMEM((rows, B_W), F32),
            pltpu.VMEM((2, rows, B_W), F32),
            pltpu.VMEM((NS, 2, B_DH, B_W), F32),
        ],
        compiler_params=_cparams(("arbitrary",)),
        name=name,
    )(pm, pm, pm, plr, s0, *prm)


def _gla_kernel(T, NS, q_ref, k_ref, v_ref, og_ref, lr_ref, s0_ref, gb_ref, bias_ref, ng_ref,
                o_ref, sfin_ref, LA, O, S):
    n_chunks = T // C_CHUNK
    nsub = C_CHUNK // C_SUB
    qscale = C_DK ** -0.5
    lr = lr_ref[...]
    for d in range(2):
        gl = _mm(lr, gb_ref[d]) + bias_ref[d]
        LA[d] = (jnp.minimum(gl, 0.0) - jnp.log(1.0 + jnp.exp(-jnp.abs(gl)))) * (1.0 / C_GATE_NORM)
    bd_state = _iota((C_KW, C_VW), 0) // C_DK == _iota((C_KW, C_VW), 1) // C_DV
    chains = [(s, d) for s in range(NS) for d in range(2)]
    for s, d in chains:
        for h in range(C_HEADS):
            pad_l = h * C_DV
            pad_r = C_VW - (h + 1) * C_DV
            blk = s0_ref[s, d, h]
            parts = ([jnp.zeros((C_DK, pad_l), F32)] if pad_l else []) + [blk] + \
                    ([jnp.zeros((C_DK, pad_r), F32)] if pad_r else [])
            S[s, d, h * C_DK:(h + 1) * C_DK, :] = jnp.concatenate(parts, axis=1)

    ti = _iota((C_CHUNK, C_CHUNK), 0)
    si = _iota((C_CHUNK, C_CHUNK), 1)
    tri = ((si <= ti).astype(BF16), (si >= ti).astype(BF16))
    trow = _iota((C_CHUNK, 1), 0)
    mask_k = _iota((C_CHUNK, C_KW), 0) // C_SUB == _iota((C_CHUNK, C_KW), 1) // C_DK
    mask_v = _iota((C_CHUNK, C_VW), 0) // C_SUB == _iota((C_CHUNK, C_VW), 1) // C_DV
    t_att = _iota((C_CHUNK, C_CHUNK), 0)
    s_att = _iota((C_CHUNK, C_CHUNK), 1) % C_SUB
    eye_k = _iota((C_KW, C_KW), 0) == _iota((C_KW, C_KW), 1)

    def body(c, carry):
        cx = []
        for s, d in chains:
            cc = c if d == 0 else n_chunks - 1 - c
            rs = pl.ds(pl.multiple_of(s * T + cc * C_CHUNK, C_CHUNK), C_CHUNK)
            b = _mm_xl(tri[d], LA[d, rs, :], 3)
            cx.append(dict(s=s, d=d, rs=rs, b=b, q=q_ref[rs, :] * qscale, k=k_ref[rs, :], v=v_ref[rs, :]))
        for x in cx:
            x["o"] = _mm(x["q"] * jnp.exp(x["b"]), S[x["s"], x["d"]])
        for j in range(nsub):
            lo, hi = j * C_SUB, (j + 1) * C_SUB
            for x in cx:
                b, q, k = x["b"], x["q"], x["k"]
                if x["d"] == 0:
                    gamma = b[hi - 1:hi, :]
                    row_ok = trow >= lo
                    att_ok = t_att >= lo + s_att
                else:
                    gamma = b[lo:lo + 1, :]
                    row_ok = trow < hi
                    att_ok = t_att <= lo + s_att
                qj = q * jnp.exp(jnp.where(row_ok, b - gamma, NEG_INF))
                kj = k[lo:hi, :] * jnp.exp(gamma - b[lo:hi, :])
                kbd = jnp.where(mask_k, jnp.concatenate([kj] * C_HEADS, axis=0), 0.0)
                x["att"] = jnp.where(att_ok, _mm(qj, kbd, NT), 0.0)
            for x in cx:
                vbd = jnp.where(mask_v, jnp.concatenate([x["v"][lo:hi, :]] * C_HEADS, axis=0), 0.0)
                x["o"] = x["o"] + _mm(x["att"], vbd)
        for x in cx:
            s, d, b = x["s"], x["d"], x["b"]
            O[d, x["rs"], :] = x["o"]
            blast = b[C_CHUNK - 1:C_CHUNK, :] if d == 0 else b[0:1, :]
            kl = x["k"] * jnp.exp(blast - b)
            upd = jnp.where(bd_state, _mm3(kl.T, x["v"]), 0.0)
            dec = jnp.where(eye_k, jnp.exp(blast), 0.0)
            S[s, d] = _mm3(dec, S[s, d]) + upd
        return carry

    def body_group(cg, carry):
        for u in range(GLA_UNROLL):
            body(cg * GLA_UNROLL + u, carry)
        return carry

    lax.fori_loop(0, n_chunks // GLA_UNROLL, body_group, 0)

    for s, d in chains:
        st = S[s, d]
        for h in range(C_HEADS):
            sfin_ref[s, d, h] = st[h * C_DK:(h + 1) * C_DK, h * C_DV:(h + 1) * C_DV]

    ones4 = _block_ones(C_VW, C_DV)
    o = O[0] + O[1]
    ms = _mm_xr(o * o, ones4, 2) * (1.0 / C_DV)
    og = og_ref[...]
    o_ref[...] = o * lax.rsqrt(ms + LN_EPS) * ng_ref[...] * (og * _sigmoid(og))


def _gla_call(T, NS, n_seq, tok0, pm, plr, s0, gb_pad, bias, ng, name):
    rows = NS * T
    blk0 = tok0 // rows
    full = lambda shape: pl.BlockSpec(shape, lambda b: (0,) * len(shape))
    return pl.pallas_call(
        functools.partial(_gla_kernel, T, NS),
        grid=(n_seq // NS,),
        in_specs=[
            pl.BlockSpec((rows, C_KW), lambda b: (blk0 + b, COL_C // C_KW)),
            pl.BlockSpec((rows, C_KW), lambda b: (blk0 + b, COL_C // C_KW + 1)),
            pl.BlockSpec((rows, C_VW), lambda b: (blk0 + b, COL_CV // C_VW)),
            pl.BlockSpec((rows, C_VW), lambda b: (blk0 + b, COL_CV // C_VW + 1)),
            pl.BlockSpec((rows, LANES), lambda b: (blk0 + b, C_LR_COL // LANES)),
            pl.BlockSpec((NS, 2, C_HEADS, C_DK, C_DV), lambda b: (b, 0, 0, 0, 0)),
            full((2, LANES, C_KW)), full((2, 1, C_KW)), full((1, C_VW)),
        ],
        out_specs=[
            pl.BlockSpec((rows, C_VW), lambda b: (b, 0)),
            pl.BlockSpec((NS, 2, C_HEADS, C_DK, C_DV), lambda b: (b, 0, 0, 0, 0)),
        ],
        out_shape=[
            jax.ShapeDtypeStruct((n_seq * T, C_VW), F32),
            jax.ShapeDtypeStruct((n_seq, 2, C_HEADS, C_DK, C_DV), F32),
        ],
        scratch_shapes=[
            pltpu.VMEM((2, rows, C_KW), F32),
            pltpu.VMEM((2, rows, C_VW), F32),
            pltpu.VMEM((NS, 2, C_KW, C_VW), F32),
        ],
        compiler_params=_cparams(("arbitrary",)),
        name=name,
    )(pm, pm, pm, pm, plr, s0, gb_pad, bias, ng)


MERGE_TM = 512


def _merge_kernel(n_x, *refs):
    x_refs = refs[:n_x]
    (g_ref, oac_ref, oad_ref, obc_ref, obd_ref, occ_ref, ocd_ref, mod_ref,
     wa_ref, wb_ref, wc_ref, wo_ref, lg_ref, lb_ref, o_ref, wa_s, wb_s, wc_s, wo_s) = refs[n_x:]

    @pl.when(pl.program_id(0) == 0)
    def _():
        wa_s[...] = wa_ref[...].astype(BF16)
        wb_s[...] = wb_ref[...].astype(BF16)
        wc_s[...] = wc_ref[...].astype(BF16)
        wo_s[...] = wo_ref[...].astype(BF16)

    is_ctx = pl.program_id(0) < N_CTX_TOK // MERGE_TM
    oa = jnp.where(is_ctx, oac_ref[...], oad_ref[...]).astype(BF16)
    ob = jnp.where(is_ctx, obc_ref[...], obd_ref[...]).astype(BF16)
    oc = jnp.where(is_ctx, occ_ref[...], ocd_ref[...]).astype(BF16)
    merged = (g_ref[:, 0:D].astype(F32) * _dg(oa, wa_s[...])
              + g_ref[:, D:2 * D].astype(F32) * _dg(ob, wb_s[...])
              + g_ref[:, 2 * D:3 * D].astype(F32) * _dg(oc, wc_s[...]))
    mix = _dg(merged.astype(BF16), wo_s[...])
    y = DN_ALPHA * _read_x(x_refs, MERGE_TM) + mod_ref[2:3, :] * mix
    o_ref[...] = _layer_norm(y, lg_ref[...], lb_ref[...])


def _merge_call(xs, gates, oa, ob, oc, mod_l, wa, wb, wc, wo, lg, lb, l):
    tm = MERGE_TM
    x_specs = [pl.BlockSpec((tm, D), lambda i: (i, 0))] if len(xs) == 1 else _pair_specs(tm, D)
    return pl.pallas_call(
        functools.partial(_merge_kernel, len(xs)),
        grid=(N_TOK // tm,),
        in_specs=[
            *x_specs,
            pl.BlockSpec((tm, IN_GATE), lambda i: (i, 0)),
            *_pair_specs(tm, A_QW), *_pair_specs(tm, B_W), *_pair_specs(tm, C_VW),
            pl.BlockSpec((None, 6, D), lambda i: (_group_of_tile(i, tm), 0, 0)),
            _layer_resident((A_QW, D), l), _layer_resident((B_W, D), l), _layer_resident((C_VW, D), l),
            _layer_resident((D, D), l), _layer_resident((1, D), l), _layer_resident((1, D), l),
        ],
        out_specs=pl.BlockSpec((tm, D), lambda i: (i, 0)),
        out_shape=jax.ShapeDtypeStruct((N_TOK, D), F32),
        scratch_shapes=[pltpu.VMEM((A_QW, D), BF16), pltpu.VMEM((B_W, D), BF16),
                        pltpu.VMEM((C_VW, D), BF16), pltpu.VMEM((D, D), BF16)],
        compiler_params=_cparams(("arbitrary",)),
        name="merge",
    )(*xs, gates, *oa, *ob, *oc, mod_l, wa, wb, wc, wo, lg.reshape(DEPTH, 1, D), lb.reshape(DEPTH, 1, D))


def _ffn_kernel(x_ref, mod_ref, wg_ref, wu_ref, wd_ref, lg_ref, lb_ref, o_ref):
    x = x_ref[...]
    h = (x * (1.0 + mod_ref[4:5, :]) + mod_ref[3:4, :]).astype(BF16)
    gate = _dg(h, wg_ref[...])
    up = _dg(h, wu_ref[...])
    f = _dg((gate * _sigmoid(gate) * up).astype(BF16), wd_ref[...])
    y = DN_ALPHA * x + mod_ref[5:6, :] * f
    o_ref[...] = _layer_norm(y, lg_ref[...], lb_ref[...])


def _ffn_call(x, mod_l, wg, wu, wd, lg, lb):
    tm = 512
    return pl.pallas_call(
        _ffn_kernel,
        grid=(N_TOK // tm,),
        in_specs=[
            pl.BlockSpec((tm, D), lambda i: (i, 0)),
            pl.BlockSpec((None, 6, D), lambda i: (_group_of_tile(i, tm), 0, 0)),
            _resident((D, D_FF)), _resident((D, D_FF)), _resident((D_FF, D)),
            _resident((1, D)), _resident((1, D)),
        ],
        out_specs=pl.BlockSpec((tm, D), lambda i: (i, 0)),
        out_shape=jax.ShapeDtypeStruct((N_TOK, D), F32),
        compiler_params=_cparams(("arbitrary",)),
        name="ffn",
    )(x, mod_l, wg, wu, wd, lg.reshape(1, D), lb.reshape(1, D))


MOE_TM = 1024
MOE_TR = 256
MOE_RS = 3072
MOE_TF = 512
MOE_ROWS = 2 * N_TOK + N_EXP * MOE_TR
MOE_NST = -(-MOE_ROWS // MOE_RS) + N_EXP
R_I1, R_I2, R_W1, R_W2, R_RANK1, R_RANK2 = range(6)


def _moe_route_kernel(x_ref, mod_ref, wr_ref, h_ref, info_ref, cnt_ref, carry_s):
    tm = MOE_TM

    @pl.when(pl.program_id(0) == 0)
    def _():
        carry_s[...] = jnp.zeros_like(carry_s)

    h = x_ref[...] * (1.0 + mod_ref[4:5, :]) + mod_ref[3:4, :]
    h_ref[...] = h
    logits = _mm3(h, wr_ref[...])
    lane = _iota(logits.shape, 1)
    logits = jnp.where(lane < N_EXP, logits, NEG_INF)
    v1 = jnp.max(logits, -1, keepdims=True)
    i1 = jnp.min(jnp.where(logits == v1, lane, LANES), -1, keepdims=True)
    rest = jnp.where(lane == i1, NEG_INF, logits)
    v2 = jnp.max(rest, -1, keepdims=True)
    i2 = jnp.min(jnp.where(rest == v2, lane, LANES), -1, keepdims=True)
    e2 = jnp.exp(v2 - v1)
    w1 = 1.0 / (1.0 + e2)
    w2 = e2 / (1.0 + e2)
    oh1 = lane == i1
    oh2 = lane == i2
    cnt = oh1.astype(F32) + oh2.astype(F32)
    earlier = (_iota((tm, tm), 1) < _iota((tm, tm), 0)).astype(BF16)
    before = _dg(earlier, cnt.astype(BF16)) + carry_s[...]
    rank1 = jnp.sum(jnp.where(oh1, before, 0.0), -1, keepdims=True)
    rank2 = jnp.sum(jnp.where(oh2, before, 0.0), -1, keepdims=True)
    info = jnp.zeros(logits.shape, F32)
    for ln, val in ((R_I1, i1.astype(F32)), (R_I2, i2.astype(F32)), (R_W1, w1), (R_W2, w2),
                    (R_RANK1, rank1), (R_RANK2, rank2)):
        info = jnp.where(lane == ln, val, info)
    info_ref[...] = info
    carry_s[...] += jnp.sum(cnt, axis=0, keepdims=True)
    cnt_ref[...] = carry_s[...]


def _moe_route_call(x, mod_l, wr_pad, i_moe):
    tm = MOE_TM
    return pl.pallas_call(
        _moe_route_kernel,
        grid=(N_TOK // tm,),
        in_specs=[
            pl.BlockSpec((tm, D), lambda i: (i, 0)),
            pl.BlockSpec((None, 6, D), lambda i: (_group_of_tile(i, tm), 0, 0)),
            pl.BlockSpec((None, D, LANES), lambda i: (i_moe, 0, 0)),
        ],
        out_specs=[
            pl.BlockSpec((tm, D), lambda i: (i, 0)),
            pl.BlockSpec((tm, LANES), lambda i: (i, 0)),
            pl.BlockSpec((1, LANES), lambda i: (0, 0)),
        ],
        out_shape=[
            jax.ShapeDtypeStruct((N_TOK, D), F32),
            jax.ShapeDtypeStruct((N_TOK, LANES), F32),
            jax.ShapeDtypeStruct((1, LANES), F32),
        ],
        scratch_shapes=[pltpu.VMEM((1, LANES), F32)],
        compiler_params=_cparams(("arbitrary",)),
        name="moe_route",
    )(x, mod_l, wr_pad)


def _moe_plan(info, cnt):
    i32 = jnp.int32
    i1 = info[:, R_I1].astype(i32)
    i2 = info[:, R_I2].astype(i32)
    counts = cnt[0, :N_EXP].astype(i32)
    padded = (counts + MOE_TR - 1) // MOE_TR * MOE_TR
    seg_start = jnp.cumsum(padded) - padded
    pos1 = seg_start[i1] + info[:, R_RANK1].astype(i32)
    pos2 = seg_start[i2] + info[:, R_RANK2].astype(i32)
    dst = jnp.zeros((MOE_ROWS,), i32).at[jnp.concatenate([pos1, pos2])].set(jnp.arange(2 * N_TOK, dtype=i32))
    n_pass = (padded + MOE_RS - 1) // MOE_RS
    pass_end = jnp.cumsum(n_pass)
    total = pass_end[-1]
    sidx = jnp.arange(MOE_NST, dtype=i32)
    used = sidx < total
    e_of = jnp.minimum(jnp.searchsorted(pass_end, jnp.minimum(sidx, total - 1), side="right"), N_EXP - 1).astype(i32)
    k = jnp.minimum(sidx, total - 1) - (pass_end - n_pass)[e_of]
    row0 = seg_start[e_of] + k * MOE_RS
    nrows = jnp.where(used, jnp.clip(padded[e_of] - k * MOE_RS, 0, MOE_RS), 0)
    nvalid = jnp.where(used, jnp.clip(counts[e_of] - k * MOE_RS, 0, MOE_RS), 0)
    src = jnp.where(dst >= N_TOK, dst - N_TOK, dst)
    return src, dst, e_of, row0.astype(i32), nrows.astype(i32), nvalid.astype(i32)


def _moe_expert_kernel(src_ref, dst_ref, exp_ref, row0_ref, nrows_ref, nvalid_ref,
                       h_hbm, wg_ref, wu_ref, wd_ref, yo_hbm, xs, xb, acc, wgb, wub, wdb, gsem, ssem):
    s = pl.program_id(0)
    f = pl.program_id(1)
    nf = pl.num_programs(1)
    nrows = pl.multiple_of(nrows_ref[s], MOE_TR)
    nvalid = nvalid_ref[s]
    row0 = row0_ref[s]
    n_chunks = nrows // MOE_TR

    def row_copy(src, dst, sem):
        return pltpu.make_async_copy(src, dst, sem)

    def hbm_row(ref, i):
        return ref.at[pl.ds(i, 1), :]

    def wait_rows(buf, n_groups, sem):
        pltpu.make_async_copy(buf.at[pl.ds(0, n_groups)], buf.at[pl.ds(0, n_groups)], sem).wait()

    @pl.when((f == 0) & (nrows > 0))
    def _gather():
        def issue(g, carry):
            for u in range(SUBLANES):
                tok = src_ref[row0 + g * SUBLANES + u]
                row_copy(hbm_row(h_hbm, tok), xs.at[g, pl.ds(u, 1), :], gsem).start()
            return carry

        ng = nrows // SUBLANES
        lax.fori_loop(0, ng, issue, 0)
        wait_rows(xs, ng, gsem)

        def cvt(c, carry):
            g0 = pl.multiple_of(c * (MOE_TR // SUBLANES), MOE_TR // SUBLANES)
            rs = pl.ds(pl.multiple_of(c * MOE_TR, MOE_TR), MOE_TR)
            xb[rs, :] = xs[pl.ds(g0, MOE_TR // SUBLANES)].reshape(MOE_TR, D).astype(BF16)
            return carry

        lax.fori_loop(0, n_chunks, cvt, 0)

    @pl.when(nrows > 0)
    def _compute():
        wgb[...] = wg_ref[...].astype(BF16)
        wub[...] = wu_ref[...].astype(BF16)
        wdb[...] = wd_ref[...].astype(BF16)

        def chunk(start, n):
            x = xb[pl.ds(pl.multiple_of(start, MOE_TR), n), :]
            gate = _dg(x, wgb[...])
            up = _dg(x, wub[...])
            y = _dg((gate * _sigmoid(gate) * up).astype(BF16), wdb[...]).reshape(n // SUBLANES, SUBLANES, D)
            gs = pl.ds(pl.multiple_of(start // SUBLANES, MOE_TR // SUBLANES), n // SUBLANES)

            @pl.when(f == 0)
            def _():
                acc[gs] = y

            @pl.when(f > 0)
            def _():
                acc[gs] += y

        def chunk_quad(c4, carry):
            chunk(c4 * (4 * MOE_TR), 4 * MOE_TR)
            return carry

        lax.fori_loop(0, n_chunks // 4, chunk_quad, 0)
        done = n_chunks // 4 * 4

        @pl.when(n_chunks % 4 >= 2)
        def _():
            chunk(done * MOE_TR, 2 * MOE_TR)

        @pl.when(n_chunks % 2 == 1)
        def _():
            chunk((n_chunks - 1) * MOE_TR, MOE_TR)

    @pl.when((f == nf - 1) & (nvalid > 0))
    def _scatter():
        n8 = nvalid // SUBLANES

        def issue(g, carry):
            for u in range(SUBLANES):
                d = dst_ref[row0 + g * SUBLANES + u]
                row_copy(acc.at[g, pl.ds(u, 1), :], hbm_row(yo_hbm, d), ssem).start()
            return carry

        def issue_tail(r, carry):
            d = dst_ref[row0 + r]
            row_copy(acc.at[n8, pl.ds(r - n8 * SUBLANES, 1), :], hbm_row(yo_hbm, d), ssem).start()
            return carry

        lax.fori_loop(0, n8, issue, 0)
        lax.fori_loop(n8 * SUBLANES, nvalid, issue_tail, 0)

        @pl.when(n8 > 0)
        def _():
            wait_rows(acc, n8, ssem)

        def wait_one(r, carry):
            row_copy(acc.at[0, pl.ds(0, 1), :], hbm_row(yo_hbm, 0), ssem).wait()
            return carry

        lax.fori_loop(n8 * SUBLANES, nvalid, wait_one, 0)


def _moe_expert_call(h, plan, wg, wu, wd, i_moe):
    nf = D_FFE // MOE_TF

    def wspec(shape, fdim):
        def imap(s, f, src, dst, exp, row0, nrows, nvalid):
            fe = jnp.where(nrows[s] > 0, f, nf - 1)
            return (i_moe, exp[s], 0, fe) if fdim == 3 else (i_moe, exp[s], fe, 0)
        return pl.BlockSpec(shape, imap)

    grid_spec = pltpu.PrefetchScalarGridSpec(
        num_scalar_prefetch=6,
        grid=(MOE_NST, nf),
        in_specs=[
            pl.BlockSpec(memory_space=pl.ANY),
            wspec((None, None, D, MOE_TF), 3),
            wspec((None, None, D, MOE_TF), 3),
            wspec((None, None, MOE_TF, D), 2),
        ],
        out_specs=pl.BlockSpec(memory_space=pl.ANY),
        scratch_shapes=[
            pltpu.VMEM((MOE_RS // SUBLANES, SUBLANES, D), F32),
            pltpu.VMEM((MOE_RS, D), BF16),
            pltpu.VMEM((MOE_RS // SUBLANES, SUBLANES, D), F32),
            pltpu.VMEM((D, MOE_TF), BF16), pltpu.VMEM((D, MOE_TF), BF16), pltpu.VMEM((MOE_TF, D), BF16),
            pltpu.SemaphoreType.DMA(()), pltpu.SemaphoreType.DMA(()),
        ],
    )
    return pl.pallas_call(
        _moe_expert_kernel,
        grid_spec=grid_spec,
        out_shape=jax.ShapeDtypeStruct((2 * N_TOK, D), F32),
        compiler_params=pltpu.CompilerParams(dimension_semantics=("arbitrary", "arbitrary"),
                                             vmem_limit_bytes=VMEM_LIMIT, disable_bounds_checks=True),
        name="moe_experts",
    )(*plan, h, wg, wu, wd)


def _moe_combine_kernel(x_ref, y1_ref, y2_ref, info_ref, mod_ref, lg_ref, lb_ref, oc_ref, od_ref):
    f = info_ref[:, R_W1:R_W1 + 1] * y1_ref[...] + info_ref[:, R_W2:R_W2 + 1] * y2_ref[...]
    y = DN_ALPHA * x_ref[...] + mod_ref[5:6, :] * f
    out = _layer_norm(y, lg_ref[...], lb_ref[...])
    is_ctx = pl.program_id(0) < N_CTX_TOK // MOE_TM

    @pl.when(is_ctx)
    def _():
        oc_ref[...] = out

    @pl.when(jnp.logical_not(is_ctx))
    def _():
        od_ref[...] = out


def _moe_combine_call(x, yo, info, mod_l, lg, lb):
    tm = MOE_TM
    nt = N_TOK // tm
    return pl.pallas_call(
        _moe_combine_kernel,
        grid=(nt,),
        in_specs=[
            pl.BlockSpec((tm, D), lambda i: (i, 0)),
            pl.BlockSpec((tm, D), lambda i: (i, 0)),
            pl.BlockSpec((tm, D), lambda i: (nt + i, 0)),
            pl.BlockSpec((tm, LANES), lambda i: (i, 0)),
            pl.BlockSpec((None, 6, D), lambda i: (_group_of_tile(i, tm), 0, 0)),
            _resident((1, D)), _resident((1, D)),
        ],
        out_specs=_pair_specs(tm, D),
        out_shape=[jax.ShapeDtypeStruct((N_CTX_TOK, D), F32), jax.ShapeDtypeStruct((N_DEC_B * DEC_T, D), F32)],
        compiler_params=_cparams(("arbitrary",)),
        name="moe_combine",
    )(x, yo, yo, info, mod_l, lg.reshape(1, D), lb.reshape(1, D))


def _moe_call(x, mod_l, wr_pad, wg, wu, wd, lg, lb, i_moe):
    h, info, cnt = _moe_route_call(x, mod_l, wr_pad, i_moe)
    yo = _moe_expert_call(h, _moe_plan(info, cnt), wg, wu, wd, i_moe)
    return _moe_combine_call(x, yo, info, mod_l, lg, lb)


def kernel(x_prompt, x_sample, cache_attn_k, cache_attn_v, state_rwkv, state_gla, c, c_ctx, w_ada, b_ada, w_in,
           attn_sink, rwkv_w0, rwkv_w_a, rwkv_w_b, rwkv_a0, rwkv_a_a, rwkv_a_b, rwkv_g_a, rwkv_g_b, rwkv_k_k,
           rwkv_k_a, rwkv_r_k, rwkv_ln_g, rwkv_ln_b, gla_gate_a, gla_gate_b, gla_gate_bias, gla_norm_g, w_up_a,
           w_up_b, w_up_c, w_out, ln1_g, ln1_b, ln2_g, ln2_b, ffn_w_gate, ffn_w_up, ffn_w_down, moe_router,
           moe_w_gate, moe_w_up, moe_w_down):
    cvec = jnp.concatenate([c_ctx[None, :], c, jnp.zeros((N_GROUPS - 1 - N_DEC_B, D), F32)], axis=0)
    mods = _ada_call(cvec, w_ada, b_ada).reshape(DEPTH, N_GROUPS, 6, D)
    xs = (x_prompt.reshape(N_CTX_TOK, D), x_sample.reshape(N_DEC_B * DEC_T, D))
    cos, sin = _rope_tables()
    kc_all = cache_attn_k.reshape(N_DEC_B, DEPTH, PAST, A_KV * A_DH)
    vc_all = cache_attn_v.reshape(N_DEC_B, DEPTH, PAST, A_KV * A_DH)
    zeros_r = jnp.zeros((N_CTX_B, 2, B_HEADS, B_DH, B_DH), F32)
    zeros_g = jnp.zeros((N_CTX_B, 2, C_HEADS, C_DK, C_DV), F32)

    new_k, new_v, new_sr, new_sg = [], [], [], []
    for l in range(DEPTH):
        mod_l = mods[l]
        w_lr = jnp.concatenate(
            [rwkv_w_a[l, 0], rwkv_w_a[l, 1], rwkv_a_a[l, 0], rwkv_a_a[l, 1], rwkv_g_a[l],
             gla_gate_a[l, 0], gla_gate_a[l, 1], jnp.zeros((D, LR_W - C_LR_COL - 2 * C_GATE_RANK), F32)], axis=1).astype(BF16)
        pm, gates, plr, k_ctx, v_ctx = _inproj_call(xs, mod_l, w_in, w_lr, l)

        sink_l = attn_sink[l]
        oa = (_attn_ctx_call(sink_l, pm), _attn_lat_call(sink_l, pm, kc_all, vc_all, l, cos, sin))

        rprm = (rwkv_w0[l].reshape(2, 1, B_W), rwkv_w_b[l], rwkv_a0[l].reshape(2, 1, B_W), rwkv_a_b[l],
                rwkv_g_b[l], rwkv_k_k[l].reshape(1, B_W), rwkv_k_a[l].reshape(1, B_W),
                rwkv_r_k[l].reshape(1, B_W), rwkv_ln_g[l].reshape(1, B_W), rwkv_ln_b[l].reshape(1, B_W))
        ob_c, sr_c = _rwkv_call(CTX_T, 4, N_CTX_B, 0, pm, plr, zeros_r, rprm, "rwkv_ctx")
        ob_d, _ = _rwkv_call(DEC_T, 2, N_DEC_B, N_CTX_TOK, pm, plr, state_rwkv[:, l], rprm, "rwkv_lat")

        gb_pad = jnp.zeros((2, LANES, C_KW), F32)
        gb_pad = gb_pad.at[0, 0:C_GATE_RANK].set(gla_gate_b[l, 0]).at[1, C_GATE_RANK:2 * C_GATE_RANK].set(
            gla_gate_b[l, 1])
        gbias = gla_gate_bias[l].reshape(2, 1, C_KW)
        ng = jnp.tile(gla_norm_g[l], C_HEADS).reshape(1, C_VW)
        oc_c, sg_c = _gla_call(CTX_T, 4, N_CTX_B, 0, pm, plr, zeros_g, gb_pad, gbias, ng, "gla_ctx")
        oc_d, _ = _gla_call(DEC_T, 2, N_DEC_B, N_CTX_TOK, pm, plr, state_gla[:, l], gb_pad, gbias, ng, "gla_lat")

        x = _merge_call(xs, gates, oa, (ob_c, ob_d), (oc_c, oc_d), mod_l, w_up_a, w_up_b, w_up_c, w_out,
                        ln1_g, ln1_b, l)
        if l % 2 == 0:
            i_ffn = l // 2
            xs = (_ffn_call(x, mod_l, ffn_w_gate[i_ffn].astype(BF16), ffn_w_up[i_ffn].astype(BF16),
                            ffn_w_down[i_ffn].astype(BF16), ln2_g[l], ln2_b[l]),)
        else:
            wr_pad = jnp.concatenate([moe_router, jnp.zeros((moe_router.shape[0], D, LANES - N_EXP), F32)], axis=2)
            xs = _moe_call(x, mod_l, wr_pad, moe_w_gate, moe_w_up, moe_w_down, ln2_g[l], ln2_b[l], l // 2)

        new_k.append(k_ctx.reshape(N_CTX_B, CTX_T, A_KV, A_DH))
        new_v.append(v_ctx.reshape(N_CTX_B, CTX_T, A_KV, A_DH))
        new_sr.append(sr_c)
        new_sg.append(sg_c)

    if len(xs) == 1:
        xs = (xs[0][:N_CTX_TOK], xs[0][N_CTX_TOK:])
    y_prompt = xs[0].reshape(N_CTX_B, CTX_T, D)
    y_sample = xs[1].reshape(N_DEC_B, DEC_T, D)
    return (y_prompt, y_sample, jnp.stack(new_k, axis=1), jnp.stack(new_v, axis=1),
            jnp.stack(new_sr, axis=1), jnp.stack(new_sg, axis=1))
```

```python
import functools

import numpy as np
import jax
import jax.numpy as jnp
from jax import lax
from jax.experimental import pallas as pl
from jax.experimental.pallas import tpu as pltpu

D = 1024
N_CTX_B, CTX_T = 32, 256
N_DEC_B, DEC_T = 2, 1024
N_CTX_TOK = N_CTX_B * CTX_T
N_TOK = N_CTX_TOK + N_DEC_B * DEC_T
DEPTH = 2
PAST = 256
GRID_W = 64
A_HEADS, A_KV, A_DH = 8, 2, 64
A_G = A_HEADS // A_KV
A_WIN, A_BLK = 128, 128
ROPE_BASE = 10000.0
B_HEADS, B_DH = 4, 64
B_W = B_HEADS * B_DH
B_GN_EPS = 64e-5
C_HEADS, C_DK, C_DV = 4, 32, 64
C_KW, C_VW = C_HEADS * C_DK, C_HEADS * C_DV
C_GATE_RANK = 16
C_GATE_NORM = 16.0
C_CHUNK = 64
C_SUB = 16
GLA_UNROLL = 2
D_FF = 2816
N_EXP = 8
D_FFE = 3584
LN_EPS = 1e-5
DN_ALPHA = (2.0 * DEPTH) ** 0.25
NEG_INF = -1e30
A_QW, A_KVW = A_HEADS * A_DH, A_KV * A_DH
B_RANK, B_GATE_RANK = 64, 128
LANES = 128
IN_MAIN = 2304
COL_B = A_QW + 2 * A_KVW
COL_C = COL_B + 3 * B_W
COL_CV = COL_C + 2 * C_KW
IN_GATE = 3072
LR_W = 512
C_LR_COL = 4 * B_RANK + B_GATE_RANK
N_GROUPS = 8

F32 = jnp.float32
BF16 = jnp.bfloat16
VMEM_LIMIT = 56 * 1024 * 1024
SUBLANES = 8

NN = ((1,), (0,))
NT = ((1,), (1,))
TN = ((0,), (0,))


def _dg(a, b, dims=NN):
    return lax.dot_general(a, b, (dims, ((), ())), preferred_element_type=F32)


def _split2(x):
    hi = x.astype(BF16)
    lo = (x - hi.astype(F32)).astype(BF16)
    return hi, lo


def _split3(x):
    hi = x.astype(BF16)
    r = x - hi.astype(F32)
    mid = r.astype(BF16)
    lo = (r - mid.astype(F32)).astype(BF16)
    return hi, mid, lo


def _mm(a, b, dims=NN):
    return _dg(a.astype(BF16), b.astype(BF16), dims)


def _mm3(a, b, dims=NN):
    ah, al = _split2(a)
    bh, bl = _split2(b)
    return _dg(ah, bh, dims) + (_dg(ah, bl, dims) + _dg(al, bh, dims))


def _mm_xr(a, b_exact, passes, dims=NN):
    parts = (a.astype(BF16),) if passes == 1 else (_split2(a) if passes == 2 else _split3(a))
    out = _dg(parts[0], b_exact, dims)
    for p in parts[1:]:
        out = out + _dg(p, b_exact, dims)
    return out


def _mm_xl(a_exact, b, passes):
    parts = (b.astype(BF16),) if passes == 1 else (_split2(b) if passes == 2 else _split3(b))
    out = _dg(a_exact, parts[0])
    for p in parts[1:]:
        out = out + _dg(a_exact, p)
    return out


def _sigmoid(x):
    return 1.0 / (1.0 + jnp.exp(-x))


def _iota(shape, dim):
    return lax.broadcasted_iota(jnp.int32, shape, dim)


def _block_ones(n, blk):
    return (_iota((n, n), 0) // blk == _iota((n, n), 1) // blk).astype(BF16)


def _layer_norm(y, g, b):
    mu = jnp.mean(y, -1, keepdims=True)
    yc = y - mu
    var = jnp.mean(yc * yc, -1, keepdims=True)
    return yc * lax.rsqrt(var + LN_EPS) * g + b


def _group_of_tile(i, tm):
    n_ctx = N_CTX_TOK // tm
    per_dec = DEC_T // tm
    return jnp.where(i < n_ctx, 0, 1 + (i - n_ctx) // per_dec)


def _cparams(sem):
    return pltpu.CompilerParams(dimension_semantics=sem, vmem_limit_bytes=VMEM_LIMIT)


def _ada_kernel(c_ref, w_ref, b_ref, o_ref):
    c = c_ref[...]
    s = c * _sigmoid(c)
    o_ref[...] = _mm3(s, w_ref[...]) + b_ref[...]


def _ada_call(cvec, w_ada, b_ada):
    tn = 1536
    return pl.pallas_call(
        _ada_kernel,
        grid=(DEPTH, 6 * D // tn),
        in_specs=[
            pl.BlockSpec((N_GROUPS, D), lambda l, j: (0, 0)),
            pl.BlockSpec((None, D, tn), lambda l, j: (l, 0, j)),
            pl.BlockSpec((None, 1, tn), lambda l, j: (l, 0, j)),
        ],
        out_specs=pl.BlockSpec((None, N_GROUPS, tn), lambda l, j: (l, 0, j)),
        out_shape=jax.ShapeDtypeStruct((DEPTH, N_GROUPS, 6 * D), F32),
        compiler_params=_cparams(("arbitrary", "arbitrary")),
        name="ada",
    )(cvec, w_ada, b_ada.reshape(DEPTH, 1, 6 * D))


def _pair_specs(tm, width):
    n_ctx = N_CTX_TOK // tm
    return [pl.BlockSpec((tm, width), lambda i: (jnp.minimum(i, n_ctx - 1), 0)),
            pl.BlockSpec((tm, width), lambda i: (jnp.maximum(i - n_ctx, 0), 0))]


def _read_x(x_refs, tm):
    if len(x_refs) == 1:
        return x_refs[0][...]
    return jnp.where(pl.program_id(0) < N_CTX_TOK // tm, x_refs[0][...], x_refs[1][...])


INPROJ_TM = 256


W_IN_TILE = 768
N_MAIN_TILES, N_GATE_TILES = IN_MAIN // W_IN_TILE, IN_GATE // W_IN_TILE


def _inproj_kernel(n_x, *refs):
    n_w = N_MAIN_TILES + N_GATE_TILES
    x_refs = refs[:n_x]
    mod_ref = refs[n_x]
    w_refs = refs[n_x + 1:n_x + 1 + n_w]
    wl_ref, om_ref, og_ref, ol_ref, ok_ref, ov_ref, wm_s, wg_s = refs[n_x + 1 + n_w:]

    @pl.when(pl.program_id(0) == 0)
    def _():
        for j, w_ref in enumerate(w_refs):
            dst, jj = (wm_s, j) if j < N_MAIN_TILES else (wg_s, j - N_MAIN_TILES)
            dst[:, jj * W_IN_TILE:(jj + 1) * W_IN_TILE] = w_ref[...].astype(BF16)

    sh = mod_ref[0:1, :]
    sc = mod_ref[1:2, :]
    h = (_read_x(x_refs, INPROJ_TM) * (1.0 + sc) + sh).astype(BF16)
    main = _dg(h, wm_s[...])
    om_ref[...] = main
    og_ref[...] = _sigmoid(_dg(h, wg_s[...])).astype(BF16)
    ol_ref[...] = _dg(h, wl_ref[...])

    @pl.when(pl.program_id(0) < N_CTX_TOK // INPROJ_TM)
    def _():
        ok_ref[...] = main[:, A_QW:A_QW + A_KVW]
        ov_ref[...] = main[:, A_QW + A_KVW:A_QW + 2 * A_KVW]


def _resident(shape):
    return pl.BlockSpec(shape, lambda *_: (0,) * len(shape), pipeline_mode=pl.Buffered(1))


def _layer_resident(shape, l):
    return pl.BlockSpec((None, *shape), lambda *_: (l,) + (0,) * len(shape), pipeline_mode=pl.Buffered(1))


def _inproj_call(xs, mod_l, w_in, w_lr, l):
    tm = INPROJ_TM
    n_ctx = N_CTX_TOK // tm
    x_specs = [pl.BlockSpec((tm, D), lambda i: (i, 0))] if len(xs) == 1 else _pair_specs(tm, D)
    kv_spec = pl.BlockSpec((tm, A_KVW), lambda i: (jnp.minimum(i, n_ctx - 1), 0))
    n_w = N_MAIN_TILES + N_GATE_TILES
    w_specs = [pl.BlockSpec((None, D, W_IN_TILE), lambda i, j=j: (l, 0, j), pipeline_mode=pl.Buffered(1))
               for j in range(n_w)]
    return pl.pallas_call(
        functools.partial(_inproj_kernel, len(xs)),
        grid=(N_TOK // tm,),
        in_specs=[
            *x_specs,
            pl.BlockSpec((None, 6, D), lambda i: (_group_of_tile(i, tm), 0, 0)),
            *w_specs,
            _resident((D, LR_W)),
        ],
        out_specs=[
            pl.BlockSpec((tm, IN_MAIN), lambda i: (i, 0)),
            pl.BlockSpec((tm, IN_GATE), lambda i: (i, 0)),
            pl.BlockSpec((tm, LR_W), lambda i: (i, 0)),
            kv_spec, kv_spec,
        ],
        out_shape=[
            jax.ShapeDtypeStruct((N_TOK, IN_MAIN), F32),
            jax.ShapeDtypeStruct((N_TOK, IN_GATE), BF16),
            jax.ShapeDtypeStruct((N_TOK, LR_W), F32),
            jax.ShapeDtypeStruct((N_CTX_TOK, A_KVW), F32),
            jax.ShapeDtypeStruct((N_CTX_TOK, A_KVW), F32),
        ],
        scratch_shapes=[pltpu.VMEM((D, IN_MAIN), BF16), pltpu.VMEM((D, IN_GATE), BF16)],
        compiler_params=_cparams(("arbitrary",)),
        name="inproj",
    )(*xs, mod_l, *([w_in] * n_w), w_lr)


def _sink_col(sink_ref, kvh, rows_per_head):
    n = A_G * rows_per_head
    r = _iota((n, 1), 0) // rows_per_head
    col = jnp.full((n, 1), sink_ref[kvh * A_G], F32)
    for g in range(1, A_G):
        col = jnp.where(r == g, sink_ref[kvh * A_G + g], col)
    return col


def _attn_ctx_kernel(sink_ref, q_ref, k_ref, v_ref, o_ref):
    scale = A_DH ** -0.5
    for kvh in range(A_KV):
        ks = k_ref[:, kvh * A_DH:(kvh + 1) * A_DH].astype(BF16)
        vs = v_ref[:, kvh * A_DH:(kvh + 1) * A_DH].astype(BF16)
        q4 = jnp.concatenate(
            [q_ref[:, (kvh * A_G + g) * A_DH:(kvh * A_G + g + 1) * A_DH] for g in range(A_G)], axis=0)
        s = _dg(q4.astype(BF16), ks, NT) * scale
        sink = _sink_col(sink_ref, kvh, CTX_T)
        m = jnp.maximum(jnp.max(s, -1, keepdims=True), sink)
        e = jnp.exp(s - m)
        p = e / (jnp.sum(e, -1, keepdims=True) + jnp.exp(sink - m))
        o = _dg(p.astype(BF16), vs)
        for g in range(A_G):
            h = kvh * A_G + g
            o_ref[:, h * A_DH:(h + 1) * A_DH] = o[g * CTX_T:(g + 1) * CTX_T, :]


def _attn_ctx_call(sink_l, pm):
    return pl.pallas_call(
        _attn_ctx_kernel,
        grid=(N_CTX_B,),
        in_specs=[
            pl.BlockSpec(memory_space=pltpu.SMEM),
            pl.BlockSpec((CTX_T, A_QW), lambda b: (b, 0)),
            pl.BlockSpec((CTX_T, A_KVW), lambda b: (b, A_QW // A_KVW)),
            pl.BlockSpec((CTX_T, A_KVW), lambda b: (b, A_QW // A_KVW + 1)),
        ],
        out_specs=pl.BlockSpec((CTX_T, A_QW), lambda b: (b, 0)),
        out_shape=jax.ShapeDtypeStruct((N_CTX_TOK, A_QW), F32),
        compiler_params=_cparams(("arbitrary",)),
        name="attn_ctx",
    )(sink_l, pm, pm, pm)


def _rope(x, cos, sin_signed):
    w = x.shape[-1]
    lane = _iota(x.shape, 1)
    q = A_DH // 4
    partner = jnp.where((lane % (2 * q)) < q, pltpu.roll(x, w - q, 1), pltpu.roll(x, q, 1))
    return x * cos + partner * sin_signed


def _attn_lat_kernel(sink_ref, q_ref, k_ref, v_ref, kc_ref, vc_ref, cos_ref, sin_ref, o_ref, kr_ref):
    n = pl.program_id(1)
    scale = A_DH ** -0.5

    @pl.when(n == 0)
    def _():
        kr_ref[...] = _rope(k_ref[...], cos_ref[:, 0:A_KVW], sin_ref[:, 0:A_KVW]).astype(BF16)

    q0 = pl.multiple_of(n * A_BLK, A_BLK)
    qr = _rope(q_ref[...], cos_ref[pl.ds(q0, A_BLK), :], sin_ref[pl.ds(q0, A_BLK), :])
    kstart = pl.multiple_of(jnp.clip((n - 1) * A_BLK, 0, DEC_T - 3 * A_BLK), A_BLK)
    kwin = kr_ref[pl.ds(kstart, 3 * A_BLK), :]
    vwin = v_ref[pl.ds(kstart, 3 * A_BLK), :].astype(BF16)
    kc = kc_ref[...].astype(BF16)
    vc = vc_ref[...].astype(BF16)
    rows = A_G * A_BLK
    qpos = q0 + _iota((rows, 3 * A_BLK), 0) % A_BLK
    kpos = kstart + _iota((rows, 3 * A_BLK), 1)
    valid = jnp.abs(qpos - kpos) <= A_WIN
    for kvh in range(A_KV):
        cs = slice(kvh * A_DH, (kvh + 1) * A_DH)
        q4 = jnp.concatenate(
            [qr[:, (kvh * A_G + g) * A_DH:(kvh * A_G + g + 1) * A_DH] for g in range(A_G)], axis=0).astype(BF16)
        s_loc = jnp.where(valid, _dg(q4, kwin[:, cs], NT) * scale, NEG_INF)
        s_ctx = _dg(q4, kc[:, cs], NT) * scale
        sink = _sink_col(sink_ref, kvh, A_BLK)
        m = jnp.maximum(jnp.maximum(jnp.max(s_loc, -1, keepdims=True), jnp.max(s_ctx, -1, keepdims=True)), sink)
        e_loc = jnp.exp(s_loc - m)
        e_ctx = jnp.exp(s_ctx - m)
        inv = 1.0 / (jnp.sum(e_loc, -1, keepdims=True) + jnp.sum(e_ctx, -1, keepdims=True) + jnp.exp(sink - m))
        o = _dg((e_loc * inv).astype(BF16), vwin[:, cs]) + _dg((e_ctx * inv).astype(BF16), vc[:, cs])
        for g in range(A_G):
            h = kvh * A_G + g
            o_ref[:, h * A_DH:(h + 1) * A_DH] = o[g * A_BLK:(g + 1) * A_BLK, :]


def _rope_tables():
    half = A_DH // 2
    t = np.arange(DEC_T)
    rows = (t // GRID_W).astype(np.float32)
    cols = (t % GRID_W).astype(np.float32)
    inv_freq = (ROPE_BASE ** (-np.arange(0, half, 2, dtype=np.float32) / half)).astype(np.float32)
    ang_r = rows[:, None] * inv_freq[None, :]
    ang_c = cols[:, None] * inv_freq[None, :]
    cos = np.concatenate([np.cos(ang_r), np.cos(ang_r), np.cos(ang_c), np.cos(ang_c)], -1)
    sin = np.concatenate([-np.sin(ang_r), np.sin(ang_r), -np.sin(ang_c), np.sin(ang_c)], -1)
    return (jnp.asarray(np.tile(cos, (1, A_HEADS)), F32), jnp.asarray(np.tile(sin, (1, A_HEADS)), F32))


def _attn_lat_call(sink_l, pm, kc, vc, l, cos, sin):
    nb = DEC_T // A_BLK
    row0 = N_CTX_TOK // A_BLK
    seq0 = N_CTX_TOK // DEC_T
    return pl.pallas_call(
        _attn_lat_kernel,
        grid=(N_DEC_B, nb),
        in_specs=[
            pl.BlockSpec(memory_space=pltpu.SMEM),
            pl.BlockSpec((A_BLK, A_QW), lambda b, n: (row0 + b * nb + n, 0)),
            pl.BlockSpec((DEC_T, A_KVW), lambda b, n: (seq0 + b, A_QW // A_KVW)),
            pl.BlockSpec((DEC_T, A_KVW), lambda b, n: (seq0 + b, A_QW // A_KVW + 1)),
            pl.BlockSpec((None, None, PAST, A_KVW), lambda b, n: (b, l, 0, 0)),
            pl.BlockSpec((None, None, PAST, A_KVW), lambda b, n: (b, l, 0, 0)),
            pl.BlockSpec((DEC_T, A_QW), lambda b, n: (0, 0)),
            pl.BlockSpec((DEC_T, A_QW), lambda b, n: (0, 0)),
        ],
        out_specs=pl.BlockSpec((A_BLK, A_QW), lambda b, n: (b * nb + n, 0)),
        out_shape=jax.ShapeDtypeStruct((N_DEC_B * DEC_T, A_QW), F32),
        scratch_shapes=[pltpu.VMEM((DEC_T, A_KVW), BF16)],
        compiler_params=_cparams(("arbitrary", "arbitrary")),
        name="attn_lat",
    )(sink_l, pm, pm, pm, kc, vc, cos, sin)


def _rwkv_kernel(T, NS, r_ref, k_ref, v_ref, lr_ref, s0_ref, w0_ref, wb_ref, a0_ref, ab_ref, gb_ref,
                 kk_ref, ka_ref, rk_ref, lng_ref, lnb_ref, o_ref, sfin_ref,
                 KK, W, WRP, AKK, KT, VC2, BON, Y, S):
    ones4 = _block_ones(B_W, B_DH)
    decay_c = float(np.exp(-0.5))
    RC = 256
    SUB = 32
    NP = 3

    def prep(c, carry):
        r0 = pl.multiple_of(c * RC, RC)
        rs = pl.ds(r0, RC)
        r = r_ref[rs, :]
        k = k_ref[rs, :]
        v = v_ref[rs, :]
        lr = lr_ref[rs, :]
        kkr = k * kk_ref[...]
        kk = kkr * lax.rsqrt(_mm_xr(kkr * kkr, ones4, 2) + 1e-12)
        KK[rs, :] = kk
        bonus = jnp.zeros((RC, B_W), F32)
        vc2 = jnp.zeros((RC, B_W), F32)
        for d in range(2):
            z = w0_ref[d] + _mm(jnp.tanh(lr[:, B_RANK * d:B_RANK * (d + 1)]), wb_ref[d])
            w = jnp.exp(-decay_c * _sigmoid(z))
            a = _sigmoid(a0_ref[d] + _mm(lr[:, B_RANK * (2 + d):B_RANK * (3 + d)], ab_ref[d]))
            kt = k * (1.0 + (a - 1.0) * ka_ref[...])
            akk = a * kk
            W[d, rs, :] = w
            WRP[d, rs, :] = w * r - _mm_xr(akk * r, ones4, 2) * kk
            AKK[d, rs, :] = akk
            KT[d, rs, :] = kt
            vc2 = vc2 + _mm_xr(kt * r, ones4, 2) * v
            bonus = bonus + _mm_xr(r * kt * rk_ref[...], ones4, 2) * v
        VC2[rs, :] = vc2
        BON[rs, :] = bonus
        return carry

    lax.fori_loop(0, NS * T // RC, prep, 0)

    chains = [(s, d) for s in range(NS) for d in range(2)]
    for s, d in chains:
        S[s, d] = jnp.concatenate([s0_ref[s, d, h] for h in range(B_HEADS)], axis=1)

    eye4 = _iota((B_DH, B_W), 0) == (_iota((B_DH, B_W), 1) % B_DH)

    def steps(i, carry):
        t0s = [pl.multiple_of(s * T + (i * SUB if d == 0 else T - SUB - i * SUB), SUB) for s, d in chains]

        def row(ref, g, j, d=None):
            tile = pl.ds(t0s[g] + (j // SUBLANES) * SUBLANES, SUBLANES)
            vals = ref[tile, :] if d is None else ref[d, tile, :]
            return vals[j % SUBLANES:j % SUBLANES + 1]

        ys = [[None] * SUB for _ in chains]
        for jj in range(SUB):
            lhs = []
            for g, (s, d) in enumerate(chains):
                j = jj if d == 0 else SUB - 1 - jj
                stb = S[s, d].astype(BF16)
                lhs += [stb * row(KK, g, j).astype(BF16), stb * row(WRP, g, j, d).astype(BF16),
                        jnp.where(eye4, row(v_ref, g, j), 0.0).astype(BF16)]
            res = _dg(jnp.concatenate(lhs, axis=0), ones4)
            for g, (s, d) in enumerate(chains):
                j = jj if d == 0 else SUB - 1 - jj
                sk, yp, vcol = [res[(g * NP + n) * B_DH:(g * NP + n + 1) * B_DH] for n in range(NP)]
                S[s, d] = S[s, d] * row(W, g, j, d) - sk * row(AKK, g, j, d) + vcol * row(KT, g, j, d)
                ys[g][j] = jnp.sum(jnp.where(eye4, yp, 0.0), axis=0, keepdims=True)
        for g, (s, d) in enumerate(chains):
            Y[d, pl.ds(t0s[g], SUB), :] = jnp.concatenate(ys[g], axis=0)
        return carry

    lax.fori_loop(0, T // SUB, steps, 0)

    for s, d in chains:
        st = S[s, d]
        for h in range(B_HEADS):
            sfin_ref[s, d, h] = st[:, h * B_DH:(h + 1) * B_DH]

    def post(c, carry):
        r0 = pl.multiple_of(c * RC, RC)
        rs = pl.ds(r0, RC)
        y = Y[0, rs, :] + Y[1, rs, :] + VC2[rs, :]
        mu = _mm_xr(y, ones4, 2) * (1.0 / B_DH)
        yc = y - mu
        var = _mm_xr(yc * yc, ones4, 2) * (1.0 / B_DH)
        yn = yc * lax.rsqrt(var + B_GN_EPS) * lng_ref[...] + lnb_ref[...] + BON[rs, :]
        g = _mm(_sigmoid(lr_ref[rs, 4 * B_RANK:4 * B_RANK + B_GATE_RANK]), gb_ref[...])
        o_ref[rs, :] = yn * g
        return carry

    lax.fori_loop(0, NS * T // RC, post, 0)


def _rwkv_call(T, NS, n_seq, tok0, pm, plr, s0, prm, name):
    rows = NS * T
    blk0 = tok0 // rows
    n_steps = n_seq // NS
    full = lambda shape: pl.BlockSpec(shape, lambda b: (0,) * len(shape))
    big = lambda shape, imap: (pl.BlockSpec(shape, imap, pipeline_mode=pl.Buffered(1)) if n_steps == 1
                               else pl.BlockSpec(shape, imap))
    kern = functools.partial(_rwkv_kernel, T, NS)
    return pl.pallas_call(
        kern,
        grid=(n_steps,),
        in_specs=[
            big((rows, B_W), lambda b: (blk0 + b, COL_B // B_W)),
            big((rows, B_W), lambda b: (blk0 + b, COL_B // B_W + 1)),
            big((rows, B_W), lambda b: (blk0 + b, COL_B // B_W + 2)),
            big((rows, LR_W), lambda b: (blk0 + b, 0)),
            pl.BlockSpec((NS, 2, B_HEADS, B_DH, B_DH), lambda b: (b, 0, 0, 0, 0)),
            full((2, 1, B_W)), full((2, B_RANK, B_W)), full((2, 1, B_W)), full((2, B_RANK, B_W)),
            full((B_GATE_RANK, B_W)),
            full((1, B_W)), full((1, B_W)), full((1, B_W)), full((1, B_W)), full((1, B_W)),
        ],
        out_specs=[
            big((rows, B_W), lambda b: (b, 0)),
            pl.BlockSpec((NS, 2, B_HEADS, B_DH, B_DH), lambda b: (b, 0, 0, 0, 0)),
        ],
        out_shape=[
            jax.ShapeDtypeStruct((n_seq * T, B_W), F32),
            jax.ShapeDtypeStruct((n_seq, 2, B_HEADS, B_DH, B_DH), F32),
        ],
        scratch_shapes=[
            pltpu.VMEM((rows, B_W), F32),
            pltpu.VMEM((2, rows, B_W), F32),
            pltpu.VMEM((2, rows, B_W), F32),
            pltpu.VMEM((2, rows, B_W), F32),
            pltpu.VMEM((2, rows, B_W), F32),
            pltpu.VMEM((rows, B_W), F32),
            pltpu.VMEM((rows, B_W), F32),
            pltpu.VMEM((2, rows, B_W), F32),
            pltpu.VMEM((NS, 2, B_DH, B_W), F32),
        ],
        compiler_params=_cparams(("arbitrary",)),
        name=name,
    )(pm, pm, pm, plr, s0, *prm)


def _gla_kernel(T, NS, q_ref, k_ref, v_ref, og_ref, lr_ref, s0_ref, gb_ref, bias_ref, ng_ref,
                o_ref, sfin_ref, LA, O, S):
    n_chunks = T // C_CHUNK
    nsub = C_CHUNK // C_SUB
    qscale = C_DK ** -0.5
    lr = lr_ref[...]
    for d in range(2):
        gl = _mm(lr, gb_ref[d]) + bias_ref[d]
        LA[d] = (jnp.minimum(gl, 0.0) - jnp.log(1.0 + jnp.exp(-jnp.abs(gl)))) * (1.0 / C_GATE_NORM)
    bd_state = _iota((C_KW, C_VW), 0) // C_DK == _iota((C_KW, C_VW), 1) // C_DV
    chains = [(s, d) for s in range(NS) for d in range(2)]
    for s, d in chains:
        for h in range(C_HEADS):
            pad_l = h * C_DV
            pad_r = C_VW - (h + 1) * C_DV
            blk = s0_ref[s, d, h]
            parts = ([jnp.zeros((C_DK, pad_l), F32)] if pad_l else []) + [blk] + \
                    ([jnp.zeros((C_DK, pad_r), F32)] if pad_r else [])
            S[s, d, h * C_DK:(h + 1) * C_DK, :] = jnp.concatenate(parts, axis=1)

    ti = _iota((C_CHUNK, C_CHUNK), 0)
    si = _iota((C_CHUNK, C_CHUNK), 1)
    tri = ((si <= ti).astype(BF16), (si >= ti).astype(BF16))
    trow = _iota((C_CHUNK, 1), 0)
    mask_k = _iota((C_CHUNK, C_KW), 0) // C_SUB == _iota((C_CHUNK, C_KW), 1) // C_DK
    mask_v = _iota((C_CHUNK, C_VW), 0) // C_SUB == _iota((C_CHUNK, C_VW), 1) // C_DV
    t_att = _iota((C_CHUNK, C_CHUNK), 0)
    s_att = _iota((C_CHUNK, C_CHUNK), 1) % C_SUB
    eye_k = _iota((C_KW, C_KW), 0) == _iota((C_KW, C_KW), 1)

    def body(c, carry):
        cx = []
        for s, d in chains:
            cc = c if d == 0 else n_chunks - 1 - c
            rs = pl.ds(pl.multiple_of(s * T + cc * C_CHUNK, C_CHUNK), C_CHUNK)
            b = _mm_xl(tri[d], LA[d, rs, :], 3)
            cx.append(dict(s=s, d=d, rs=rs, b=b, q=q_ref[rs, :] * qscale, k=k_ref[rs, :], v=v_ref[rs, :]))
        for x in cx:
            x["o"] = _mm(x["q"] * jnp.exp(x["b"]), S[x["s"], x["d"]])
        for j in range(nsub):
            lo, hi = j * C_SUB, (j + 1) * C_SUB
            for x in cx:
                b, q, k = x["b"], x["q"], x["k"]
                if x["d"] == 0:
                    gamma = b[hi - 1:hi, :]
                    row_ok = trow >= lo
                    att_ok = t_att >= lo + s_att
                else:
                    gamma = b[lo:lo + 1, :]
                    row_ok = trow < hi
                    att_ok = t_att <= lo + s_att
                qj = q * jnp.exp(jnp.where(row_ok, b - gamma, NEG_INF))
                kj = k[lo:hi, :] * jnp.exp(gamma - b[lo:hi, :])
                kbd = jnp.where(mask_k, jnp.concatenate([kj] * C_HEADS, axis=0), 0.0)
                x["att"] = jnp.where(att_ok, _mm(qj, kbd, NT), 0.0)
            for x in cx:
                vbd = jnp.where(mask_v, jnp.concatenate([x["v"][lo:hi, :]] * C_HEADS, axis=0), 0.0)
                x["o"] = x["o"] + _mm(x["att"], vbd)
        for x in cx:
            s, d, b = x["s"], x["d"], x["b"]
            O[d, x["rs"], :] = x["o"]
            blast = b[C_CHUNK - 1:C_CHUNK, :] if d == 0 else b[0:1, :]
            kl = x["k"] * jnp.exp(blast - b)
            upd = jnp.where(bd_state, _mm3(kl.T, x["v"]), 0.0)
            dec = jnp.where(eye_k, jnp.exp(blast), 0.0)
            S[s, d] = _mm3(dec, S[s, d]) + upd
        return carry

    def body_group(cg, carry):
        for u in range(GLA_UNROLL):
            body(cg * GLA_UNROLL + u, carry)
        return carry

    lax.fori_loop(0, n_chunks // GLA_UNROLL, body_group, 0)

    for s, d in chains:
        st = S[s, d]
        for h in range(C_HEADS):
            sfin_ref[s, d, h] = st[h * C_DK:(h + 1) * C_DK, h * C_DV:(h + 1) * C_DV]

    ones4 = _block_ones(C_VW, C_DV)
    o = O[0] + O[1]
    ms = _mm_xr(o * o, ones4, 2) * (1.0 / C_DV)
    og = og_ref[...]
    o_ref[...] = o * lax.rsqrt(ms + LN_EPS) * ng_ref[...] * (og * _sigmoid(og))


def _gla_call(T, NS, n_seq, tok0, pm, plr, s0, gb_pad, bias, ng, name):
    rows = NS * T
    blk0 = tok0 // rows
    full = lambda shape: pl.BlockSpec(shape, lambda b: (0,) * len(shape))
    return pl.pallas_call(
        functools.partial(_gla_kernel, T, NS),
        grid=(n_seq // NS,),
        in_specs=[
            pl.BlockSpec((rows, C_KW), lambda b: (blk0 + b, COL_C // C_KW)),
            pl.BlockSpec((rows, C_KW), lambda b: (blk0 + b, COL_C // C_KW + 1)),
            pl.BlockSpec((rows, C_VW), lambda b: (blk0 + b, COL_CV // C_VW)),
            pl.BlockSpec((rows, C_VW), lambda b: (blk0 + b, COL_CV // C_VW + 1)),
            pl.BlockSpec((rows, LANES), lambda b: (blk0 + b, C_LR_COL // LANES)),
            pl.BlockSpec((NS, 2, C_HEADS, C_DK, C_DV), lambda b: (b, 0, 0, 0, 0)),
            full((2, LANES, C_KW)), full((2, 1, C_KW)), full((1, C_VW)),
        ],
        out_specs=[
            pl.BlockSpec((rows, C_VW), lambda b: (b, 0)),
            pl.BlockSpec((NS, 2, C_HEADS, C_DK, C_DV), lambda b: (b, 0, 0, 0, 0)),
        ],
        out_shape=[
            jax.ShapeDtypeStruct((n_seq * T, C_VW), F32),
            jax.ShapeDtypeStruct((n_seq, 2, C_HEADS, C_DK, C_DV), F32),
        ],
        scratch_shapes=[
            pltpu.VMEM((2, rows, C_KW), F32),
            pltpu.VMEM((2, rows, C_VW), F32),
            pltpu.VMEM((NS, 2, C_KW, C_VW), F32),
        ],
        compiler_params=_cparams(("arbitrary",)),
        name=name,
    )(pm, pm, pm, pm, plr, s0, gb_pad, bias, ng)


MERGE_TM = 512


def _merge_kernel(n_x, *refs):
    x_refs = refs[:n_x]
    (g_ref, oac_ref, oad_ref, obc_ref, obd_ref, occ_ref, ocd_ref, mod_ref,
     wa_ref, wb_ref, wc_ref, wo_ref, lg_ref, lb_ref, o_ref, wa_s, wb_s, wc_s, wo_s) = refs[n_x:]

    @pl.when(pl.program_id(0) == 0)
    def _():
        wa_s[...] = wa_ref[...].astype(BF16)
        wb_s[...] = wb_ref[...].astype(BF16)
        wc_s[...] = wc_ref[...].astype(BF16)
        wo_s[...] = wo_ref[...].astype(BF16)

    is_ctx = pl.program_id(0) < N_CTX_TOK // MERGE_TM
    oa = jnp.where(is_ctx, oac_ref[...], oad_ref[...]).astype(BF16)
    ob = jnp.where(is_ctx, obc_ref[...], obd_ref[...]).astype(BF16)
    oc = jnp.where(is_ctx, occ_ref[...], ocd_ref[...]).astype(BF16)
    merged = (g_ref[:, 0:D].astype(F32) * _dg(oa, wa_s[...])
              + g_ref[:, D:2 * D].astype(F32) * _dg(ob, wb_s[...])
              + g_ref[:, 2 * D:3 * D].astype(F32) * _dg(oc, wc_s[...]))
    mix = _dg(merged.astype(BF16), wo_s[...])
    y = DN_ALPHA * _read_x(x_refs, MERGE_TM) + mod_ref[2:3, :] * mix
    o_ref[...] = _layer_norm(y, lg_ref[...], lb_ref[...])


def _merge_call(xs, gates, oa, ob, oc, mod_l, wa, wb, wc, wo, lg, lb, l):
    tm = MERGE_TM
    x_specs = [pl.BlockSpec((tm, D), lambda i: (i, 0))] if len(xs) == 1 else _pair_specs(tm, D)
    return pl.pallas_call(
        functools.partial(_merge_kernel, len(xs)),
        grid=(N_TOK // tm,),
        in_specs=[
            *x_specs,
            pl.BlockSpec((tm, IN_GATE), lambda i: (i, 0)),
            *_pair_specs(tm, A_QW), *_pair_specs(tm, B_W), *_pair_specs(tm, C_VW),
            pl.BlockSpec((None, 6, D), lambda i: (_group_of_tile(i, tm), 0, 0)),
            _layer_resident((A_QW, D), l), _layer_resident((B_W, D), l), _layer_resident((C_VW, D), l),
            _layer_resident((D, D), l), _layer_resident((1, D), l), _layer_resident((1, D), l),
        ],
        out_specs=pl.BlockSpec((tm, D), lambda i: (i, 0)),
        out_shape=jax.ShapeDtypeStruct((N_TOK, D), F32),
        scratch_shapes=[pltpu.VMEM((A_QW, D), BF16), pltpu.VMEM((B_W, D), BF16),
                        pltpu.VMEM((C_VW, D), BF16), pltpu.VMEM((D, D), BF16)],
        compiler_params=_cparams(("arbitrary",)),
        name="merge",
    )(*xs, gates, *oa, *ob, *oc, mod_l, wa, wb, wc, wo, lg.reshape(DEPTH, 1, D), lb.reshape(DEPTH, 1, D))


def _ffn_kernel(x_ref, mod_ref, wg_ref, wu_ref, wd_ref, lg_ref, lb_ref, o_ref):
    x = x_ref[...]
    h = (x * (1.0 + mod_ref[4:5, :]) + mod_ref[3:4, :]).astype(BF16)
    gate = _dg(h, wg_ref[...])
    up = _dg(h, wu_ref[...])
    f = _dg((gate * _sigmoid(gate) * up).astype(BF16), wd_ref[...])
    y = DN_ALPHA * x + mod_ref[5:6, :] * f
    o_ref[...] = _layer_norm(y, lg_ref[...], lb_ref[...])


def _ffn_call(x, mod_l, wg, wu, wd, lg, lb):
    tm = 512
    return pl.pallas_call(
        _ffn_kernel,
        grid=(N_TOK // tm,),
        in_specs=[
            pl.BlockSpec((tm, D), lambda i: (i, 0)),
            pl.BlockSpec((None, 6, D), lambda i: (_group_of_tile(i, tm), 0, 0)),
            _resident((D, D_FF)), _resident((D, D_FF)), _resident((D_FF, D)),
            _resident((1, D)), _resident((1, D)),
        ],
        out_specs=pl.BlockSpec((tm, D), lambda i: (i, 0)),
        out_shape=jax.ShapeDtypeStruct((N_TOK, D), F32),
        compiler_params=_cparams(("arbitrary",)),
        name="ffn",
    )(x, mod_l, wg, wu, wd, lg.reshape(1, D), lb.reshape(1, D))


MOE_TM = 1024
MOE_TR = 256
MOE_RS = 3072
MOE_TF = 512
MOE_ROWS = 2 * N_TOK + N_EXP * MOE_TR
MOE_NST = -(-MOE_ROWS // MOE_RS) + N_EXP
R_I1, R_I2, R_W1, R_W2, R_RANK1, R_RANK2 = range(6)


def _moe_route_kernel(x_ref, mod_ref, wr_ref, h_ref, info_ref, cnt_ref, carry_s):
    tm = MOE_TM

    @pl.when(pl.program_id(0) == 0)
    def _():
        carry_s[...] = jnp.zeros_like(carry_s)

    h = x_ref[...] * (1.0 + mod_ref[4:5, :]) + mod_ref[3:4, :]
    h_ref[...] = h
    logits = _mm3(h, wr_ref[...])
    lane = _iota(logits.shape, 1)
    logits = jnp.where(lane < N_EXP, logits, NEG_INF)
    v1 = jnp.max(logits, -1, keepdims=True)
    i1 = jnp.min(jnp.where(logits == v1, lane, LANES), -1, keepdims=True)
    rest = jnp.where(lane == i1, NEG_INF, logits)
    v2 = jnp.max(rest, -1, keepdims=True)
    i2 = jnp.min(jnp.where(rest == v2, lane, LANES), -1, keepdims=True)
    e2 = jnp.exp(v2 - v1)
    w1 = 1.0 / (1.0 + e2)
    w2 = e2 / (1.0 + e2)
    oh1 = lane == i1
    oh2 = lane == i2
    cnt = oh1.astype(F32) + oh2.astype(F32)
    earlier = (_iota((tm, tm), 1) < _iota((tm, tm), 0)).astype(BF16)
    before = _dg(earlier, cnt.astype(BF16)) + carry_s[...]
    rank1 = jnp.sum(jnp.where(oh1, before, 0.0), -1, keepdims=True)
    rank2 = jnp.sum(jnp.where(oh2, before, 0.0), -1, keepdims=True)
    info = jnp.zeros(logits.shape, F32)
    for ln, val in ((R_I1, i1.astype(F32)), (R_I2, i2.astype(F32)), (R_W1, w1), (R_W2, w2),
                    (R_RANK1, rank1), (R_RANK2, rank2)):
        info = jnp.where(lane == ln, val, info)
    info_ref[...] = info
    carry_s[...] += jnp.sum(cnt, axis=0, keepdims=True)
    cnt_ref[...] = carry_s[...]


def _moe_route_call(x, mod_l, wr_pad, i_moe):
    tm = MOE_TM
    return pl.pallas_call(
        _moe_route_kernel,
        grid=(N_TOK // tm,),
        in_specs=[
            pl.BlockSpec((tm, D), lambda i: (i, 0)),
            pl.BlockSpec((None, 6, D), lambda i: (_group_of_tile(i, tm), 0, 0)),
            pl.BlockSpec((None, D, LANES), lambda i: (i_moe, 0, 0)),
        ],
        out_specs=[
            pl.BlockSpec((tm, D), lambda i: (i, 0)),
            pl.BlockSpec((tm, LANES), lambda i: (i, 0)),
            pl.BlockSpec((1, LANES), lambda i: (0, 0)),
        ],
        out_shape=[
            jax.ShapeDtypeStruct((N_TOK, D), F32),
            jax.ShapeDtypeStruct((N_TOK, LANES), F32),
            jax.ShapeDtypeStruct((1, LANES), F32),
        ],
        scratch_shapes=[pltpu.VMEM((1, LANES), F32)],
        compiler_params=_cparams(("arbitrary",)),
        name="moe_route",
    )(x, mod_l, wr_pad)


def _moe_plan(info, cnt):
    i32 = jnp.int32
    i1 = info[:, R_I1].astype(i32)
    i2 = info[:, R_I2].astype(i32)
    counts = cnt[0, :N_EXP].astype(i32)
    padded = (counts + MOE_TR - 1) // MOE_TR * MOE_TR
    seg_start = jnp.cumsum(padded) - padded
    pos1 = seg_start[i1] + info[:, R_RANK1].astype(i32)
    pos2 = seg_start[i2] + info[:, R_RANK2].astype(i32)
    dst = jnp.zeros((MOE_ROWS,), i32).at[jnp.concatenate([pos1, pos2])].set(jnp.arange(2 * N_TOK, dtype=i32))
    n_pass = (padded + MOE_RS - 1) // MOE_RS
    pass_end = jnp.cumsum(n_pass)
    total = pass_end[-1]
    sidx = jnp.arange(MOE_NST, dtype=i32)
    used = sidx < total
    e_of = jnp.minimum(jnp.searchsorted(pass_end, jnp.minimum(sidx, total - 1), side="right"), N_EXP - 1).astype(i32)
    k = jnp.minimum(sidx, total - 1) - (pass_end - n_pass)[e_of]
    row0 = seg_start[e_of] + k * MOE_RS
    nrows = jnp.where(used, jnp.clip(padded[e_of] - k * MOE_RS, 0, MOE_RS), 0)
    nvalid = jnp.where(used, jnp.clip(counts[e_of] - k * MOE_RS, 0, MOE_RS), 0)
    src = jnp.where(dst >= N_TOK, dst - N_TOK, dst)
    return src, dst, e_of, row0.astype(i32), nrows.astype(i32), nvalid.astype(i32)


def _moe_expert_kernel(src_ref, dst_ref, exp_ref, row0_ref, nrows_ref, nvalid_ref,
                       h_hbm, wg_ref, wu_ref, wd_ref, yo_hbm, xs, xb, acc, wgb, wub, wdb, gsem, ssem):
    s = pl.program_id(0)
    f = pl.program_id(1)
    nf = pl.num_programs(1)
    nrows = pl.multiple_of(nrows_ref[s], MOE_TR)
    nvalid = nvalid_ref[s]
    row0 = row0_ref[s]
    n_chunks = nrows // MOE_TR

    def row_copy(src, dst, sem):
        return pltpu.make_async_copy(src, dst, sem)

    def hbm_row(ref, i):
        return ref.at[pl.ds(i, 1), :]

    def wait_rows(buf, n_groups, sem):
        pltpu.make_async_copy(buf.at[pl.ds(0, n_groups)], buf.at[pl.ds(0, n_groups)], sem).wait()

    @pl.when((f == 0) & (nrows > 0))
    def _gather():
        def issue(g, carry):
            for u in range(SUBLANES):
                tok = src_ref[row0 + g * SUBLANES + u]
                row_copy(hbm_row(h_hbm, tok), xs.at[g, pl.ds(u, 1), :], gsem).start(priority=u % 2)
            return carry

        ng = nrows // SUBLANES
        lax.fori_loop(0, ng, issue, 0)
        wait_rows(xs, ng, gsem)

        def cvt(c, carry):
            g0 = pl.multiple_of(c * (MOE_TR // SUBLANES), MOE_TR // SUBLANES)
            rs = pl.ds(pl.multiple_of(c * MOE_TR, MOE_TR), MOE_TR)
            xb[rs, :] = xs[pl.ds(g0, MOE_TR // SUBLANES)].reshape(MOE_TR, D).astype(BF16)
            return carry

        lax.fori_loop(0, n_chunks, cvt, 0)

    @pl.when(nrows > 0)
    def _compute():
        wgb[...] = wg_ref[...].astype(BF16)
        wub[...] = wu_ref[...].astype(BF16)
        wdb[...] = wd_ref[...].astype(BF16)

        def chunk(start, n):
            x = xb[pl.ds(pl.multiple_of(start, MOE_TR), n), :]
            gate = _dg(x, wgb[...])
            up = _dg(x, wub[...])
            y = _dg((gate * _sigmoid(gate) * up).astype(BF16), wdb[...]).reshape(n // SUBLANES, SUBLANES, D)
            gs = pl.ds(pl.multiple_of(start // SUBLANES, MOE_TR // SUBLANES), n // SUBLANES)

            @pl.when(f == 0)
            def _():
                acc[gs] = y

            @pl.when(f > 0)
            def _():
                acc[gs] += y

        def chunk_quad(c4, carry):
            chunk(c4 * (4 * MOE_TR), 4 * MOE_TR)
            return carry

        lax.fori_loop(0, n_chunks // 4, chunk_quad, 0)
        done = n_chunks // 4 * 4

        @pl.when(n_chunks % 4 >= 2)
        def _():
            chunk(done * MOE_TR, 2 * MOE_TR)

        @pl.when(n_chunks % 2 == 1)
        def _():
            chunk((n_chunks - 1) * MOE_TR, MOE_TR)

    @pl.when((f == nf - 1) & (nvalid > 0))
    def _scatter():
        n8 = nvalid // SUBLANES

        def issue(g, carry):
            for u in range(SUBLANES):
                d = dst_ref[row0 + g * SUBLANES + u]
                row_copy(acc.at[g, pl.ds(u, 1), :], hbm_row(yo_hbm, d), ssem).start(priority=u % 2)
            return carry

        def issue_tail(r, carry):
            d = dst_ref[row0 + r]
            row_copy(acc.at[n8, pl.ds(r - n8 * SUBLANES, 1), :], hbm_row(yo_hbm, d), ssem).start()
            return carry

        lax.fori_loop(0, n8, issue, 0)
        lax.fori_loop(n8 * SUBLANES, nvalid, issue_tail, 0)

        @pl.when(n8 > 0)
        def _():
            wait_rows(acc, n8, ssem)

        def wait_one(r, carry):
            row_copy(acc.at[0, pl.ds(0, 1), :], hbm_row(yo_hbm, 0), ssem).wait()
            return carry

        lax.fori_loop(n8 * SUBLANES, nvalid, wait_one, 0)


def _moe_expert_call(h, plan, wg, wu, wd, i_moe):
    nf = D_FFE // MOE_TF

    def wspec(shape, fdim):
        def imap(s, f, src, dst, exp, row0, nrows, nvalid):
            fe = jnp.where(nrows[s] > 0, f, nf - 1)
            return (i_moe, exp[s], 0, fe) if fdim == 3 else (i_moe, exp[s], fe, 0)
        return pl.BlockSpec(shape, imap)

    grid_spec = pltpu.PrefetchScalarGridSpec(
        num_scalar_prefetch=6,
        grid=(MOE_NST, nf),
        in_specs=[
            pl.BlockSpec(memory_space=pl.ANY),
            wspec((None, None, D, MOE_TF), 3),
            wspec((None, None, D, MOE_TF), 3),
            wspec((None, None, MOE_TF, D), 2),
        ],
        out_specs=pl.BlockSpec(memory_space=pl.ANY),
        scratch_shapes=[
            pltpu.VMEM((MOE_RS // SUBLANES, SUBLANES, D), F32),
            pltpu.VMEM((MOE_RS, D), BF16),
            pltpu.VMEM((MOE_RS // SUBLANES, SUBLANES, D), F32),
            pltpu.VMEM((D, MOE_TF), BF16), pltpu.VMEM((D, MOE_TF), BF16), pltpu.VMEM((MOE_TF, D), BF16),
            pltpu.SemaphoreType.DMA(()), pltpu.SemaphoreType.DMA(()),
        ],
    )
    return pl.pallas_call(
        _moe_expert_kernel,
        grid_spec=grid_spec,
        out_shape=jax.ShapeDtypeStruct((2 * N_TOK, D), F32),
        compiler_params=pltpu.CompilerParams(dimension_semantics=("arbitrary", "arbitrary"),
                                             vmem_limit_bytes=VMEM_LIMIT, disable_bounds_checks=True),
        name="moe_experts",
    )(*plan, h, wg, wu, wd)


def _moe_combine_kernel(x_ref, y1_ref, y2_ref, info_ref, mod_ref, lg_ref, lb_ref, oc_ref, od_ref):
    f = info_ref[:, R_W1:R_W1 + 1] * y1_ref[...] + info_ref[:, R_W2:R_W2 + 1] * y2_ref[...]
    y = DN_ALPHA * x_ref[...] + mod_ref[5:6, :] * f
    out = _layer_norm(y, lg_ref[...], lb_ref[...])
    is_ctx = pl.program_id(0) < N_CTX_TOK // MOE_TM

    @pl.when(is_ctx)
    def _():
        oc_ref[...] = out

    @pl.when(jnp.logical_not(is_ctx))
    def _():
        od_ref[...] = out


def _moe_combine_call(x, yo, info, mod_l, lg, lb):
    tm = MOE_TM
    nt = N_TOK // tm
    return pl.pallas_call(
        _moe_combine_kernel,
        grid=(nt,),
        in_specs=[
            pl.BlockSpec((tm, D), lambda i: (i, 0)),
            pl.BlockSpec((tm, D), lambda i: (i, 0)),
            pl.BlockSpec((tm, D), lambda i: (nt + i, 0)),
            pl.BlockSpec((tm, LANES), lambda i: (i, 0)),
            pl.BlockSpec((None, 6, D), lambda i: (_group_of_tile(i, tm), 0, 0)),
            _resident((1, D)), _resident((1, D)),
        ],
        out_specs=_pair_specs(tm, D),
        out_shape=[jax.ShapeDtypeStruct((N_CTX_TOK, D), F32), jax.ShapeDtypeStruct((N_DEC_B * DEC_T, D), F32)],
        compiler_params=_cparams(("arbitrary",)),
        name="moe_combine",
    )(x, yo, yo, info, mod_l, lg.reshape(1, D), lb.reshape(1, D))


def _moe_call(x, mod_l, wr_pad, wg, wu, wd, lg, lb, i_moe):
    h, info, cnt = _moe_route_call(x, mod_l, wr_pad, i_moe)
    yo = _moe_expert_call(h, _moe_plan(info, cnt), wg, wu, wd, i_moe)
    return _moe_combine_call(x, yo, info, mod_l, lg, lb)


def kernel(x_prompt, x_sample, cache_attn_k, cache_attn_v, state_rwkv, state_gla, c, c_ctx, w_ada, b_ada, w_in,
           attn_sink, rwkv_w0, rwkv_w_a, rwkv_w_b, rwkv_a0, rwkv_a_a, rwkv_a_b, rwkv_g_a, rwkv_g_b, rwkv_k_k,
           rwkv_k_a, rwkv_r_k, rwkv_ln_g, rwkv_ln_b, gla_gate_a, gla_gate_b, gla_gate_bias, gla_norm_g, w_up_a,
           w_up_b, w_up_c, w_out, ln1_g, ln1_b, ln2_g, ln2_b, ffn_w_gate, ffn_w_up, ffn_w_down, moe_router,
           moe_w_gate, moe_w_up, moe_w_down):
    cvec = jnp.concatenate([c_ctx[None, :], c, jnp.zeros((N_GROUPS - 1 - N_DEC_B, D), F32)], axis=0)
    mods = _ada_call(cvec, w_ada, b_ada).reshape(DEPTH, N_GROUPS, 6, D)
    xs = (x_prompt.reshape(N_CTX_TOK, D), x_sample.reshape(N_DEC_B * DEC_T, D))
    cos, sin = _rope_tables()
    kc_all = cache_attn_k.reshape(N_DEC_B, DEPTH, PAST, A_KV * A_DH)
    vc_all = cache_attn_v.reshape(N_DEC_B, DEPTH, PAST, A_KV * A_DH)
    zeros_r = jnp.zeros((N_CTX_B, 2, B_HEADS, B_DH, B_DH), F32)
    zeros_g = jnp.zeros((N_CTX_B, 2, C_HEADS, C_DK, C_DV), F32)

    new_k, new_v, new_sr, new_sg = [], [], [], []
    for l in range(DEPTH):
        mod_l = mods[l]
        w_lr = jnp.concatenate(
            [rwkv_w_a[l, 0], rwkv_w_a[l, 1], rwkv_a_a[l, 0], rwkv_a_a[l, 1], rwkv_g_a[l],
             gla_gate_a[l, 0], gla_gate_a[l, 1], jnp.zeros((D, LR_W - C_LR_COL - 2 * C_GATE_RANK), F32)], axis=1).astype(BF16)
        pm, gates, plr, k_ctx, v_ctx = _inproj_call(xs, mod_l, w_in, w_lr, l)

        sink_l = attn_sink[l]
        oa = (_attn_ctx_call(sink_l, pm), _attn_lat_call(sink_l, pm, kc_all, vc_all, l, cos, sin))

        rprm = (rwkv_w0[l].reshape(2, 1, B_W), rwkv_w_b[l], rwkv_a0[l].reshape(2, 1, B_W), rwkv_a_b[l],
                rwkv_g_b[l], rwkv_k_k[l].reshape(1, B_W), rwkv_k_a[l].reshape(1, B_W),
                rwkv_r_k[l].reshape(1, B_W), rwkv_ln_g[l].reshape(1, B_W), rwkv_ln_b[l].reshape(1, B_W))
        ob_c, sr_c = _rwkv_call(CTX_T, 4, N_CTX_B, 0, pm, plr, zeros_r, rprm, "rwkv_ctx")
        ob_d, _ = _rwkv_call(DEC_T, 2, N_DEC_B, N_CTX_TOK, pm, plr, state_rwkv[:, l], rprm, "rwkv_lat")

        gb_pad = jnp.zeros((2, LANES, C_KW), F32)
        gb_pad = gb_pad.at[0, 0:C_GATE_RANK].set(gla_gate_b[l, 0]).at[1, C_GATE_RANK:2 * C_GATE_RANK].set(
            gla_gate_b[l, 1])
        gbias = gla_gate_bias[l].reshape(2, 1, C_KW)
        ng = jnp.tile(gla_norm_g[l], C_HEADS).reshape(1, C_VW)
        oc_c, sg_c = _gla_call(CTX_T, 4, N_CTX_B, 0, pm, plr, zeros_g, gb_pad, gbias, ng, "gla_ctx")
        oc_d, _ = _gla_call(DEC_T, 2, N_DEC_B, N_CTX_TOK, pm, plr, state_gla[:, l], gb_pad, gbias, ng, "gla_lat")

        x = _merge_call(xs, gates, oa, (ob_c, ob_d), (oc_c, oc_d), mod_l, w_up_a, w_up_b, w_up_c, w_out,
                        ln1_g, ln1_b, l)
        if l % 2 == 0:
            i_ffn = l // 2
            xs = (_ffn_call(x, mod_l, ffn_w_gate[i_ffn].astype(BF16), ffn_w_up[i_ffn].astype(BF16),
                            ffn_w_down[i_ffn].astype(BF16), ln2_g[l], ln2_b[l]),)
        else:
            wr_pad = jnp.concatenate([moe_router, jnp.zeros((moe_router.shape[0], D, LANES - N_EXP), F32)], axis=2)
            xs = _moe_call(x, mod_l, wr_pad, moe_w_gate, moe_w_up, moe_w_down, ln2_g[l], ln2_b[l], l // 2)

        new_k.append(k_ctx.reshape(N_CTX_B, CTX_T, A_KV, A_DH))
        new_v.append(v_ctx.reshape(N_CTX_B, CTX_T, A_KV, A_DH))
        new_sr.append(sr_c)
        new_sg.append(sg_c)

    if len(xs) == 1:
        xs = (xs[0][:N_CTX_TOK], xs[0][N_CTX_TOK:])
    y_prompt = xs[0].reshape(N_CTX_B, CTX_T, D)
    y_sample = xs[1].reshape(N_DEC_B, DEC_T, D)
    return (y_prompt, y_sample, jnp.stack(new_k, axis=1), jnp.stack(new_v, axis=1),
            jnp.stack(new_sr, axis=1), jnp.stack(new_sg, axis=1))
```

```python
import functools

import numpy as np
import jax
import jax.numpy as jnp
from jax import lax
from jax.experimental import pallas as pl
from jax.experimental.pallas import tpu as pltpu

D = 1024
N_CTX_B, CTX_T = 32, 256
N_DEC_B, DEC_T = 2, 1024
N_CTX_TOK = N_CTX_B * CTX_T
N_TOK = N_CTX_TOK + N_DEC_B * DEC_T
DEPTH = 2
PAST = 256
GRID_W = 64
A_HEADS, A_KV, A_DH = 8, 2, 64
A_G = A_HEADS // A_KV
A_WIN, A_BLK = 128, 128
ROPE_BASE = 10000.0
B_HEADS, B_DH = 4, 64
B_W = B_HEADS * B_DH
B_GN_EPS = 64e-5
C_HEADS, C_DK, C_DV = 4, 32, 64
C_KW, C_VW = C_HEADS * C_DK, C_HEADS * C_DV
C_GATE_RANK = 16
C_GATE_NORM = 16.0
C_CHUNK = 64
C_SUB = 16
GLA_UNROLL = 2
D_FF = 2816
N_EXP = 8
D_FFE = 3584
LN_EPS = 1e-5
DN_ALPHA = (2.0 * DEPTH) ** 0.25
NEG_INF = -1e30
A_QW, A_KVW = A_HEADS * A_DH, A_KV * A_DH
B_RANK, B_GATE_RANK = 64, 128
LANES = 128
IN_MAIN = 2304
COL_B = A_QW + 2 * A_KVW
COL_C = COL_B + 3 * B_W
COL_CV = COL_C + 2 * C_KW
IN_GATE = 3072
LR_W = 512
C_LR_COL = 4 * B_RANK + B_GATE_RANK
N_GROUPS = 8

F32 = jnp.float32
BF16 = jnp.bfloat16
VMEM_LIMIT = 56 * 1024 * 1024
SUBLANES = 8

NN = ((1,), (0,))
NT = ((1,), (1,))
TN = ((0,), (0,))


def _dg(a, b, dims=NN):
    return lax.dot_general(a, b, (dims, ((), ())), preferred_element_type=F32)


def _split2(x):
    hi = x.astype(BF16)
    lo = (x - hi.astype(F32)).astype(BF16)
    return hi, lo


def _split3(x):
    hi = x.astype(BF16)
    r = x - hi.astype(F32)
    mid = r.astype(BF16)
    lo = (r - mid.astype(F32)).astype(BF16)
    return hi, mid, lo


def _mm(a, b, dims=NN):
    return _dg(a.astype(BF16), b.astype(BF16), dims)


def _mm3(a, b, dims=NN):
    ah, al = _split2(a)
    bh, bl = _split2(b)
    return _dg(ah, bh, dims) + (_dg(ah, bl, dims) + _dg(al, bh, dims))


def _mm_xr(a, b_exact, passes, dims=NN):
    parts = (a.astype(BF16),) if passes == 1 else (_split2(a) if passes == 2 else _split3(a))
    out = _dg(parts[0], b_exact, dims)
    for p in parts[1:]:
        out = out + _dg(p, b_exact, dims)
    return out


def _mm_xl(a_exact, b, passes):
    parts = (b.astype(BF16),) if passes == 1 else (_split2(b) if passes == 2 else _split3(b))
    out = _dg(a_exact, parts[0])
    for p in parts[1:]:
        out = out + _dg(a_exact, p)
    return out


def _sigmoid(x):
    return 1.0 / (1.0 + jnp.exp(-x))


def _iota(shape, dim):
    return lax.broadcasted_iota(jnp.int32, shape, dim)


def _block_ones(n, blk):
    return (_iota((n, n), 0) // blk == _iota((n, n), 1) // blk).astype(BF16)


def _layer_norm(y, g, b):
    mu = jnp.mean(y, -1, keepdims=True)
    yc = y - mu
    var = jnp.mean(yc * yc, -1, keepdims=True)
    return yc * lax.rsqrt(var + LN_EPS) * g + b


def _group_of_tile(i, tm):
    n_ctx = N_CTX_TOK // tm
    per_dec = DEC_T // tm
    return jnp.where(i < n_ctx, 0, 1 + (i - n_ctx) // per_dec)


def _cparams(sem):
    return pltpu.CompilerParams(dimension_semantics=sem, vmem_limit_bytes=VMEM_LIMIT)


def _ada_kernel(c_ref, w_ref, b_ref, o_ref):
    c = c_ref[...]
    s = c * _sigmoid(c)
    o_ref[...] = _mm3(s, w_ref[...]) + b_ref[...]


def _ada_call(cvec, w_ada, b_ada):
    tn = 1536
    return pl.pallas_call(
        _ada_kernel,
        grid=(DEPTH, 6 * D // tn),
        in_specs=[
            pl.BlockSpec((N_GROUPS, D), lambda l, j: (0, 0)),
            pl.BlockSpec((None, D, tn), lambda l, j: (l, 0, j)),
            pl.BlockSpec((None, 1, tn), lambda l, j: (l, 0, j)),
        ],
        out_specs=pl.BlockSpec((None, N_GROUPS, tn), lambda l, j: (l, 0, j)),
        out_shape=jax.ShapeDtypeStruct((DEPTH, N_GROUPS, 6 * D), F32),
        compiler_params=_cparams(("arbitrary", "arbitrary")),
        name="ada",
    )(cvec, w_ada, b_ada.reshape(DEPTH, 1, 6 * D))


def _pair_specs(tm, width):
    n_ctx = N_CTX_TOK // tm
    return [pl.BlockSpec((tm, width), lambda i: (jnp.minimum(i, n_ctx - 1), 0)),
            pl.BlockSpec((tm, width), lambda i: (jnp.maximum(i - n_ctx, 0), 0))]


def _read_x(x_refs, tm):
    if len(x_refs) == 1:
        return x_refs[0][...]
    return jnp.where(pl.program_id(0) < N_CTX_TOK // tm, x_refs[0][...], x_refs[1][...])


INPROJ_TM = 256


W_IN_TILE = 768
N_MAIN_TILES, N_GATE_TILES = IN_MAIN // W_IN_TILE, IN_GATE // W_IN_TILE


def _inproj_kernel(n_x, *refs):
    n_w = N_MAIN_TILES + N_GATE_TILES
    x_refs = refs[:n_x]
    mod_ref = refs[n_x]
    w_refs = refs[n_x + 1:n_x + 1 + n_w]
    wl_ref, om_ref, og_ref, ol_ref, ok_ref, ov_ref, wm_s, wg_s = refs[n_x + 1 + n_w:]

    @pl.when(pl.program_id(0) == 0)
    def _():
        for j, w_ref in enumerate(w_refs):
            dst, jj = (wm_s, j) if j < N_MAIN_TILES else (wg_s, j - N_MAIN_TILES)
            dst[:, jj * W_IN_TILE:(jj + 1) * W_IN_TILE] = w_ref[...].astype(BF16)

    sh = mod_ref[0:1, :]
    sc = mod_ref[1:2, :]
    h = (_read_x(x_refs, INPROJ_TM) * (1.0 + sc) + sh).astype(BF16)
    main = _dg(h, wm_s[...])
    om_ref[...] = main
    og_ref[...] = _sigmoid(_dg(h, wg_s[...])).astype(BF16)
    ol_ref[...] = _dg(h, wl_ref[...])

    @pl.when(pl.program_id(0) < N_CTX_TOK // INPROJ_TM)
    def _():
        ok_ref[...] = main[:, A_QW:A_QW + A_KVW]
        ov_ref[...] = main[:, A_QW + A_KVW:A_QW + 2 * A_KVW]


def _resident(shape):
    return pl.BlockSpec(shape, lambda *_: (0,) * len(shape), pipeline_mode=pl.Buffered(1))


def _layer_resident(shape, l):
    return pl.BlockSpec((None, *shape), lambda *_: (l,) + (0,) * len(shape), pipeline_mode=pl.Buffered(1))


def _inproj_call(xs, mod_l, w_in, w_lr, l):
    tm = INPROJ_TM
    n_ctx = N_CTX_TOK // tm
    x_specs = [pl.BlockSpec((tm, D), lambda i: (i, 0))] if len(xs) == 1 else _pair_specs(tm, D)
    kv_spec = pl.BlockSpec((tm, A_KVW), lambda i: (jnp.minimum(i, n_ctx - 1), 0))
    n_w = N_MAIN_TILES + N_GATE_TILES
    w_specs = [pl.BlockSpec((None, D, W_IN_TILE), lambda i, j=j: (l, 0, j), pipeline_mode=pl.Buffered(1))
               for j in range(n_w)]
    return pl.pallas_call(
        functools.partial(_inproj_kernel, len(xs)),
        grid=(N_TOK // tm,),
        in_specs=[
            *x_specs,
            pl.BlockSpec((None, 6, D), lambda i: (_group_of_tile(i, tm), 0, 0)),
            *w_specs,
            _resident((D, LR_W)),
        ],
        out_specs=[
            pl.BlockSpec((tm, IN_MAIN), lambda i: (i, 0)),
            pl.BlockSpec((tm, IN_GATE), lambda i: (i, 0)),
            pl.BlockSpec((tm, LR_W), lambda i: (i, 0)),
            kv_spec, kv_spec,
        ],
        out_shape=[
            jax.ShapeDtypeStruct((N_TOK, IN_MAIN), F32),
            jax.ShapeDtypeStruct((N_TOK, IN_GATE), BF16),
            jax.ShapeDtypeStruct((N_TOK, LR_W), F32),
            jax.ShapeDtypeStruct((N_CTX_TOK, A_KVW), F32),
            jax.ShapeDtypeStruct((N_CTX_TOK, A_KVW), F32),
        ],
        scratch_shapes=[pltpu.VMEM((D, IN_MAIN), BF16), pltpu.VMEM((D, IN_GATE), BF16)],
        compiler_params=_cparams(("arbitrary",)),
        name="inproj",
    )(*xs, mod_l, *([w_in] * n_w), w_lr)


def _sink_col(sink_ref, kvh, rows_per_head):
    n = A_G * rows_per_head
    r = _iota((n, 1), 0) // rows_per_head
    col = jnp.full((n, 1), sink_ref[kvh * A_G], F32)
    for g in range(1, A_G):
        col = jnp.where(r == g, sink_ref[kvh * A_G + g], col)
    return col


def _attn_ctx_kernel(sink_ref, q_ref, k_ref, v_ref, o_ref):
    scale = A_DH ** -0.5
    for kvh in range(A_KV):
        ks = k_ref[:, kvh * A_DH:(kvh + 1) * A_DH].astype(BF16)
        vs = v_ref[:, kvh * A_DH:(kvh + 1) * A_DH].astype(BF16)
        q4 = jnp.concatenate(
            [q_ref[:, (kvh * A_G + g) * A_DH:(kvh * A_G + g + 1) * A_DH] for g in range(A_G)], axis=0)
        s = _dg(q4.astype(BF16), ks, NT) * scale
        sink = _sink_col(sink_ref, kvh, CTX_T)
        m = jnp.maximum(jnp.max(s, -1, keepdims=True), sink)
        e = jnp.exp(s - m)
        p = e / (jnp.sum(e, -1, keepdims=True) + jnp.exp(sink - m))
        o = _dg(p.astype(BF16), vs)
        for g in range(A_G):
            h = kvh * A_G + g
            o_ref[:, h * A_DH:(h + 1) * A_DH] = o[g * CTX_T:(g + 1) * CTX_T, :]


def _attn_ctx_call(sink_l, pm):
    return pl.pallas_call(
        _attn_ctx_kernel,
        grid=(N_CTX_B,),
        in_specs=[
            pl.BlockSpec(memory_space=pltpu.SMEM),
            pl.BlockSpec((CTX_T, A_QW), lambda b: (b, 0)),
            pl.BlockSpec((CTX_T, A_KVW), lambda b: (b, A_QW // A_KVW)),
            pl.BlockSpec((CTX_T, A_KVW), lambda b: (b, A_QW // A_KVW + 1)),
        ],
        out_specs=pl.BlockSpec((CTX_T, A_QW), lambda b: (b, 0)),
        out_shape=jax.ShapeDtypeStruct((N_CTX_TOK, A_QW), F32),
        compiler_params=_cparams(("arbitrary",)),
        name="attn_ctx",
    )(sink_l, pm, pm, pm)


def _rope(x, cos, sin_signed):
    w = x.shape[-1]
    lane = _iota(x.shape, 1)
    q = A_DH // 4
    partner = jnp.where((lane % (2 * q)) < q, pltpu.roll(x, w - q, 1), pltpu.roll(x, q, 1))
    return x * cos + partner * sin_signed


def _attn_lat_kernel(sink_ref, q_ref, k_ref, v_ref, kc_ref, vc_ref, cos_ref, sin_ref, o_ref, kr_ref):
    n = pl.program_id(1)
    scale = A_DH ** -0.5

    @pl.when(n == 0)
    def _():
        kr_ref[...] = _rope(k_ref[...], cos_ref[:, 0:A_KVW], sin_ref[:, 0:A_KVW]).astype(BF16)

    q0 = pl.multiple_of(n * A_BLK, A_BLK)
    qr = _rope(q_ref[...], cos_ref[pl.ds(q0, A_BLK), :], sin_ref[pl.ds(q0, A_BLK), :])
    kstart = pl.multiple_of(jnp.clip((n - 1) * A_BLK, 0, DEC_T - 3 * A_BLK), A_BLK)
    kwin = kr_ref[pl.ds(kstart, 3 * A_BLK), :]
    vwin = v_ref[pl.ds(kstart, 3 * A_BLK), :].astype(BF16)
    kc = kc_ref[...].astype(BF16)
    vc = vc_ref[...].astype(BF16)
    rows = A_G * A_BLK
    qpos = q0 + _iota((rows, 3 * A_BLK), 0) % A_BLK
    kpos = kstart + _iota((rows, 3 * A_BLK), 1)
    valid = jnp.abs(qpos - kpos) <= A_WIN
    for kvh in range(A_KV):
        cs = slice(kvh * A_DH, (kvh + 1) * A_DH)
        q4 = jnp.concatenate(
            [qr[:, (kvh * A_G + g) * A_DH:(kvh * A_G + g + 1) * A_DH] for g in range(A_G)], axis=0).astype(BF16)
        s_loc = jnp.where(valid, _dg(q4, kwin[:, cs], NT) * scale, NEG_INF)
        s_ctx = _dg(q4, kc[:, cs], NT) * scale
        sink = _sink_col(sink_ref, kvh, A_BLK)
        m = jnp.maximum(jnp.maximum(jnp.max(s_loc, -1, keepdims=True), jnp.max(s_ctx, -1, keepdims=True)), sink)
        e_loc = jnp.exp(s_loc - m)
        e_ctx = jnp.exp(s_ctx - m)
        inv = 1.0 / (jnp.sum(e_loc, -1, keepdims=True) + jnp.sum(e_ctx, -1, keepdims=True) + jnp.exp(sink - m))
        o = _dg((e_loc * inv).astype(BF16), vwin[:, cs]) + _dg((e_ctx * inv).astype(BF16), vc[:, cs])
        for g in range(A_G):
            h = kvh * A_G + g
            o_ref[:, h * A_DH:(h + 1) * A_DH] = o[g * A_BLK:(g + 1) * A_BLK, :]


def _rope_tables():
    half = A_DH // 2
    t = np.arange(DEC_T)
    rows = (t // GRID_W).astype(np.float32)
    cols = (t % GRID_W).astype(np.float32)
    inv_freq = (ROPE_BASE ** (-np.arange(0, half, 2, dtype=np.float32) / half)).astype(np.float32)
    ang_r = rows[:, None] * inv_freq[None, :]
    ang_c = cols[:, None] * inv_freq[None, :]
    cos = np.concatenate([np.cos(ang_r), np.cos(ang_r), np.cos(ang_c), np.cos(ang_c)], -1)
    sin = np.concatenate([-np.sin(ang_r), np.sin(ang_r), -np.sin(ang_c), np.sin(ang_c)], -1)
    return (jnp.asarray(np.tile(cos, (1, A_HEADS)), F32), jnp.asarray(np.tile(sin, (1, A_HEADS)), F32))


def _attn_lat_call(sink_l, pm, kc, vc, l, cos, sin):
    nb = DEC_T // A_BLK
    row0 = N_CTX_TOK // A_BLK
    seq0 = N_CTX_TOK // DEC_T
    return pl.pallas_call(
        _attn_lat_kernel,
        grid=(N_DEC_B, nb),
        in_specs=[
            pl.BlockSpec(memory_space=pltpu.SMEM),
            pl.BlockSpec((A_BLK, A_QW), lambda b, n: (row0 + b * nb + n, 0)),
            pl.BlockSpec((DEC_T, A_KVW), lambda b, n: (seq0 + b, A_QW // A_KVW)),
            pl.BlockSpec((DEC_T, A_KVW), lambda b, n: (seq0 + b, A_QW // A_KVW + 1)),
            pl.BlockSpec((None, None, PAST, A_KVW), lambda b, n: (b, l, 0, 0)),
            pl.BlockSpec((None, None, PAST, A_KVW), lambda b, n: (b, l, 0, 0)),
            pl.BlockSpec((DEC_T, A_QW), lambda b, n: (0, 0)),
            pl.BlockSpec((DEC_T, A_QW), lambda b, n: (0, 0)),
        ],
        out_specs=pl.BlockSpec((A_BLK, A_QW), lambda b, n: (b * nb + n, 0)),
        out_shape=jax.ShapeDtypeStruct((N_DEC_B * DEC_T, A_QW), F32),
        scratch_shapes=[pltpu.VMEM((DEC_T, A_KVW), BF16)],
        compiler_params=_cparams(("arbitrary", "arbitrary")),
        name="attn_lat",
    )(sink_l, pm, pm, pm, kc, vc, cos, sin)


def _rwkv_kernel(T, NS, r_ref, k_ref, v_ref, lr_ref, s0_ref, w0_ref, wb_ref, a0_ref, ab_ref, gb_ref,
                 kk_ref, ka_ref, rk_ref, lng_ref, lnb_ref, o_ref, sfin_ref,
                 KK, W, WRP, AKK, KT, VC2, BON, Y, S):
    ones4 = _block_ones(B_W, B_DH)
    decay_c = float(np.exp(-0.5))
    RC = 256
    SUB = 32
    NP = 3

    def prep(c, carry):
        r0 = pl.multiple_of(c * RC, RC)
        rs = pl.ds(r0, RC)
        r = r_ref[rs, :]
        k = k_ref[rs, :]
        v = v_ref[rs, :]
        lr = lr_ref[rs, :]
        kkr = k * kk_ref[...]
        kk = kkr * lax.rsqrt(_mm_xr(kkr * kkr, ones4, 2) + 1e-12)
        KK[rs, :] = kk
        bonus = jnp.zeros((RC, B_W), F32)
        vc2 = jnp.zeros((RC, B_W), F32)
        for d in range(2):
            z = w0_ref[d] + _mm(jnp.tanh(lr[:, B_RANK * d:B_RANK * (d + 1)]), wb_ref[d])
            w = jnp.exp(-decay_c * _sigmoid(z))
            a = _sigmoid(a0_ref[d] + _mm(lr[:, B_RANK * (2 + d):B_RANK * (3 + d)], ab_ref[d]))
            kt = k * (1.0 + (a - 1.0) * ka_ref[...])
            akk = a * kk
            W[d, rs, :] = w
            WRP[d, rs, :] = w * r - _mm_xr(akk * r, ones4, 1) * kk
            AKK[d, rs, :] = akk
            KT[d, rs, :] = kt
            vc2 = vc2 + _mm_xr(kt * r, ones4, 1) * v
            bonus = bonus + _mm_xr(r * kt * rk_ref[...], ones4, 1) * v
        VC2[rs, :] = vc2
        BON[rs, :] = bonus
        return carry

    lax.fori_loop(0, NS * T // RC, prep, 0)

    chains = [(s, d) for s in range(NS) for d in range(2)]
    for s, d in chains:
        S[s, d] = jnp.concatenate([s0_ref[s, d, h] for h in range(B_HEADS)], axis=1)

    eye4 = _iota((B_DH, B_W), 0) == (_iota((B_DH, B_W), 1) % B_DH)

    def steps(i, carry):
        t0s = [pl.multiple_of(s * T + (i * SUB if d == 0 else T - SUB - i * SUB), SUB) for s, d in chains]

        def row(ref, g, j, d=None):
            tile = pl.ds(t0s[g] + (j // SUBLANES) * SUBLANES, SUBLANES)
            vals = ref[tile, :] if d is None else ref[d, tile, :]
            return vals[j % SUBLANES:j % SUBLANES + 1]

        ys = [[None] * SUB for _ in chains]
        for jj in range(SUB):
            lhs = []
            for g, (s, d) in enumerate(chains):
                j = jj if d == 0 else SUB - 1 - jj
                stb = S[s, d].astype(BF16)
                lhs += [stb * row(KK, g, j).astype(BF16), stb * row(WRP, g, j, d).astype(BF16),
                        jnp.where(eye4, row(v_ref, g, j), 0.0).astype(BF16)]
            res = _dg(jnp.concatenate(lhs, axis=0), ones4)
            for g, (s, d) in enumerate(chains):
                j = jj if d == 0 else SUB - 1 - jj
                sk, yp, vcol = [res[(g * NP + n) * B_DH:(g * NP + n + 1) * B_DH] for n in range(NP)]
                S[s, d] = S[s, d] * row(W, g, j, d) - sk * row(AKK, g, j, d) + vcol * row(KT, g, j, d)
                ys[g][j] = jnp.sum(jnp.where(eye4, yp, 0.0), axis=0, keepdims=True)
        for g, (s, d) in enumerate(chains):
            Y[d, pl.ds(t0s[g], SUB), :] = jnp.concatenate(ys[g], axis=0)
        return carry

    lax.fori_loop(0, T // SUB, steps, 0)

    for s, d in chains:
        st = S[s, d]
        for h in range(B_HEADS):
            sfin_ref[s, d, h] = st[:, h * B_DH:(h + 1) * B_DH]

    def post(c, carry):
        r0 = pl.multiple_of(c * RC, RC)
        rs = pl.ds(r0, RC)
        y = Y[0, rs, :] + Y[1, rs, :] + VC2[rs, :]
        mu = _mm_xr(y, ones4, 2) * (1.0 / B_DH)
        yc = y - mu
        var = _mm_xr(yc * yc, ones4, 2) * (1.0 / B_DH)
        yn = yc * lax.rsqrt(var + B_GN_EPS) * lng_ref[...] + lnb_ref[...] + BON[rs, :]
        g = _mm(_sigmoid(lr_ref[rs, 4 * B_RANK:4 * B_RANK + B_GATE_RANK]), gb_ref[...])
        o_ref[rs, :] = yn * g
        return carry

    lax.fori_loop(0, NS * T // RC, post, 0)


def _rwkv_call(T, NS, n_seq, tok0, pm, plr, s0, prm, name):
    rows = NS * T
    blk0 = tok0 // rows
    n_steps = n_seq // NS
    full = lambda shape: pl.BlockSpec(shape, lambda b: (0,) * len(shape))
    big = lambda shape, imap: (pl.BlockSpec(shape, imap, pipeline_mode=pl.Buffered(1)) if n_steps == 1
                               else pl.BlockSpec(shape, imap))
    kern = functools.partial(_rwkv_kernel, T, NS)
    return pl.pallas_call(
        kern,
        grid=(n_steps,),
        in_specs=[
            big((rows, B_W), lambda b: (blk0 + b, COL_B // B_W)),
            big((rows, B_W), lambda b: (blk0 + b, COL_B // B_W + 1)),
            big((rows, B_W), lambda b: (blk0 + b, COL_B // B_W + 2)),
            big((rows, LR_W), lambda b: (blk0 + b, 0)),
            pl.BlockSpec((NS, 2, B_HEADS, B_DH, B_DH), lambda b: (b, 0, 0, 0, 0)),
            full((2, 1, B_W)), full((2, B_RANK, B_W)), full((2, 1, B_W)), full((2, B_RANK, B_W)),
            full((B_GATE_RANK, B_W)),
            full((1, B_W)), full((1, B_W)), full((1, B_W)), full((1, B_W)), full((1, B_W)),
        ],
        out_specs=[
            big((rows, B_W), lambda b: (b, 0)),
            pl.BlockSpec((NS, 2, B_HEADS, B_DH, B_DH), lambda b: (b, 0, 0, 0, 0)),
        ],
        out_shape=[
            jax.ShapeDtypeStruct((n_seq * T, B_W), F32),
            jax.ShapeDtypeStruct((n_seq, 2, B_HEADS, B_DH, B_DH), F32),
        ],
        scratch_shapes=[
            pltpu.VMEM((rows, B_W), F32),
            pltpu.VMEM((2, rows, B_W), F32),
            pltpu.VMEM((2, rows, B_W), F32),
            pltpu.VMEM((2, rows, B_W), F32),
            pltpu.VMEM((2, rows, B_W), F32),
            pltpu.VMEM((rows, B_W), F32),
            pltpu.VMEM((rows, B_W), F32),
            pltpu.VMEM((2, rows, B_W), F32),
            pltpu.VMEM((NS, 2, B_DH, B_W), F32),
        ],
        compiler_params=_cparams(("arbitrary",)),
        name=name,
    )(pm, pm, pm, plr, s0, *prm)


def _gla_kernel(T, NS, q_ref, k_ref, v_ref, og_ref, lr_ref, s0_ref, gb_ref, bias_ref, ng_ref,
                o_ref, sfin_ref, LA, O, S):
    n_chunks = T // C_CHUNK
    nsub = C_CHUNK // C_SUB
    qscale = C_DK ** -0.5
    lr = lr_ref[...]
    for d in range(2):
        gl = _mm(lr, gb_ref[d]) + bias_ref[d]
        LA[d] = (jnp.minimum(gl, 0.0) - jnp.log(1.0 + jnp.exp(-jnp.abs(gl)))) * (1.0 / C_GATE_NORM)
    bd_state = _iota((C_KW, C_VW), 0) // C_DK == _iota((C_KW, C_VW), 1) // C_DV
    chains = [(s, d) for s in range(NS) for d in range(2)]
    for s, d in chains:
        for h in range(C_HEADS):
            pad_l = h * C_DV
            pad_r = C_VW - (h + 1) * C_DV
            blk = s0_ref[s, d, h]
            parts = ([jnp.zeros((C_DK, pad_l), F32)] if pad_l else []) + [blk] + \
                    ([jnp.zeros((C_DK, pad_r), F32)] if pad_r else [])
            S[s, d, h * C_DK:(h + 1) * C_DK, :] = jnp.concatenate(parts, axis=1)

    ti = _iota((C_CHUNK, C_CHUNK), 0)
    si = _iota((C_CHUNK, C_CHUNK), 1)
    tri = ((si <= ti).astype(BF16), (si >= ti).astype(BF16))
    trow = _iota((C_CHUNK, 1), 0)
    mask_k = _iota((C_CHUNK, C_KW), 0) // C_SUB == _iota((C_CHUNK, C_KW), 1) // C_DK
    mask_v = _iota((C_CHUNK, C_VW), 0) // C_SUB == _iota((C_CHUNK, C_VW), 1) // C_DV
    t_att = _iota((C_CHUNK, C_CHUNK), 0)
    s_att = _iota((C_CHUNK, C_CHUNK), 1) % C_SUB
    eye_k = _iota((C_KW, C_KW), 0) == _iota((C_KW, C_KW), 1)

    def body(c, carry):
        cx = []
        for s, d in chains:
            cc = c if d == 0 else n_chunks - 1 - c
            rs = pl.ds(pl.multiple_of(s * T + cc * C_CHUNK, C_CHUNK), C_CHUNK)
            b = _mm_xl(tri[d], LA[d, rs, :], 3)
            cx.append(dict(s=s, d=d, rs=rs, b=b, q=q_ref[rs, :] * qscale, k=k_ref[rs, :], v=v_ref[rs, :]))
        for x in cx:
            x["o"] = _mm(x["q"] * jnp.exp(x["b"]), S[x["s"], x["d"]])
        for j in range(nsub):
            lo, hi = j * C_SUB, (j + 1) * C_SUB
            for x in cx:
                b, q, k = x["b"], x["q"], x["k"]
                if x["d"] == 0:
                    gamma = b[hi - 1:hi, :]
                    row_ok = trow >= lo
                    att_ok = t_att >= lo + s_att
                else:
                    gamma = b[lo:lo + 1, :]
                    row_ok = trow < hi
                    att_ok = t_att <= lo + s_att
                qj = q * jnp.exp(jnp.where(row_ok, b - gamma, NEG_INF))
                kj = k[lo:hi, :] * jnp.exp(gamma - b[lo:hi, :])
                kbd = jnp.where(mask_k, jnp.concatenate([kj] * C_HEADS, axis=0), 0.0)
                x["att"] = jnp.where(att_ok, _mm(qj, kbd, NT), 0.0)
            for x in cx:
                vbd = jnp.where(mask_v, jnp.concatenate([x["v"][lo:hi, :]] * C_HEADS, axis=0), 0.0)
                x["o"] = x["o"] + _mm(x["att"], vbd)
        for x in cx:
            s, d, b = x["s"], x["d"], x["b"]
            O[d, x["rs"], :] = x["o"]
            blast = b[C_CHUNK - 1:C_CHUNK, :] if d == 0 else b[0:1, :]
            kl = x["k"] * jnp.exp(blast - b)
            upd = jnp.where(bd_state, _mm3(kl.T, x["v"]), 0.0)
            dec = jnp.where(eye_k, jnp.exp(blast), 0.0)
            S[s, d] = _mm3(dec, S[s, d]) + upd
        return carry

    def body_group(cg, carry):
        for u in range(GLA_UNROLL):
            body(cg * GLA_UNROLL + u, carry)
        return carry

    lax.fori_loop(0, n_chunks // GLA_UNROLL, body_group, 0)

    for s, d in chains:
        st = S[s, d]
        for h in range(C_HEADS):
            sfin_ref[s, d, h] = st[h * C_DK:(h + 1) * C_DK, h * C_DV:(h + 1) * C_DV]

    ones4 = _block_ones(C_VW, C_DV)
    o = O[0] + O[1]
    ms = _mm_xr(o * o, ones4, 2) * (1.0 / C_DV)
    og = og_ref[...]
    o_ref[...] = o * lax.rsqrt(ms + LN_EPS) * ng_ref[...] * (og * _sigmoid(og))


def _gla_call(T, NS, n_seq, tok0, pm, plr, s0, gb_pad, bias, ng, name):
    rows = NS * T
    blk0 = tok0 // rows
    full = lambda shape: pl.BlockSpec(shape, lambda b: (0,) * len(shape))
    return pl.pallas_call(
        functools.partial(_gla_kernel, T, NS),
        grid=(n_seq // NS,),
        in_specs=[
            pl.BlockSpec((rows, C_KW), lambda b: (blk0 + b, COL_C // C_KW)),
            pl.BlockSpec((rows, C_KW), lambda b: (blk0 + b, COL_C // C_KW + 1)),
            pl.BlockSpec((rows, C_VW), lambda b: (blk0 + b, COL_CV // C_VW)),
            pl.BlockSpec((rows, C_VW), lambda b: (blk0 + b, COL_CV // C_VW + 1)),
            pl.BlockSpec((rows, LANES), lambda b: (blk0 + b, C_LR_COL // LANES)),
            pl.BlockSpec((NS, 2, C_HEADS, C_DK, C_DV), lambda b: (b, 0, 0, 0, 0)),
            full((2, LANES, C_KW)), full((2, 1, C_KW)), full((1, C_VW)),
        ],
        out_specs=[
            pl.BlockSpec((rows, C_VW), lambda b: (b, 0)),
            pl.BlockSpec((NS, 2, C_HEADS, C_DK, C_DV), lambda b: (b, 0, 0, 0, 0)),
        ],
        out_shape=[
            jax.ShapeDtypeStruct((n_seq * T, C_VW), F32),
            jax.ShapeDtypeStruct((n_seq, 2, C_HEADS, C_DK, C_DV), F32),
        ],
        scratch_shapes=[
            pltpu.VMEM((2, rows, C_KW), F32),
            pltpu.VMEM((2, rows, C_VW), F32),
            pltpu.VMEM((NS, 2, C_KW, C_VW), F32),
        ],
        compiler_params=_cparams(("arbitrary",)),
        name=name,
    )(pm, pm, pm, pm, plr, s0, gb_pad, bias, ng)


MERGE_TM = 512


def _merge_kernel(n_x, *refs):
    x_refs = refs[:n_x]
    (g_ref, oac_ref, oad_ref, obc_ref, obd_ref, occ_ref, ocd_ref, mod_ref,
     wa_ref, wb_ref, wc_ref, wo_ref, lg_ref, lb_ref, o_ref, wa_s, wb_s, wc_s, wo_s) = refs[n_x:]

    @pl.when(pl.program_id(0) == 0)
    def _():
        wa_s[...] = wa_ref[...].astype(BF16)
        wb_s[...] = wb_ref[...].astype(BF16)
        wc_s[...] = wc_ref[...].astype(BF16)
        wo_s[...] = wo_ref[...].astype(BF16)

    is_ctx = pl.program_id(0) < N_CTX_TOK // MERGE_TM
    oa = jnp.where(is_ctx, oac_ref[...], oad_ref[...]).astype(BF16)
    ob = jnp.where(is_ctx, obc_ref[...], obd_ref[...]).astype(BF16)
    oc = jnp.where(is_ctx, occ_ref[...], ocd_ref[...]).astype(BF16)
    merged = (g_ref[:, 0:D].astype(F32) * _dg(oa, wa_s[...])
              + g_ref[:, D:2 * D].astype(F32) * _dg(ob, wb_s[...])
              + g_ref[:, 2 * D:3 * D].astype(F32) * _dg(oc, wc_s[...]))
    mix = _dg(merged.astype(BF16), wo_s[...])
    y = DN_ALPHA * _read_x(x_refs, MERGE_TM) + mod_ref[2:3, :] * mix
    o_ref[...] = _layer_norm(y, lg_ref[...], lb_ref[...])


def _merge_call(xs, gates, oa, ob, oc, mod_l, wa, wb, wc, wo, lg, lb, l):
    tm = MERGE_TM
    x_specs = [pl.BlockSpec((tm, D), lambda i: (i, 0))] if len(xs) == 1 else _pair_specs(tm, D)
    return pl.pallas_call(
        functools.partial(_merge_kernel, len(xs)),
        grid=(N_TOK // tm,),
        in_specs=[
            *x_specs,
            pl.BlockSpec((tm, IN_GATE), lambda i: (i, 0)),
            *_pair_specs(tm, A_QW), *_pair_specs(tm, B_W), *_pair_specs(tm, C_VW),
            pl.BlockSpec((None, 6, D), lambda i: (_group_of_tile(i, tm), 0, 0)),
            _layer_resident((A_QW, D), l), _layer_resident((B_W, D), l), _layer_resident((C_VW, D), l),
            _layer_resident((D, D), l), _layer_resident((1, D), l), _layer_resident((1, D), l),
        ],
        out_specs=pl.BlockSpec((tm, D), lambda i: (i, 0)),
        out_shape=jax.ShapeDtypeStruct((N_TOK, D), F32),
        scratch_shapes=[pltpu.VMEM((A_QW, D), BF16), pltpu.VMEM((B_W, D), BF16),
                        pltpu.VMEM((C_VW, D), BF16), pltpu.VMEM((D, D), BF16)],
        compiler_params=_cparams(("arbitrary",)),
        name="merge",
    )(*xs, gates, *oa, *ob, *oc, mod_l, wa, wb, wc, wo, lg.reshape(DEPTH, 1, D), lb.reshape(DEPTH, 1, D))


def _ffn_kernel(x_ref, mod_ref, wg_ref, wu_ref, wd_ref, lg_ref, lb_ref, o_ref):
    x = x_ref[...]
    h = (x * (1.0 + mod_ref[4:5, :]) + mod_ref[3:4, :]).astype(BF16)
    gate = _dg(h, wg_ref[...])
    up = _dg(h, wu_ref[...])
    f = _dg((gate * _sigmoid(gate) * up).astype(BF16), wd_ref[...])
    y = DN_ALPHA * x + mod_ref[5:6, :] * f
    o_ref[...] = _layer_norm(y, lg_ref[...], lb_ref[...])


def _ffn_call(x, mod_l, wg, wu, wd, lg, lb):
    tm = 512
    return pl.pallas_call(
        _ffn_kernel,
        grid=(N_TOK // tm,),
        in_specs=[
            pl.BlockSpec((tm, D), lambda i: (i, 0)),
            pl.BlockSpec((None, 6, D), lambda i: (_group_of_tile(i, tm), 0, 0)),
            _resident((D, D_FF)), _resident((D, D_FF)), _resident((D_FF, D)),
            _resident((1, D)), _resident((1, D)),
        ],
        out_specs=pl.BlockSpec((tm, D), lambda i: (i, 0)),
        out_shape=jax.ShapeDtypeStruct((N_TOK, D), F32),
        compiler_params=_cparams(("arbitrary",)),
        name="ffn",
    )(x, mod_l, wg, wu, wd, lg.reshape(1, D), lb.reshape(1, D))


MOE_TM = 1024
MOE_TR = 256
MOE_RS = 3072
MOE_TF = 512
MOE_ROWS = 2 * N_TOK + N_EXP * MOE_TR
MOE_NST = -(-MOE_ROWS // MOE_RS) + N_EXP
R_I1, R_I2, R_W1, R_W2, R_RANK1, R_RANK2 = range(6)


def _moe_route_kernel(x_ref, mod_ref, wr_ref, h_ref, info_ref, cnt_ref, carry_s):
    tm = MOE_TM

    @pl.when(pl.program_id(0) == 0)
    def _():
        carry_s[...] = jnp.zeros_like(carry_s)

    h = x_ref[...] * (1.0 + mod_ref[4:5, :]) + mod_ref[3:4, :]
    h_ref[...] = h
    logits = _mm3(h, wr_ref[...])
    lane = _iota(logits.shape, 1)
    logits = jnp.where(lane < N_EXP, logits, NEG_INF)
    v1 = jnp.max(logits, -1, keepdims=True)
    i1 = jnp.min(jnp.where(logits == v1, lane, LANES), -1, keepdims=True)
    rest = jnp.where(lane == i1, NEG_INF, logits)
    v2 = jnp.max(rest, -1, keepdims=True)
    i2 = jnp.min(jnp.where(rest == v2, lane, LANES), -1, keepdims=True)
    e2 = jnp.exp(v2 - v1)
    w1 = 1.0 / (1.0 + e2)
    w2 = e2 / (1.0 + e2)
    oh1 = lane == i1
    oh2 = lane == i2
    cnt = oh1.astype(F32) + oh2.astype(F32)
    earlier = (_iota((tm, tm), 1) < _iota((tm, tm), 0)).astype(BF16)
    before = _dg(earlier, cnt.astype(BF16)) + carry_s[...]
    rank1 = jnp.sum(jnp.where(oh1, before, 0.0), -1, keepdims=True)
    rank2 = jnp.sum(jnp.where(oh2, before, 0.0), -1, keepdims=True)
    info = jnp.zeros(logits.shape, F32)
    for ln, val in ((R_I1, i1.astype(F32)), (R_I2, i2.astype(F32)), (R_W1, w1), (R_W2, w2),
                    (R_RANK1, rank1), (R_RANK2, rank2)):
        info = jnp.where(lane == ln, val, info)
    info_ref[...] = info
    carry_s[...] += jnp.sum(cnt, axis=0, keepdims=True)
    cnt_ref[...] = carry_s[...]


def _moe_route_call(x, mod_l, wr_pad, i_moe):
    tm = MOE_TM
    return pl.pallas_call(
        _moe_route_kernel,
        grid=(N_TOK // tm,),
        in_specs=[
            pl.BlockSpec((tm, D), lambda i: (i, 0)),
            pl.BlockSpec((None, 6, D), lambda i: (_group_of_tile(i, tm), 0, 0)),
            pl.BlockSpec((None, D, LANES), lambda i: (i_moe, 0, 0)),
        ],
        out_specs=[
            pl.BlockSpec((tm, D), lambda i: (i, 0)),
            pl.BlockSpec((tm, LANES), lambda i: (i, 0)),
            pl.BlockSpec((1, LANES), lambda i: (0, 0)),
        ],
        out_shape=[
            jax.ShapeDtypeStruct((N_TOK, D), F32),
            jax.ShapeDtypeStruct((N_TOK, LANES), F32),
            jax.ShapeDtypeStruct((1, LANES), F32),
        ],
        scratch_shapes=[pltpu.VMEM((1, LANES), F32)],
        compiler_params=_cparams(("arbitrary",)),
        name="moe_route",
    )(x, mod_l, wr_pad)


def _moe_plan(info, cnt):
    i32 = jnp.int32
    i1 = info[:, R_I1].astype(i32)
    i2 = info[:, R_I2].astype(i32)
    counts = cnt[0, :N_EXP].astype(i32)
    padded = (counts + MOE_TR - 1) // MOE_TR * MOE_TR
    seg_start = jnp.cumsum(padded) - padded
    pos1 = seg_start[i1] + info[:, R_RANK1].astype(i32)
    pos2 = seg_start[i2] + info[:, R_RANK2].astype(i32)
    dst = jnp.zeros((MOE_ROWS,), i32).at[jnp.concatenate([pos1, pos2])].set(jnp.arange(2 * N_TOK, dtype=i32))
    n_pass = (padded + MOE_RS - 1) // MOE_RS
    pass_end = jnp.cumsum(n_pass)
    total = pass_end[-1]
    sidx = jnp.arange(MOE_NST, dtype=i32)
    used = sidx < total
    e_of = jnp.minimum(jnp.searchsorted(pass_end, jnp.minimum(sidx, total - 1), side="right"), N_EXP - 1).astype(i32)
    k = jnp.minimum(sidx, total - 1) - (pass_end - n_pass)[e_of]
    row0 = seg_start[e_of] + k * MOE_RS
    nrows = jnp.where(used, jnp.clip(padded[e_of] - k * MOE_RS, 0, MOE_RS), 0)
    nvalid = jnp.where(used, jnp.clip(counts[e_of] - k * MOE_RS, 0, MOE_RS), 0)
    src = jnp.where(dst >= N_TOK, dst - N_TOK, dst)
    return src, dst, e_of, row0.astype(i32), nrows.astype(i32), nvalid.astype(i32)


def _moe_expert_kernel(src_ref, dst_ref, exp_ref, row0_ref, nrows_ref, nvalid_ref,
                       h_hbm, wg_ref, wu_ref, wd_ref, yo_hbm, xs, xb, acc, wgb, wub, wdb, gsem, ssem):
    s = pl.program_id(0)
    f = pl.program_id(1)
    nf = pl.num_programs(1)
    nrows = pl.multiple_of(nrows_ref[s], MOE_TR)
    nvalid = nvalid_ref[s]
    row0 = row0_ref[s]
    n_chunks = nrows // MOE_TR

    def row_copy(src, dst, sem):
        return pltpu.make_async_copy(src, dst, sem)

    def hbm_row(ref, i):
        return ref.at[pl.ds(i, 1), :]

    def wait_rows(buf, n_groups, sem):
        pltpu.make_async_copy(buf.at[pl.ds(0, n_groups)], buf.at[pl.ds(0, n_groups)], sem).wait()

    @pl.when((f == 0) & (nrows > 0))
    def _gather():
        def issue(g, carry):
            for u in range(SUBLANES):
                tok = src_ref[row0 + g * SUBLANES + u]
                row_copy(hbm_row(h_hbm, tok), xs.at[g, pl.ds(u, 1), :], gsem).start(priority=u % 2)
            return carry

        ng = nrows // SUBLANES
        lax.fori_loop(0, ng, issue, 0)
        wait_rows(xs, ng, gsem)

        def cvt(c, carry):
            g0 = pl.multiple_of(c * (MOE_TR // SUBLANES), MOE_TR // SUBLANES)
            rs = pl.ds(pl.multiple_of(c * MOE_TR, MOE_TR), MOE_TR)
            xb[rs, :] = xs[pl.ds(g0, MOE_TR // SUBLANES)].reshape(MOE_TR, D).astype(BF16)
            return carry

        lax.fori_loop(0, n_chunks, cvt, 0)

    @pl.when(nrows > 0)
    def _compute():
        wgb[...] = wg_ref[...].astype(BF16)
        wub[...] = wu_ref[...].astype(BF16)
        wdb[...] = wd_ref[...].astype(BF16)

        def chunk(start, n):
            x = xb[pl.ds(pl.multiple_of(start, MOE_TR), n), :]
            gate = _dg(x, wgb[...])
            up = _dg(x, wub[...])
            y = _dg((gate * _sigmoid(gate) * up).astype(BF16), wdb[...]).reshape(n // SUBLANES, SUBLANES, D)
            gs = pl.ds(pl.multiple_of(start // SUBLANES, MOE_TR // SUBLANES), n // SUBLANES)

            @pl.when(f == 0)
            def _():
                acc[gs] = y

            @pl.when(f > 0)
            def _():
                acc[gs] += y

        def chunk_quad(c4, carry):
            chunk(c4 * (4 * MOE_TR), 4 * MOE_TR)
            return carry

        lax.fori_loop(0, n_chunks // 4, chunk_quad, 0)
        done = n_chunks // 4 * 4

        @pl.when(n_chunks % 4 >= 2)
        def _():
            chunk(done * MOE_TR, 2 * MOE_TR)

        @pl.when(n_chunks % 2 == 1)
        def _():
            chunk((n_chunks - 1) * MOE_TR, MOE_TR)

    @pl.when((f == nf - 1) & (nvalid > 0))
    def _scatter():
        n8 = nvalid // SUBLANES

        def issue(g, carry):
            for u in range(SUBLANES):
                d = dst_ref[row0 + g * SUBLANES + u]
                row_copy(acc.at[g, pl.ds(u, 1), :], hbm_row(yo_hbm, d), ssem).start(priority=u % 2)
            return carry

        def issue_tail(r, carry):
            d = dst_ref[row0 + r]
            row_copy(acc.at[n8, pl.ds(r - n8 * SUBLANES, 1), :], hbm_row(yo_hbm, d), ssem).start()
            return carry

        lax.fori_loop(0, n8, issue, 0)
        lax.fori_loop(n8 * SUBLANES, nvalid, issue_tail, 0)

        @pl.when(n8 > 0)
        def _():
            wait_rows(acc, n8, ssem)

        def wait_one(r, carry):
            row_copy(acc.at[0, pl.ds(0, 1), :], hbm_row(yo_hbm, 0), ssem).wait()
            return carry

        lax.fori_loop(n8 * SUBLANES, nvalid, wait_one, 0)


def _moe_expert_call(h, plan, wg, wu, wd, i_moe):
    nf = D_FFE // MOE_TF

    def wspec(shape, fdim):
        def imap(s, f, src, dst, exp, row0, nrows, nvalid):
            fe = jnp.where(nrows[s] > 0, f, nf - 1)
            return (i_moe, exp[s], 0, fe) if fdim == 3 else (i_moe, exp[s], fe, 0)
        return pl.BlockSpec(shape, imap)

    grid_spec = pltpu.PrefetchScalarGridSpec(
        num_scalar_prefetch=6,
        grid=(MOE_NST, nf),
        in_specs=[
            pl.BlockSpec(memory_space=pl.ANY),
            wspec((None, None, D, MOE_TF), 3),
            wspec((None, None, D, MOE_TF), 3),
            wspec((None, None, MOE_TF, D), 2),
        ],
        out_specs=pl.BlockSpec(memory_space=pl.ANY),
        scratch_shapes=[
            pltpu.VMEM((MOE_RS // SUBLANES, SUBLANES, D), F32),
            pltpu.VMEM((MOE_RS, D), BF16),
            pltpu.VMEM((MOE_RS // SUBLANES, SUBLANES, D), F32),
            pltpu.VMEM((D, MOE_TF), BF16), pltpu.VMEM((D, MOE_TF), BF16), pltpu.VMEM((MOE_TF, D), BF16),
            pltpu.SemaphoreType.DMA(()), pltpu.SemaphoreType.DMA(()),
        ],
    )
    return pl.pallas_call(
        _moe_expert_kernel,
        grid_spec=grid_spec,
        out_shape=jax.ShapeDtypeStruct((2 * N_TOK, D), F32),
        compiler_params=pltpu.CompilerParams(dimension_semantics=("arbitrary", "arbitrary"),
                                             vmem_limit_bytes=VMEM_LIMIT, disable_bounds_checks=True),
        name="moe_experts",
    )(*plan, h, wg, wu, wd)


def _moe_combine_kernel(x_ref, y1_ref, y2_ref, info_ref, mod_ref, lg_ref, lb_ref, oc_ref, od_ref):
    f = info_ref[:, R_W1:R_W1 + 1] * y1_ref[...] + info_ref[:, R_W2:R_W2 + 1] * y2_ref[...]
    y = DN_ALPHA * x_ref[...] + mod_ref[5:6, :] * f
    out = _layer_norm(y, lg_ref[...], lb_ref[...])
    is_ctx = pl.program_id(0) < N_CTX_TOK // MOE_TM

    @pl.when(is_ctx)
    def _():
        oc_ref[...] = out

    @pl.when(jnp.logical_not(is_ctx))
    def _():
        od_ref[...] = out


def _moe_combine_call(x, yo, info, mod_l, lg, lb):
    tm = MOE_TM
    nt = N_TOK // tm
    return pl.pallas_call(
        _moe_combine_kernel,
        grid=(nt,),
        in_specs=[
            pl.BlockSpec((tm, D), lambda i: (i, 0)),
            pl.BlockSpec((tm, D), lambda i: (i, 0)),
            pl.BlockSpec((tm, D), lambda i: (nt + i, 0)),
            pl.BlockSpec((tm, LANES), lambda i: (i, 0)),
            pl.BlockSpec((None, 6, D), lambda i: (_group_of_tile(i, tm), 0, 0)),
            _resident((1, D)), _resident((1, D)),
        ],
        out_specs=_pair_specs(tm, D),
        out_shape=[jax.ShapeDtypeStruct((N_CTX_TOK, D), F32), jax.ShapeDtypeStruct((N_DEC_B * DEC_T, D), F32)],
        compiler_params=_cparams(("arbitrary",)),
        name="moe_combine",
    )(x, yo, yo, info, mod_l, lg.reshape(1, D), lb.reshape(1, D))


def _moe_call(x, mod_l, wr_pad, wg, wu, wd, lg, lb, i_moe):
    h, info, cnt = _moe_route_call(x, mod_l, wr_pad, i_moe)
    yo = _moe_expert_call(h, _moe_plan(info, cnt), wg, wu, wd, i_moe)
    return _moe_combine_call(x, yo, info, mod_l, lg, lb)


def kernel(x_prompt, x_sample, cache_attn_k, cache_attn_v, state_rwkv, state_gla, c, c_ctx, w_ada, b_ada, w_in,
           attn_sink, rwkv_w0, rwkv_w_a, rwkv_w_b, rwkv_a0, rwkv_a_a, rwkv_a_b, rwkv_g_a, rwkv_g_b, rwkv_k_k,
           rwkv_k_a, rwkv_r_k, rwkv_ln_g, rwkv_ln_b, gla_gate_a, gla_gate_b, gla_gate_bias, gla_norm_g, w_up_a,
           w_up_b, w_up_c, w_out, ln1_g, ln1_b, ln2_g, ln2_b, ffn_w_gate, ffn_w_up, ffn_w_down, moe_router,
           moe_w_gate, moe_w_up, moe_w_down):
    cvec = jnp.concatenate([c_ctx[None, :], c, jnp.zeros((N_GROUPS - 1 - N_DEC_B, D), F32)], axis=0)
    mods = _ada_call(cvec, w_ada, b_ada).reshape(DEPTH, N_GROUPS, 6, D)
    xs = (x_prompt.reshape(N_CTX_TOK, D), x_sample.reshape(N_DEC_B * DEC_T, D))
    cos, sin = _rope_tables()
    kc_all = cache_attn_k.reshape(N_DEC_B, DEPTH, PAST, A_KV * A_DH)
    vc_all = cache_attn_v.reshape(N_DEC_B, DEPTH, PAST, A_KV * A_DH)
    zeros_r = jnp.zeros((N_CTX_B, 2, B_HEADS, B_DH, B_DH), F32)
    zeros_g = jnp.zeros((N_CTX_B, 2, C_HEADS, C_DK, C_DV), F32)

    new_k, new_v, new_sr, new_sg = [], [], [], []
    for l in range(DEPTH):
        mod_l = mods[l]
        w_lr = jnp.concatenate(
            [rwkv_w_a[l, 0], rwkv_w_a[l, 1], rwkv_a_a[l, 0], rwkv_a_a[l, 1], rwkv_g_a[l],
             gla_gate_a[l, 0], gla_gate_a[l, 1], jnp.zeros((D, LR_W - C_LR_COL - 2 * C_GATE_RANK), F32)], axis=1).astype(BF16)
        pm, gates, plr, k_ctx, v_ctx = _inproj_call(xs, mod_l, w_in, w_lr, l)

        sink_l = attn_sink[l]
        oa = (_attn_ctx_call(sink_l, pm), _attn_lat_call(sink_l, pm, kc_all, vc_all, l, cos, sin))

        rprm = (rwkv_w0[l].reshape(2, 1, B_W), rwkv_w_b[l], rwkv_a0[l].reshape(2, 1, B_W), rwkv_a_b[l],
                rwkv_g_b[l], rwkv_k_k[l].reshape(1, B_W), rwkv_k_a[l].reshape(1, B_W),
                rwkv_r_k[l].reshape(1, B_W), rwkv_ln_g[l].reshape(1, B_W), rwkv_ln_b[l].reshape(1, B_W))
        ob_c, sr_c = _rwkv_call(CTX_T, 4, N_CTX_B, 0, pm, plr, zeros_r, rprm, "rwkv_ctx")
        ob_d, _ = _rwkv_call(DEC_T, 2, N_DEC_B, N_CTX_TOK, pm, plr, state_rwkv[:, l], rprm, "rwkv_lat")

        gb_pad = jnp.zeros((2, LANES, C_KW), F32)
        gb_pad = gb_pad.at[0, 0:C_GATE_RANK].set(gla_gate_b[l, 0]).at[1, C_GATE_RANK:2 * C_GATE_RANK].set(
            gla_gate_b[l, 1])
        gbias = gla_gate_bias[l].reshape(2, 1, C_KW)
        ng = jnp.tile(gla_norm_g[l], C_HEADS).reshape(1, C_VW)
        oc_c, sg_c = _gla_call(CTX_T, 4, N_CTX_B, 0, pm, plr, zeros_g, gb_pad, gbias, ng, "gla_ctx")
        oc_d, _ = _gla_call(DEC_T, 2, N_DEC_B, N_CTX_TOK, pm, plr, state_gla[:, l], gb_pad, gbias, ng, "gla_lat")

        x = _merge_call(xs, gates, oa, (ob_c, ob_d), (oc_c, oc_d), mod_l, w_up_a, w_up_b, w_up_c, w_out,
                        ln1_g, ln1_b, l)
        if l % 2 == 0:
            i_ffn = l // 2
            xs = (_ffn_call(x, mod_l, ffn_w_gate[i_ffn].astype(BF16), ffn_w_up[i_ffn].astype(BF16),
                            ffn_w_down[i_ffn].astype(BF16), ln2_g[l], ln2_b[l]),)
        else:
            wr_pad = jnp.concatenate([moe_router, jnp.zeros((moe_router.shape[0], D, LANES - N_EXP), F32)], axis=2)
            xs = _moe_call(x, mod_l, wr_pad, moe_w_gate, moe_w_up, moe_w_down, ln2_g[l], ln2_b[l], l // 2)

        new_k.append(k_ctx.reshape(N_CTX_B, CTX_T, A_KV, A_DH))
        new_v.append(v_ctx.reshape(N_CTX_B, CTX_T, A_KV, A_DH))
        new_sr.append(sr_c)
        new_sg.append(sg_c)

    if len(xs) == 1:
        xs = (xs[0][:N_CTX_TOK], xs[0][N_CTX_TOK:])
    y_prompt = xs[0].reshape(N_CTX_B, CTX_T, D)
    y_sample = xs[1].reshape(N_DEC_B, DEC_T, D)
    return (y_prompt, y_sample, jnp.stack(new_k, axis=1), jnp.stack(new_v, axis=1),
            jnp.stack(new_sr, axis=1), jnp.stack(new_sg, axis=1))
```
